```python
import jax
import jax.numpy as jnp
from jax import lax
import numpy as np

D_MODEL = 1024
BATCH = 8
SEQ = 4096
DEPTH = 2

CTX_LEN = 256
GRID_W = 64
HEAD_DIM = 64
N_HEADS = D_MODEL // 128
N_KV_HEADS = 2
ATTN_WIDTH = N_HEADS * HEAD_DIM
KV_WIDTH = N_KV_HEADS * HEAD_DIM
Q_BLOCK = 128
ROPE_THETA = 10000.0
LRU_WIDTH = D_MODEL // 4
LRU_BLOCKS = 4
LRU_BLOCK = LRU_WIDTH // LRU_BLOCKS
LRU_CONV = 4
LRU_CONV_LEFT = 2
LRU_C = 8.0
SC_WIDTH = D_MODEL // 4
SC_CONV = 3
MIX_WIDTH = ATTN_WIDTH + LRU_WIDTH + SC_WIDTH
IN_SIZES = (ATTN_WIDTH, KV_WIDTH, KV_WIDTH,
            LRU_WIDTH, LRU_WIDTH,
            SC_WIDTH, SC_WIDTH, SC_WIDTH)
IN_SPLITS = tuple(sum(IN_SIZES[:i + 1]) for i in range(len(IN_SIZES) - 1))
IN_WIDTH = sum(IN_SIZES)
D_FF = 2816
FFN_CONV = 3
EPS = 1e-6

kernel_name = "hymba_style_rglru_gqa_shortconv_convffn_prefix_ctx"

F32 = jnp.float32


def _rmsnorm(x, g):
    xf = x.astype(F32)
    y = xf * lax.rsqrt(jnp.mean(xf * xf, axis=-1, keepdims=True) + EPS)
    return (y * g.astype(F32)).astype(x.dtype)


def _heads(t, n):
    return t.reshape(*t.shape[:-1], n, HEAD_DIM)


def _rope_tables(n_tok):
    rows = n_tok // GRID_W
    pos_r = jnp.repeat(jnp.arange(rows, dtype=F32), GRID_W)
    pos_c = jnp.tile(jnp.arange(GRID_W, dtype=F32), rows)
    n_f = HEAD_DIM // 4
    inv = ROPE_THETA ** (-jnp.arange(n_f, dtype=F32) / n_f)
    ang = jnp.concatenate([pos_r[:, None] * inv, pos_c[:, None] * inv], axis=-1)
    return jnp.cos(ang), jnp.sin(ang)


def _rope(x, cos, sin):
    xf = x.astype(F32)
    x1, x2 = jnp.split(xf, 2, axis=-1)
    cs = cos[:, None, :]
    sn = sin[:, None, :]
    return jnp.concatenate([x1 * cs - x2 * sn, x1 * sn + x2 * cs], axis=-1).astype(x.dtype)


def _attend(q, k, v):
    b, t = q.shape[0], q.shape[1]
    grp = N_HEADS // N_KV_HEADS
    nb = t // Q_BLOCK
    qb = q.reshape(b, nb, Q_BLOCK, N_KV_HEADS, grp, HEAD_DIM).swapaxes(0, 1)
    scale = HEAD_DIM ** -0.5

    def one_block(qblk):
        s = jnp.einsum("bqkgd,bskd->bkgqs", qblk, k).astype(F32) * scale
        pr = jax.nn.softmax(s, axis=-1).astype(v.dtype)
        return jnp.einsum("bkgqs,bskd->bqkgd", pr, v)

    o = lax.map(one_block, qb)
    return o.swapaxes(0, 1).reshape(b, t, N_HEADS * HEAD_DIM)


def _dwconv(x, w, left):
    kw = w.shape[0]
    t = x.shape[1]
    xp = jnp.pad(x, ((0, 0), (left, kw - 1 - left), (0, 0)))
    y = xp[:, 0:t] * w[0]
    for j in range(1, kw):
        y = y + xp[:, j:j + t] * w[j]
    return y


def _block_diag(xf, w, b):
    xb = xf.reshape(*xf.shape[:-1], LRU_BLOCKS, LRU_BLOCK)
    y = jnp.einsum("btnd,nde->btne", xb, w.astype(F32)).reshape(xf.shape)
    return y + b.astype(F32)


def _rglru_coeffs(xc, w_a, b_a, w_i, b_i, lam):
    xf = xc.astype(F32)
    r = jax.nn.sigmoid(_block_diag(xf, w_a, b_a))
    i = jax.nn.sigmoid(_block_diag(xf, w_i, b_i))
    log_a = -LRU_C * r * jax.nn.softplus(-lam.astype(F32))
    a = jnp.exp(log_a)
    bb = jnp.sqrt(-jnp.expm1(2.0 * log_a)) * (i * xf)
    return a, bb


def _linear_scan(a, b, h0, reverse):
    if h0 is not None:
        edge = -1 if reverse else 0
        b = b.at[:, edge].add(a[:, edge] * h0)

    def combine(prev, nxt):
        a_p, b_p = prev
        a_n, b_n = nxt
        return a_p * a_n, a_n * b_p + b_n

    return lax.associative_scan(combine, (a, b), axis=1, reverse=reverse)[1]


def _conv_ffn(h, p):
    up = _dwconv(h @ p["w_up"], p["ffn_conv_w"], FFN_CONV // 2)
    u, g = jnp.split(up, 2, axis=-1)
    return (jax.nn.silu(g) * u) @ p["w_down"]


def _layer(x, ctx, c, c_ctx, p, cos, sin, update_ctx):
    dt = x.dtype
    mod_x = (jax.nn.silu(c) @ p["w_mod"] + p["b_mod"])[:, None, :]
    mod_c = jax.nn.silu(c_ctx) @ p["w_mod"] + p["b_mod"]
    sa_x, ca_x, ga_x, sf_x, cf_x, gf_x = jnp.split(mod_x, 6, axis=-1)
    sa_c, ca_c, ga_c, sf_c, cf_c, gf_c = jnp.split(mod_c, 6, axis=-1)

    hx = _rmsnorm(x, p["g_mix"]) * (1 + ca_x) + sa_x
    hc = _rmsnorm(ctx, p["g_mix"]) * (1 + ca_c) + sa_c
    qx, kx, vx, rx, gx, bx, cx, ux = jnp.split(hx @ p["w_in"], IN_SPLITS, axis=-1)
    qc, kc, vc, rc, gc, bc, cc, uc = jnp.split(hc @ p["w_in"], IN_SPLITS, axis=-1)

    kc_h = _rmsnorm(_heads(kc, N_KV_HEADS), p["g_k"])
    vc_h = _heads(vc, N_KV_HEADS)
    qx_h = _rope(_rmsnorm(_heads(qx, N_HEADS), p["g_q"]), cos, sin)
    kx_h = _rope(_rmsnorm(_heads(kx, N_KV_HEADS), p["g_k"]), cos, sin)
    k_all = jnp.concatenate([kx_h, kc_h], axis=1)
    v_all = jnp.concatenate([_heads(vx, N_KV_HEADS), vc_h], axis=1)
    att_x = _attend(qx_h, k_all, v_all)

    rc_conv = _dwconv(rc, p["lru_conv_w"], LRU_CONV_LEFT) + p["lru_conv_b"]
    rx_conv = _dwconv(rx, p["lru_conv_w"], LRU_CONV_LEFT) + p["lru_conv_b"]
    dir_par = [(p["lru_wa"][d], p["lru_ba"][d], p["lru_wi"][d], p["lru_bi"][d], p["lru_lam"][d]) for d in range(2)]
    a_cf, b_cf = _rglru_coeffs(rc_conv, *dir_par[0])
    a_cb, b_cb = _rglru_coeffs(rc_conv, *dir_par[1])
    a_xf, b_xf = _rglru_coeffs(rx_conv, *dir_par[0])
    a_xb, b_xb = _rglru_coeffs(rx_conv, *dir_par[1])
    h_cf = _linear_scan(a_cf, b_cf, None, False)
    h_cb = _linear_scan(a_cb, b_cb, None, True)
    h_xf = _linear_scan(a_xf, b_xf, h_cf[:, -1], False)
    h_xb = _linear_scan(a_xb, b_xb, h_cb[:, 0], True)
    lru_x = jax.nn.gelu(gx) * (h_xf + h_xb).astype(dt)

    sc_x = bx * _dwconv(cx * ux, p["sc_conv_w"], SC_CONV // 2)

    x = x + ga_x * (jnp.concatenate([att_x, lru_x, sc_x], axis=-1) @ p["w_out"])
    x = x + gf_x * _conv_ffn(_rmsnorm(x, p["g_ffn"]) * (1 + cf_x) + sf_x, p)

    if update_ctx:
        qc_h = _rmsnorm(_heads(qc, N_HEADS), p["g_q"])
        att_c = _attend(qc_h, kc_h, vc_h)
        lru_c = jax.nn.gelu(gc) * (h_cf + h_cb).astype(dt)
        sc_c = bc * _dwconv(cc * uc, p["sc_conv_w"], SC_CONV // 2)
        ctx = ctx + ga_c * (jnp.concatenate([att_c, lru_c, sc_c], axis=-1) @ p["w_out"])
        ctx = ctx + gf_c * _conv_ffn(_rmsnorm(ctx, p["g_ffn"]) * (1 + cf_c) + sf_c, p)
    return x, ctx


def setup_inputs(seed: int = 0) -> dict:
    key = jax.random.key(seed)
    ks = iter(jax.random.split(key, 32))
    D = D_MODEL

    def nrm(shape, scale):
        return jax.random.normal(next(ks), shape, F32) * scale

    x = nrm((BATCH, SEQ, D), 1.0)
    c = nrm((BATCH, D), 1.0)
    ctx = nrm((BATCH, CTX_LEN, D), 1.0)
    c_ctx = nrm((D,), 1.0)
    w_mod = nrm((DEPTH, D, 6 * D), 0.5 * D ** -0.5)
    b_mod = nrm((DEPTH, 6 * D), 0.02)
    g_mix = 1.0 + nrm((DEPTH, D), 0.05)
    g_ffn = 1.0 + nrm((DEPTH, D), 0.05)
    w_in = nrm((DEPTH, D, IN_WIDTH), D ** -0.5)
    g_q = 1.0 + nrm((DEPTH, HEAD_DIM), 0.05)
    g_k = 1.0 + nrm((DEPTH, HEAD_DIM), 0.05)
    lru_conv_w = nrm((DEPTH, LRU_CONV, LRU_WIDTH), LRU_CONV ** -0.5)
    lru_conv_b = nrm((DEPTH, LRU_WIDTH), 0.01)
    lru_wa = nrm((DEPTH, 2, LRU_BLOCKS, LRU_BLOCK, LRU_BLOCK), LRU_BLOCK ** -0.5)
    lru_ba = nrm((DEPTH, 2, LRU_WIDTH), 0.01)
    lru_wi = nrm((DEPTH, 2, LRU_BLOCKS, LRU_BLOCK, LRU_BLOCK), LRU_BLOCK ** -0.5)
    lru_bi = nrm((DEPTH, 2, LRU_WIDTH), 0.01)
    u = jax.random.uniform(next(ks), (DEPTH, 2, LRU_WIDTH), F32, 0.9, 0.999)
    a0 = u ** (1.0 / LRU_C)
    lru_lam = jnp.log(a0) - jnp.log1p(-a0)
    sc_conv_w = nrm((DEPTH, SC_CONV, SC_WIDTH), SC_CONV ** -0.5)
    w_out = nrm((DEPTH, MIX_WIDTH, D), MIX_WIDTH ** -0.5)
    w_up = nrm((DEPTH, D, 2 * D_FF), D ** -0.5)
    ffn_conv_w = nrm((DEPTH, FFN_CONV, 2 * D_FF), FFN_CONV ** -0.5)
    w_down = nrm((DEPTH, D_FF, D), D_FF ** -0.5)
    g_final = 1.0 + nrm((D,), 0.05)
    return {"x": x, "c": c, "ctx": ctx, "c_ctx": c_ctx, "w_mod": w_mod, "b_mod": b_mod,
            "g_mix": g_mix, "g_ffn": g_ffn, "w_in": w_in, "g_q": g_q, "g_k": g_k,
            "lru_conv_w": lru_conv_w, "lru_conv_b": lru_conv_b, "lru_wa": lru_wa, "lru_ba": lru_ba,
            "lru_wi": lru_wi, "lru_bi": lru_bi, "lru_lam": lru_lam, "sc_conv_w": sc_conv_w,
            "w_out": w_out, "w_up": w_up, "ffn_conv_w": ffn_conv_w, "w_down": w_down,
            "g_final": g_final}


def reference(x, c, ctx, c_ctx, w_mod, b_mod, g_mix, g_ffn, w_in, g_q, g_k,
              lru_conv_w, lru_conv_b, lru_wa, lru_ba, lru_wi, lru_bi, lru_lam, sc_conv_w,
              w_out, w_up, ffn_conv_w, w_down, g_final):
    cos, sin = _rope_tables(x.shape[1])
    for l in range(DEPTH):
        p = {"w_mod": w_mod[l], "b_mod": b_mod[l], "g_mix": g_mix[l], "g_ffn": g_ffn[l],
             "w_in": w_in[l], "g_q": g_q[l], "g_k": g_k[l],
             "lru_conv_w": lru_conv_w[l], "lru_conv_b": lru_conv_b[l],
             "lru_wa": lru_wa[l], "lru_ba": lru_ba[l], "lru_wi": lru_wi[l], "lru_bi": lru_bi[l],
             "lru_lam": lru_lam[l], "sc_conv_w": sc_conv_w[l], "w_out": w_out[l],
             "w_up": w_up[l], "ffn_conv_w": ffn_conv_w[l], "w_down": w_down[l]}
        x, ctx = _layer(x, ctx, c, c_ctx, p, cos, sin, l < DEPTH - 1)
    return _rmsnorm(x, g_final)
```

```python
import functools

import numpy as np
import jax
import jax.numpy as jnp
from jax import lax
from jax.experimental import pallas as pl
from jax.experimental.pallas import tpu as pltpu

F32 = jnp.float32
BF16 = jnp.bfloat16

D_MODEL = 1024
HEAD_DIM = 64
N_HEADS = 8
N_KV_HEADS = 2
HEADS_PER_KV = N_HEADS // N_KV_HEADS
ATTN_WIDTH = N_HEADS * HEAD_DIM
KV_WIDTH = N_KV_HEADS * HEAD_DIM
LRU_WIDTH = 256
LRU_BLOCKS = 4
LRU_BLOCK = LRU_WIDTH // LRU_BLOCKS
LRU_CONV = 4
LRU_C = 8.0
SC_WIDTH = 256
IN_WIDTH = 2048
D_FF = 2816
FF_CHUNK = 256
GRID_W = 64
ROPE_THETA = 10000.0
EPS = 1e-6

TM = 256
F32_SUBLANES = 8
BF16_SUBLANES = 16
N_MOD = 6
MOD_ROWS = 8

OFF_Q = 0
OFF_K = OFF_Q + ATTN_WIDTH
OFF_V = OFF_K + KV_WIDTH
OFF_R = OFF_V + KV_WIDTH
OFF_G = OFF_R + LRU_WIDTH
OFF_B = OFF_G + LRU_WIDTH
OFF_C = OFF_B + SC_WIDTH
OFF_U = OFF_C + SC_WIDTH


def _const_spec(shape):
    nd = len(shape)
    return pl.BlockSpec(shape, lambda *_: (0,) * nd, pipeline_mode=pl.Buffered(1))


def _params(vmem_mb):
    return pltpu.CompilerParams(dimension_semantics=("arbitrary", "arbitrary"),
                                vmem_limit_bytes=vmem_mb * 1024 * 1024)


def _rms_scale(t):
    return lax.rsqrt(jnp.mean(t * t, axis=-1, keepdims=True) + EPS)


def _mod_kernel(c_ref, w_ref, b_ref, o_ref):
    c = c_ref[...]
    h = (c * jax.nn.sigmoid(c)).astype(BF16)
    o_ref[...] = jnp.dot(h, w_ref[...], preferred_element_type=F32) + b_ref[...]


def _modulation(cc, w_mod, b_mod):
    rows = cc.shape[0]
    nblk = N_MOD
    return pl.pallas_call(
        _mod_kernel,
        grid=(nblk,),
        in_specs=[pl.BlockSpec((rows, D_MODEL), lambda j: (0, 0)),
                  pl.BlockSpec((D_MODEL, D_MODEL), lambda j: (0, j)),
                  pl.BlockSpec((1, D_MODEL), lambda j: (0, j))],
        out_specs=pl.BlockSpec((rows, D_MODEL), lambda j: (0, j)),
        out_shape=jax.ShapeDtypeStruct((rows, N_MOD * D_MODEL), F32),
        name="modulation",
    )(cc, w_mod, b_mod)


def _swap_half_heads(t):
    w = t.shape[-1]
    lane = lax.broadcasted_iota(jnp.int32, t.shape, 1)
    first = (lane % HEAD_DIM) < (HEAD_DIM // 2)
    return jnp.where(first, pltpu.roll(t, w - HEAD_DIM // 2, 1), pltpu.roll(t, HEAD_DIM // 2, 1))


def _head_rmsnorm(t, ones_blockdiag, g):
    t2 = t * t
    hi = t2.astype(BF16)
    lo = (t2 - hi.astype(F32)).astype(BF16)
    ssum = (jnp.dot(hi, ones_blockdiag, preferred_element_type=F32)
            + jnp.dot(lo, ones_blockdiag, preferred_element_type=F32))
    return t * lax.rsqrt(ssum * (1.0 / HEAD_DIM) + EPS) * g


def _inproj_kernel(x_ref, mod_ref, gmix_ref, w_ref, gq_ref, gk_ref, cos_ref, sin_ref, ones_ref,
                   qT_ref, k_ref, vT_ref, rest_ref):
    x = x_ref[0]
    mod = mod_ref[0, 0]
    shift, scale = mod[0:1], mod[1:2]
    h = x * _rms_scale(x) * gmix_ref[...]
    h = h * (1.0 + scale) + shift
    y = jnp.dot(h.astype(BF16), w_ref[...], preferred_element_type=F32)

    cos2 = cos_ref[...]
    sin2 = sin_ref[...]
    ones_q = ones_ref[...]

    q = _head_rmsnorm(y[:, OFF_Q:OFF_Q + ATTN_WIDTH], ones_q, gq_ref[...])
    cos_q = jnp.concatenate([cos2] * (ATTN_WIDTH // KV_WIDTH), axis=1)
    sin_q = jnp.concatenate([sin2] * (ATTN_WIDTH // KV_WIDTH), axis=1)
    q = (q * cos_q + _swap_half_heads(q) * sin_q) * (HEAD_DIM ** -0.5)
    qT_ref[0] = q.T.astype(BF16)

    k = _head_rmsnorm(y[:, OFF_K:OFF_K + KV_WIDTH], ones_q[:KV_WIDTH, :KV_WIDTH], gk_ref[...])
    k = k * cos2 + _swap_half_heads(k) * sin2
    k_ref[0] = k.astype(BF16)

    vT_ref[0] = y[:, OFF_V:OFF_V + KV_WIDTH].T.astype(BF16)

    rest_ref[0, :, 0:LRU_WIDTH] = y[:, OFF_R:OFF_R + LRU_WIDTH]
    rest_ref[0, :, LRU_WIDTH:2 * LRU_WIDTH] = jax.nn.gelu(y[:, OFF_G:OFF_G + LRU_WIDTH])
    rest_ref[0, :, 2 * LRU_WIDTH:2 * LRU_WIDTH + SC_WIDTH] = y[:, OFF_B:OFF_B + SC_WIDTH]
    rest_ref[0, :, 2 * LRU_WIDTH + SC_WIDTH:] = (y[:, OFF_C:OFF_C + SC_WIDTH]
                                                 * y[:, OFF_U:OFF_U + SC_WIDTH])


def _inproj(xcat, mods, g_mix, w_in, gq, gk, cos_t, sin_t, ones_bd):
    bsz, t_all, _ = xcat.shape
    nt = t_all // TM
    rest_w = 2 * LRU_WIDTH + 2 * SC_WIDTH
    return pl.pallas_call(
        _inproj_kernel,
        grid=(bsz, nt),
        in_specs=[pl.BlockSpec((1, TM, D_MODEL), lambda b, i: (b, i, 0)),
                  pl.BlockSpec((1, 1, MOD_ROWS, D_MODEL), lambda b, i: (b, jnp.minimum(i, 1), 0, 0)),
                  _const_spec((1, D_MODEL)),
                  _const_spec((D_MODEL, IN_WIDTH)),
                  _const_spec((1, ATTN_WIDTH)),
                  _const_spec((1, KV_WIDTH)),
                  pl.BlockSpec((TM, KV_WIDTH), lambda b, i: (i, 0)),
                  pl.BlockSpec((TM, KV_WIDTH), lambda b, i: (i, 0)),
                  _const_spec((ATTN_WIDTH, ATTN_WIDTH))],
        out_specs=[pl.BlockSpec((1, ATTN_WIDTH, TM), lambda b, i: (b, 0, i)),
                   pl.BlockSpec((1, TM, KV_WIDTH), lambda b, i: (b, i, 0)),
                   pl.BlockSpec((1, KV_WIDTH, TM), lambda b, i: (b, 0, i)),
                   pl.BlockSpec((1, TM, rest_w), lambda b, i: (b, i, 0))],
        out_shape=[jax.ShapeDtypeStruct((bsz, ATTN_WIDTH, t_all), BF16),
                   jax.ShapeDtypeStruct((bsz, t_all, KV_WIDTH), BF16),
                   jax.ShapeDtypeStruct((bsz, KV_WIDTH, t_all), BF16),
                   jax.ShapeDtypeStruct((bsz, t_all, rest_w), F32)],
        compiler_params=_params(40),
        name="inproj",
    )(xcat, mods, g_mix, w_in, gq, gk, cos_t, sin_t, ones_bd)


def _with_halo(prev_ref, cur_ref, next_ref, valid_prev, valid_next):
    prev = prev_ref[0] * valid_prev.astype(F32)
    nxt = next_ref[0] * valid_next.astype(F32)
    return jnp.concatenate([prev, cur_ref[0], nxt], axis=0)


def _shifted(tcat, offset):
    rows = tcat.shape[0]
    if offset == 0:
        return tcat[F32_SUBLANES:F32_SUBLANES + TM]
    return pltpu.roll(tcat, (-offset) % rows, 0)[F32_SUBLANES:F32_SUBLANES + TM]


def _lru_coeffs(rcat, conv_w, conv_b, w_gate, b_gate, sp_lam):
    xc = conv_b
    for j in range(LRU_CONV):
        xc = xc + conv_w[j:j + 1] * _shifted(rcat, j - 2)
    z = jnp.dot(xc.astype(BF16), w_gate, preferred_element_type=F32) + b_gate
    r_gate = jax.nn.sigmoid(z[:, :LRU_WIDTH])
    i_gate = jax.nn.sigmoid(z[:, LRU_WIDTH:])
    log_a = (-LRU_C) * r_gate * sp_lam
    a = jnp.exp(log_a)
    one_minus_a2 = -jnp.tanh(log_a) * (1.0 + a * a)
    return a, jnp.sqrt(one_minus_a2) * (i_gate * xc)


def _lru_kernel(rf_ref, rfp_ref, rfn_ref, rb_ref, rbp_ref, rbn_ref, bg_ref, cu_ref, cup_ref, cun_ref,
                cw_ref, cb_ref, wg_ref, bgate_ref, lam_ref, scw_ref,
                hf_ref, hb_ref, sc_ref,
                af_s, bf_s, ab_s, bb_s, hf_state, hb_state, *, nt):
    i = pl.program_id(1)
    j = jnp.where(i == 0, 0, nt - i)

    @pl.when(i == 0)
    def _():
        hf_state[...] = jnp.zeros_like(hf_state)
        hb_state[...] = jnp.zeros_like(hb_state)

    def halo_valid(t):
        return jnp.logical_and(t != 0, t != 1), jnp.logical_and(t != 0, t != nt - 1)

    lam = lam_ref[...]
    sp_lam = jnp.maximum(-lam, 0.0) + jnp.log1p(jnp.exp(-jnp.abs(lam)))
    cw = cw_ref[...]
    cb = cb_ref[...]

    vp, vn = halo_valid(i)
    a, b = _lru_coeffs(_with_halo(rfp_ref, rf_ref, rfn_ref, vp, vn), cw, cb,
                       wg_ref[:, :2 * LRU_WIDTH], bgate_ref[:, :2 * LRU_WIDTH], sp_lam[0:1])
    af_s[...] = a
    bf_s[...] = b
    cucat = _with_halo(cup_ref, cu_ref, cun_ref, vp, vn)
    scw = scw_ref[...]
    conv = scw[0:1] * _shifted(cucat, -1) + scw[1:2] * _shifted(cucat, 0) + scw[2:3] * _shifted(cucat, 1)
    sc_ref[0] = (bg_ref[0] * conv).astype(BF16)

    vp, vn = halo_valid(j)
    a, b = _lru_coeffs(_with_halo(rbp_ref, rb_ref, rbn_ref, vp, vn), cw, cb,
                       wg_ref[:, 2 * LRU_WIDTH:], bgate_ref[:, 2 * LRU_WIDTH:], sp_lam[1:2])
    ab_s[...] = a
    bb_s[...] = b

    def step(s, carry):
        hf, hb = carry
        hf = af_s[pl.ds(s, 1), :] * hf + bf_s[pl.ds(s, 1), :]
        hf_ref[0, pl.ds(s, 1), :] = hf
        sb = TM - 1 - s
        hb = ab_s[pl.ds(sb, 1), :] * hb + bb_s[pl.ds(sb, 1), :]
        hb_ref[0, pl.ds(sb, 1), :] = hb
        return hf, hb

    hf, hb = lax.fori_loop(0, TM, step, (hf_state[...], hb_state[...]), unroll=8)
    hf_state[...] = hf
    hb_state[...] = hb


def _lru(rest, conv_w, conv_b, w_gate, b_gate, lam, sc_w):
    bsz, t_all, _ = rest.shape
    nt = t_all // TM
    hpt = TM // F32_SUBLANES
    nhb = t_all // F32_SUBLANES

    def rev(i):
        return jnp.where(i == 0, 0, nt - i)

    def tile(col, order):
        return pl.BlockSpec((1, TM, LRU_WIDTH), lambda b, i: (b, order(i), col))

    def prev_halo(col, order):
        return pl.BlockSpec((1, F32_SUBLANES, LRU_WIDTH),
                            lambda b, i: (b, jnp.maximum(order(i) * hpt - 1, 0), col))

    def next_halo(col, order):
        return pl.BlockSpec((1, F32_SUBLANES, LRU_WIDTH),
                            lambda b, i: (b, jnp.minimum((order(i) + 1) * hpt, nhb - 1), col))

    fwd = lambda i: i
    out_f32 = jax.ShapeDtypeStruct((bsz, t_all, LRU_WIDTH), F32)
    return pl.pallas_call(
        functools.partial(_lru_kernel, nt=nt),
        grid=(bsz, nt),
        in_specs=[tile(0, fwd), prev_halo(0, fwd), next_halo(0, fwd),
                  tile(0, rev), prev_halo(0, rev), next_halo(0, rev),
                  tile(2, fwd),
                  tile(3, fwd), prev_halo(3, fwd), next_halo(3, fwd),
                  _const_spec((LRU_CONV, LRU_WIDTH)),
                  _const_spec((1, LRU_WIDTH)),
                  _const_spec((LRU_WIDTH, 4 * LRU_WIDTH)),
                  _const_spec((1, 4 * LRU_WIDTH)),
                  _const_spec((2, LRU_WIDTH)),
                  _const_spec((3, SC_WIDTH))],
        out_specs=[pl.BlockSpec((1, TM, LRU_WIDTH), lambda b, i: (b, i, 0)),
                   pl.BlockSpec((1, TM, LRU_WIDTH), lambda b, i: (b, rev(i), 0)),
                   pl.BlockSpec((1, TM, SC_WIDTH), lambda b, i: (b, i, 0))],
        out_shape=[out_f32, out_f32, jax.ShapeDtypeStruct((bsz, t_all, SC_WIDTH), BF16)],
        scratch_shapes=[pltpu.VMEM((TM, LRU_WIDTH), F32)] * 4 + [pltpu.VMEM((1, LRU_WIDTH), F32)] * 2,
        compiler_params=_params(32),
        name="lru_scan",
    )(rest, rest, rest, rest, rest, rest, rest, rest, rest, rest,
      conv_w, conv_b, w_gate, b_gate, lam, sc_w)


def _attn_kernel(qT_ref, k_ref, vT_ref, o_ref, s_ref, p_ref, oT_ref, *, tile_offset, n_key_tiles):
    i = pl.program_id(1) + tile_offset

    def run(n_tiles):
        for g in range(N_KV_HEADS):
            def head(hh, carry, g=g):
                h = g * HEADS_PER_KV + hh
                row0 = pl.multiple_of(h * HEAD_DIM, HEAD_DIM)
                qh = qT_ref[0, pl.ds(row0, HEAD_DIM), :]
                zero = jnp.zeros_like(qh)
                qpad = jnp.concatenate([qh, zero] if g == 0 else [zero, qh], axis=0)
                m8 = jnp.full((F32_SUBLANES, TM), -jnp.inf, F32)
                for c in range(n_tiles):
                    s = jnp.dot(k_ref[0, c * TM:(c + 1) * TM, :], qpad, preferred_element_type=F32)
                    s_ref[c * TM:(c + 1) * TM, :] = s
                    m8 = jnp.maximum(m8, s.reshape(TM // F32_SUBLANES, F32_SUBLANES, TM).max(axis=0))
                m = m8.max(axis=0, keepdims=True)
                l8 = jnp.zeros((F32_SUBLANES, TM), F32)
                acc = jnp.zeros((HEAD_DIM, TM), F32)
                for c in range(n_tiles):
                    p = jnp.exp(s_ref[c * TM:(c + 1) * TM, :] - m)
                    l8 = l8 + p.reshape(TM // F32_SUBLANES, F32_SUBLANES, TM).sum(axis=0)
                    vt = vT_ref[0, g * HEAD_DIM:(g + 1) * HEAD_DIM, c * TM:(c + 1) * TM]
                    acc = acc + jnp.dot(vt, p.astype(BF16), preferred_element_type=F32)
                l = l8.sum(axis=0, keepdims=True)
                oT_ref[pl.ds(row0, HEAD_DIM), :] = acc / l
                return carry
            lax.fori_loop(0, HEADS_PER_KV, head, 0)
        o_ref[0] = oT_ref[...].T.astype(BF16)

    @pl.when(i == 0)
    def _():
        run(1)

    @pl.when(i != 0)
    def _():
        run(n_key_tiles)


def _attention(qT, k, vT, tile_offset):
    bsz, t_all, _ = k.shape
    nt = t_all // TM
    nq = nt - tile_offset
    return pl.pallas_call(
        functools.partial(_attn_kernel, tile_offset=tile_offset, n_key_tiles=nt),
        grid=(bsz, nq),
        in_specs=[pl.BlockSpec((1, ATTN_WIDTH, TM), lambda b, i: (b, 0, i + tile_offset)),
                  pl.BlockSpec((1, t_all, KV_WIDTH), lambda b, i: (b, 0, 0)),
                  pl.BlockSpec((1, KV_WIDTH, t_all), lambda b, i: (b, 0, 0))],
        out_specs=pl.BlockSpec((1, TM, ATTN_WIDTH), lambda b, i: (b, i, 0)),
        out_shape=jax.ShapeDtypeStruct((bsz, nq * TM, ATTN_WIDTH), BF16),
        scratch_shapes=[pltpu.VMEM((t_all, TM), F32), pltpu.VMEM((t_all, TM), BF16),
                        pltpu.VMEM((ATTN_WIDTH, TM), F32)],
        compiler_params=_params(40),
        name="attention",
    )(qT, k, vT)


def _outproj_kernel(x_ref, att_ref, hf_ref, hb_ref, gg_ref, sc_ref, mod_ref, wo_ref, gffn_ref,
                    x1_ref, h2_ref):
    mod = mod_ref[0, 0]
    gate_mix, shift, scale = mod[2:3], mod[3:4], mod[4:5]
    lru = (gg_ref[0] * (hf_ref[0] + hb_ref[0])).astype(BF16)
    y = (jnp.dot(att_ref[0], wo_ref[0:ATTN_WIDTH], preferred_element_type=F32)
         + jnp.dot(lru, wo_ref[ATTN_WIDTH:ATTN_WIDTH + LRU_WIDTH], preferred_element_type=F32)
         + jnp.dot(sc_ref[0], wo_ref[ATTN_WIDTH + LRU_WIDTH:], preferred_element_type=F32))
    x1 = x_ref[0] + gate_mix * y
    x1_ref[0] = x1
    h2 = x1 * _rms_scale(x1) * gffn_ref[...]
    h2_ref[0] = (h2 * (1.0 + scale) + shift).astype(BF16)


def _outproj(xcat, att, hf, hb, rest, sc, mods, w_out, g_ffn, tile_offset):
    bsz, t_all, _ = xcat.shape
    nq = t_all // TM - tile_offset
    off = lambda b, i: (b, i + tile_offset, 0)
    return pl.pallas_call(
        _outproj_kernel,
        grid=(bsz, nq),
        in_specs=[pl.BlockSpec((1, TM, D_MODEL), off),
                  pl.BlockSpec((1, TM, ATTN_WIDTH), lambda b, i: (b, i, 0)),
                  pl.BlockSpec((1, TM, LRU_WIDTH), off),
                  pl.BlockSpec((1, TM, LRU_WIDTH), off),
                  pl.BlockSpec((1, TM, LRU_WIDTH), lambda b, i: (b, i + tile_offset, 1)),
                  pl.BlockSpec((1, TM, SC_WIDTH), off),
                  pl.BlockSpec((1, 1, MOD_ROWS, D_MODEL),
                               lambda b, i: (b, jnp.minimum(i + tile_offset, 1), 0, 0)),
                  _const_spec((D_MODEL, D_MODEL)),
                  _const_spec((1, D_MODEL))],
        out_specs=[pl.BlockSpec((1, TM, D_MODEL), lambda b, i: (b, i, 0)),
                   pl.BlockSpec((1, TM, D_MODEL), lambda b, i: (b, i, 0))],
        out_shape=[jax.ShapeDtypeStruct((bsz, nq * TM, D_MODEL), F32),
                   jax.ShapeDtypeStruct((bsz, nq * TM, D_MODEL), BF16)],
        compiler_params=_params(32),
        name="outproj",
    )(xcat, att, hf, hb, rest, sc, mods, w_out, g_ffn)


def _ffn_kernel(h_ref, hp_ref, hn_ref, x1_ref, mod_ref, wup_ref, cw_ref, wdn_ref, gfin_ref,
                o_ref, acc_ref, *, ctx_tiles, nt, final_norm):
    i = pl.program_id(1)
    valid_prev = jnp.logical_and(i != 0, i != ctx_tiles)
    valid_next = jnp.logical_and(i != nt - 1, i != ctx_tiles - 1)
    hp = hp_ref[0] * valid_prev.astype(BF16)
    hn = hn_ref[0] * valid_next.astype(BF16)
    hcat = jnp.concatenate([hp, h_ref[0], hn], axis=0)
    rows = hcat.shape[0]

    def conv(t, w):
        lo, hi = BF16_SUBLANES, BF16_SUBLANES + TM
        return (w[0:1] * pltpu.roll(t, 1, 0)[lo:hi] + w[1:2] * t[lo:hi]
                + w[2:3] * pltpu.roll(t, rows - 1, 0)[lo:hi])

    for c in range(D_FF // FF_CHUNK):
        cu = slice(c * FF_CHUNK, (c + 1) * FF_CHUNK)
        cg = slice(D_FF + c * FF_CHUNK, D_FF + (c + 1) * FF_CHUNK)
        up_u = jnp.dot(hcat, wup_ref[:, cu], preferred_element_type=F32)
        up_g = jnp.dot(hcat, wup_ref[:, cg], preferred_element_type=F32)
        u = conv(up_u, cw_ref[:, cu])
        g = conv(up_g, cw_ref[:, cg])
        act = ((g * jax.nn.sigmoid(g)) * u).astype(BF16)
        part = jnp.dot(act, wdn_ref[cu, :], preferred_element_type=F32)
        if c == 0:
            acc_ref[...] = part
        else:
            acc_ref[...] += part

    gate_ffn = mod_ref[0, 0][5:6]
    x2 = x1_ref[0] + gate_ffn * acc_ref[...]
    if final_norm:
        x2 = x2 * _rms_scale(x2) * gfin_ref[...]
    o_ref[0] = x2


def _ffn(h2, x1, mods, w_up, conv_w, w_down, g_final, ctx_tiles, final_norm):
    bsz, t_len, _ = h2.shape
    nt = t_len // TM
    hpt = TM // BF16_SUBLANES
    nhb = t_len // BF16_SUBLANES
    return pl.pallas_call(
        functools.partial(_ffn_kernel, ctx_tiles=ctx_tiles, nt=nt, final_norm=final_norm),
        grid=(bsz, nt),
        in_specs=[pl.BlockSpec((1, TM, D_MODEL), lambda b, i: (b, i, 0)),
                  pl.BlockSpec((1, BF16_SUBLANES, D_MODEL),
                               lambda b, i: (b, jnp.maximum(i * hpt - 1, 0), 0)),
                  pl.BlockSpec((1, BF16_SUBLANES, D_MODEL),
                               lambda b, i: (b, jnp.minimum((i + 1) * hpt, nhb - 1), 0)),
                  pl.BlockSpec((1, TM, D_MODEL), lambda b, i: (b, i, 0)),
                  pl.BlockSpec((1, 1, MOD_ROWS, D_MODEL),
                               lambda b, i: (b, jnp.minimum(i + 1 - ctx_tiles, 1), 0, 0)),
                  _const_spec((D_MODEL, 2 * D_FF)),
                  _const_spec((3, 2 * D_FF)),
                  _const_spec((D_FF, D_MODEL)),
                  _const_spec((1, D_MODEL))],
        out_specs=pl.BlockSpec((1, TM, D_MODEL), lambda b, i: (b, i, 0)),
        out_shape=jax.ShapeDtypeStruct((bsz, t_len, D_MODEL), F32),
        scratch_shapes=[pltpu.VMEM((TM, D_MODEL), F32)],
        compiler_params=_params(48),
        name="conv_ffn",
    )(h2, h2, h2, x1, mods, w_up, conv_w, w_down, g_final)


def _rope_tables(seq, ctx_len):
    rows = seq // GRID_W
    pos_r = np.repeat(np.arange(rows, dtype=np.float32), GRID_W)
    pos_c = np.tile(np.arange(GRID_W, dtype=np.float32), rows)
    n_f = HEAD_DIM // 4
    inv = (np.float32(ROPE_THETA) ** (-np.arange(n_f, dtype=np.float32) / n_f)).astype(np.float32)
    ang = np.concatenate([pos_r[:, None] * inv, pos_c[:, None] * inv], axis=-1).astype(np.float32)
    cos = np.cos(ang.astype(np.float64))
    sin = np.sin(ang.astype(np.float64))
    cos_h = np.concatenate([cos, cos], axis=-1)
    sin_h = np.concatenate([-sin, sin], axis=-1)
    cos_h = np.concatenate([np.ones((ctx_len, HEAD_DIM)), cos_h], axis=0)
    sin_h = np.concatenate([np.zeros((ctx_len, HEAD_DIM)), sin_h], axis=0)
    two = KV_WIDTH // HEAD_DIM
    return (jnp.asarray(np.tile(cos_h, (1, two)), F32), jnp.asarray(np.tile(sin_h, (1, two)), F32))


def _ones_blockdiag():
    idx = np.arange(ATTN_WIDTH) // HEAD_DIM
    return jnp.asarray(idx[:, None] == idx[None, :], BF16)


def _gate_weights(wa, ba, wi, bi):
    def dense(w):
        eye = jnp.eye(LRU_BLOCKS, dtype=w.dtype)
        return jnp.einsum("nde,nm->ndme", w, eye).reshape(LRU_WIDTH, LRU_WIDTH)
    w = jnp.concatenate([dense(wa[0]), dense(wi[0]), dense(wa[1]), dense(wi[1])], axis=1)
    b = jnp.concatenate([ba[0], bi[0], ba[1], bi[1]])[None, :]
    return w.astype(BF16), b


def kernel(x, c, ctx, c_ctx, w_mod, b_mod, g_mix, g_ffn, w_in, g_q, g_k, lru_conv_w, lru_conv_b,
           lru_wa, lru_ba, lru_wi, lru_bi, lru_lam, sc_conv_w, w_out, w_up, ffn_conv_w, w_down,
           g_final):
    bsz, seq, _ = x.shape
    ctx_len = ctx.shape[1]
    depth = w_mod.shape[0]
    assert ctx_len == TM and seq % TM == 0 and seq % GRID_W == 0

    cos_t, sin_t = _rope_tables(seq, ctx_len)
    ones_bd = _ones_blockdiag()
    xcat = jnp.concatenate([ctx, x], axis=1)
    cc_rows = -(-(bsz + 1) // F32_SUBLANES) * F32_SUBLANES
    cc = jnp.zeros((cc_rows, D_MODEL), F32).at[:bsz].set(c).at[bsz].set(c_ctx)

    out = None
    for l in range(depth):
        last = l == depth - 1
        mod_all = _modulation(cc, w_mod[l].astype(BF16), b_mod[l][None, :])
        mod_all = mod_all.reshape(cc_rows, N_MOD, D_MODEL)
        mod_x = mod_all[:bsz]
        mod_c = jnp.broadcast_to(mod_all[bsz][None], (bsz, N_MOD, D_MODEL))
        mods = jnp.stack([mod_c, mod_x], axis=1)
        mods = jnp.pad(mods, ((0, 0), (0, 0), (0, MOD_ROWS - N_MOD), (0, 0)))

        gq = jnp.tile(g_q[l], N_HEADS)[None, :]
        gk = jnp.tile(g_k[l], N_KV_HEADS)[None, :]
        qT, k, vT, rest = _inproj(xcat, mods, g_mix[l][None, :], w_in[l].astype(BF16), gq, gk,
                                  cos_t, sin_t, ones_bd)

        w_gate, b_gate = _gate_weights(lru_wa[l], lru_ba[l], lru_wi[l], lru_bi[l])
        hf, hb, sc = _lru(rest, lru_conv_w[l], lru_conv_b[l][None, :], w_gate, b_gate, lru_lam[l],
                          sc_conv_w[l])

        tile_offset = 1 if last else 0
        att = _attention(qT, k, vT, tile_offset)
        x1, h2 = _outproj(xcat, att, hf, hb, rest, sc, mods, w_out[l].astype(BF16),
                          g_ffn[l][None, :], tile_offset)
        out = _ffn(h2, x1, mods, w_up[l].astype(BF16), ffn_conv_w[l], w_down[l].astype(BF16),
                   g_final[None, :], ctx_tiles=1 - tile_offset, final_norm=last)
        xcat = out
    return out
```

```python
import functools

import numpy as np
import jax
import jax.numpy as jnp
from jax import lax
from jax.experimental import pallas as pl
from jax.experimental.pallas import tpu as pltpu

F32 = jnp.float32
BF16 = jnp.bfloat16

D_MODEL = 1024
HEAD_DIM = 64
N_HEADS = 8
N_KV_HEADS = 2
HEADS_PER_KV = N_HEADS // N_KV_HEADS
ATTN_WIDTH = N_HEADS * HEAD_DIM
KV_WIDTH = N_KV_HEADS * HEAD_DIM
LRU_WIDTH = 256
LRU_BLOCKS = 4
LRU_BLOCK = LRU_WIDTH // LRU_BLOCKS
LRU_CONV = 4
LRU_C = 8.0
SC_WIDTH = 256
IN_WIDTH = 2048
D_FF = 2816
FF_CHUNK = 256
GRID_W = 64
ROPE_THETA = 10000.0
EPS = 1e-6
Q_SCALE = HEAD_DIM ** -0.5 * 1.4426950408889634

TM = 256
SCORE_LOOKAHEAD = 3
SCORE_SLOTS = SCORE_LOOKAHEAD + 1
UP_LOOKAHEAD = 2
UP_SLOTS = UP_LOOKAHEAD + 1
LANES = 128
F32_SUBLANES = 8
BF16_SUBLANES = 16
N_MOD = 6
MOD_ROWS = 8

OFF_Q = 0
OFF_K = OFF_Q + ATTN_WIDTH
OFF_V = OFF_K + KV_WIDTH
OFF_R = OFF_V + KV_WIDTH
OFF_G = OFF_R + LRU_WIDTH
OFF_B = OFF_G + LRU_WIDTH
OFF_C = OFF_B + SC_WIDTH
OFF_U = OFF_C + SC_WIDTH


def _const_spec(shape):
    nd = len(shape)
    return pl.BlockSpec(shape, lambda *_: (0,) * nd, pipeline_mode=pl.Buffered(1))


def _params(vmem_mb):
    return pltpu.CompilerParams(dimension_semantics=("arbitrary", "arbitrary"),
                                vmem_limit_bytes=vmem_mb * 1024 * 1024)


def _rms_scale(t):
    return lax.rsqrt(jnp.mean(t * t, axis=-1, keepdims=True) + EPS)


def _mod_kernel(c_ref, w_ref, b_ref, o_ref):
    c = c_ref[...]
    h = (c * jax.nn.sigmoid(c)).astype(BF16)
    o_ref[...] = jnp.dot(h, w_ref[...], preferred_element_type=F32) + b_ref[...]


def _modulation(cc, w_mod, b_mod):
    rows = cc.shape[0]
    nblk = N_MOD
    return pl.pallas_call(
        _mod_kernel,
        grid=(nblk,),
        in_specs=[pl.BlockSpec((rows, D_MODEL), lambda j: (0, 0)),
                  pl.BlockSpec((D_MODEL, D_MODEL), lambda j: (0, j)),
                  pl.BlockSpec((1, D_MODEL), lambda j: (0, j))],
        out_specs=pl.BlockSpec((rows, D_MODEL), lambda j: (0, j)),
        out_shape=jax.ShapeDtypeStruct((rows, N_MOD * D_MODEL), F32),
        name="modulation",
    )(cc, w_mod, b_mod)


def _swap_half_heads(t):
    w = t.shape[-1]
    lane = lax.broadcasted_iota(jnp.int32, t.shape, 1)
    first = (lane % HEAD_DIM) < (HEAD_DIM // 2)
    return jnp.where(first, pltpu.roll(t, w - HEAD_DIM // 2, 1), pltpu.roll(t, HEAD_DIM // 2, 1))


def _head_rmsnorm(t, ones_blockdiag, g):
    t2 = t * t
    hi = t2.astype(BF16)
    lo = (t2 - hi.astype(F32)).astype(BF16)
    ssum = (jnp.dot(hi, ones_blockdiag, preferred_element_type=F32)
            + jnp.dot(lo, ones_blockdiag, preferred_element_type=F32))
    return t * lax.rsqrt(ssum * (1.0 / HEAD_DIM) + EPS) * g


def _inproj_kernel(x_ref, mod_ref, gmix_ref, w_ref, gq_ref, gk_ref, cos_ref, sin_ref, ones_ref,
                   qT_ref, k_ref, vT_ref, rest_ref):
    x = x_ref[0]
    mod = mod_ref[0, 0]
    shift, scale = mod[0:1], mod[1:2]
    h = x * _rms_scale(x) * gmix_ref[...]
    h = h * (1.0 + scale) + shift
    y = jnp.dot(h.astype(BF16), w_ref[...], preferred_element_type=F32)

    cos2 = cos_ref[...]
    sin2 = sin_ref[...]
    ones_q = ones_ref[...]

    q = _head_rmsnorm(y[:, OFF_Q:OFF_Q + ATTN_WIDTH], ones_q, gq_ref[...])
    cos_q = jnp.concatenate([cos2] * (ATTN_WIDTH // KV_WIDTH), axis=1)
    sin_q = jnp.concatenate([sin2] * (ATTN_WIDTH // KV_WIDTH), axis=1)
    q = (q * cos_q + _swap_half_heads(q) * sin_q) * Q_SCALE
    qT_ref[0] = q.T.astype(BF16)

    k = _head_rmsnorm(y[:, OFF_K:OFF_K + KV_WIDTH], ones_q[:KV_WIDTH, :KV_WIDTH], gk_ref[...])
    k = k * cos2 + _swap_half_heads(k) * sin2
    k_ref[0] = k.astype(BF16)

    vT_ref[0] = y[:, OFF_V:OFF_V + KV_WIDTH].T.astype(BF16)

    rest_ref[0, :, 0:LRU_WIDTH] = y[:, OFF_R:OFF_R + LRU_WIDTH]
    rest_ref[0, :, LRU_WIDTH:2 * LRU_WIDTH] = jax.nn.gelu(y[:, OFF_G:OFF_G + LRU_WIDTH])
    rest_ref[0, :, 2 * LRU_WIDTH:2 * LRU_WIDTH + SC_WIDTH] = y[:, OFF_B:OFF_B + SC_WIDTH]
    rest_ref[0, :, 2 * LRU_WIDTH + SC_WIDTH:] = (y[:, OFF_C:OFF_C + SC_WIDTH]
                                                 * y[:, OFF_U:OFF_U + SC_WIDTH])


def _inproj(xcat, mods, g_mix, w_in, gq, gk, cos_t, sin_t, ones_bd):
    bsz, t_all, _ = xcat.shape
    nt = t_all // TM
    rest_w = 2 * LRU_WIDTH + 2 * SC_WIDTH
    return pl.pallas_call(
        _inproj_kernel,
        grid=(bsz, nt),
        in_specs=[pl.BlockSpec((1, TM, D_MODEL), lambda b, i: (b, i, 0)),
                  pl.BlockSpec((1, 1, MOD_ROWS, D_MODEL), lambda b, i: (b, jnp.minimum(i, 1), 0, 0)),
                  _const_spec((1, D_MODEL)),
                  _const_spec((D_MODEL, IN_WIDTH)),
                  _const_spec((1, ATTN_WIDTH)),
                  _const_spec((1, KV_WIDTH)),
                  pl.BlockSpec((TM, KV_WIDTH), lambda b, i: (i, 0)),
                  pl.BlockSpec((TM, KV_WIDTH), lambda b, i: (i, 0)),
                  _const_spec((ATTN_WIDTH, ATTN_WIDTH))],
        out_specs=[pl.BlockSpec((1, ATTN_WIDTH, TM), lambda b, i: (b, 0, i)),
                   pl.BlockSpec((1, TM, KV_WIDTH), lambda b, i: (b, i, 0)),
                   pl.BlockSpec((1, KV_WIDTH, TM), lambda b, i: (b, 0, i)),
                   pl.BlockSpec((1, TM, rest_w), lambda b, i: (b, i, 0))],
        out_shape=[jax.ShapeDtypeStruct((bsz, ATTN_WIDTH, t_all), BF16),
                   jax.ShapeDtypeStruct((bsz, t_all, KV_WIDTH), BF16),
                   jax.ShapeDtypeStruct((bsz, KV_WIDTH, t_all), BF16),
                   jax.ShapeDtypeStruct((bsz, t_all, rest_w), F32)],
        compiler_params=_params(40),
        name="inproj",
    )(xcat, mods, g_mix, w_in, gq, gk, cos_t, sin_t, ones_bd)


def _with_halo(prev_ref, cur_ref, next_ref, valid_prev, valid_next):
    prev = prev_ref[0] * valid_prev.astype(F32)
    nxt = next_ref[0] * valid_next.astype(F32)
    return jnp.concatenate([prev, cur_ref[0], nxt], axis=0)


def _shifted(tcat, offset):
    rows = tcat.shape[0]
    if offset == 0:
        return tcat[F32_SUBLANES:F32_SUBLANES + TM]
    return pltpu.roll(tcat, (-offset) % rows, 0)[F32_SUBLANES:F32_SUBLANES + TM]


def _lru_coeffs(rcat, conv_w, conv_b, w_gate, b_gate, sp_lam):
    xc = conv_b
    for j in range(LRU_CONV):
        xc = xc + conv_w[j:j + 1] * _shifted(rcat, j - 2)
    z = jnp.dot(xc.astype(BF16), w_gate, preferred_element_type=F32) + b_gate
    r_gate = jax.nn.sigmoid(z[:, :LRU_WIDTH])
    i_gate = jax.nn.sigmoid(z[:, LRU_WIDTH:])
    log_a = (-LRU_C) * r_gate * sp_lam
    a = jnp.exp(log_a)
    one_minus_a2 = -jnp.tanh(log_a) * (1.0 + a * a)
    return a, jnp.sqrt(one_minus_a2) * (i_gate * xc)


def _lru_kernel(rf_ref, rfp_ref, rfn_ref, rb_ref, rbp_ref, rbn_ref, bg_ref, cu_ref, cup_ref, cun_ref,
                cw_ref, cb_ref, wg_ref, bgate_ref, lam_ref, scw_ref,
                hf_ref, hb_ref, sc_ref,
                af_s, bf_s, ab_s, bb_s, hf_state, hb_state, *, nt):
    i = pl.program_id(1)
    j = jnp.where(i == 0, 0, nt - i)

    @pl.when(i == 0)
    def _():
        hf_state[...] = jnp.zeros_like(hf_state)
        hb_state[...] = jnp.zeros_like(hb_state)

    def halo_valid(t):
        return jnp.logical_and(t != 0, t != 1), jnp.logical_and(t != 0, t != nt - 1)

    lam = lam_ref[...]
    sp_lam = jnp.maximum(-lam, 0.0) + jnp.log1p(jnp.exp(-jnp.abs(lam)))
    cw = cw_ref[...]
    cb = cb_ref[...]

    vp, vn = halo_valid(i)
    a, b = _lru_coeffs(_with_halo(rfp_ref, rf_ref, rfn_ref, vp, vn), cw, cb,
                       wg_ref[:, :2 * LRU_WIDTH], bgate_ref[:, :2 * LRU_WIDTH], sp_lam[0:1])
    _block_prefix(a, b, af_s, bf_s, reverse=False)
    cucat = _with_halo(cup_ref, cu_ref, cun_ref, vp, vn)
    scw = scw_ref[...]
    conv = scw[0:1] * _shifted(cucat, -1) + scw[1:2] * _shifted(cucat, 0) + scw[2:3] * _shifted(cucat, 1)
    sc_ref[0] = (bg_ref[0] * conv).astype(BF16)

    vp, vn = halo_valid(j)
    a, b = _lru_coeffs(_with_halo(rbp_ref, rb_ref, rbn_ref, vp, vn), cw, cb,
                       wg_ref[:, 2 * LRU_WIDTH:], bgate_ref[:, 2 * LRU_WIDTH:], sp_lam[1:2])
    _block_prefix(a, b, ab_s, bb_s, reverse=True)

    _carry_blocks(af_s, bf_s, hf_ref, hf_state, reverse=False)
    _carry_blocks(ab_s, bb_s, hb_ref, hb_state, reverse=True)


def _block_prefix(a, b, a_s, b_s, reverse):
    n = F32_SUBLANES
    a = a.reshape(TM // n, n, LRU_WIDTH)
    b = b.reshape(TM // n, n, LRU_WIDTH)
    row = lax.broadcasted_iota(jnp.int32, a.shape, 1)
    d = 1
    while d < n:
        shift, keep = (n - d, row < n - d) if reverse else (d, row >= d)
        a_prev = jnp.where(keep, pltpu.roll(a, shift, 1), 1.0)
        b_prev = jnp.where(keep, pltpu.roll(b, shift, 1), 0.0)
        b = a * b_prev + b
        a = a * a_prev
        d *= 2
    a_s[...] = a
    b_s[...] = b


def _carry_blocks(a_s, b_s, out_ref, state_ref, reverse):
    n = F32_SUBLANES
    nb = TM // n
    last = 0 if reverse else n - 1
    h_in = state_ref[...]
    for v in (reversed(range(nb)) if reverse else range(nb)):
        h = a_s[v] * h_in + b_s[v]
        out_ref[0, v * n:(v + 1) * n, :] = h
        h_in = h[last:last + 1]
    state_ref[...] = h_in


def _lru(rest, conv_w, conv_b, w_gate, b_gate, lam, sc_w):
    bsz, t_all, _ = rest.shape
    nt = t_all // TM
    hpt = TM // F32_SUBLANES
    nhb = t_all // F32_SUBLANES

    def rev(i):
        return jnp.where(i == 0, 0, nt - i)

    def tile(col, order):
        return pl.BlockSpec((1, TM, LRU_WIDTH), lambda b, i: (b, order(i), col))

    def prev_halo(col, order):
        return pl.BlockSpec((1, F32_SUBLANES, LRU_WIDTH),
                            lambda b, i: (b, jnp.maximum(order(i) * hpt - 1, 0), col))

    def next_halo(col, order):
        return pl.BlockSpec((1, F32_SUBLANES, LRU_WIDTH),
                            lambda b, i: (b, jnp.minimum((order(i) + 1) * hpt, nhb - 1), col))

    fwd = lambda i: i
    out_f32 = jax.ShapeDtypeStruct((bsz, t_all, LRU_WIDTH), F32)
    return pl.pallas_call(
        functools.partial(_lru_kernel, nt=nt),
        grid=(bsz, nt),
        in_specs=[tile(0, fwd), prev_halo(0, fwd), next_halo(0, fwd),
                  tile(0, rev), prev_halo(0, rev), next_halo(0, rev),
                  tile(2, fwd),
                  tile(3, fwd), prev_halo(3, fwd), next_halo(3, fwd),
                  _const_spec((LRU_CONV, LRU_WIDTH)),
                  _const_spec((1, LRU_WIDTH)),
                  _const_spec((LRU_WIDTH, 4 * LRU_WIDTH)),
                  _const_spec((1, 4 * LRU_WIDTH)),
                  _const_spec((2, LRU_WIDTH)),
                  _const_spec((3, SC_WIDTH))],
        out_specs=[pl.BlockSpec((1, TM, LRU_WIDTH), lambda b, i: (b, i, 0)),
                   pl.BlockSpec((1, TM, LRU_WIDTH), lambda b, i: (b, rev(i), 0)),
                   pl.BlockSpec((1, TM, SC_WIDTH), lambda b, i: (b, i, 0))],
        out_shape=[out_f32, out_f32, jax.ShapeDtypeStruct((bsz, t_all, SC_WIDTH), BF16)],
        scratch_shapes=([pltpu.VMEM((TM // F32_SUBLANES, F32_SUBLANES, LRU_WIDTH), F32)] * 4
                        + [pltpu.VMEM((1, LRU_WIDTH), F32)] * 2),
        compiler_params=_params(32),
        name="lru_scan",
    )(rest, rest, rest, rest, rest, rest, rest, rest, rest, rest,
      conv_w, conv_b, w_gate, b_gate, lam, sc_w)


def _attn_kernel(qT_ref, k_ref, vT_ref, o_ref, oT_ref, s_ref, *, tile_offset, n_key_tiles):
    i = pl.program_id(1) + tile_offset
    ones_rows = jnp.ones((BF16_SUBLANES, TM), BF16)

    def run(n_tiles):
        for g in range(N_KV_HEADS):
            def head_pair(pp, carry, g=g):
                row0 = pl.multiple_of((g * HEADS_PER_KV + 2 * pp) * HEAD_DIM, 2 * HEAD_DIM)
                q2 = jnp.concatenate([qT_ref[0, pl.ds(row0, HEAD_DIM), :],
                                      qT_ref[0, pl.ds(row0 + HEAD_DIM, HEAD_DIM), :]], axis=1)
                zero = jnp.zeros_like(q2)
                qpad = jnp.concatenate([q2, zero] if g == 0 else [zero, q2], axis=0)
                def scores(c):
                    s = jnp.dot(k_ref[0, c * TM:(c + 1) * TM, :], qpad, preferred_element_type=F32)
                    s_ref[c % SCORE_SLOTS] = s
                    return jnp.max(s, axis=0, keepdims=True)

                m = jnp.full((1, 2 * TM), -jnp.inf, F32)
                acc = jnp.zeros((HEAD_DIM + BF16_SUBLANES, 2 * TM), F32)
                tile_max = [scores(c) for c in range(min(SCORE_LOOKAHEAD, n_tiles))]
                for c in range(n_tiles):
                    if c + SCORE_LOOKAHEAD < n_tiles:
                        tile_max.append(scores(c + SCORE_LOOKAHEAD))
                    m_new = jnp.maximum(m, tile_max[c])
                    alpha = jnp.exp2(m - m_new)
                    p = jnp.exp2(s_ref[c % SCORE_SLOTS] - m_new).astype(BF16)
                    vt = jnp.concatenate([vT_ref[0, g * HEAD_DIM:(g + 1) * HEAD_DIM, c * TM:(c + 1) * TM],
                                          ones_rows], axis=0)
                    acc = alpha * acc + jnp.dot(vt, p, preferred_element_type=F32)
                    m = m_new
                o = acc[:HEAD_DIM] / acc[HEAD_DIM:HEAD_DIM + 1]
                oT_ref[pl.ds(row0, HEAD_DIM), :] = o[:, :TM]
                oT_ref[pl.ds(row0 + HEAD_DIM, HEAD_DIM), :] = o[:, TM:]
                return carry
            lax.fori_loop(0, HEADS_PER_KV // 2, head_pair, 0)
        o_ref[0] = oT_ref[...].T.astype(BF16)

    @pl.when(i == 0)
    def _():
        run(1)

    @pl.when(i != 0)
    def _():
        run(n_key_tiles)


def _attention(qT, k, vT, tile_offset):
    bsz, t_all, _ = k.shape
    nt = t_all // TM
    nq = nt - tile_offset
    return pl.pallas_call(
        functools.partial(_attn_kernel, tile_offset=tile_offset, n_key_tiles=nt),
        grid=(bsz, nq),
        in_specs=[pl.BlockSpec((1, ATTN_WIDTH, TM), lambda b, i: (b, 0, i + tile_offset)),
                  pl.BlockSpec((1, t_all, KV_WIDTH), lambda b, i: (b, 0, 0)),
                  pl.BlockSpec((1, KV_WIDTH, t_all), lambda b, i: (b, 0, 0))],
        out_specs=pl.BlockSpec((1, TM, ATTN_WIDTH), lambda b, i: (b, i, 0)),
        out_shape=jax.ShapeDtypeStruct((bsz, nq * TM, ATTN_WIDTH), BF16),
        scratch_shapes=[pltpu.VMEM((ATTN_WIDTH, TM), F32), pltpu.VMEM((SCORE_SLOTS, TM, 2 * TM), F32)],
        compiler_params=_params(40),
        name="attention",
    )(qT, k, vT)


def _outproj_kernel(x_ref, att_ref, hf_ref, hb_ref, gg_ref, sc_ref, mod_ref, wo_ref, gffn_ref,
                    x1_ref, h2_ref):
    mod = mod_ref[0, 0]
    gate_mix, shift, scale = mod[2:3], mod[3:4], mod[4:5]
    lru = (gg_ref[0] * (hf_ref[0] + hb_ref[0])).astype(BF16)
    y = (jnp.dot(att_ref[0], wo_ref[0:ATTN_WIDTH], preferred_element_type=F32)
         + jnp.dot(lru, wo_ref[ATTN_WIDTH:ATTN_WIDTH + LRU_WIDTH], preferred_element_type=F32)
         + jnp.dot(sc_ref[0], wo_ref[ATTN_WIDTH + LRU_WIDTH:], preferred_element_type=F32))
    x1 = x_ref[0] + gate_mix * y
    x1_ref[0] = x1
    h2 = x1 * _rms_scale(x1) * gffn_ref[...]
    h2_ref[0] = (h2 * (1.0 + scale) + shift).astype(BF16)


def _outproj(xcat, att, hf, hb, rest, sc, mods, w_out, g_ffn, tile_offset):
    bsz, t_all, _ = xcat.shape
    nq = t_all // TM - tile_offset
    off = lambda b, i: (b, i + tile_offset, 0)
    return pl.pallas_call(
        _outproj_kernel,
        grid=(bsz, nq),
        in_specs=[pl.BlockSpec((1, TM, D_MODEL), off),
                  pl.BlockSpec((1, TM, ATTN_WIDTH), lambda b, i: (b, i, 0)),
                  pl.BlockSpec((1, TM, LRU_WIDTH), off),
                  pl.BlockSpec((1, TM, LRU_WIDTH), off),
                  pl.BlockSpec((1, TM, LRU_WIDTH), lambda b, i: (b, i + tile_offset, 1)),
                  pl.BlockSpec((1, TM, SC_WIDTH), off),
                  pl.BlockSpec((1, 1, MOD_ROWS, D_MODEL),
                               lambda b, i: (b, jnp.minimum(i + tile_offset, 1), 0, 0)),
                  _const_spec((D_MODEL, D_MODEL)),
                  _const_spec((1, D_MODEL))],
        out_specs=[pl.BlockSpec((1, TM, D_MODEL), lambda b, i: (b, i, 0)),
                   pl.BlockSpec((1, TM, D_MODEL), lambda b, i: (b, i, 0))],
        out_shape=[jax.ShapeDtypeStruct((bsz, nq * TM, D_MODEL), F32),
                   jax.ShapeDtypeStruct((bsz, nq * TM, D_MODEL), BF16)],
        compiler_params=_params(32),
        name="outproj",
    )(xcat, att, hf, hb, rest, sc, mods, w_out, g_ffn)


def _ffn_kernel(h_ref, hp_ref, hn_ref, x1_ref, mod_ref, wup_ref, cw_ref, wdn_ref, gfin_ref,
                o_ref, acc_ref, up_ref, *, ctx_tiles, nt, final_norm):
    i = pl.program_id(1)
    valid_prev = jnp.logical_and(i != 0, i != ctx_tiles)
    valid_next = jnp.logical_and(i != nt - 1, i != ctx_tiles - 1)
    hp = hp_ref[0] * valid_prev.astype(BF16)
    hn = hn_ref[0] * valid_next.astype(BF16)
    hcat = jnp.concatenate([hp, h_ref[0], hn], axis=0)
    rows = hcat.shape[0]

    def conv(t, w):
        lo, hi = BF16_SUBLANES, BF16_SUBLANES + TM
        return (w[0:1] * pltpu.roll(t, 1, 0)[lo:hi] + w[1:2] * t[lo:hi]
                + w[2:3] * pltpu.roll(t, rows - 1, 0)[lo:hi])

    n_chunks = D_FF // FF_CHUNK

    def up_project(c):
        cu = slice(c * FF_CHUNK, (c + 1) * FF_CHUNK)
        cg = slice(D_FF + c * FF_CHUNK, D_FF + (c + 1) * FF_CHUNK)
        up_ref[c % UP_SLOTS, :, :FF_CHUNK] = jnp.dot(hcat, wup_ref[:, cu], preferred_element_type=F32)
        up_ref[c % UP_SLOTS, :, FF_CHUNK:] = jnp.dot(hcat, wup_ref[:, cg], preferred_element_type=F32)

    for c in range(min(UP_LOOKAHEAD, n_chunks)):
        up_project(c)
    for c in range(n_chunks):
        if c + UP_LOOKAHEAD < n_chunks:
            up_project(c + UP_LOOKAHEAD)
        cu = slice(c * FF_CHUNK, (c + 1) * FF_CHUNK)
        cg = slice(D_FF + c * FF_CHUNK, D_FF + (c + 1) * FF_CHUNK)
        u = conv(up_ref[c % UP_SLOTS, :, :FF_CHUNK], cw_ref[:, cu])
        g = conv(up_ref[c % UP_SLOTS, :, FF_CHUNK:], cw_ref[:, cg])
        act = ((g * jax.nn.sigmoid(g)) * u).astype(BF16)
        part = jnp.dot(act, wdn_ref[cu, :], preferred_element_type=F32)
        if c == 0:
            acc_ref[...] = part
        else:
            acc_ref[...] += part

    gate_ffn = mod_ref[0, 0][5:6]
    x2 = x1_ref[0] + gate_ffn * acc_ref[...]
    if final_norm:
        x2 = x2 * _rms_scale(x2) * gfin_ref[...]
    o_ref[0] = x2


def _ffn(h2, x1, mods, w_up, conv_w, w_down, g_final, ctx_tiles, final_norm):
    bsz, t_len, _ = h2.shape
    nt = t_len // TM
    hpt = TM // BF16_SUBLANES
    nhb = t_len // BF16_SUBLANES
    return pl.pallas_call(
        functools.partial(_ffn_kernel, ctx_tiles=ctx_tiles, nt=nt, final_norm=final_norm),
        grid=(bsz, nt),
        in_specs=[pl.BlockSpec((1, TM, D_MODEL), lambda b, i: (b, i, 0)),
                  pl.BlockSpec((1, BF16_SUBLANES, D_MODEL),
                               lambda b, i: (b, jnp.maximum(i * hpt - 1, 0), 0)),
                  pl.BlockSpec((1, BF16_SUBLANES, D_MODEL),
                               lambda b, i: (b, jnp.minimum((i + 1) * hpt, nhb - 1), 0)),
                  pl.BlockSpec((1, TM, D_MODEL), lambda b, i: (b, i, 0)),
                  pl.BlockSpec((1, 1, MOD_ROWS, D_MODEL),
                               lambda b, i: (b, jnp.minimum(i + 1 - ctx_tiles, 1), 0, 0)),
                  _const_spec((D_MODEL, 2 * D_FF)),
                  _const_spec((3, 2 * D_FF)),
                  _const_spec((D_FF, D_MODEL)),
                  _const_spec((1, D_MODEL))],
        out_specs=pl.BlockSpec((1, TM, D_MODEL), lambda b, i: (b, i, 0)),
        out_shape=jax.ShapeDtypeStruct((bsz, t_len, D_MODEL), F32),
        scratch_shapes=[pltpu.VMEM((TM, D_MODEL), F32),
                        pltpu.VMEM((UP_SLOTS, TM + 2 * BF16_SUBLANES, 2 * FF_CHUNK), F32)],
        compiler_params=_params(48),
        name="conv_ffn",
    )(h2, h2, h2, x1, mods, w_up, conv_w, w_down, g_final)


def _rope_tables(seq, ctx_len):
    rows = seq // GRID_W
    pos_r = np.repeat(np.arange(rows, dtype=np.float32), GRID_W)
    pos_c = np.tile(np.arange(GRID_W, dtype=np.float32), rows)
    n_f = HEAD_DIM // 4
    inv = (np.float32(ROPE_THETA) ** (-np.arange(n_f, dtype=np.float32) / n_f)).astype(np.float32)
    ang = np.concatenate([pos_r[:, None] * inv, pos_c[:, None] * inv], axis=-1).astype(np.float32)
    cos = np.cos(ang.astype(np.float64))
    sin = np.sin(ang.astype(np.float64))
    cos_h = np.concatenate([cos, cos], axis=-1)
    sin_h = np.concatenate([-sin, sin], axis=-1)
    cos_h = np.concatenate([np.ones((ctx_len, HEAD_DIM)), cos_h], axis=0)
    sin_h = np.concatenate([np.zeros((ctx_len, HEAD_DIM)), sin_h], axis=0)
    two = KV_WIDTH // HEAD_DIM
    return (jnp.asarray(np.tile(cos_h, (1, two)), F32), jnp.asarray(np.tile(sin_h, (1, two)), F32))


def _ones_blockdiag():
    idx = np.arange(ATTN_WIDTH) // HEAD_DIM
    return jnp.asarray(idx[:, None] == idx[None, :], BF16)


def _gate_weights(wa, ba, wi, bi):
    def dense(w):
        eye = jnp.eye(LRU_BLOCKS, dtype=w.dtype)
        return jnp.einsum("nde,nm->ndme", w, eye).reshape(LRU_WIDTH, LRU_WIDTH)
    w = jnp.concatenate([dense(wa[0]), dense(wi[0]), dense(wa[1]), dense(wi[1])], axis=1)
    b = jnp.concatenate([ba[0], bi[0], ba[1], bi[1]])[None, :]
    return w.astype(BF16), b


def kernel(x, c, ctx, c_ctx, w_mod, b_mod, g_mix, g_ffn, w_in, g_q, g_k, lru_conv_w, lru_conv_b,
           lru_wa, lru_ba, lru_wi, lru_bi, lru_lam, sc_conv_w, w_out, w_up, ffn_conv_w, w_down,
           g_final):
    bsz, seq, _ = x.shape
    ctx_len = ctx.shape[1]
    depth = w_mod.shape[0]
    assert ctx_len == TM and seq % TM == 0 and seq % GRID_W == 0

    cos_t, sin_t = _rope_tables(seq, ctx_len)
    ones_bd = _ones_blockdiag()
    xcat = jnp.concatenate([ctx, x], axis=1)
    cc_rows = -(-(bsz + 1) // F32_SUBLANES) * F32_SUBLANES
    cc = jnp.zeros((cc_rows, D_MODEL), F32).at[:bsz].set(c).at[bsz].set(c_ctx)

    out = None
    for l in range(depth):
        last = l == depth - 1
        mod_all = _modulation(cc, w_mod[l].astype(BF16), b_mod[l][None, :])
        mod_all = mod_all.reshape(cc_rows, N_MOD, D_MODEL)
        mod_x = mod_all[:bsz]
        mod_c = jnp.broadcast_to(mod_all[bsz][None], (bsz, N_MOD, D_MODEL))
        mods = jnp.stack([mod_c, mod_x], axis=1)
        mods = jnp.pad(mods, ((0, 0), (0, 0), (0, MOD_ROWS - N_MOD), (0, 0)))

        gq = jnp.tile(g_q[l], N_HEADS)[None, :]
        gk = jnp.tile(g_k[l], N_KV_HEADS)[None, :]
        qT, k, vT, rest = _inproj(xcat, mods, g_mix[l][None, :], w_in[l].astype(BF16), gq, gk,
                                  cos_t, sin_t, ones_bd)

        w_gate, b_gate = _gate_weights(lru_wa[l], lru_ba[l], lru_wi[l], lru_bi[l])
        hf, hb, sc = _lru(rest, lru_conv_w[l], lru_conv_b[l][None, :], w_gate, b_gate, lru_lam[l],
                          sc_conv_w[l])

        tile_offset = 1 if last else 0
        att = _attention(qT, k, vT, tile_offset)
        x1, h2 = _outproj(xcat, att, hf, hb, rest, sc, mods, w_out[l].astype(BF16),
                          g_ffn[l][None, :], tile_offset)
        out = _ffn(h2, x1, mods, w_up[l].astype(BF16), ffn_conv_w[l], w_down[l].astype(BF16),
                   g_final[None, :], ctx_tiles=1 - tile_offset, final_norm=last)
        xcat = out
    return out
```

```python
import functools

import numpy as np
import jax
import jax.numpy as jnp
from jax import lax
from jax.experimental import pallas as pl
from jax.experimental.pallas import tpu as pltpu

F32 = jnp.float32
BF16 = jnp.bfloat16

D_MODEL = 1024
HEAD_DIM = 64
N_HEADS = 8
N_KV_HEADS = 2
HEADS_PER_KV = N_HEADS // N_KV_HEADS
ATTN_WIDTH = N_HEADS * HEAD_DIM
KV_WIDTH = N_KV_HEADS * HEAD_DIM
LRU_WIDTH = 256
LRU_BLOCKS = 4
LRU_BLOCK = LRU_WIDTH // LRU_BLOCKS
LRU_CONV = 4
LRU_C = 8.0
SC_WIDTH = 256
IN_WIDTH = 2048
D_FF = 2816
FF_CHUNK = 256
GRID_W = 64
ROPE_THETA = 10000.0
EPS = 1e-6
Q_SCALE = HEAD_DIM ** -0.5 * 1.4426950408889634

TM = 256
SCORE_LOOKAHEAD = 3
SCORE_SLOTS = SCORE_LOOKAHEAD + 1
UP_LOOKAHEAD = 2
UP_SLOTS = UP_LOOKAHEAD + 1
LANES = 128
F32_SUBLANES = 8
BF16_SUBLANES = 16
N_MOD = 6
MOD_ROWS = 8

OFF_Q = 0
OFF_K = OFF_Q + ATTN_WIDTH
OFF_V = OFF_K + KV_WIDTH
OFF_R = OFF_V + KV_WIDTH
OFF_G = OFF_R + LRU_WIDTH
OFF_B = OFF_G + LRU_WIDTH
OFF_C = OFF_B + SC_WIDTH
OFF_U = OFF_C + SC_WIDTH


def _const_spec(shape):
    nd = len(shape)
    return pl.BlockSpec(shape, lambda *_: (0,) * nd, pipeline_mode=pl.Buffered(1))


def _params(vmem_mb):
    return pltpu.CompilerParams(dimension_semantics=("arbitrary", "arbitrary"),
                                vmem_limit_bytes=vmem_mb * 1024 * 1024)


def _rms_scale(t):
    return lax.rsqrt(jnp.mean(t * t, axis=-1, keepdims=True) + EPS)


def _mod_kernel(c_ref, w_ref, b_ref, o_ref):
    c = c_ref[...]
    h = (c * jax.nn.sigmoid(c)).astype(BF16)
    o_ref[...] = jnp.dot(h, w_ref[...].astype(BF16), preferred_element_type=F32) + b_ref[...]


def _modulation(cc, w_mod, b_mod):
    rows = cc.shape[0]
    nblk = N_MOD
    return pl.pallas_call(
        _mod_kernel,
        grid=(nblk,),
        in_specs=[pl.BlockSpec((rows, D_MODEL), lambda j: (0, 0)),
                  pl.BlockSpec((D_MODEL, D_MODEL), lambda j: (0, j)),
                  pl.BlockSpec((1, D_MODEL), lambda j: (0, j))],
        out_specs=pl.BlockSpec((rows, D_MODEL), lambda j: (0, j)),
        out_shape=jax.ShapeDtypeStruct((rows, N_MOD * D_MODEL), F32),
        name="modulation",
    )(cc, w_mod, b_mod)


def _swap_half_heads(t):
    w = t.shape[-1]
    lane = lax.broadcasted_iota(jnp.int32, t.shape, 1)
    first = (lane % HEAD_DIM) < (HEAD_DIM // 2)
    return jnp.where(first, pltpu.roll(t, w - HEAD_DIM // 2, 1), pltpu.roll(t, HEAD_DIM // 2, 1))


def _head_rmsnorm(t, ones_blockdiag, g):
    t2 = t * t
    hi = t2.astype(BF16)
    lo = (t2 - hi.astype(F32)).astype(BF16)
    ssum = (jnp.dot(hi, ones_blockdiag, preferred_element_type=F32)
            + jnp.dot(lo, ones_blockdiag, preferred_element_type=F32))
    return t * lax.rsqrt(ssum * (1.0 / HEAD_DIM) + EPS) * g


def _inproj_kernel(ctx_ref, x_ref, mod_ref, gmix_ref, w_ref, gq_ref, gk_ref, cos_ref, sin_ref, ones_ref,
                   qT_ref, k_ref, vT_ref, rest_ref):
    x = jnp.where(pl.program_id(1) == 0, ctx_ref[0], x_ref[0])
    mod = mod_ref[0, 0]
    shift, scale = mod[0:1], mod[1:2]
    h = x * _rms_scale(x) * gmix_ref[...]
    h = h * (1.0 + scale) + shift
    y = jnp.dot(h.astype(BF16), w_ref[...], preferred_element_type=F32)

    cos2 = cos_ref[...]
    sin2 = sin_ref[...]
    ones_q = ones_ref[...]

    q = _head_rmsnorm(y[:, OFF_Q:OFF_Q + ATTN_WIDTH], ones_q, gq_ref[...])
    cos_q = jnp.concatenate([cos2] * (ATTN_WIDTH // KV_WIDTH), axis=1)
    sin_q = jnp.concatenate([sin2] * (ATTN_WIDTH // KV_WIDTH), axis=1)
    q = (q * cos_q + _swap_half_heads(q) * sin_q) * Q_SCALE
    qT_ref[0] = q.T.astype(BF16)

    k = _head_rmsnorm(y[:, OFF_K:OFF_K + KV_WIDTH], ones_q[:KV_WIDTH, :KV_WIDTH], gk_ref[...])
    k = k * cos2 + _swap_half_heads(k) * sin2
    k_ref[0] = k.astype(BF16)

    vT_ref[0] = y[:, OFF_V:OFF_V + KV_WIDTH].T.astype(BF16)

    rest_ref[0, :, 0:LRU_WIDTH] = y[:, OFF_R:OFF_R + LRU_WIDTH]
    rest_ref[0, :, LRU_WIDTH:2 * LRU_WIDTH] = jax.nn.gelu(y[:, OFF_G:OFF_G + LRU_WIDTH])
    rest_ref[0, :, 2 * LRU_WIDTH:2 * LRU_WIDTH + SC_WIDTH] = y[:, OFF_B:OFF_B + SC_WIDTH]
    rest_ref[0, :, 2 * LRU_WIDTH + SC_WIDTH:] = (y[:, OFF_C:OFF_C + SC_WIDTH]
                                                 * y[:, OFF_U:OFF_U + SC_WIDTH])


def _token_specs(x_base, tile_offset=0):
    return [pl.BlockSpec((1, TM, D_MODEL), lambda b, i: (b, 0, 0)),
            pl.BlockSpec((1, TM, D_MODEL),
                         lambda b, i: (b, x_base + jnp.maximum(i + tile_offset - 1, 0), 0))]


def _inproj(ctx_arr, x_arr, x_base, mods, g_mix, w_in, gq, gk, cos_t, sin_t, ones_bd):
    bsz = x_arr.shape[0]
    nt = x_arr.shape[1] // TM - x_base + 1
    t_all = nt * TM
    rest_w = 2 * LRU_WIDTH + 2 * SC_WIDTH
    return pl.pallas_call(
        _inproj_kernel,
        grid=(bsz, nt),
        in_specs=_token_specs(x_base) + [
                  pl.BlockSpec((1, 1, MOD_ROWS, D_MODEL), lambda b, i: (b, jnp.minimum(i, 1), 0, 0)),
                  _const_spec((1, D_MODEL)),
                  _const_spec((D_MODEL, IN_WIDTH)),
                  _const_spec((1, ATTN_WIDTH)),
                  _const_spec((1, KV_WIDTH)),
                  pl.BlockSpec((TM, KV_WIDTH), lambda b, i: (i, 0)),
                  pl.BlockSpec((TM, KV_WIDTH), lambda b, i: (i, 0)),
                  _const_spec((ATTN_WIDTH, ATTN_WIDTH))],
        out_specs=[pl.BlockSpec((1, ATTN_WIDTH, TM), lambda b, i: (b, 0, i)),
                   pl.BlockSpec((1, TM, KV_WIDTH), lambda b, i: (b, i, 0)),
                   pl.BlockSpec((1, KV_WIDTH, TM), lambda b, i: (b, 0, i)),
                   pl.BlockSpec((1, TM, rest_w), lambda b, i: (b, i, 0))],
        out_shape=[jax.ShapeDtypeStruct((bsz, ATTN_WIDTH, t_all), BF16),
                   jax.ShapeDtypeStruct((bsz, t_all, KV_WIDTH), BF16),
                   jax.ShapeDtypeStruct((bsz, KV_WIDTH, t_all), BF16),
                   jax.ShapeDtypeStruct((bsz, t_all, rest_w), F32)],
        compiler_params=_params(40),
        name="inproj",
    )(ctx_arr, x_arr, mods, g_mix, w_in, gq, gk, cos_t, sin_t, ones_bd)


def _with_halo(prev_ref, cur_ref, next_ref, valid_prev, valid_next):
    prev = prev_ref[0] * valid_prev.astype(F32)
    nxt = next_ref[0] * valid_next.astype(F32)
    return jnp.concatenate([prev, cur_ref[0], nxt], axis=0)


def _shifted(tcat, offset):
    rows = tcat.shape[0]
    if offset == 0:
        return tcat[F32_SUBLANES:F32_SUBLANES + TM]
    return pltpu.roll(tcat, (-offset) % rows, 0)[F32_SUBLANES:F32_SUBLANES + TM]


def _lru_coeffs(rcat, conv_w, conv_b, w_gate, b_gate, sp_lam):
    xc = conv_b
    for j in range(LRU_CONV):
        xc = xc + conv_w[j:j + 1] * _shifted(rcat, j - 2)
    z = jnp.dot(xc.astype(BF16), w_gate, preferred_element_type=F32) + b_gate
    r_gate = jax.nn.sigmoid(z[:, :LRU_WIDTH])
    i_gate = jax.nn.sigmoid(z[:, LRU_WIDTH:])
    log_a = (-LRU_C) * r_gate * sp_lam
    a = jnp.exp(log_a)
    one_minus_a2 = -jnp.tanh(log_a) * (1.0 + a * a)
    return a, jnp.sqrt(one_minus_a2) * (i_gate * xc)


def _lru_kernel(rf_ref, rfp_ref, rfn_ref, rb_ref, rbp_ref, rbn_ref, bg_ref, cu_ref, cup_ref, cun_ref,
                cw_ref, cb_ref, wg_ref, bgate_ref, lam_ref, scw_ref,
                hf_ref, hb_ref, sc_ref,
                af_s, bf_s, ab_s, bb_s, hf_state, hb_state, *, nt):
    i = pl.program_id(1)
    j = jnp.where(i == 0, 0, nt - i)

    @pl.when(i == 0)
    def _():
        hf_state[...] = jnp.zeros_like(hf_state)
        hb_state[...] = jnp.zeros_like(hb_state)

    def halo_valid(t):
        return jnp.logical_and(t != 0, t != 1), jnp.logical_and(t != 0, t != nt - 1)

    lam = lam_ref[...]
    sp_lam = jnp.maximum(-lam, 0.0) + jnp.log1p(jnp.exp(-jnp.abs(lam)))
    cw = cw_ref[...]
    cb = cb_ref[...]

    vp, vn = halo_valid(i)
    a, b = _lru_coeffs(_with_halo(rfp_ref, rf_ref, rfn_ref, vp, vn), cw, cb,
                       wg_ref[:, :2 * LRU_WIDTH], bgate_ref[:, :2 * LRU_WIDTH], sp_lam[0:1])
    _block_prefix(a, b, af_s, bf_s, reverse=False)
    cucat = _with_halo(cup_ref, cu_ref, cun_ref, vp, vn)
    scw = scw_ref[...]
    conv = scw[0:1] * _shifted(cucat, -1) + scw[1:2] * _shifted(cucat, 0) + scw[2:3] * _shifted(cucat, 1)
    sc_ref[0] = (bg_ref[0] * conv).astype(BF16)

    vp, vn = halo_valid(j)
    a, b = _lru_coeffs(_with_halo(rbp_ref, rb_ref, rbn_ref, vp, vn), cw, cb,
                       wg_ref[:, 2 * LRU_WIDTH:], bgate_ref[:, 2 * LRU_WIDTH:], sp_lam[1:2])
    _block_prefix(a, b, ab_s, bb_s, reverse=True)

    _carry_blocks(af_s, bf_s, hf_ref, hf_state, reverse=False)
    _carry_blocks(ab_s, bb_s, hb_ref, hb_state, reverse=True)


def _block_prefix(a, b, a_s, b_s, reverse):
    n = F32_SUBLANES
    a = a.reshape(TM // n, n, LRU_WIDTH)
    b = b.reshape(TM // n, n, LRU_WIDTH)
    row = lax.broadcasted_iota(jnp.int32, a.shape, 1)
    d = 1
    while d < n:
        shift, keep = (n - d, row < n - d) if reverse else (d, row >= d)
        a_prev = jnp.where(keep, pltpu.roll(a, shift, 1), 1.0)
        b_prev = jnp.where(keep, pltpu.roll(b, shift, 1), 0.0)
        b = a * b_prev + b
        a = a * a_prev
        d *= 2
    a_s[...] = a
    b_s[...] = b


def _carry_blocks(a_s, b_s, out_ref, state_ref, reverse):
    n = F32_SUBLANES
    nb = TM // n
    last = 0 if reverse else n - 1
    h_in = state_ref[...]
    for v in (reversed(range(nb)) if reverse else range(nb)):
        h = a_s[v] * h_in + b_s[v]
        out_ref[0, v * n:(v + 1) * n, :] = h
        h_in = h[last:last + 1]
    state_ref[...] = h_in


def _lru(rest, conv_w, conv_b, w_gate, b_gate, lam, sc_w):
    bsz, t_all, _ = rest.shape
    nt = t_all // TM
    hpt = TM // F32_SUBLANES
    nhb = t_all // F32_SUBLANES

    def rev(i):
        return jnp.where(i == 0, 0, nt - i)

    def tile(col, order):
        return pl.BlockSpec((1, TM, LRU_WIDTH), lambda b, i: (b, order(i), col))

    def prev_halo(col, order):
        return pl.BlockSpec((1, F32_SUBLANES, LRU_WIDTH),
                            lambda b, i: (b, jnp.maximum(order(i) * hpt - 1, 0), col))

    def next_halo(col, order):
        return pl.BlockSpec((1, F32_SUBLANES, LRU_WIDTH),
                            lambda b, i: (b, jnp.minimum((order(i) + 1) * hpt, nhb - 1), col))

    fwd = lambda i: i
    out_f32 = jax.ShapeDtypeStruct((bsz, t_all, LRU_WIDTH), F32)
    return pl.pallas_call(
        functools.partial(_lru_kernel, nt=nt),
        grid=(bsz, nt),
        in_specs=[tile(0, fwd), prev_halo(0, fwd), next_halo(0, fwd),
                  tile(0, rev), prev_halo(0, rev), next_halo(0, rev),
                  tile(2, fwd),
                  tile(3, fwd), prev_halo(3, fwd), next_halo(3, fwd),
                  _const_spec((LRU_CONV, LRU_WIDTH)),
                  _const_spec((1, LRU_WIDTH)),
                  _const_spec((LRU_WIDTH, 4 * LRU_WIDTH)),
                  _const_spec((1, 4 * LRU_WIDTH)),
                  _const_spec((2, LRU_WIDTH)),
                  _const_spec((3, SC_WIDTH))],
        out_specs=[pl.BlockSpec((1, TM, LRU_WIDTH), lambda b, i: (b, i, 0)),
                   pl.BlockSpec((1, TM, LRU_WIDTH), lambda b, i: (b, rev(i), 0)),
                   pl.BlockSpec((1, TM, SC_WIDTH), lambda b, i: (b, i, 0))],
        out_shape=[out_f32, out_f32, jax.ShapeDtypeStruct((bsz, t_all, SC_WIDTH), BF16)],
        scratch_shapes=([pltpu.VMEM((TM // F32_SUBLANES, F32_SUBLANES, LRU_WIDTH), F32)] * 4
                        + [pltpu.VMEM((1, LRU_WIDTH), F32)] * 2),
        compiler_params=_params(32),
        name="lru_scan",
    )(rest, rest, rest, rest, rest, rest, rest, rest, rest, rest,
      conv_w, conv_b, w_gate, b_gate, lam, sc_w)


def _attn_kernel(qT_ref, k_ref, vT_ref, o_ref, oT_ref, s_ref, *, tile_offset, n_key_tiles):
    i = pl.program_id(1) + tile_offset
    ones_rows = jnp.ones((BF16_SUBLANES, TM), BF16)

    def run(n_tiles):
        items = [(pr, c) for pr in range(N_HEADS // 2) for c in range(n_tiles)]
        qpads = {}

        def qpad_of(pr):
            if pr not in qpads:
                row0 = pr * 2 * HEAD_DIM
                q2 = jnp.concatenate([qT_ref[0, row0:row0 + HEAD_DIM, :],
                                      qT_ref[0, row0 + HEAD_DIM:row0 + 2 * HEAD_DIM, :]], axis=1)
                zero = jnp.zeros_like(q2)
                first_kv = (2 * pr) // HEADS_PER_KV == 0
                qpads[pr] = jnp.concatenate([q2, zero] if first_kv else [zero, q2], axis=0)
            return qpads[pr]

        def scores(t):
            pr, c = items[t]
            s = jnp.dot(k_ref[0, c * TM:(c + 1) * TM, :], qpad_of(pr), preferred_element_type=F32)
            s_ref[t % SCORE_SLOTS] = s
            return jnp.max(s, axis=0, keepdims=True)

        tile_max = [scores(t) for t in range(min(SCORE_LOOKAHEAD, len(items)))]
        m = acc = None
        for t, (pr, c) in enumerate(items):
            if t + SCORE_LOOKAHEAD < len(items):
                tile_max.append(scores(t + SCORE_LOOKAHEAD))
            if c == 0:
                m = jnp.full((1, 2 * TM), -jnp.inf, F32)
                acc = jnp.zeros((HEAD_DIM + BF16_SUBLANES, 2 * TM), F32)
            g = (2 * pr) // HEADS_PER_KV
            m_new = jnp.maximum(m, tile_max[t])
            alpha = jnp.exp2(m - m_new)
            p = jnp.exp2(s_ref[t % SCORE_SLOTS] - m_new).astype(BF16)
            vt = jnp.concatenate([vT_ref[0, g * HEAD_DIM:(g + 1) * HEAD_DIM, c * TM:(c + 1) * TM],
                                  ones_rows], axis=0)
            acc = alpha * acc + jnp.dot(vt, p, preferred_element_type=F32)
            m = m_new
            tile_max[t] = None
            if c == n_tiles - 1:
                o = acc[:HEAD_DIM] / acc[HEAD_DIM:HEAD_DIM + 1]
                row0 = pr * 2 * HEAD_DIM
                oT_ref[row0:row0 + HEAD_DIM, :] = o[:, :TM]
                oT_ref[row0 + HEAD_DIM:row0 + 2 * HEAD_DIM, :] = o[:, TM:]
        o_ref[0] = oT_ref[...].T.astype(BF16)

    @pl.when(i == 0)
    def _():
        run(1)

    @pl.when(i != 0)
    def _():
        run(n_key_tiles)


def _attention(qT, k, vT, tile_offset):
    bsz, t_all, _ = k.shape
    nt = t_all // TM
    nq = nt - tile_offset
    return pl.pallas_call(
        functools.partial(_attn_kernel, tile_offset=tile_offset, n_key_tiles=nt),
        grid=(bsz, nq),
        in_specs=[pl.BlockSpec((1, ATTN_WIDTH, TM), lambda b, i: (b, 0, i + tile_offset)),
                  pl.BlockSpec((1, t_all, KV_WIDTH), lambda b, i: (b, 0, 0)),
                  pl.BlockSpec((1, KV_WIDTH, t_all), lambda b, i: (b, 0, 0))],
        out_specs=pl.BlockSpec((1, TM, ATTN_WIDTH), lambda b, i: (b, i, 0)),
        out_shape=jax.ShapeDtypeStruct((bsz, nq * TM, ATTN_WIDTH), BF16),
        scratch_shapes=[pltpu.VMEM((ATTN_WIDTH, TM), F32), pltpu.VMEM((SCORE_SLOTS, TM, 2 * TM), F32)],
        compiler_params=_params(40),
        name="attention",
    )(qT, k, vT)


def _outproj_kernel(ctx_ref, x_ref, att_ref, hf_ref, hb_ref, gg_ref, sc_ref, mod_ref, wo_ref, gffn_ref,
                    x1_ref, h2_ref, *, tile_offset):
    x = x_ref[0]
    if tile_offset == 0:
        x = jnp.where(pl.program_id(1) == 0, ctx_ref[0], x)
    mod = mod_ref[0, 0]
    gate_mix, shift, scale = mod[2:3], mod[3:4], mod[4:5]
    lru = (gg_ref[0] * (hf_ref[0] + hb_ref[0])).astype(BF16)
    y = (jnp.dot(att_ref[0], wo_ref[0:ATTN_WIDTH], preferred_element_type=F32)
         + jnp.dot(lru, wo_ref[ATTN_WIDTH:ATTN_WIDTH + LRU_WIDTH], preferred_element_type=F32)
         + jnp.dot(sc_ref[0], wo_ref[ATTN_WIDTH + LRU_WIDTH:], preferred_element_type=F32))
    x1 = x + gate_mix * y
    x1_ref[0] = x1
    h2 = x1 * _rms_scale(x1) * gffn_ref[...]
    h2_ref[0] = (h2 * (1.0 + scale) + shift).astype(BF16)


def _outproj(ctx_arr, x_arr, x_base, att, hf, hb, rest, sc, mods, w_out, g_ffn, tile_offset):
    bsz, t_all, _ = hf.shape
    nq = t_all // TM - tile_offset
    off = lambda b, i: (b, i + tile_offset, 0)
    return pl.pallas_call(
        functools.partial(_outproj_kernel, tile_offset=tile_offset),
        grid=(bsz, nq),
        in_specs=_token_specs(x_base, tile_offset) + [
                  pl.BlockSpec((1, TM, ATTN_WIDTH), lambda b, i: (b, i, 0)),
                  pl.BlockSpec((1, TM, LRU_WIDTH), off),
                  pl.BlockSpec((1, TM, LRU_WIDTH), off),
                  pl.BlockSpec((1, TM, LRU_WIDTH), lambda b, i: (b, i + tile_offset, 1)),
                  pl.BlockSpec((1, TM, SC_WIDTH), off),
                  pl.BlockSpec((1, 1, MOD_ROWS, D_MODEL),
                               lambda b, i: (b, jnp.minimum(i + tile_offset, 1), 0, 0)),
                  _const_spec((D_MODEL, D_MODEL)),
                  _const_spec((1, D_MODEL))],
        out_specs=[pl.BlockSpec((1, TM, D_MODEL), lambda b, i: (b, i, 0)),
                   pl.BlockSpec((1, TM, D_MODEL), lambda b, i: (b, i, 0))],
        out_shape=[jax.ShapeDtypeStruct((bsz, nq * TM, D_MODEL), F32),
                   jax.ShapeDtypeStruct((bsz, nq * TM, D_MODEL), BF16)],
        compiler_params=_params(32),
        name="outproj",
    )(ctx_arr, x_arr, att, hf, hb, rest, sc, mods, w_out, g_ffn)


def _ffn_kernel(h_ref, hp_ref, hn_ref, x1_ref, mod_ref, wup_ref, cw_ref, wdn_ref, gfin_ref,
                o_ref, acc_ref, up_ref, *, ctx_tiles, nt, final_norm):
    i = pl.program_id(1)
    valid_prev = jnp.logical_and(i != 0, i != ctx_tiles)
    valid_next = jnp.logical_and(i != nt - 1, i != ctx_tiles - 1)
    hp = hp_ref[0] * valid_prev.astype(BF16)
    hn = hn_ref[0] * valid_next.astype(BF16)
    hcat = jnp.concatenate([hp, h_ref[0], hn], axis=0)
    rows = hcat.shape[0]

    def conv(t, w):
        lo, hi = BF16_SUBLANES, BF16_SUBLANES + TM
        return (w[0:1] * pltpu.roll(t, 1, 0)[lo:hi] + w[1:2] * t[lo:hi]
                + w[2:3] * pltpu.roll(t, rows - 1, 0)[lo:hi])

    n_chunks = D_FF // FF_CHUNK

    def up_project(c):
        cu = slice(c * FF_CHUNK, (c + 1) * FF_CHUNK)
        cg = slice(D_FF + c * FF_CHUNK, D_FF + (c + 1) * FF_CHUNK)
        up_ref[c % UP_SLOTS, :, :FF_CHUNK] = jnp.dot(hcat, wup_ref[:, cu], preferred_element_type=F32)
        up_ref[c % UP_SLOTS, :, FF_CHUNK:] = jnp.dot(hcat, wup_ref[:, cg], preferred_element_type=F32)

    for c in range(min(UP_LOOKAHEAD, n_chunks)):
        up_project(c)
    for c in range(n_chunks):
        if c + UP_LOOKAHEAD < n_chunks:
            up_project(c + UP_LOOKAHEAD)
        cu = slice(c * FF_CHUNK, (c + 1) * FF_CHUNK)
        cg = slice(D_FF + c * FF_CHUNK, D_FF + (c + 1) * FF_CHUNK)
        u = conv(up_ref[c % UP_SLOTS, :, :FF_CHUNK], cw_ref[:, cu])
        g = conv(up_ref[c % UP_SLOTS, :, FF_CHUNK:], cw_ref[:, cg])
        act = ((g * jax.nn.sigmoid(g)) * u).astype(BF16)
        part = jnp.dot(act, wdn_ref[cu, :], preferred_element_type=F32)
        if c == 0:
            acc_ref[...] = part
        else:
            acc_ref[...] += part

    gate_ffn = mod_ref[0, 0][5:6]
    x2 = x1_ref[0] + gate_ffn * acc_ref[...]
    if final_norm:
        x2 = x2 * _rms_scale(x2) * gfin_ref[...]
    o_ref[0] = x2


def _ffn(h2, x1, mods, w_up, conv_w, w_down, g_final, ctx_tiles, final_norm):
    bsz, t_len, _ = h2.shape
    nt = t_len // TM
    hpt = TM // BF16_SUBLANES
    nhb = t_len // BF16_SUBLANES
    return pl.pallas_call(
        functools.partial(_ffn_kernel, ctx_tiles=ctx_tiles, nt=nt, final_norm=final_norm),
        grid=(bsz, nt),
        in_specs=[pl.BlockSpec((1, TM, D_MODEL), lambda b, i: (b, i, 0)),
                  pl.BlockSpec((1, BF16_SUBLANES, D_MODEL),
                               lambda b, i: (b, jnp.maximum(i * hpt - 1, 0), 0)),
                  pl.BlockSpec((1, BF16_SUBLANES, D_MODEL),
                               lambda b, i: (b, jnp.minimum((i + 1) * hpt, nhb - 1), 0)),
                  pl.BlockSpec((1, TM, D_MODEL), lambda b, i: (b, i, 0)),
                  pl.BlockSpec((1, 1, MOD_ROWS, D_MODEL),
                               lambda b, i: (b, jnp.minimum(i + 1 - ctx_tiles, 1), 0, 0)),
                  _const_spec((D_MODEL, 2 * D_FF)),
                  _const_spec((3, 2 * D_FF)),
                  _const_spec((D_FF, D_MODEL)),
                  _const_spec((1, D_MODEL))],
        out_specs=pl.BlockSpec((1, TM, D_MODEL), lambda b, i: (b, i, 0)),
        out_shape=jax.ShapeDtypeStruct((bsz, t_len, D_MODEL), F32),
        scratch_shapes=[pltpu.VMEM((TM, D_MODEL), F32),
                        pltpu.VMEM((UP_SLOTS, TM + 2 * BF16_SUBLANES, 2 * FF_CHUNK), F32)],
        compiler_params=_params(48),
        name="conv_ffn",
    )(h2, h2, h2, x1, mods, w_up, conv_w, w_down, g_final)


def _rope_tables(seq, ctx_len):
    rows = seq // GRID_W
    pos_r = np.repeat(np.arange(rows, dtype=np.float32), GRID_W)
    pos_c = np.tile(np.arange(GRID_W, dtype=np.float32), rows)
    n_f = HEAD_DIM // 4
    inv = (np.float32(ROPE_THETA) ** (-np.arange(n_f, dtype=np.float32) / n_f)).astype(np.float32)
    ang = np.concatenate([pos_r[:, None] * inv, pos_c[:, None] * inv], axis=-1).astype(np.float32)
    cos = np.cos(ang.astype(np.float64))
    sin = np.sin(ang.astype(np.float64))
    cos_h = np.concatenate([cos, cos], axis=-1)
    sin_h = np.concatenate([-sin, sin], axis=-1)
    cos_h = np.concatenate([np.ones((ctx_len, HEAD_DIM)), cos_h], axis=0)
    sin_h = np.concatenate([np.zeros((ctx_len, HEAD_DIM)), sin_h], axis=0)
    two = KV_WIDTH // HEAD_DIM
    return (jnp.asarray(np.tile(cos_h, (1, two)), F32), jnp.asarray(np.tile(sin_h, (1, two)), F32))


def _ones_blockdiag():
    idx = np.arange(ATTN_WIDTH) // HEAD_DIM
    return jnp.asarray(idx[:, None] == idx[None, :], BF16)


def _gate_weights(wa, ba, wi, bi):
    def dense(w):
        eye = jnp.eye(LRU_BLOCKS, dtype=w.dtype)
        return jnp.einsum("nde,nm->ndme", w, eye).reshape(LRU_WIDTH, LRU_WIDTH)
    w = jnp.concatenate([dense(wa[0]), dense(wi[0]), dense(wa[1]), dense(wi[1])], axis=1)
    b = jnp.concatenate([ba[0], bi[0], ba[1], bi[1]])[None, :]
    return w.astype(BF16), b


def kernel(x, c, ctx, c_ctx, w_mod, b_mod, g_mix, g_ffn, w_in, g_q, g_k, lru_conv_w, lru_conv_b,
           lru_wa, lru_ba, lru_wi, lru_bi, lru_lam, sc_conv_w, w_out, w_up, ffn_conv_w, w_down,
           g_final):
    bsz, seq, _ = x.shape
    ctx_len = ctx.shape[1]
    depth = w_mod.shape[0]
    assert ctx_len == TM and seq % TM == 0 and seq % GRID_W == 0

    cos_t, sin_t = _rope_tables(seq, ctx_len)
    ones_bd = _ones_blockdiag()
    ctx_arr, x_arr, x_base = ctx, x, 0
    cc_rows = -(-(bsz + 1) // F32_SUBLANES) * F32_SUBLANES
    cc = jnp.zeros((cc_rows, D_MODEL), F32).at[:bsz].set(c).at[bsz].set(c_ctx)

    out = None
    for l in range(depth):
        last = l == depth - 1
        mod_all = _modulation(cc, w_mod[l], b_mod[l][None, :])
        mod_all = mod_all.reshape(cc_rows, N_MOD, D_MODEL)
        mod_x = mod_all[:bsz]
        mod_c = jnp.broadcast_to(mod_all[bsz][None], (bsz, N_MOD, D_MODEL))
        mods = jnp.stack([mod_c, mod_x], axis=1)
        mods = jnp.pad(mods, ((0, 0), (0, 0), (0, MOD_ROWS - N_MOD), (0, 0)))

        gq = jnp.tile(g_q[l], N_HEADS)[None, :]
        gk = jnp.tile(g_k[l], N_KV_HEADS)[None, :]
        qT, k, vT, rest = _inproj(ctx_arr, x_arr, x_base, mods, g_mix[l][None, :],
                                  w_in[l].astype(BF16), gq, gk, cos_t, sin_t, ones_bd)

        w_gate, b_gate = _gate_weights(lru_wa[l], lru_ba[l], lru_wi[l], lru_bi[l])
        hf, hb, sc = _lru(rest, lru_conv_w[l], lru_conv_b[l][None, :], w_gate, b_gate, lru_lam[l],
                          sc_conv_w[l])

        tile_offset = 1 if last else 0
        att = _attention(qT, k, vT, tile_offset)
        x1, h2 = _outproj(ctx_arr, x_arr, x_base, att, hf, hb, rest, sc, mods, w_out[l].astype(BF16),
                          g_ffn[l][None, :], tile_offset)
        out = _ffn(h2, x1, mods, w_up[l].astype(BF16), ffn_conv_w[l], w_down[l].astype(BF16),
                   g_final[None, :], ctx_tiles=1 - tile_offset, final_norm=last)
        ctx_arr, x_arr, x_base = out, out, 1
    return out
```

```python
import functools

import numpy as np
import jax
import jax.numpy as jnp
from jax import lax
from jax.experimental import pallas as pl
from jax.experimental.pallas import tpu as pltpu

F32 = jnp.float32
BF16 = jnp.bfloat16

D_MODEL = 1024
HEAD_DIM = 64
N_HEADS = 8
N_KV_HEADS = 2
HEADS_PER_KV = N_HEADS // N_KV_HEADS
ATTN_WIDTH = N_HEADS * HEAD_DIM
KV_WIDTH = N_KV_HEADS * HEAD_DIM
LRU_WIDTH = 256
LRU_BLOCKS = 4
LRU_BLOCK = LRU_WIDTH // LRU_BLOCKS
LRU_CONV = 4
LRU_C = 8.0
SC_WIDTH = 256
IN_WIDTH = 2048
D_FF = 2816
FF_CHUNK = 256
GRID_W = 64
ROPE_THETA = 10000.0
EPS = 1e-6
Q_SCALE = HEAD_DIM ** -0.5 * 1.4426950408889634

TM = 256
SCORE_LOOKAHEAD = 3
SCORE_SLOTS = SCORE_LOOKAHEAD + 1
UP_LOOKAHEAD = 2
UP_SLOTS = UP_LOOKAHEAD + 1
LANES = 128
F32_SUBLANES = 8
BF16_SUBLANES = 16
N_MOD = 6
MOD_ROWS = 8

OFF_Q = 0
OFF_K = OFF_Q + ATTN_WIDTH
OFF_V = OFF_K + KV_WIDTH
OFF_R = OFF_V + KV_WIDTH
OFF_G = OFF_R + LRU_WIDTH
OFF_B = OFF_G + LRU_WIDTH
OFF_C = OFF_B + SC_WIDTH
OFF_U = OFF_C + SC_WIDTH


def _const_spec(shape):
    nd = len(shape)
    return pl.BlockSpec(shape, lambda *_: (0,) * nd, pipeline_mode=pl.Buffered(1))


def _params(vmem_mb, n_grid_dims=2):
    return pltpu.CompilerParams(dimension_semantics=("arbitrary",) * n_grid_dims,
                                vmem_limit_bytes=vmem_mb * 1024 * 1024)


def _rms_scale(t):
    return lax.rsqrt(jnp.mean(t * t, axis=-1, keepdims=True) + EPS)


def _mod_kernel(c_ref, w_ref, b_ref, o_ref):
    c = c_ref[...]
    h = (c * jax.nn.sigmoid(c)).astype(BF16)
    o_ref[...] = jnp.dot(h, w_ref[...].astype(BF16), preferred_element_type=F32) + b_ref[...]


def _modulation(cc, w_mod, b_mod):
    rows = cc.shape[0]
    nblk = N_MOD
    return pl.pallas_call(
        _mod_kernel,
        grid=(nblk,),
        in_specs=[pl.BlockSpec((rows, D_MODEL), lambda j: (0, 0)),
                  pl.BlockSpec((D_MODEL, D_MODEL), lambda j: (0, j)),
                  pl.BlockSpec((1, D_MODEL), lambda j: (0, j))],
        out_specs=pl.BlockSpec((rows, D_MODEL), lambda j: (0, j)),
        out_shape=jax.ShapeDtypeStruct((rows, N_MOD * D_MODEL), F32),
        name="modulation",
    )(cc, w_mod, b_mod)


def _norm_rope_transposed(t, gain, cos, sin):
    half = HEAD_DIM // 2
    out = []
    for h in range(t.shape[0] // HEAD_DIM):
        r = t[h * HEAD_DIM:(h + 1) * HEAD_DIM]
        r = r * lax.rsqrt(jnp.mean(r * r, axis=0, keepdims=True) + EPS) * gain
        x1, x2 = r[:half], r[half:]
        out += [x1 * cos - x2 * sin, x1 * sin + x2 * cos]
    return jnp.concatenate(out, axis=0)


def _inproj_kernel(ctx_ref, x_ref, mod_ref, gmix_ref, w_ref, gq_ref, gk_ref, cos_ref, sin_ref,
                   qT_ref, k_ref, vT_ref, rest_ref, y_even_ref, y_odd_ref, *, nt, n_tiles):
    s = pl.program_id(0)
    tile = jnp.minimum(s, n_tiles - 1) % nt

    def project(y_ref):
        x = jnp.where(tile == 0, ctx_ref[0], x_ref[0])
        mod = mod_ref[0, 0]
        shift, scale = mod[0:1], mod[1:2]
        h = x * _rms_scale(x) * gmix_ref[...]
        h = h * (1.0 + scale) + shift
        y_ref[...] = jnp.dot(h.astype(BF16), w_ref[...], preferred_element_type=F32)

    def finish(y_ref):
        cos, sin = cos_ref[...], sin_ref[...]
        q = _norm_rope_transposed(y_ref[:, OFF_Q:OFF_Q + ATTN_WIDTH].T, gq_ref[...], cos, sin)
        qT_ref[0] = q.astype(BF16)
        k = _norm_rope_transposed(y_ref[:, OFF_K:OFF_K + KV_WIDTH].T, gk_ref[...], cos, sin)
        k_ref[0] = k.T.astype(BF16)
        vT_ref[0] = y_ref[:, OFF_V:OFF_V + KV_WIDTH].T.astype(BF16)
        rest_ref[0, :, 0:LRU_WIDTH] = y_ref[:, OFF_R:OFF_R + LRU_WIDTH]
        rest_ref[0, :, LRU_WIDTH:2 * LRU_WIDTH] = jax.nn.gelu(y_ref[:, OFF_G:OFF_G + LRU_WIDTH])
        rest_ref[0, :, 2 * LRU_WIDTH:2 * LRU_WIDTH + SC_WIDTH] = y_ref[:, OFF_B:OFF_B + SC_WIDTH]
        rest_ref[0, :, 2 * LRU_WIDTH + SC_WIDTH:] = (y_ref[:, OFF_C:OFF_C + SC_WIDTH]
                                                     * y_ref[:, OFF_U:OFF_U + SC_WIDTH])

    @pl.when(s == 0)
    def _():
        y_odd_ref[...] = jnp.zeros_like(y_odd_ref)

    @pl.when(s % 2 == 0)
    def _():
        finish(y_odd_ref)
        project(y_even_ref)

    @pl.when(s % 2 == 1)
    def _():
        finish(y_even_ref)
        project(y_odd_ref)


def _token_specs(x_base, tile_offset=0):
    return [pl.BlockSpec((1, TM, D_MODEL), lambda b, i: (b, 0, 0)),
            pl.BlockSpec((1, TM, D_MODEL),
                         lambda b, i: (b, x_base + jnp.maximum(i + tile_offset - 1, 0), 0))]


def _inproj(ctx_arr, x_arr, x_base, mods, g_mix, w_in, gq, gk, cos_t, sin_t):
    bsz = x_arr.shape[0]
    nt = x_arr.shape[1] // TM - x_base + 1
    t_all = nt * TM
    n_tiles = bsz * nt
    rest_w = 2 * LRU_WIDTH + 2 * SC_WIDTH

    def cur(s):
        sc = jnp.minimum(s, n_tiles - 1)
        return sc // nt, sc % nt

    def prev(s):
        sp = jnp.maximum(s - 1, 0)
        return sp // nt, sp % nt

    rope_spec = pl.BlockSpec((HEAD_DIM // 2, TM), lambda s: (0, prev(s)[1]))
    return pl.pallas_call(
        functools.partial(_inproj_kernel, nt=nt, n_tiles=n_tiles),
        grid=(n_tiles + 1,),
        in_specs=[pl.BlockSpec((1, TM, D_MODEL), lambda s: (cur(s)[0], 0, 0)),
                  pl.BlockSpec((1, TM, D_MODEL),
                               lambda s: (cur(s)[0], x_base + jnp.maximum(cur(s)[1] - 1, 0), 0)),
                  pl.BlockSpec((1, 1, MOD_ROWS, D_MODEL),
                               lambda s: (cur(s)[0], jnp.minimum(cur(s)[1], 1), 0, 0)),
                  _const_spec((1, D_MODEL)),
                  _const_spec((D_MODEL, IN_WIDTH)),
                  _const_spec((HEAD_DIM, TM)),
                  _const_spec((HEAD_DIM, TM)),
                  rope_spec,
                  rope_spec],
        out_specs=[pl.BlockSpec((1, ATTN_WIDTH, TM), lambda s: (prev(s)[0], 0, prev(s)[1])),
                   pl.BlockSpec((1, TM, KV_WIDTH), lambda s: (prev(s)[0], prev(s)[1], 0)),
                   pl.BlockSpec((1, KV_WIDTH, TM), lambda s: (prev(s)[0], 0, prev(s)[1])),
                   pl.BlockSpec((1, TM, rest_w), lambda s: (prev(s)[0], prev(s)[1], 0))],
        out_shape=[jax.ShapeDtypeStruct((bsz, ATTN_WIDTH, t_all), BF16),
                   jax.ShapeDtypeStruct((bsz, t_all, KV_WIDTH), BF16),
                   jax.ShapeDtypeStruct((bsz, KV_WIDTH, t_all), BF16),
                   jax.ShapeDtypeStruct((bsz, t_all, rest_w), F32)],
        scratch_shapes=[pltpu.VMEM((TM, IN_WIDTH), F32)] * 2,
        compiler_params=_params(40, n_grid_dims=1),
        name="inproj",
    )(ctx_arr, x_arr, mods, g_mix, w_in, gq, gk, cos_t, sin_t)


def _with_halo(prev_ref, cur_ref, next_ref, valid_prev, valid_next):
    prev = prev_ref[0] * valid_prev.astype(F32)
    nxt = next_ref[0] * valid_next.astype(F32)
    return jnp.concatenate([prev, cur_ref[0], nxt], axis=0)


def _shifted(tcat, offset):
    rows = tcat.shape[0]
    if offset == 0:
        return tcat[F32_SUBLANES:F32_SUBLANES + TM]
    return pltpu.roll(tcat, (-offset) % rows, 0)[F32_SUBLANES:F32_SUBLANES + TM]


def _lru_coeffs(rcat, conv_w, conv_b, w_gate, b_gate, sp_lam):
    xc = conv_b
    for j in range(LRU_CONV):
        xc = xc + conv_w[j:j + 1] * _shifted(rcat, j - 2)
    z = jnp.dot(xc.astype(BF16), w_gate, preferred_element_type=F32) + b_gate
    r_gate = jax.nn.sigmoid(z[:, :LRU_WIDTH])
    i_gate = jax.nn.sigmoid(z[:, LRU_WIDTH:])
    log_a = (-LRU_C) * r_gate * sp_lam
    a = jnp.exp(log_a)
    one_minus_a2 = -jnp.tanh(log_a) * (1.0 + a * a)
    return a, jnp.sqrt(one_minus_a2) * (i_gate * xc)


def _lru_kernel(rf_ref, rfp_ref, rfn_ref, rb_ref, rbp_ref, rbn_ref, bg_ref, cu_ref, cup_ref, cun_ref,
                cw_ref, cb_ref, wg_ref, bgate_ref, lam_ref, scw_ref,
                hf_ref, hb_ref, sc_ref,
                af_s, bf_s, ab_s, bb_s, hf_state, hb_state, *, nt):
    i = pl.program_id(1)
    j = jnp.where(i == 0, 0, nt - i)

    @pl.when(i == 0)
    def _():
        hf_state[...] = jnp.zeros_like(hf_state)
        hb_state[...] = jnp.zeros_like(hb_state)

    def halo_valid(t):
        return jnp.logical_and(t != 0, t != 1), jnp.logical_and(t != 0, t != nt - 1)

    lam = lam_ref[...]
    sp_lam = jnp.maximum(-lam, 0.0) + jnp.log1p(jnp.exp(-jnp.abs(lam)))
    cw = cw_ref[...]
    cb = cb_ref[...]

    vp, vn = halo_valid(i)
    a, b = _lru_coeffs(_with_halo(rfp_ref, rf_ref, rfn_ref, vp, vn), cw, cb,
                       wg_ref[:, :2 * LRU_WIDTH], bgate_ref[:, :2 * LRU_WIDTH], sp_lam[0:1])
    _block_prefix(a, b, af_s, bf_s, reverse=False)
    cucat = _with_halo(cup_ref, cu_ref, cun_ref, vp, vn)
    scw = scw_ref[...]
    conv = scw[0:1] * _shifted(cucat, -1) + scw[1:2] * _shifted(cucat, 0) + scw[2:3] * _shifted(cucat, 1)
    sc_ref[0] = (bg_ref[0] * conv).astype(BF16)

    vp, vn = halo_valid(j)
    a, b = _lru_coeffs(_with_halo(rbp_ref, rb_ref, rbn_ref, vp, vn), cw, cb,
                       wg_ref[:, 2 * LRU_WIDTH:], bgate_ref[:, 2 * LRU_WIDTH:], sp_lam[1:2])
    _block_prefix(a, b, ab_s, bb_s, reverse=True)

    _carry_blocks(af_s, bf_s, hf_ref, hf_state, reverse=False)
    _carry_blocks(ab_s, bb_s, hb_ref, hb_state, reverse=True)


def _block_prefix(a, b, a_s, b_s, reverse):
    n = F32_SUBLANES
    a = a.reshape(TM // n, n, LRU_WIDTH)
    b = b.reshape(TM // n, n, LRU_WIDTH)
    row = lax.broadcasted_iota(jnp.int32, a.shape, 1)
    d = 1
    while d < n:
        shift, keep = (n - d, row < n - d) if reverse else (d, row >= d)
        a_prev = jnp.where(keep, pltpu.roll(a, shift, 1), 1.0)
        b_prev = jnp.where(keep, pltpu.roll(b, shift, 1), 0.0)
        b = a * b_prev + b
        a = a * a_prev
        d *= 2
    a_s[...] = a
    b_s[...] = b


def _carry_blocks(a_s, b_s, out_ref, state_ref, reverse):
    n = F32_SUBLANES
    nb = TM // n
    last = 0 if reverse else n - 1
    h_in = state_ref[...]
    for v in (reversed(range(nb)) if reverse else range(nb)):
        h = a_s[v] * h_in + b_s[v]
        out_ref[0, v * n:(v + 1) * n, :] = h
        h_in = h[last:last + 1]
    state_ref[...] = h_in


def _lru(rest, conv_w, conv_b, w_gate, b_gate, lam, sc_w):
    bsz, t_all, _ = rest.shape
    nt = t_all // TM
    hpt = TM // F32_SUBLANES
    nhb = t_all // F32_SUBLANES

    def rev(i):
        return jnp.where(i == 0, 0, nt - i)

    def tile(col, order):
        return pl.BlockSpec((1, TM, LRU_WIDTH), lambda b, i: (b, order(i), col))

    def prev_halo(col, order):
        return pl.BlockSpec((1, F32_SUBLANES, LRU_WIDTH),
                            lambda b, i: (b, jnp.maximum(order(i) * hpt - 1, 0), col))

    def next_halo(col, order):
        return pl.BlockSpec((1, F32_SUBLANES, LRU_WIDTH),
                            lambda b, i: (b, jnp.minimum((order(i) + 1) * hpt, nhb - 1), col))

    fwd = lambda i: i
    out_f32 = jax.ShapeDtypeStruct((bsz, t_all, LRU_WIDTH), F32)
    return pl.pallas_call(
        functools.partial(_lru_kernel, nt=nt),
        grid=(bsz, nt),
        in_specs=[tile(0, fwd), prev_halo(0, fwd), next_halo(0, fwd),
                  tile(0, rev), prev_halo(0, rev), next_halo(0, rev),
                  tile(2, fwd),
                  tile(3, fwd), prev_halo(3, fwd), next_halo(3, fwd),
                  _const_spec((LRU_CONV, LRU_WIDTH)),
                  _const_spec((1, LRU_WIDTH)),
                  _const_spec((LRU_WIDTH, 4 * LRU_WIDTH)),
                  _const_spec((1, 4 * LRU_WIDTH)),
                  _const_spec((2, LRU_WIDTH)),
                  _const_spec((3, SC_WIDTH))],
        out_specs=[pl.BlockSpec((1, TM, LRU_WIDTH), lambda b, i: (b, i, 0)),
                   pl.BlockSpec((1, TM, LRU_WIDTH), lambda b, i: (b, rev(i), 0)),
                   pl.BlockSpec((1, TM, SC_WIDTH), lambda b, i: (b, i, 0))],
        out_shape=[out_f32, out_f32, jax.ShapeDtypeStruct((bsz, t_all, SC_WIDTH), BF16)],
        scratch_shapes=([pltpu.VMEM((TM // F32_SUBLANES, F32_SUBLANES, LRU_WIDTH), F32)] * 4
                        + [pltpu.VMEM((1, LRU_WIDTH), F32)] * 2),
        compiler_params=_params(32),
        name="lru_scan",
    )(rest, rest, rest, rest, rest, rest, rest, rest, rest, rest,
      conv_w, conv_b, w_gate, b_gate, lam, sc_w)


def _mixer_kernel(qT_ref, k_ref, vT_ref, ctx_ref, x_ref, hf_ref, hb_ref, gg_ref, sc_ref, mod_ref, wo_ref,
                  gffn_ref, x1_ref, h2_ref, oT_ref, s_ref, *, tile_offset, n_key_tiles):
    i = pl.program_id(1) + tile_offset
    ones_rows = jnp.ones((BF16_SUBLANES, TM), BF16)

    def project_out():
        x = x_ref[0]
        if tile_offset == 0:
            x = jnp.where(i == 0, ctx_ref[0], x)
        mod = mod_ref[0, 0]
        gate_mix, shift, scale = mod[2:3], mod[3:4], mod[4:5]
        att = oT_ref[...].T.astype(BF16)
        lru = (gg_ref[0] * (hf_ref[0] + hb_ref[0])).astype(BF16)
        y = (jnp.dot(att, wo_ref[0:ATTN_WIDTH], preferred_element_type=F32)
             + jnp.dot(lru, wo_ref[ATTN_WIDTH:ATTN_WIDTH + LRU_WIDTH], preferred_element_type=F32)
             + jnp.dot(sc_ref[0], wo_ref[ATTN_WIDTH + LRU_WIDTH:], preferred_element_type=F32))
        x1 = x + gate_mix * y
        x1_ref[0] = x1
        h2 = x1 * _rms_scale(x1) * gffn_ref[...]
        h2_ref[0] = (h2 * (1.0 + scale) + shift).astype(BF16)

    def run(n_tiles):
        items = [(pr, c) for pr in range(N_HEADS // 2) for c in range(n_tiles)]
        qpads = {}

        def qpad_of(pr):
            if pr not in qpads:
                row0 = pr * 2 * HEAD_DIM
                q2 = jnp.concatenate([qT_ref[0, row0:row0 + HEAD_DIM, :],
                                      qT_ref[0, row0 + HEAD_DIM:row0 + 2 * HEAD_DIM, :]], axis=1)
                zero = jnp.zeros_like(q2)
                first_kv = (2 * pr) // HEADS_PER_KV == 0
                qpads[pr] = jnp.concatenate([q2, zero] if first_kv else [zero, q2], axis=0)
            return qpads[pr]

        def scores(t):
            pr, c = items[t]
            s = jnp.dot(k_ref[0, c * TM:(c + 1) * TM, :], qpad_of(pr), preferred_element_type=F32)
            s_ref[t % SCORE_SLOTS] = s
            return jnp.max(s, axis=0, keepdims=True)

        tile_max = [scores(t) for t in range(min(SCORE_LOOKAHEAD, len(items)))]
        m = acc = None
        for t, (pr, c) in enumerate(items):
            if t + SCORE_LOOKAHEAD < len(items):
                tile_max.append(scores(t + SCORE_LOOKAHEAD))
            if c == 0:
                m = jnp.full((1, 2 * TM), -jnp.inf, F32)
                acc = jnp.zeros((HEAD_DIM + BF16_SUBLANES, 2 * TM), F32)
            g = (2 * pr) // HEADS_PER_KV
            m_new = jnp.maximum(m, tile_max[t])
            alpha = jnp.exp2(m - m_new)
            p = jnp.exp2(s_ref[t % SCORE_SLOTS] - m_new).astype(BF16)
            vt = jnp.concatenate([vT_ref[0, g * HEAD_DIM:(g + 1) * HEAD_DIM, c * TM:(c + 1) * TM],
                                  ones_rows], axis=0)
            acc = alpha * acc + jnp.dot(vt, p, preferred_element_type=F32)
            m = m_new
            tile_max[t] = None
            if c == n_tiles - 1:
                o = acc[:HEAD_DIM] / acc[HEAD_DIM:HEAD_DIM + 1]
                row0 = pr * 2 * HEAD_DIM
                oT_ref[row0:row0 + HEAD_DIM, :] = o[:, :TM]
                oT_ref[row0 + HEAD_DIM:row0 + 2 * HEAD_DIM, :] = o[:, TM:]
        project_out()

    @pl.when(i == 0)
    def _():
        run(1)

    @pl.when(i != 0)
    def _():
        run(n_key_tiles)


def _mixer(qT, k, vT, ctx_arr, x_arr, x_base, hf, hb, rest, sc, mods, w_out, g_ffn, tile_offset):
    bsz, t_all, _ = k.shape
    nt = t_all // TM
    nq = nt - tile_offset
    off = lambda b, i: (b, i + tile_offset, 0)
    return pl.pallas_call(
        functools.partial(_mixer_kernel, tile_offset=tile_offset, n_key_tiles=nt),
        grid=(bsz, nq),
        in_specs=[pl.BlockSpec((1, ATTN_WIDTH, TM), lambda b, i: (b, 0, i + tile_offset)),
                  pl.BlockSpec((1, t_all, KV_WIDTH), lambda b, i: (b, 0, 0)),
                  pl.BlockSpec((1, KV_WIDTH, t_all), lambda b, i: (b, 0, 0))]
                 + _token_specs(x_base, tile_offset) + [
                  pl.BlockSpec((1, TM, LRU_WIDTH), off),
                  pl.BlockSpec((1, TM, LRU_WIDTH), off),
                  pl.BlockSpec((1, TM, LRU_WIDTH), lambda b, i: (b, i + tile_offset, 1)),
                  pl.BlockSpec((1, TM, SC_WIDTH), off),
                  pl.BlockSpec((1, 1, MOD_ROWS, D_MODEL),
                               lambda b, i: (b, jnp.minimum(i + tile_offset, 1), 0, 0)),
                  _const_spec((D_MODEL, D_MODEL)),
                  _const_spec((1, D_MODEL))],
        out_specs=[pl.BlockSpec((1, TM, D_MODEL), lambda b, i: (b, i, 0)),
                   pl.BlockSpec((1, TM, D_MODEL), lambda b, i: (b, i, 0))],
        out_shape=[jax.ShapeDtypeStruct((bsz, nq * TM, D_MODEL), F32),
                   jax.ShapeDtypeStruct((bsz, nq * TM, D_MODEL), BF16)],
        scratch_shapes=[pltpu.VMEM((ATTN_WIDTH, TM), F32), pltpu.VMEM((SCORE_SLOTS, TM, 2 * TM), F32)],
        compiler_params=_params(40),
        name="mixer",
    )(qT, k, vT, ctx_arr, x_arr, hf, hb, rest, sc, mods, w_out, g_ffn)


def _ffn_kernel(h_ref, hp_ref, hn_ref, x1_ref, mod_ref, wup_ref, cw_ref, wdn_ref, gfin_ref,
                o_ref, acc_ref, up_ref, *, ctx_tiles, nt, final_norm):
    i = pl.program_id(1)
    valid_prev = jnp.logical_and(i != 0, i != ctx_tiles)
    valid_next = jnp.logical_and(i != nt - 1, i != ctx_tiles - 1)
    hp = hp_ref[0] * valid_prev.astype(BF16)
    hn = hn_ref[0] * valid_next.astype(BF16)
    hcat = jnp.concatenate([hp, h_ref[0], hn], axis=0)
    rows = hcat.shape[0]

    def conv(t, w):
        lo, hi = BF16_SUBLANES, BF16_SUBLANES + TM
        return (w[0:1] * pltpu.roll(t, 1, 0)[lo:hi] + w[1:2] * t[lo:hi]
                + w[2:3] * pltpu.roll(t, rows - 1, 0)[lo:hi])

    n_chunks = D_FF // FF_CHUNK

    def up_project(c):
        cu = slice(c * FF_CHUNK, (c + 1) * FF_CHUNK)
        cg = slice(D_FF + c * FF_CHUNK, D_FF + (c + 1) * FF_CHUNK)
        up_ref[c % UP_SLOTS, :, :FF_CHUNK] = jnp.dot(hcat, wup_ref[:, cu], preferred_element_type=F32)
        up_ref[c % UP_SLOTS, :, FF_CHUNK:] = jnp.dot(hcat, wup_ref[:, cg], preferred_element_type=F32)

    for c in range(min(UP_LOOKAHEAD, n_chunks)):
        up_project(c)
    for c in range(n_chunks):
        if c + UP_LOOKAHEAD < n_chunks:
            up_project(c + UP_LOOKAHEAD)
        cu = slice(c * FF_CHUNK, (c + 1) * FF_CHUNK)
        cg = slice(D_FF + c * FF_CHUNK, D_FF + (c + 1) * FF_CHUNK)
        u = conv(up_ref[c % UP_SLOTS, :, :FF_CHUNK], cw_ref[:, cu])
        g = conv(up_ref[c % UP_SLOTS, :, FF_CHUNK:], cw_ref[:, cg])
        act = ((g * jax.nn.sigmoid(g)) * u).astype(BF16)
        part = jnp.dot(act, wdn_ref[cu, :], preferred_element_type=F32)
        if c == 0:
            acc_ref[...] = part
        else:
            acc_ref[...] += part

    gate_ffn = mod_ref[0, 0][5:6]
    x2 = x1_ref[0] + gate_ffn * acc_ref[...]
    if final_norm:
        x2 = x2 * _rms_scale(x2) * gfin_ref[...]
    o_ref[0] = x2


def _ffn(h2, x1, mods, w_up, conv_w, w_down, g_final, ctx_tiles, final_norm):
    bsz, t_len, _ = h2.shape
    nt = t_len // TM
    hpt = TM // BF16_SUBLANES
    nhb = t_len // BF16_SUBLANES
    return pl.pallas_call(
        functools.partial(_ffn_kernel, ctx_tiles=ctx_tiles, nt=nt, final_norm=final_norm),
        grid=(bsz, nt),
        in_specs=[pl.BlockSpec((1, TM, D_MODEL), lambda b, i: (b, i, 0)),
                  pl.BlockSpec((1, BF16_SUBLANES, D_MODEL),
                               lambda b, i: (b, jnp.maximum(i * hpt - 1, 0), 0)),
                  pl.BlockSpec((1, BF16_SUBLANES, D_MODEL),
                               lambda b, i: (b, jnp.minimum((i + 1) * hpt, nhb - 1), 0)),
                  pl.BlockSpec((1, TM, D_MODEL), lambda b, i: (b, i, 0)),
                  pl.BlockSpec((1, 1, MOD_ROWS, D_MODEL),
                               lambda b, i: (b, jnp.minimum(i + 1 - ctx_tiles, 1), 0, 0)),
                  _const_spec((D_MODEL, 2 * D_FF)),
                  _const_spec((3, 2 * D_FF)),
                  _const_spec((D_FF, D_MODEL)),
                  _const_spec((1, D_MODEL))],
        out_specs=pl.BlockSpec((1, TM, D_MODEL), lambda b, i: (b, i, 0)),
        out_shape=jax.ShapeDtypeStruct((bsz, t_len, D_MODEL), F32),
        scratch_shapes=[pltpu.VMEM((TM, D_MODEL), F32),
                        pltpu.VMEM((UP_SLOTS, TM + 2 * BF16_SUBLANES, 2 * FF_CHUNK), F32)],
        compiler_params=_params(48),
        name="conv_ffn",
    )(h2, h2, h2, x1, mods, w_up, conv_w, w_down, g_final)


def _rope_tables(seq, ctx_len):
    rows = seq // GRID_W
    pos_r = np.repeat(np.arange(rows, dtype=np.float32), GRID_W)
    pos_c = np.tile(np.arange(GRID_W, dtype=np.float32), rows)
    n_f = HEAD_DIM // 4
    inv = (np.float32(ROPE_THETA) ** (-np.arange(n_f, dtype=np.float32) / n_f)).astype(np.float32)
    ang = np.concatenate([pos_r[:, None] * inv, pos_c[:, None] * inv], axis=-1).astype(np.float32)
    cos = np.concatenate([np.ones((ctx_len, HEAD_DIM // 2)), np.cos(ang.astype(np.float64))], axis=0)
    sin = np.concatenate([np.zeros((ctx_len, HEAD_DIM // 2)), np.sin(ang.astype(np.float64))], axis=0)
    return jnp.asarray(cos.T, F32), jnp.asarray(sin.T, F32)


def _gate_weights(wa, ba, wi, bi):
    def dense(w):
        eye = jnp.eye(LRU_BLOCKS, dtype=w.dtype)
        return jnp.einsum("nde,nm->ndme", w, eye).reshape(LRU_WIDTH, LRU_WIDTH)
    w = jnp.concatenate([dense(wa[0]), dense(wi[0]), dense(wa[1]), dense(wi[1])], axis=1)
    b = jnp.concatenate([ba[0], bi[0], ba[1], bi[1]])[None, :]
    return w.astype(BF16), b


def kernel(x, c, ctx, c_ctx, w_mod, b_mod, g_mix, g_ffn, w_in, g_q, g_k, lru_conv_w, lru_conv_b,
           lru_wa, lru_ba, lru_wi, lru_bi, lru_lam, sc_conv_w, w_out, w_up, ffn_conv_w, w_down,
           g_final):
    bsz, seq, _ = x.shape
    ctx_len = ctx.shape[1]
    depth = w_mod.shape[0]
    assert ctx_len == TM and seq % TM == 0 and seq % GRID_W == 0

    cos_t, sin_t = _rope_tables(seq, ctx_len)
    ctx_arr, x_arr, x_base = ctx, x, 0
    cc_rows = -(-(bsz + 1) // F32_SUBLANES) * F32_SUBLANES
    cc = jnp.zeros((cc_rows, D_MODEL), F32).at[:bsz].set(c).at[bsz].set(c_ctx)

    out = None
    for l in range(depth):
        last = l == depth - 1
        mod_all = _modulation(cc, w_mod[l], b_mod[l][None, :])
        mod_all = mod_all.reshape(cc_rows, N_MOD, D_MODEL)
        mod_x = mod_all[:bsz]
        mod_c = jnp.broadcast_to(mod_all[bsz][None], (bsz, N_MOD, D_MODEL))
        mods = jnp.stack([mod_c, mod_x], axis=1)
        mods = jnp.pad(mods, ((0, 0), (0, 0), (0, MOD_ROWS - N_MOD), (0, 0)))

        gq = jnp.broadcast_to((g_q[l] * Q_SCALE)[:, None], (HEAD_DIM, TM))
        gk = jnp.broadcast_to(g_k[l][:, None], (HEAD_DIM, TM))
        qT, k, vT, rest = _inproj(ctx_arr, x_arr, x_base, mods, g_mix[l][None, :],
                                  w_in[l].astype(BF16), gq, gk, cos_t, sin_t)

        w_gate, b_gate = _gate_weights(lru_wa[l], lru_ba[l], lru_wi[l], lru_bi[l])
        hf, hb, sc = _lru(rest, lru_conv_w[l], lru_conv_b[l][None, :], w_gate, b_gate, lru_lam[l],
                          sc_conv_w[l])

        tile_offset = 1 if last else 0
        x1, h2 = _mixer(qT, k, vT, ctx_arr, x_arr, x_base, hf, hb, rest, sc, mods,
                        w_out[l].astype(BF16), g_ffn[l][None, :], tile_offset)
        out = _ffn(h2, x1, mods, w_up[l].astype(BF16), ffn_conv_w[l], w_down[l].astype(BF16),
                   g_final[None, :], ctx_tiles=1 - tile_offset, final_norm=last)
        ctx_arr, x_arr, x_base = out, out, 1
    return out
```

```python
import functools

import numpy as np
import jax
import jax.numpy as jnp
from jax import lax
from jax.experimental import pallas as pl
from jax.experimental.pallas import tpu as pltpu

F32 = jnp.float32
BF16 = jnp.bfloat16

D_MODEL = 1024
HEAD_DIM = 64
N_HEADS = 8
N_KV_HEADS = 2
HEADS_PER_KV = N_HEADS // N_KV_HEADS
ATTN_WIDTH = N_HEADS * HEAD_DIM
KV_WIDTH = N_KV_HEADS * HEAD_DIM
LRU_WIDTH = 256
LRU_BLOCKS = 4
LRU_BLOCK = LRU_WIDTH // LRU_BLOCKS
LRU_CONV = 4
LRU_C = 8.0
SC_WIDTH = 256
IN_WIDTH = 2048
D_FF = 2816
FF_CHUNK = 256
GRID_W = 64
ROPE_THETA = 10000.0
EPS = 1e-6
Q_SCALE = HEAD_DIM ** -0.5 * 1.4426950408889634

TM = 256
SCORE_LOOKAHEAD = 2
SCORE_SLOTS = SCORE_LOOKAHEAD + 1
UP_LOOKAHEAD = 2
UP_SLOTS = UP_LOOKAHEAD + 1
LANES = 128
F32_SUBLANES = 8
BF16_SUBLANES = 16
N_MOD = 6
MOD_ROWS = 8

OFF_Q = 0
OFF_K = OFF_Q + ATTN_WIDTH
OFF_V = OFF_K + KV_WIDTH
OFF_R = OFF_V + KV_WIDTH
OFF_G = OFF_R + LRU_WIDTH
OFF_B = OFF_G + LRU_WIDTH
OFF_C = OFF_B + SC_WIDTH
OFF_U = OFF_C + SC_WIDTH


def _const_spec(shape):
    nd = len(shape)
    return pl.BlockSpec(shape, lambda *_: (0,) * nd, pipeline_mode=pl.Buffered(1))


def _params(vmem_mb, n_grid_dims=2):
    return pltpu.CompilerParams(dimension_semantics=("arbitrary",) * n_grid_dims,
                                vmem_limit_bytes=vmem_mb * 1024 * 1024)


def _rms_scale(t):
    return lax.rsqrt(jnp.mean(t * t, axis=-1, keepdims=True) + EPS)


def _mod_kernel(c_ref, w_ref, b_ref, o_ref):
    c = c_ref[...]
    h = (c * jax.nn.sigmoid(c)).astype(BF16)
    o_ref[...] = jnp.dot(h, w_ref[...].astype(BF16), preferred_element_type=F32) + b_ref[...]


def _modulation(cc, w_mod, b_mod):
    rows = cc.shape[0]
    depth = w_mod.shape[0]
    return pl.pallas_call(
        _mod_kernel,
        grid=(depth, N_MOD),
        in_specs=[pl.BlockSpec((rows, D_MODEL), lambda l, j: (0, 0)),
                  pl.BlockSpec((None, D_MODEL, D_MODEL), lambda l, j: (l, 0, j)),
                  pl.BlockSpec((None, 1, D_MODEL), lambda l, j: (l, 0, j))],
        out_specs=pl.BlockSpec((None, rows, D_MODEL), lambda l, j: (l, 0, j)),
        out_shape=jax.ShapeDtypeStruct((depth, rows, N_MOD * D_MODEL), F32),
        name="modulation",
    )(cc, w_mod, b_mod[:, None, :])


def _norm_rope_transposed(t, gain, cos, sin):
    half = HEAD_DIM // 2
    out = []
    for h in range(t.shape[0] // HEAD_DIM):
        r = t[h * HEAD_DIM:(h + 1) * HEAD_DIM]
        r = r * lax.rsqrt(jnp.mean(r * r, axis=0, keepdims=True) + EPS) * gain
        x1, x2 = r[:half], r[half:]
        out += [x1 * cos - x2 * sin, x1 * sin + x2 * cos]
    return jnp.concatenate(out, axis=0)


def _inproj_kernel(ctx_ref, x_ref, mod_ref, gmix_ref, w_ref, gq_ref, gk_ref, cos_ref, sin_ref,
                   qT_ref, k_ref, vT_ref, rest_ref, y_even_ref, y_odd_ref, *, nt, n_tiles):
    s = pl.program_id(0)
    tile = jnp.minimum(s, n_tiles - 1) % nt

    def project(y_ref):
        x = jnp.where(tile == 0, ctx_ref[0], x_ref[0])
        mod = mod_ref[0, 0]
        shift, scale = mod[0:1], mod[1:2]
        h = x * _rms_scale(x) * gmix_ref[...]
        h = h * (1.0 + scale) + shift
        y_ref[...] = jnp.dot(h.astype(BF16), w_ref[...], preferred_element_type=F32)

    def finish(y_ref):
        cos, sin = cos_ref[...], sin_ref[...]
        q = _norm_rope_transposed(y_ref[:, OFF_Q:OFF_Q + ATTN_WIDTH].T, gq_ref[...], cos, sin)
        qT_ref[0] = q.astype(BF16)
        k = _norm_rope_transposed(y_ref[:, OFF_K:OFF_K + KV_WIDTH].T, gk_ref[...], cos, sin)
        k_ref[0] = k.T.astype(BF16)
        vT_ref[0] = y_ref[:, OFF_V:OFF_V + KV_WIDTH].T.astype(BF16)
        rest_ref[0, :, 0:LRU_WIDTH] = y_ref[:, OFF_R:OFF_R + LRU_WIDTH]
        rest_ref[0, :, LRU_WIDTH:2 * LRU_WIDTH] = jax.nn.gelu(y_ref[:, OFF_G:OFF_G + LRU_WIDTH])
        rest_ref[0, :, 2 * LRU_WIDTH:2 * LRU_WIDTH + SC_WIDTH] = y_ref[:, OFF_B:OFF_B + SC_WIDTH]
        rest_ref[0, :, 2 * LRU_WIDTH + SC_WIDTH:] = (y_ref[:, OFF_C:OFF_C + SC_WIDTH]
                                                     * y_ref[:, OFF_U:OFF_U + SC_WIDTH])

    @pl.when(s == 0)
    def _():
        y_odd_ref[...] = jnp.zeros_like(y_odd_ref)

    @pl.when(s % 2 == 0)
    def _():
        finish(y_odd_ref)
        project(y_even_ref)

    @pl.when(s % 2 == 1)
    def _():
        finish(y_even_ref)
        project(y_odd_ref)


def _token_specs(x_base, tile_offset=0):
    return [pl.BlockSpec((1, TM, D_MODEL), lambda b, i: (b, 0, 0)),
            pl.BlockSpec((1, TM, D_MODEL),
                         lambda b, i: (b, x_base + jnp.maximum(i + tile_offset - 1, 0), 0))]


def _inproj(ctx_arr, x_arr, x_base, mods, g_mix, w_in, gq, gk, cos_t, sin_t):
    bsz = x_arr.shape[0]
    nt = x_arr.shape[1] // TM - x_base + 1
    t_all = nt * TM
    n_tiles = bsz * nt
    rest_w = 2 * LRU_WIDTH + 2 * SC_WIDTH

    def cur(s):
        sc = jnp.minimum(s, n_tiles - 1)
        return sc // nt, sc % nt

    def prev(s):
        sp = jnp.maximum(s - 1, 0)
        return sp // nt, sp % nt

    rope_spec = pl.BlockSpec((HEAD_DIM // 2, TM), lambda s: (0, prev(s)[1]))
    return pl.pallas_call(
        functools.partial(_inproj_kernel, nt=nt, n_tiles=n_tiles),
        grid=(n_tiles + 1,),
        in_specs=[pl.BlockSpec((1, TM, D_MODEL), lambda s: (cur(s)[0], 0, 0)),
                  pl.BlockSpec((1, TM, D_MODEL),
                               lambda s: (cur(s)[0], x_base + jnp.maximum(cur(s)[1] - 1, 0), 0)),
                  pl.BlockSpec((1, 1, MOD_ROWS, D_MODEL),
                               lambda s: (cur(s)[0], jnp.minimum(cur(s)[1], 1), 0, 0)),
                  _const_spec((1, D_MODEL)),
                  _const_spec((D_MODEL, IN_WIDTH)),
                  _const_spec((HEAD_DIM, TM)),
                  _const_spec((HEAD_DIM, TM)),
                  rope_spec,
                  rope_spec],
        out_specs=[pl.BlockSpec((1, ATTN_WIDTH, TM), lambda s: (prev(s)[0], 0, prev(s)[1])),
                   pl.BlockSpec((1, TM, KV_WIDTH), lambda s: (prev(s)[0], prev(s)[1], 0)),
                   pl.BlockSpec((1, KV_WIDTH, TM), lambda s: (prev(s)[0], 0, prev(s)[1])),
                   pl.BlockSpec((1, TM, rest_w), lambda s: (prev(s)[0], prev(s)[1], 0))],
        out_shape=[jax.ShapeDtypeStruct((bsz, ATTN_WIDTH, t_all), BF16),
                   jax.ShapeDtypeStruct((bsz, t_all, KV_WIDTH), BF16),
                   jax.ShapeDtypeStruct((bsz, KV_WIDTH, t_all), BF16),
                   jax.ShapeDtypeStruct((bsz, t_all, rest_w), F32)],
        scratch_shapes=[pltpu.VMEM((TM, IN_WIDTH), F32)] * 2,
        compiler_params=_params(40, n_grid_dims=1),
        name="inproj",
    )(ctx_arr, x_arr, mods, g_mix, w_in, gq, gk, cos_t, sin_t)


def _with_halo(prev_ref, cur_ref, next_ref, valid_prev, valid_next):
    prev = prev_ref[0] * valid_prev.astype(F32)
    nxt = next_ref[0] * valid_next.astype(F32)
    return jnp.concatenate([prev, cur_ref[0], nxt], axis=0)


def _shifted(tcat, offset):
    rows = tcat.shape[0]
    if offset == 0:
        return tcat[F32_SUBLANES:F32_SUBLANES + TM]
    return pltpu.roll(tcat, (-offset) % rows, 0)[F32_SUBLANES:F32_SUBLANES + TM]


def _lru_coeffs(rcat, conv_w, conv_b, w_gate, b_gate, sp_lam):
    xc = conv_b
    for j in range(LRU_CONV):
        xc = xc + conv_w[j:j + 1] * _shifted(rcat, j - 2)
    z = jnp.dot(xc.astype(BF16), w_gate, preferred_element_type=F32) + b_gate
    r_gate = jax.nn.sigmoid(z[:, :LRU_WIDTH])
    i_gate = jax.nn.sigmoid(z[:, LRU_WIDTH:])
    log_a = (-LRU_C) * r_gate * sp_lam
    a = jnp.exp(log_a)
    one_minus_a2 = -jnp.tanh(log_a) * (1.0 + a * a)
    return a, jnp.sqrt(one_minus_a2) * (i_gate * xc)


def _lru_kernel(rf_ref, rfp_ref, rfn_ref, rb_ref, rbp_ref, rbn_ref, bg_ref, cu_ref, cup_ref, cun_ref,
                cw_ref, cb_ref, wg_ref, bgate_ref, lam_ref, scw_ref,
                hf_ref, hb_ref, sc_ref,
                af_s, bf_s, ab_s, bb_s, hf_state, hb_state, *, nt):
    i = pl.program_id(1)
    j = jnp.where(i == 0, 0, nt - i)

    @pl.when(i == 0)
    def _():
        hf_state[...] = jnp.zeros_like(hf_state)
        hb_state[...] = jnp.zeros_like(hb_state)

    def halo_valid(t):
        return jnp.logical_and(t != 0, t != 1), jnp.logical_and(t != 0, t != nt - 1)

    lam = lam_ref[...]
    sp_lam = jnp.maximum(-lam, 0.0) + jnp.log1p(jnp.exp(-jnp.abs(lam)))
    cw = cw_ref[...]
    cb = cb_ref[...]

    vp, vn = halo_valid(i)
    a, b = _lru_coeffs(_with_halo(rfp_ref, rf_ref, rfn_ref, vp, vn), cw, cb,
                       wg_ref[:, :2 * LRU_WIDTH], bgate_ref[:, :2 * LRU_WIDTH], sp_lam[0:1])
    _block_prefix(a, b, af_s, bf_s, reverse=False)
    cucat = _with_halo(cup_ref, cu_ref, cun_ref, vp, vn)
    scw = scw_ref[...]
    conv = scw[0:1] * _shifted(cucat, -1) + scw[1:2] * _shifted(cucat, 0) + scw[2:3] * _shifted(cucat, 1)
    sc_ref[0] = (bg_ref[0] * conv).astype(BF16)

    vp, vn = halo_valid(j)
    a, b = _lru_coeffs(_with_halo(rbp_ref, rb_ref, rbn_ref, vp, vn), cw, cb,
                       wg_ref[:, 2 * LRU_WIDTH:], bgate_ref[:, 2 * LRU_WIDTH:], sp_lam[1:2])
    _block_prefix(a, b, ab_s, bb_s, reverse=True)

    _carry_blocks(af_s, bf_s, hf_ref, hf_state, reverse=False)
    _carry_blocks(ab_s, bb_s, hb_ref, hb_state, reverse=True)


def _block_prefix(a, b, a_s, b_s, reverse):
    n = F32_SUBLANES
    a = a.reshape(TM // n, n, LRU_WIDTH)
    b = b.reshape(TM // n, n, LRU_WIDTH)
    row = lax.broadcasted_iota(jnp.int32, a.shape, 1)
    d = 1
    while d < n:
        shift, keep = (n - d, row < n - d) if reverse else (d, row >= d)
        a_prev = jnp.where(keep, pltpu.roll(a, shift, 1), 1.0)
        b_prev = jnp.where(keep, pltpu.roll(b, shift, 1), 0.0)
        b = a * b_prev + b
        a = a * a_prev
        d *= 2
    a_s[...] = a
    b_s[...] = b


def _carry_blocks(a_s, b_s, out_ref, state_ref, reverse):
    n = F32_SUBLANES
    nb = TM // n
    last = 0 if reverse else n - 1
    h_in = state_ref[...]
    for v in (reversed(range(nb)) if reverse else range(nb)):
        h = a_s[v] * h_in + b_s[v]
        out_ref[0, v * n:(v + 1) * n, :] = h
        h_in = h[last:last + 1]
    state_ref[...] = h_in


def _lru(rest, conv_w, conv_b, w_gate, b_gate, lam, sc_w):
    bsz, t_all, _ = rest.shape
    nt = t_all // TM
    hpt = TM // F32_SUBLANES
    nhb = t_all // F32_SUBLANES

    def rev(i):
        return jnp.where(i == 0, 0, nt - i)

    def tile(col, order):
        return pl.BlockSpec((1, TM, LRU_WIDTH), lambda b, i: (b, order(i), col))

    def prev_halo(col, order):
        return pl.BlockSpec((1, F32_SUBLANES, LRU_WIDTH),
                            lambda b, i: (b, jnp.maximum(order(i) * hpt - 1, 0), col))

    def next_halo(col, order):
        return pl.BlockSpec((1, F32_SUBLANES, LRU_WIDTH),
                            lambda b, i: (b, jnp.minimum((order(i) + 1) * hpt, nhb - 1), col))

    fwd = lambda i: i
    out_f32 = jax.ShapeDtypeStruct((bsz, t_all, LRU_WIDTH), F32)
    return pl.pallas_call(
        functools.partial(_lru_kernel, nt=nt),
        grid=(bsz, nt),
        in_specs=[tile(0, fwd), prev_halo(0, fwd), next_halo(0, fwd),
                  tile(0, rev), prev_halo(0, rev), next_halo(0, rev),
                  tile(2, fwd),
                  tile(3, fwd), prev_halo(3, fwd), next_halo(3, fwd),
                  _const_spec((LRU_CONV, LRU_WIDTH)),
                  _const_spec((1, LRU_WIDTH)),
                  _const_spec((LRU_WIDTH, 4 * LRU_WIDTH)),
                  _const_spec((1, 4 * LRU_WIDTH)),
                  _const_spec((2, LRU_WIDTH)),
                  _const_spec((3, SC_WIDTH))],
        out_specs=[pl.BlockSpec((1, TM, LRU_WIDTH), lambda b, i: (b, i, 0)),
                   pl.BlockSpec((1, TM, LRU_WIDTH), lambda b, i: (b, rev(i), 0)),
                   pl.BlockSpec((1, TM, SC_WIDTH), lambda b, i: (b, i, 0))],
        out_shape=[out_f32, out_f32, jax.ShapeDtypeStruct((bsz, t_all, SC_WIDTH), BF16)],
        scratch_shapes=([pltpu.VMEM((TM // F32_SUBLANES, F32_SUBLANES, LRU_WIDTH), F32)] * 4
                        + [pltpu.VMEM((1, LRU_WIDTH), F32)] * 2),
        compiler_params=_params(32),
        name="lru_scan",
    )(rest, rest, rest, rest, rest, rest, rest, rest, rest, rest,
      conv_w, conv_b, w_gate, b_gate, lam, sc_w)


def _mixer_kernel(qT_ref, k_ref, vT_ref, ctx_ref, x_ref, hf_ref, hb_ref, gg_ref, sc_ref, mod_ref, wo_ref,
                  gffn_ref, x1_ref, h2_ref, oT_ref, s_ref, *, tile_offset, n_key_tiles):
    i = pl.program_id(1) + tile_offset
    ones_rows = jnp.ones((BF16_SUBLANES, TM), BF16)

    def project_out():
        x = x_ref[0]
        if tile_offset == 0:
            x = jnp.where(i == 0, ctx_ref[0], x)
        mod = mod_ref[0, 0]
        gate_mix, shift, scale = mod[2:3], mod[3:4], mod[4:5]
        att = oT_ref[...].T.astype(BF16)
        lru = (gg_ref[0] * (hf_ref[0] + hb_ref[0])).astype(BF16)
        y = (jnp.dot(att, wo_ref[0:ATTN_WIDTH], preferred_element_type=F32)
             + jnp.dot(lru, wo_ref[ATTN_WIDTH:ATTN_WIDTH + LRU_WIDTH], preferred_element_type=F32)
             + jnp.dot(sc_ref[0], wo_ref[ATTN_WIDTH + LRU_WIDTH:], preferred_element_type=F32))
        x1 = x + gate_mix * y
        x1_ref[0] = x1
        h2 = x1 * _rms_scale(x1) * gffn_ref[...]
        h2_ref[0] = (h2 * (1.0 + scale) + shift).astype(BF16)

    def run(n_tiles):
        items = [(pr, c) for pr in range(N_HEADS // 2) for c in range(n_tiles)]
        qpads = {}

        def qpad_of(pr):
            if pr not in qpads:
                row0 = pr * 2 * HEAD_DIM
                q2 = jnp.concatenate([qT_ref[0, row0:row0 + HEAD_DIM, :],
                                      qT_ref[0, row0 + HEAD_DIM:row0 + 2 * HEAD_DIM, :]], axis=1)
                zero = jnp.zeros_like(q2)
                first_kv = (2 * pr) // HEADS_PER_KV == 0
                qpads[pr] = jnp.concatenate([q2, zero] if first_kv else [zero, q2], axis=0)
            return qpads[pr]

        def scores(t):
            pr, c = items[t]
            s = jnp.dot(k_ref[0, c * TM:(c + 1) * TM, :], qpad_of(pr), preferred_element_type=F32)
            s_ref[t % SCORE_SLOTS] = s
            return jnp.max(s, axis=0, keepdims=True)

        tile_max = [scores(t) for t in range(min(SCORE_LOOKAHEAD, len(items)))]
        m = acc = None
        for t, (pr, c) in enumerate(items):
            if t + SCORE_LOOKAHEAD < len(items):
                tile_max.append(scores(t + SCORE_LOOKAHEAD))
            if c == 0:
                m = jnp.full((1, 2 * TM), -jnp.inf, F32)
                acc = jnp.zeros((HEAD_DIM + BF16_SUBLANES, 2 * TM), F32)
            g = (2 * pr) // HEADS_PER_KV
            m_new = jnp.maximum(m, tile_max[t])
            alpha = jnp.exp2(m - m_new)
            p = jnp.exp2(s_ref[t % SCORE_SLOTS] - m_new).astype(BF16)
            vt = jnp.concatenate([vT_ref[0, g * HEAD_DIM:(g + 1) * HEAD_DIM, c * TM:(c + 1) * TM],
                                  ones_rows], axis=0)
            acc = alpha * acc + jnp.dot(vt, p, preferred_element_type=F32)
            m = m_new
            tile_max[t] = None
            if c == n_tiles - 1:
                o = acc[:HEAD_DIM] / acc[HEAD_DIM:HEAD_DIM + 1]
                row0 = pr * 2 * HEAD_DIM
                oT_ref[row0:row0 + HEAD_DIM, :] = o[:, :TM]
                oT_ref[row0 + HEAD_DIM:row0 + 2 * HEAD_DIM, :] = o[:, TM:]
        project_out()

    @pl.when(i == 0)
    def _():
        run(1)

    @pl.when(i != 0)
    def _():
        run(n_key_tiles)


def _mixer(qT, k, vT, ctx_arr, x_arr, x_base, hf, hb, rest, sc, mods, w_out, g_ffn, tile_offset):
    bsz, t_all, _ = k.shape
    nt = t_all // TM
    nq = nt - tile_offset
    off = lambda b, i: (b, i + tile_offset, 0)
    return pl.pallas_call(
        functools.partial(_mixer_kernel, tile_offset=tile_offset, n_key_tiles=nt),
        grid=(bsz, nq),
        in_specs=[pl.BlockSpec((1, ATTN_WIDTH, TM), lambda b, i: (b, 0, i + tile_offset)),
                  pl.BlockSpec((1, t_all, KV_WIDTH), lambda b, i: (b, 0, 0)),
                  pl.BlockSpec((1, KV_WIDTH, t_all), lambda b, i: (b, 0, 0))]
                 + _token_specs(x_base, tile_offset) + [
                  pl.BlockSpec((1, TM, LRU_WIDTH), off),
                  pl.BlockSpec((1, TM, LRU_WIDTH), off),
                  pl.BlockSpec((1, TM, LRU_WIDTH), lambda b, i: (b, i + tile_offset, 1)),
                  pl.BlockSpec((1, TM, SC_WIDTH), off),
                  pl.BlockSpec((1, 1, MOD_ROWS, D_MODEL),
                               lambda b, i: (b, jnp.minimum(i + tile_offset, 1), 0, 0)),
                  _const_spec((D_MODEL, D_MODEL)),
                  _const_spec((1, D_MODEL))],
        out_specs=[pl.BlockSpec((1, TM, D_MODEL), lambda b, i: (b, i, 0)),
                   pl.BlockSpec((1, TM, D_MODEL), lambda b, i: (b, i, 0))],
        out_shape=[jax.ShapeDtypeStruct((bsz, nq * TM, D_MODEL), F32),
                   jax.ShapeDtypeStruct((bsz, nq * TM, D_MODEL), BF16)],
        scratch_shapes=[pltpu.VMEM((ATTN_WIDTH, TM), F32), pltpu.VMEM((SCORE_SLOTS, TM, 2 * TM), F32)],
        compiler_params=_params(40),
        name="mixer",
    )(qT, k, vT, ctx_arr, x_arr, hf, hb, rest, sc, mods, w_out, g_ffn)


def _ffn_kernel(h_ref, hp_ref, hn_ref, x1_ref, mod_ref, wup_ref, cw_ref, wdn_ref, gfin_ref,
                o_ref, acc_ref, up_ref, *, ctx_tiles, nt, final_norm):
    i = pl.program_id(1)
    valid_prev = jnp.logical_and(i != 0, i != ctx_tiles)
    valid_next = jnp.logical_and(i != nt - 1, i != ctx_tiles - 1)
    hp = hp_ref[0] * valid_prev.astype(BF16)
    hn = hn_ref[0] * valid_next.astype(BF16)
    hcat = jnp.concatenate([hp, h_ref[0], hn], axis=0)
    rows = hcat.shape[0]

    def conv(t, w):
        lo, hi = BF16_SUBLANES, BF16_SUBLANES + TM
        return (w[0:1] * pltpu.roll(t, 1, 0)[lo:hi] + w[1:2] * t[lo:hi]
                + w[2:3] * pltpu.roll(t, rows - 1, 0)[lo:hi])

    n_chunks = D_FF // FF_CHUNK

    def up_project(c):
        cu = slice(c * FF_CHUNK, (c + 1) * FF_CHUNK)
        cg = slice(D_FF + c * FF_CHUNK, D_FF + (c + 1) * FF_CHUNK)
        up_ref[c % UP_SLOTS, :, :FF_CHUNK] = jnp.dot(hcat, wup_ref[:, cu], preferred_element_type=F32)
        up_ref[c % UP_SLOTS, :, FF_CHUNK:] = jnp.dot(hcat, wup_ref[:, cg], preferred_element_type=F32)

    for c in range(min(UP_LOOKAHEAD, n_chunks)):
        up_project(c)
    for c in range(n_chunks):
        if c + UP_LOOKAHEAD < n_chunks:
            up_project(c + UP_LOOKAHEAD)
        cu = slice(c * FF_CHUNK, (c + 1) * FF_CHUNK)
        cg = slice(D_FF + c * FF_CHUNK, D_FF + (c + 1) * FF_CHUNK)
        u = conv(up_ref[c % UP_SLOTS, :, :FF_CHUNK], cw_ref[:, cu])
        g = conv(up_ref[c % UP_SLOTS, :, FF_CHUNK:], cw_ref[:, cg])
        act = ((g * jax.nn.sigmoid(g)) * u).astype(BF16)
        part = jnp.dot(act, wdn_ref[cu, :], preferred_element_type=F32)
        if c == 0:
            acc_ref[...] = part
        else:
            acc_ref[...] += part

    gate_ffn = mod_ref[0, 0][5:6]
    x2 = x1_ref[0] + gate_ffn * acc_ref[...]
    if final_norm:
        x2 = x2 * _rms_scale(x2) * gfin_ref[...]
    o_ref[0] = x2


def _ffn(h2, x1, mods, w_up, conv_w, w_down, g_final, ctx_tiles, final_norm):
    bsz, t_len, _ = h2.shape
    nt = t_len // TM
    hpt = TM // BF16_SUBLANES
    nhb = t_len // BF16_SUBLANES
    return pl.pallas_call(
        functools.partial(_ffn_kernel, ctx_tiles=ctx_tiles, nt=nt, final_norm=final_norm),
        grid=(bsz, nt),
        in_specs=[pl.BlockSpec((1, TM, D_MODEL), lambda b, i: (b, i, 0)),
                  pl.BlockSpec((1, BF16_SUBLANES, D_MODEL),
                               lambda b, i: (b, jnp.maximum(i * hpt - 1, 0), 0)),
                  pl.BlockSpec((1, BF16_SUBLANES, D_MODEL),
                               lambda b, i: (b, jnp.minimum((i + 1) * hpt, nhb - 1), 0)),
                  pl.BlockSpec((1, TM, D_MODEL), lambda b, i: (b, i, 0)),
                  pl.BlockSpec((1, 1, MOD_ROWS, D_MODEL),
                               lambda b, i: (b, jnp.minimum(i + 1 - ctx_tiles, 1), 0, 0)),
                  _const_spec((D_MODEL, 2 * D_FF)),
                  _const_spec((3, 2 * D_FF)),
                  _const_spec((D_FF, D_MODEL)),
                  _const_spec((1, D_MODEL))],
        out_specs=pl.BlockSpec((1, TM, D_MODEL), lambda b, i: (b, i, 0)),
        out_shape=jax.ShapeDtypeStruct((bsz, t_len, D_MODEL), F32),
        scratch_shapes=[pltpu.VMEM((TM, D_MODEL), F32),
                        pltpu.VMEM((UP_SLOTS, TM + 2 * BF16_SUBLANES, 2 * FF_CHUNK), F32)],
        compiler_params=_params(48),
        name="conv_ffn",
    )(h2, h2, h2, x1, mods, w_up, conv_w, w_down, g_final)


def _rope_tables(seq, ctx_len):
    rows = seq // GRID_W
    pos_r = np.repeat(np.arange(rows, dtype=np.float32), GRID_W)
    pos_c = np.tile(np.arange(GRID_W, dtype=np.float32), rows)
    n_f = HEAD_DIM // 4
    inv = (np.float32(ROPE_THETA) ** (-np.arange(n_f, dtype=np.float32) / n_f)).astype(np.float32)
    ang = np.concatenate([pos_r[:, None] * inv, pos_c[:, None] * inv], axis=-1).astype(np.float32)
    cos = np.concatenate([np.ones((ctx_len, HEAD_DIM // 2)), np.cos(ang.astype(np.float64))], axis=0)
    sin = np.concatenate([np.zeros((ctx_len, HEAD_DIM // 2)), np.sin(ang.astype(np.float64))], axis=0)
    return jnp.asarray(cos.T, F32), jnp.asarray(sin.T, F32)


def _gate_weights(wa, ba, wi, bi):
    def dense(w):
        eye = jnp.eye(LRU_BLOCKS, dtype=w.dtype)
        return jnp.einsum("nde,nm->ndme", w, eye).reshape(LRU_WIDTH, LRU_WIDTH)
    w = jnp.concatenate([dense(wa[0]), dense(wi[0]), dense(wa[1]), dense(wi[1])], axis=1)
    b = jnp.concatenate([ba[0], bi[0], ba[1], bi[1]])[None, :]
    return w.astype(BF16), b


def kernel(x, c, ctx, c_ctx, w_mod, b_mod, g_mix, g_ffn, w_in, g_q, g_k, lru_conv_w, lru_conv_b,
           lru_wa, lru_ba, lru_wi, lru_bi, lru_lam, sc_conv_w, w_out, w_up, ffn_conv_w, w_down,
           g_final):
    bsz, seq, _ = x.shape
    ctx_len = ctx.shape[1]
    depth = w_mod.shape[0]
    assert ctx_len == TM and seq % TM == 0 and seq % GRID_W == 0

    cos_t, sin_t = _rope_tables(seq, ctx_len)
    ctx_arr, x_arr, x_base = ctx, x, 0
    cc_rows = -(-(bsz + 1) // F32_SUBLANES) * F32_SUBLANES
    cc = jnp.zeros((cc_rows, D_MODEL), F32).at[:bsz].set(c).at[bsz].set(c_ctx)

    mod_layers = _modulation(cc, w_mod, b_mod)

    out = None
    for l in range(depth):
        last = l == depth - 1
        mod_all = mod_layers[l].reshape(cc_rows, N_MOD, D_MODEL)
        mod_x = mod_all[:bsz]
        mod_c = jnp.broadcast_to(mod_all[bsz][None], (bsz, N_MOD, D_MODEL))
        mods = jnp.stack([mod_c, mod_x], axis=1)
        mods = jnp.pad(mods, ((0, 0), (0, 0), (0, MOD_ROWS - N_MOD), (0, 0)))

        gq = jnp.broadcast_to((g_q[l] * Q_SCALE)[:, None], (HEAD_DIM, TM))
        gk = jnp.broadcast_to(g_k[l][:, None], (HEAD_DIM, TM))
        qT, k, vT, rest = _inproj(ctx_arr, x_arr, x_base, mods, g_mix[l][None, :],
                                  w_in[l].astype(BF16), gq, gk, cos_t, sin_t)

        w_gate, b_gate = _gate_weights(lru_wa[l], lru_ba[l], lru_wi[l], lru_bi[l])
        hf, hb, sc = _lru(rest, lru_conv_w[l], lru_conv_b[l][None, :], w_gate, b_gate, lru_lam[l],
                          sc_conv_w[l])

        tile_offset = 1 if last else 0
        x1, h2 = _mixer(qT, k, vT, ctx_arr, x_arr, x_base, hf, hb, rest, sc, mods,
                        w_out[l].astype(BF16), g_ffn[l][None, :], tile_offset)
        out = _ffn(h2, x1, mods, w_up[l].astype(BF16), ffn_conv_w[l], w_down[l].astype(BF16),
                   g_final[None, :], ctx_tiles=1 - tile_offset, final_norm=last)
        ctx_arr, x_arr, x_base = out, out, 1
    return out
```

```python
import functools

import numpy as np
import jax
import jax.numpy as jnp
from jax import lax
from jax.experimental import pallas as pl
from jax.experimental.pallas import tpu as pltpu

F32 = jnp.float32
BF16 = jnp.bfloat16

D_MODEL = 1024
HEAD_DIM = 64
N_HEADS = 8
N_KV_HEADS = 2
HEADS_PER_KV = N_HEADS // N_KV_HEADS
ATTN_WIDTH = N_HEADS * HEAD_DIM
KV_WIDTH = N_KV_HEADS * HEAD_DIM
LRU_WIDTH = 256
LRU_BLOCKS = 4
LRU_BLOCK = LRU_WIDTH // LRU_BLOCKS
LRU_CONV = 4
LRU_C = 8.0
SC_WIDTH = 256
IN_WIDTH = 2048
D_FF = 2816
FF_CHUNK = 256
GRID_W = 64
ROPE_THETA = 10000.0
EPS = 1e-6
Q_SCALE = HEAD_DIM ** -0.5 * 1.4426950408889634

TM = 256
SCORE_LOOKAHEAD = 4
SCORE_SLOTS = SCORE_LOOKAHEAD + 1
UP_LOOKAHEAD = 2
UP_SLOTS = UP_LOOKAHEAD + 1
LANES = 128
F32_SUBLANES = 8
BF16_SUBLANES = 16
N_MOD = 6
MOD_ROWS = 8

OFF_Q = 0
OFF_K = OFF_Q + ATTN_WIDTH
OFF_V = OFF_K + KV_WIDTH
OFF_R = OFF_V + KV_WIDTH
OFF_G = OFF_R + LRU_WIDTH
OFF_B = OFF_G + LRU_WIDTH
OFF_C = OFF_B + SC_WIDTH
OFF_U = OFF_C + SC_WIDTH


def _const_spec(shape):
    nd = len(shape)
    return pl.BlockSpec(shape, lambda *_: (0,) * nd, pipeline_mode=pl.Buffered(1))


def _params(vmem_mb, n_grid_dims=2):
    return pltpu.CompilerParams(dimension_semantics=("arbitrary",) * n_grid_dims,
                                vmem_limit_bytes=vmem_mb * 1024 * 1024)


def _rms_scale(t):
    return lax.rsqrt(jnp.mean(t * t, axis=-1, keepdims=True) + EPS)


def _mod_kernel(c_ref, w_ref, b_ref, o_ref):
    c = c_ref[...]
    h = (c * jax.nn.sigmoid(c)).astype(BF16)
    o_ref[...] = jnp.dot(h, w_ref[...].astype(BF16), preferred_element_type=F32) + b_ref[...]


def _modulation(cc, w_mod, b_mod):
    rows = cc.shape[0]
    depth = w_mod.shape[0]
    return pl.pallas_call(
        _mod_kernel,
        grid=(depth, N_MOD),
        in_specs=[pl.BlockSpec((rows, D_MODEL), lambda l, j: (0, 0)),
                  pl.BlockSpec((None, D_MODEL, D_MODEL), lambda l, j: (l, 0, j)),
                  pl.BlockSpec((None, 1, D_MODEL), lambda l, j: (l, 0, j))],
        out_specs=pl.BlockSpec((None, rows, D_MODEL), lambda l, j: (l, 0, j)),
        out_shape=jax.ShapeDtypeStruct((depth, rows, N_MOD * D_MODEL), F32),
        name="modulation",
    )(cc, w_mod, b_mod[:, None, :])


def _norm_rope_transposed(t, gain, cos, sin):
    half = HEAD_DIM // 2
    out = []
    for h in range(t.shape[0] // HEAD_DIM):
        r = t[h * HEAD_DIM:(h + 1) * HEAD_DIM]
        r = r * lax.rsqrt(jnp.mean(r * r, axis=0, keepdims=True) + EPS) * gain
        x1, x2 = r[:half], r[half:]
        out += [x1 * cos - x2 * sin, x1 * sin + x2 * cos]
    return jnp.concatenate(out, axis=0)


def _inproj_kernel(ctx_ref, x_ref, mod_ref, gmix_ref, w_ref, gq_ref, gk_ref, cos_ref, sin_ref,
                   qT_ref, k_ref, vT_ref, rest_ref, y_even_ref, y_odd_ref, *, nt, n_tiles):
    s = pl.program_id(0)
    tile = jnp.minimum(s, n_tiles - 1) % nt

    def project(y_ref):
        x = jnp.where(tile == 0, ctx_ref[0], x_ref[0])
        mod = mod_ref[0, 0]
        shift, scale = mod[0:1], mod[1:2]
        h = x * _rms_scale(x) * gmix_ref[...]
        h = h * (1.0 + scale) + shift
        y_ref[...] = jnp.dot(h.astype(BF16), w_ref[...], preferred_element_type=F32)

    def finish(y_ref):
        cos, sin = cos_ref[...], sin_ref[...]
        q = _norm_rope_transposed(y_ref[:, OFF_Q:OFF_Q + ATTN_WIDTH].T, gq_ref[...], cos, sin)
        qT_ref[0] = q.astype(BF16)
        k = _norm_rope_transposed(y_ref[:, OFF_K:OFF_K + KV_WIDTH].T, gk_ref[...], cos, sin)
        k_ref[0] = k.T.astype(BF16)
        vT_ref[0] = y_ref[:, OFF_V:OFF_V + KV_WIDTH].T.astype(BF16)
        rest_ref[0, :, 0:LRU_WIDTH] = y_ref[:, OFF_R:OFF_R + LRU_WIDTH]
        rest_ref[0, :, LRU_WIDTH:2 * LRU_WIDTH] = jax.nn.gelu(y_ref[:, OFF_G:OFF_G + LRU_WIDTH])
        rest_ref[0, :, 2 * LRU_WIDTH:2 * LRU_WIDTH + SC_WIDTH] = y_ref[:, OFF_B:OFF_B + SC_WIDTH]
        rest_ref[0, :, 2 * LRU_WIDTH + SC_WIDTH:] = (y_ref[:, OFF_C:OFF_C + SC_WIDTH]
                                                     * y_ref[:, OFF_U:OFF_U + SC_WIDTH])

    @pl.when(s == 0)
    def _():
        y_odd_ref[...] = jnp.zeros_like(y_odd_ref)

    @pl.when(s % 2 == 0)
    def _():
        finish(y_odd_ref)
        project(y_even_ref)

    @pl.when(s % 2 == 1)
    def _():
        finish(y_even_ref)
        project(y_odd_ref)


def _token_specs(x_base, tile_offset=0):
    return [pl.BlockSpec((1, TM, D_MODEL), lambda b, i: (b, 0, 0)),
            pl.BlockSpec((1, TM, D_MODEL),
                         lambda b, i: (b, x_base + jnp.maximum(i + tile_offset - 1, 0), 0))]


def _inproj(ctx_arr, x_arr, x_base, mods, g_mix, w_in, gq, gk, cos_t, sin_t):
    bsz = x_arr.shape[0]
    nt = x_arr.shape[1] // TM - x_base + 1
    t_all = nt * TM
    n_tiles = bsz * nt
    rest_w = 2 * LRU_WIDTH + 2 * SC_WIDTH

    def cur(s):
        sc = jnp.minimum(s, n_tiles - 1)
        return sc // nt, sc % nt

    def prev(s):
        sp = jnp.maximum(s - 1, 0)
        return sp // nt, sp % nt

    rope_spec = pl.BlockSpec((HEAD_DIM // 2, TM), lambda s: (0, prev(s)[1]))
    return pl.pallas_call(
        functools.partial(_inproj_kernel, nt=nt, n_tiles=n_tiles),
        grid=(n_tiles + 1,),
        in_specs=[pl.BlockSpec((1, TM, D_MODEL), lambda s: (cur(s)[0], 0, 0)),
                  pl.BlockSpec((1, TM, D_MODEL),
                               lambda s: (cur(s)[0], x_base + jnp.maximum(cur(s)[1] - 1, 0), 0)),
                  pl.BlockSpec((1, 1, MOD_ROWS, D_MODEL),
                               lambda s: (cur(s)[0], jnp.minimum(cur(s)[1], 1), 0, 0)),
                  _const_spec((1, D_MODEL)),
                  _const_spec((D_MODEL, IN_WIDTH)),
                  _const_spec((HEAD_DIM, TM)),
                  _const_spec((HEAD_DIM, TM)),
                  rope_spec,
                  rope_spec],
        out_specs=[pl.BlockSpec((1, ATTN_WIDTH, TM), lambda s: (prev(s)[0], 0, prev(s)[1])),
                   pl.BlockSpec((1, TM, KV_WIDTH), lambda s: (prev(s)[0], prev(s)[1], 0)),
                   pl.BlockSpec((1, KV_WIDTH, TM), lambda s: (prev(s)[0], 0, prev(s)[1])),
                   pl.BlockSpec((1, TM, rest_w), lambda s: (prev(s)[0], prev(s)[1], 0))],
        out_shape=[jax.ShapeDtypeStruct((bsz, ATTN_WIDTH, t_all), BF16),
                   jax.ShapeDtypeStruct((bsz, t_all, KV_WIDTH), BF16),
                   jax.ShapeDtypeStruct((bsz, KV_WIDTH, t_all), BF16),
                   jax.ShapeDtypeStruct((bsz, t_all, rest_w), F32)],
        scratch_shapes=[pltpu.VMEM((TM, IN_WIDTH), F32)] * 2,
        compiler_params=_params(40, n_grid_dims=1),
        name="inproj",
    )(ctx_arr, x_arr, mods, g_mix, w_in, gq, gk, cos_t, sin_t)


def _with_halo(prev_ref, cur_ref, next_ref, valid_prev, valid_next):
    prev = prev_ref[0] * valid_prev.astype(F32)
    nxt = next_ref[0] * valid_next.astype(F32)
    return jnp.concatenate([prev, cur_ref[0], nxt], axis=0)


def _shifted(tcat, offset):
    rows = tcat.shape[0]
    if offset == 0:
        return tcat[F32_SUBLANES:F32_SUBLANES + TM]
    return pltpu.roll(tcat, (-offset) % rows, 0)[F32_SUBLANES:F32_SUBLANES + TM]


def _lru_coeffs(rcat, conv_w, conv_b, w_gate, b_gate, sp_lam):
    xc = conv_b
    for j in range(LRU_CONV):
        xc = xc + conv_w[j:j + 1] * _shifted(rcat, j - 2)
    z = jnp.dot(xc.astype(BF16), w_gate, preferred_element_type=F32) + b_gate
    r_gate = jax.nn.sigmoid(z[:, :LRU_WIDTH])
    i_gate = jax.nn.sigmoid(z[:, LRU_WIDTH:])
    log_a = (-LRU_C) * r_gate * sp_lam
    a = jnp.exp(log_a)
    one_minus_a2 = -jnp.tanh(log_a) * (1.0 + a * a)
    return a, jnp.sqrt(one_minus_a2) * (i_gate * xc)


def _lru_kernel(rf_ref, rfp_ref, rfn_ref, rb_ref, rbp_ref, rbn_ref, bg_ref, cu_ref, cup_ref, cun_ref,
                cw_ref, cb_ref, wg_ref, bgate_ref, lam_ref, scw_ref,
                hf_ref, hb_ref, sc_ref,
                af_s, bf_s, ab_s, bb_s, hf_state, hb_state, *, nt):
    i = pl.program_id(1)
    j = jnp.where(i == 0, 0, nt - i)

    @pl.when(i == 0)
    def _():
        hf_state[...] = jnp.zeros_like(hf_state)
        hb_state[...] = jnp.zeros_like(hb_state)

    def halo_valid(t):
        return jnp.logical_and(t != 0, t != 1), jnp.logical_and(t != 0, t != nt - 1)

    lam = lam_ref[...]
    sp_lam = jnp.maximum(-lam, 0.0) + jnp.log1p(jnp.exp(-jnp.abs(lam)))
    cw = cw_ref[...]
    cb = cb_ref[...]

    vp, vn = halo_valid(i)
    a, b = _lru_coeffs(_with_halo(rfp_ref, rf_ref, rfn_ref, vp, vn), cw, cb,
                       wg_ref[:, :2 * LRU_WIDTH], bgate_ref[:, :2 * LRU_WIDTH], sp_lam[0:1])
    _block_prefix(a, b, af_s, bf_s, reverse=False)
    cucat = _with_halo(cup_ref, cu_ref, cun_ref, vp, vn)
    scw = scw_ref[...]
    conv = scw[0:1] * _shifted(cucat, -1) + scw[1:2] * _shifted(cucat, 0) + scw[2:3] * _shifted(cucat, 1)
    sc_ref[0] = (bg_ref[0] * conv).astype(BF16)

    vp, vn = halo_valid(j)
    a, b = _lru_coeffs(_with_halo(rbp_ref, rb_ref, rbn_ref, vp, vn), cw, cb,
                       wg_ref[:, 2 * LRU_WIDTH:], bgate_ref[:, 2 * LRU_WIDTH:], sp_lam[1:2])
    _block_prefix(a, b, ab_s, bb_s, reverse=True)

    _carry_blocks(af_s, bf_s, hf_ref, hf_state, reverse=False)
    _carry_blocks(ab_s, bb_s, hb_ref, hb_state, reverse=True)


def _block_prefix(a, b, a_s, b_s, reverse):
    n = F32_SUBLANES
    a = a.reshape(TM // n, n, LRU_WIDTH)
    b = b.reshape(TM // n, n, LRU_WIDTH)
    row = lax.broadcasted_iota(jnp.int32, a.shape, 1)
    d = 1
    while d < n:
        shift, keep = (n - d, row < n - d) if reverse else (d, row >= d)
        a_prev = jnp.where(keep, pltpu.roll(a, shift, 1), 1.0)
        b_prev = jnp.where(keep, pltpu.roll(b, shift, 1), 0.0)
        b = a * b_prev + b
        a = a * a_prev
        d *= 2
    a_s[...] = a
    b_s[...] = b


def _carry_blocks(a_s, b_s, out_ref, state_ref, reverse):
    n = F32_SUBLANES
    nb = TM // n
    last = 0 if reverse else n - 1
    h_in = state_ref[...]
    for v in (reversed(range(nb)) if reverse else range(nb)):
        h = a_s[v] * h_in + b_s[v]
        out_ref[0, v * n:(v + 1) * n, :] = h
        h_in = h[last:last + 1]
    state_ref[...] = h_in


def _lru(rest, conv_w, conv_b, w_gate, b_gate, lam, sc_w):
    bsz, t_all, _ = rest.shape
    nt = t_all // TM
    hpt = TM // F32_SUBLANES
    nhb = t_all // F32_SUBLANES

    def rev(i):
        return jnp.where(i == 0, 0, nt - i)

    def tile(col, order):
        return pl.BlockSpec((1, TM, LRU_WIDTH), lambda b, i: (b, order(i), col))

    def prev_halo(col, order):
        return pl.BlockSpec((1, F32_SUBLANES, LRU_WIDTH),
                            lambda b, i: (b, jnp.maximum(order(i) * hpt - 1, 0), col))

    def next_halo(col, order):
        return pl.BlockSpec((1, F32_SUBLANES, LRU_WIDTH),
                            lambda b, i: (b, jnp.minimum((order(i) + 1) * hpt, nhb - 1), col))

    fwd = lambda i: i
    out_f32 = jax.ShapeDtypeStruct((bsz, t_all, LRU_WIDTH), F32)
    return pl.pallas_call(
        functools.partial(_lru_kernel, nt=nt),
        grid=(bsz, nt),
        in_specs=[tile(0, fwd), prev_halo(0, fwd), next_halo(0, fwd),
                  tile(0, rev), prev_halo(0, rev), next_halo(0, rev),
                  tile(2, fwd),
                  tile(3, fwd), prev_halo(3, fwd), next_halo(3, fwd),
                  _const_spec((LRU_CONV, LRU_WIDTH)),
                  _const_spec((1, LRU_WIDTH)),
                  _const_spec((LRU_WIDTH, 4 * LRU_WIDTH)),
                  _const_spec((1, 4 * LRU_WIDTH)),
                  _const_spec((2, LRU_WIDTH)),
                  _const_spec((3, SC_WIDTH))],
        out_specs=[pl.BlockSpec((1, TM, LRU_WIDTH), lambda b, i: (b, i, 0)),
                   pl.BlockSpec((1, TM, LRU_WIDTH), lambda b, i: (b, rev(i), 0)),
                   pl.BlockSpec((1, TM, SC_WIDTH), lambda b, i: (b, i, 0))],
        out_shape=[out_f32, out_f32, jax.ShapeDtypeStruct((bsz, t_all, SC_WIDTH), BF16)],
        scratch_shapes=([pltpu.VMEM((TM // F32_SUBLANES, F32_SUBLANES, LRU_WIDTH), F32)] * 4
                        + [pltpu.VMEM((1, LRU_WIDTH), F32)] * 2),
        compiler_params=_params(32),
        name="lru_scan",
    )(rest, rest, rest, rest, rest, rest, rest, rest, rest, rest,
      conv_w, conv_b, w_gate, b_gate, lam, sc_w)


def _mixer_kernel(qT_ref, k_ref, vT_ref, ctx_ref, x_ref, hf_ref, hb_ref, gg_ref, sc_ref, mod_ref, wo_ref,
                  gffn_ref, x1_ref, h2_ref, oT_ref, s_ref, *, tile_offset, n_key_tiles):
    i = pl.program_id(1) + tile_offset
    ones_rows = jnp.ones((BF16_SUBLANES, TM), BF16)

    def project_out():
        x = x_ref[0]
        if tile_offset == 0:
            x = jnp.where(i == 0, ctx_ref[0], x)
        mod = mod_ref[0, 0]
        gate_mix, shift, scale = mod[2:3], mod[3:4], mod[4:5]
        att = oT_ref[...].T.astype(BF16)
        lru = (gg_ref[0] * (hf_ref[0] + hb_ref[0])).astype(BF16)
        y = (jnp.dot(att, wo_ref[0:ATTN_WIDTH], preferred_element_type=F32)
             + jnp.dot(lru, wo_ref[ATTN_WIDTH:ATTN_WIDTH + LRU_WIDTH], preferred_element_type=F32)
             + jnp.dot(sc_ref[0], wo_ref[ATTN_WIDTH + LRU_WIDTH:], preferred_element_type=F32))
        x1 = x + gate_mix * y
        x1_ref[0] = x1
        h2 = x1 * _rms_scale(x1) * gffn_ref[...]
        h2_ref[0] = (h2 * (1.0 + scale) + shift).astype(BF16)

    def run(n_tiles):
        items = [(pr, c) for pr in range(N_HEADS // 2) for c in range(n_tiles)]
        qpads = {}

        def qpad_of(pr):
            if pr not in qpads:
                row0 = pr * 2 * HEAD_DIM
                q2 = jnp.concatenate([qT_ref[0, row0:row0 + HEAD_DIM, :],
                                      qT_ref[0, row0 + HEAD_DIM:row0 + 2 * HEAD_DIM, :]], axis=1)
                zero = jnp.zeros_like(q2)
                first_kv = (2 * pr) // HEADS_PER_KV == 0
                qpads[pr] = jnp.concatenate([q2, zero] if first_kv else [zero, q2], axis=0)
            return qpads[pr]

        def scores(t):
            pr, c = items[t]
            s = jnp.dot(k_ref[0, c * TM:(c + 1) * TM, :], qpad_of(pr), preferred_element_type=F32)
            s_ref[t % SCORE_SLOTS] = s
            return jnp.max(s, axis=0, keepdims=True)

        tile_max = [scores(t) for t in range(min(SCORE_LOOKAHEAD, len(items)))]
        m = acc = None
        for t, (pr, c) in enumerate(items):
            if t + SCORE_LOOKAHEAD < len(items):
                tile_max.append(scores(t + SCORE_LOOKAHEAD))
            if c == 0:
                m = jnp.full((1, 2 * TM), -jnp.inf, F32)
                acc = jnp.zeros((HEAD_DIM + BF16_SUBLANES, 2 * TM), F32)
            g = (2 * pr) // HEADS_PER_KV
            m_new = jnp.maximum(m, tile_max[t])
            alpha = jnp.exp2(m - m_new)
            p = jnp.exp2(s_ref[t % SCORE_SLOTS] - m_new).astype(BF16)
            vt = jnp.concatenate([vT_ref[0, g * HEAD_DIM:(g + 1) * HEAD_DIM, c * TM:(c + 1) * TM],
                                  ones_rows], axis=0)
            acc = alpha * acc + jnp.dot(vt, p, preferred_element_type=F32)
            m = m_new
            tile_max[t] = None
            if c == n_tiles - 1:
                o = acc[:HEAD_DIM] / acc[HEAD_DIM:HEAD_DIM + 1]
                row0 = pr * 2 * HEAD_DIM
                oT_ref[row0:row0 + HEAD_DIM, :] = o[:, :TM]
                oT_ref[row0 + HEAD_DIM:row0 + 2 * HEAD_DIM, :] = o[:, TM:]
        project_out()

    @pl.when(i == 0)
    def _():
        run(1)

    @pl.when(i != 0)
    def _():
        run(n_key_tiles)


def _mixer(qT, k, vT, ctx_arr, x_arr, x_base, hf, hb, rest, sc, mods, w_out, g_ffn, tile_offset):
    bsz, t_all, _ = k.shape
    nt = t_all // TM
    nq = nt - tile_offset
    off = lambda b, i: (b, i + tile_offset, 0)
    return pl.pallas_call(
        functools.partial(_mixer_kernel, tile_offset=tile_offset, n_key_tiles=nt),
        grid=(bsz, nq),
        in_specs=[pl.BlockSpec((1, ATTN_WIDTH, TM), lambda b, i: (b, 0, i + tile_offset)),
                  pl.BlockSpec((1, t_all, KV_WIDTH), lambda b, i: (b, 0, 0)),
                  pl.BlockSpec((1, KV_WIDTH, t_all), lambda b, i: (b, 0, 0))]
                 + _token_specs(x_base, tile_offset) + [
                  pl.BlockSpec((1, TM, LRU_WIDTH), off),
                  pl.BlockSpec((1, TM, LRU_WIDTH), off),
                  pl.BlockSpec((1, TM, LRU_WIDTH), lambda b, i: (b, i + tile_offset, 1)),
                  pl.BlockSpec((1, TM, SC_WIDTH), off),
                  pl.BlockSpec((1, 1, MOD_ROWS, D_MODEL),
                               lambda b, i: (b, jnp.minimum(i + tile_offset, 1), 0, 0)),
                  _const_spec((D_MODEL, D_MODEL)),
                  _const_spec((1, D_MODEL))],
        out_specs=[pl.BlockSpec((1, TM, D_MODEL), lambda b, i: (b, i, 0)),
                   pl.BlockSpec((1, TM, D_MODEL), lambda b, i: (b, i, 0))],
        out_shape=[jax.ShapeDtypeStruct((bsz, nq * TM, D_MODEL), F32),
                   jax.ShapeDtypeStruct((bsz, nq * TM, D_MODEL), BF16)],
        scratch_shapes=[pltpu.VMEM((ATTN_WIDTH, TM), F32), pltpu.VMEM((SCORE_SLOTS, TM, 2 * TM), F32)],
        compiler_params=_params(40),
        name="mixer",
    )(qT, k, vT, ctx_arr, x_arr, hf, hb, rest, sc, mods, w_out, g_ffn)


def _ffn_kernel(h_ref, hp_ref, hn_ref, x1_ref, mod_ref, wup_ref, cw_ref, wdn_ref, gfin_ref,
                o_ref, acc_ref, up_ref, *, ctx_tiles, nt, final_norm):
    i = pl.program_id(1)
    valid_prev = jnp.logical_and(i != 0, i != ctx_tiles)
    valid_next = jnp.logical_and(i != nt - 1, i != ctx_tiles - 1)
    hp = hp_ref[0] * valid_prev.astype(BF16)
    hn = hn_ref[0] * valid_next.astype(BF16)
    hcat = jnp.concatenate([hp, h_ref[0], hn], axis=0)
    rows = hcat.shape[0]

    def conv(t, w):
        lo, hi = BF16_SUBLANES, BF16_SUBLANES + TM
        return (w[0:1] * pltpu.roll(t, 1, 0)[lo:hi] + w[1:2] * t[lo:hi]
                + w[2:3] * pltpu.roll(t, rows - 1, 0)[lo:hi])

    n_chunks = D_FF // FF_CHUNK

    def up_project(c):
        cu = slice(c * FF_CHUNK, (c + 1) * FF_CHUNK)
        cg = slice(D_FF + c * FF_CHUNK, D_FF + (c + 1) * FF_CHUNK)
        up_ref[c % UP_SLOTS, :, :FF_CHUNK] = jnp.dot(hcat, wup_ref[:, cu], preferred_element_type=F32)
        up_ref[c % UP_SLOTS, :, FF_CHUNK:] = jnp.dot(hcat, wup_ref[:, cg], preferred_element_type=F32)

    for c in range(min(UP_LOOKAHEAD, n_chunks)):
        up_project(c)
    for c in range(n_chunks):
        if c + UP_LOOKAHEAD < n_chunks:
            up_project(c + UP_LOOKAHEAD)
        cu = slice(c * FF_CHUNK, (c + 1) * FF_CHUNK)
        cg = slice(D_FF + c * FF_CHUNK, D_FF + (c + 1) * FF_CHUNK)
        u = conv(up_ref[c % UP_SLOTS, :, :FF_CHUNK], cw_ref[:, cu])
        g = conv(up_ref[c % UP_SLOTS, :, FF_CHUNK:], cw_ref[:, cg])
        act = ((g * jax.nn.sigmoid(g)) * u).astype(BF16)
        part = jnp.dot(act, wdn_ref[cu, :], preferred_element_type=F32)
        if c == 0:
            acc_ref[...] = part
        else:
            acc_ref[...] += part

    gate_ffn = mod_ref[0, 0][5:6]
    x2 = x1_ref[0] + gate_ffn * acc_ref[...]
    if final_norm:
        x2 = x2 * _rms_scale(x2) * gfin_ref[...]
    o_ref[0] = x2


def _ffn(h2, x1, mods, w_up, conv_w, w_down, g_final, ctx_tiles, final_norm):
    bsz, t_len, _ = h2.shape
    nt = t_len // TM
    hpt = TM // BF16_SUBLANES
    nhb = t_len // BF16_SUBLANES
    return pl.pallas_call(
        functools.partial(_ffn_kernel, ctx_tiles=ctx_tiles, nt=nt, final_norm=final_norm),
        grid=(bsz, nt),
        in_specs=[pl.BlockSpec((1, TM, D_MODEL), lambda b, i: (b, i, 0)),
                  pl.BlockSpec((1, BF16_SUBLANES, D_MODEL),
                               lambda b, i: (b, jnp.maximum(i * hpt - 1, 0), 0)),
                  pl.BlockSpec((1, BF16_SUBLANES, D_MODEL),
                               lambda b, i: (b, jnp.minimum((i + 1) * hpt, nhb - 1), 0)),
                  pl.BlockSpec((1, TM, D_MODEL), lambda b, i: (b, i, 0)),
                  pl.BlockSpec((1, 1, MOD_ROWS, D_MODEL),
                               lambda b, i: (b, jnp.minimum(i + 1 - ctx_tiles, 1), 0, 0)),
                  _const_spec((D_MODEL, 2 * D_FF)),
                  _const_spec((3, 2 * D_FF)),
                  _const_spec((D_FF, D_MODEL)),
                  _const_spec((1, D_MODEL))],
        out_specs=pl.BlockSpec((1, TM, D_MODEL), lambda b, i: (b, i, 0)),
        out_shape=jax.ShapeDtypeStruct((bsz, t_len, D_MODEL), F32),
        scratch_shapes=[pltpu.VMEM((TM, D_MODEL), F32),
                        pltpu.VMEM((UP_SLOTS, TM + 2 * BF16_SUBLANES, 2 * FF_CHUNK), F32)],
        compiler_params=_params(48),
        name="conv_ffn",
    )(h2, h2, h2, x1, mods, w_up, conv_w, w_down, g_final)


def _rope_tables(seq, ctx_len):
    rows = seq // GRID_W
    pos_r = np.repeat(np.arange(rows, dtype=np.float32), GRID_W)
    pos_c = np.tile(np.arange(GRID_W, dtype=np.float32), rows)
    n_f = HEAD_DIM // 4
    inv = (np.float32(ROPE_THETA) ** (-np.arange(n_f, dtype=np.float32) / n_f)).astype(np.float32)
    ang = np.concatenate([pos_r[:, None] * inv, pos_c[:, None] * inv], axis=-1).astype(np.float32)
    cos = np.concatenate([np.ones((ctx_len, HEAD_DIM // 2)), np.cos(ang.astype(np.float64))], axis=0)
    sin = np.concatenate([np.zeros((ctx_len, HEAD_DIM // 2)), np.sin(ang.astype(np.float64))], axis=0)
    return jnp.asarray(cos.T, F32), jnp.asarray(sin.T, F32)


def _gate_weights(wa, ba, wi, bi):
    def dense(w):
        eye = jnp.eye(LRU_BLOCKS, dtype=w.dtype)
        return jnp.einsum("nde,nm->ndme", w, eye).reshape(LRU_WIDTH, LRU_WIDTH)
    w = jnp.concatenate([dense(wa[0]), dense(wi[0]), dense(wa[1]), dense(wi[1])], axis=1)
    b = jnp.concatenate([ba[0], bi[0], ba[1], bi[1]])[None, :]
    return w.astype(BF16), b


def kernel(x, c, ctx, c_ctx, w_mod, b_mod, g_mix, g_ffn, w_in, g_q, g_k, lru_conv_w, lru_conv_b,
           lru_wa, lru_ba, lru_wi, lru_bi, lru_lam, sc_conv_w, w_out, w_up, ffn_conv_w, w_down,
           g_final):
    bsz, seq, _ = x.shape
    ctx_len = ctx.shape[1]
    depth = w_mod.shape[0]
    assert ctx_len == TM and seq % TM == 0 and seq % GRID_W == 0

    cos_t, sin_t = _rope_tables(seq, ctx_len)
    ctx_arr, x_arr, x_base = ctx, x, 0
    cc_rows = -(-(bsz + 1) // F32_SUBLANES) * F32_SUBLANES
    cc = jnp.zeros((cc_rows, D_MODEL), F32).at[:bsz].set(c).at[bsz].set(c_ctx)

    mod_layers = _modulation(cc, w_mod, b_mod)

    out = None
    for l in range(depth):
        last = l == depth - 1
        mod_all = mod_layers[l].reshape(cc_rows, N_MOD, D_MODEL)
        mod_x = mod_all[:bsz]
        mod_c = jnp.broadcast_to(mod_all[bsz][None], (bsz, N_MOD, D_MODEL))
        mods = jnp.stack([mod_c, mod_x], axis=1)
        mods = jnp.pad(mods, ((0, 0), (0, 0), (0, MOD_ROWS - N_MOD), (0, 0)))

        gq = jnp.broadcast_to((g_q[l] * Q_SCALE)[:, None], (HEAD_DIM, TM))
        gk = jnp.broadcast_to(g_k[l][:, None], (HEAD_DIM, TM))
        qT, k, vT, rest = _inproj(ctx_arr, x_arr, x_base, mods, g_mix[l][None, :],
                                  w_in[l].astype(BF16), gq, gk, cos_t, sin_t)

        w_gate, b_gate = _gate_weights(lru_wa[l], lru_ba[l], lru_wi[l], lru_bi[l])
        hf, hb, sc = _lru(rest, lru_conv_w[l], lru_conv_b[l][None, :], w_gate, b_gate, lru_lam[l],
                          sc_conv_w[l])

        tile_offset = 1 if last else 0
        x1, h2 = _mixer(qT, k, vT, ctx_arr, x_arr, x_base, hf, hb, rest, sc, mods,
                        w_out[l].astype(BF16), g_ffn[l][None, :], tile_offset)
        out = _ffn(h2, x1, mods, w_up[l].astype(BF16), ffn_conv_w[l], w_down[l].astype(BF16),
                   g_final[None, :], ctx_tiles=1 - tile_offset, final_norm=last)
        ctx_arr, x_arr, x_base = out, out, 1
    return out
```

```python
import functools

import numpy as np
import jax
import jax.numpy as jnp
from jax import lax
from jax.experimental import pallas as pl
from jax.experimental.pallas import tpu as pltpu

F32 = jnp.float32
BF16 = jnp.bfloat16

D_MODEL = 1024
HEAD_DIM = 64
N_HEADS = 8
N_KV_HEADS = 2
HEADS_PER_KV = N_HEADS // N_KV_HEADS
ATTN_WIDTH = N_HEADS * HEAD_DIM
KV_WIDTH = N_KV_HEADS * HEAD_DIM
LRU_WIDTH = 256
LRU_BLOCKS = 4
LRU_BLOCK = LRU_WIDTH // LRU_BLOCKS
LRU_CONV = 4
LRU_C = 8.0
SC_WIDTH = 256
IN_WIDTH = 2048
D_FF = 2816
FF_CHUNK = 256
GRID_W = 64
ROPE_THETA = 10000.0
EPS = 1e-6
Q_SCALE = HEAD_DIM ** -0.5 * 1.4426950408889634

TM = 256
SCORE_LOOKAHEAD = 3
SCORE_SLOTS = SCORE_LOOKAHEAD + 1
UP_LOOKAHEAD = 3
UP_SLOTS = UP_LOOKAHEAD + 1
LANES = 128
F32_SUBLANES = 8
BF16_SUBLANES = 16
N_MOD = 6
MOD_ROWS = 8

OFF_Q = 0
OFF_K = OFF_Q + ATTN_WIDTH
OFF_V = OFF_K + KV_WIDTH
OFF_R = OFF_V + KV_WIDTH
OFF_G = OFF_R + LRU_WIDTH
OFF_B = OFF_G + LRU_WIDTH
OFF_C = OFF_B + SC_WIDTH
OFF_U = OFF_C + SC_WIDTH


def _const_spec(shape):
    nd = len(shape)
    return pl.BlockSpec(shape, lambda *_: (0,) * nd, pipeline_mode=pl.Buffered(1))


def _params(vmem_mb, n_grid_dims=2):
    return pltpu.CompilerParams(dimension_semantics=("arbitrary",) * n_grid_dims,
                                vmem_limit_bytes=vmem_mb * 1024 * 1024)


def _rms_scale(t):
    return lax.rsqrt(jnp.mean(t * t, axis=-1, keepdims=True) + EPS)


def _mod_kernel(c_ref, w_ref, b_ref, o_ref):
    c = c_ref[...]
    h = (c * jax.nn.sigmoid(c)).astype(BF16)
    o_ref[...] = jnp.dot(h, w_ref[...].astype(BF16), preferred_element_type=F32) + b_ref[...]


def _modulation(cc, w_mod, b_mod):
    rows = cc.shape[0]
    depth = w_mod.shape[0]
    return pl.pallas_call(
        _mod_kernel,
        grid=(depth, N_MOD),
        in_specs=[pl.BlockSpec((rows, D_MODEL), lambda l, j: (0, 0)),
                  pl.BlockSpec((None, D_MODEL, D_MODEL), lambda l, j: (l, 0, j)),
                  pl.BlockSpec((None, 1, D_MODEL), lambda l, j: (l, 0, j))],
        out_specs=pl.BlockSpec((None, rows, D_MODEL), lambda l, j: (l, 0, j)),
        out_shape=jax.ShapeDtypeStruct((depth, rows, N_MOD * D_MODEL), F32),
        name="modulation",
    )(cc, w_mod, b_mod[:, None, :])


def _norm_rope_transposed(t, gain, cos, sin):
    half = HEAD_DIM // 2
    out = []
    for h in range(t.shape[0] // HEAD_DIM):
        r = t[h * HEAD_DIM:(h + 1) * HEAD_DIM]
        r = r * lax.rsqrt(jnp.mean(r * r, axis=0, keepdims=True) + EPS) * gain
        x1, x2 = r[:half], r[half:]
        out += [x1 * cos - x2 * sin, x1 * sin + x2 * cos]
    return jnp.concatenate(out, axis=0)


def _inproj_kernel(ctx_ref, x_ref, mod_ref, gmix_ref, w_ref, gq_ref, gk_ref, cos_ref, sin_ref,
                   qT_ref, k_ref, vT_ref, rest_ref, y_even_ref, y_odd_ref, *, nt, n_tiles):
    s = pl.program_id(0)
    tile = jnp.minimum(s, n_tiles - 1) % nt

    def project(y_ref):
        x = jnp.where(tile == 0, ctx_ref[0], x_ref[0])
        mod = mod_ref[0, 0]
        shift, scale = mod[0:1], mod[1:2]
        h = x * _rms_scale(x) * gmix_ref[...]
        h = h * (1.0 + scale) + shift
        y_ref[...] = jnp.dot(h.astype(BF16), w_ref[...], preferred_element_type=F32)

    def finish(y_ref):
        cos, sin = cos_ref[...], sin_ref[...]
        q = _norm_rope_transposed(y_ref[:, OFF_Q:OFF_Q + ATTN_WIDTH].T, gq_ref[...], cos, sin)
        qT_ref[0] = q.astype(BF16)
        k = _norm_rope_transposed(y_ref[:, OFF_K:OFF_K + KV_WIDTH].T, gk_ref[...], cos, sin)
        k_ref[0] = k.T.astype(BF16)
        vT_ref[0] = y_ref[:, OFF_V:OFF_V + KV_WIDTH].T.astype(BF16)
        rest_ref[0, :, 0:LRU_WIDTH] = y_ref[:, OFF_R:OFF_R + LRU_WIDTH]
        rest_ref[0, :, LRU_WIDTH:2 * LRU_WIDTH] = jax.nn.gelu(y_ref[:, OFF_G:OFF_G + LRU_WIDTH])
        rest_ref[0, :, 2 * LRU_WIDTH:2 * LRU_WIDTH + SC_WIDTH] = y_ref[:, OFF_B:OFF_B + SC_WIDTH]
        rest_ref[0, :, 2 * LRU_WIDTH + SC_WIDTH:] = (y_ref[:, OFF_C:OFF_C + SC_WIDTH]
                                                     * y_ref[:, OFF_U:OFF_U + SC_WIDTH])

    @pl.when(s == 0)
    def _():
        y_odd_ref[...] = jnp.zeros_like(y_odd_ref)

    @pl.when(s % 2 == 0)
    def _():
        finish(y_odd_ref)
        project(y_even_ref)

    @pl.when(s % 2 == 1)
    def _():
        finish(y_even_ref)
        project(y_odd_ref)


def _token_specs(x_base, tile_offset=0):
    return [pl.BlockSpec((1, TM, D_MODEL), lambda b, i: (b, 0, 0)),
            pl.BlockSpec((1, TM, D_MODEL),
                         lambda b, i: (b, x_base + jnp.maximum(i + tile_offset - 1, 0), 0))]


def _inproj(ctx_arr, x_arr, x_base, mods, g_mix, w_in, gq, gk, cos_t, sin_t):
    bsz = x_arr.shape[0]
    nt = x_arr.shape[1] // TM - x_base + 1
    t_all = nt * TM
    n_tiles = bsz * nt
    rest_w = 2 * LRU_WIDTH + 2 * SC_WIDTH

    def cur(s):
        sc = jnp.minimum(s, n_tiles - 1)
        return sc // nt, sc % nt

    def prev(s):
        sp = jnp.maximum(s - 1, 0)
        return sp // nt, sp % nt

    rope_spec = pl.BlockSpec((HEAD_DIM // 2, TM), lambda s: (0, prev(s)[1]))
    return pl.pallas_call(
        functools.partial(_inproj_kernel, nt=nt, n_tiles=n_tiles),
        grid=(n_tiles + 1,),
        in_specs=[pl.BlockSpec((1, TM, D_MODEL), lambda s: (cur(s)[0], 0, 0)),
                  pl.BlockSpec((1, TM, D_MODEL),
                               lambda s: (cur(s)[0], x_base + jnp.maximum(cur(s)[1] - 1, 0), 0)),
                  pl.BlockSpec((1, 1, MOD_ROWS, D_MODEL),
                               lambda s: (cur(s)[0], jnp.minimum(cur(s)[1], 1), 0, 0)),
                  _const_spec((1, D_MODEL)),
                  _const_spec((D_MODEL, IN_WIDTH)),
                  _const_spec((HEAD_DIM, TM)),
                  _const_spec((HEAD_DIM, TM)),
                  rope_spec,
                  rope_spec],
        out_specs=[pl.BlockSpec((1, ATTN_WIDTH, TM), lambda s: (prev(s)[0], 0, prev(s)[1])),
                   pl.BlockSpec((1, TM, KV_WIDTH), lambda s: (prev(s)[0], prev(s)[1], 0)),
                   pl.BlockSpec((1, KV_WIDTH, TM), lambda s: (prev(s)[0], 0, prev(s)[1])),
                   pl.BlockSpec((1, TM, rest_w), lambda s: (prev(s)[0], prev(s)[1], 0))],
        out_shape=[jax.ShapeDtypeStruct((bsz, ATTN_WIDTH, t_all), BF16),
                   jax.ShapeDtypeStruct((bsz, t_all, KV_WIDTH), BF16),
                   jax.ShapeDtypeStruct((bsz, KV_WIDTH, t_all), BF16),
                   jax.ShapeDtypeStruct((bsz, t_all, rest_w), F32)],
        scratch_shapes=[pltpu.VMEM((TM, IN_WIDTH), F32)] * 2,
        compiler_params=_params(40, n_grid_dims=1),
        name="inproj",
    )(ctx_arr, x_arr, mods, g_mix, w_in, gq, gk, cos_t, sin_t)


def _with_halo(prev_ref, cur_ref, next_ref, valid_prev, valid_next):
    prev = prev_ref[0] * valid_prev.astype(F32)
    nxt = next_ref[0] * valid_next.astype(F32)
    return jnp.concatenate([prev, cur_ref[0], nxt], axis=0)


def _shifted(tcat, offset):
    rows = tcat.shape[0]
    if offset == 0:
        return tcat[F32_SUBLANES:F32_SUBLANES + TM]
    return pltpu.roll(tcat, (-offset) % rows, 0)[F32_SUBLANES:F32_SUBLANES + TM]


def _lru_coeffs(rcat, conv_w, conv_b, w_gate, b_gate, sp_lam):
    xc = conv_b
    for j in range(LRU_CONV):
        xc = xc + conv_w[j:j + 1] * _shifted(rcat, j - 2)
    z = jnp.dot(xc.astype(BF16), w_gate, preferred_element_type=F32) + b_gate
    r_gate = jax.nn.sigmoid(z[:, :LRU_WIDTH])
    i_gate = jax.nn.sigmoid(z[:, LRU_WIDTH:])
    log_a = (-LRU_C) * r_gate * sp_lam
    a = jnp.exp(log_a)
    one_minus_a2 = -jnp.tanh(log_a) * (1.0 + a * a)
    return a, jnp.sqrt(one_minus_a2) * (i_gate * xc)


def _lru_kernel(rf_ref, rfp_ref, rfn_ref, rb_ref, rbp_ref, rbn_ref, bg_ref, cu_ref, cup_ref, cun_ref,
                cw_ref, cb_ref, wg_ref, bgate_ref, lam_ref, scw_ref,
                hf_ref, hb_ref, sc_ref,
                af_s, bf_s, ab_s, bb_s, hf_state, hb_state, *, nt):
    i = pl.program_id(1)
    j = jnp.where(i == 0, 0, nt - i)

    @pl.when(i == 0)
    def _():
        hf_state[...] = jnp.zeros_like(hf_state)
        hb_state[...] = jnp.zeros_like(hb_state)

    def halo_valid(t):
        return jnp.logical_and(t != 0, t != 1), jnp.logical_and(t != 0, t != nt - 1)

    lam = lam_ref[...]
    sp_lam = jnp.maximum(-lam, 0.0) + jnp.log1p(jnp.exp(-jnp.abs(lam)))
    cw = cw_ref[...]
    cb = cb_ref[...]

    vp, vn = halo_valid(i)
    a, b = _lru_coeffs(_with_halo(rfp_ref, rf_ref, rfn_ref, vp, vn), cw, cb,
                       wg_ref[:, :2 * LRU_WIDTH], bgate_ref[:, :2 * LRU_WIDTH], sp_lam[0:1])
    _block_prefix(a, b, af_s, bf_s, reverse=False)
    cucat = _with_halo(cup_ref, cu_ref, cun_ref, vp, vn)
    scw = scw_ref[...]
    conv = scw[0:1] * _shifted(cucat, -1) + scw[1:2] * _shifted(cucat, 0) + scw[2:3] * _shifted(cucat, 1)
    sc_ref[0] = (bg_ref[0] * conv).astype(BF16)

    vp, vn = halo_valid(j)
    a, b = _lru_coeffs(_with_halo(rbp_ref, rb_ref, rbn_ref, vp, vn), cw, cb,
                       wg_ref[:, 2 * LRU_WIDTH:], bgate_ref[:, 2 * LRU_WIDTH:], sp_lam[1:2])
    _block_prefix(a, b, ab_s, bb_s, reverse=True)

    _carry_blocks(af_s, bf_s, hf_ref, hf_state, reverse=False)
    _carry_blocks(ab_s, bb_s, hb_ref, hb_state, reverse=True)


def _block_prefix(a, b, a_s, b_s, reverse):
    n = F32_SUBLANES
    a = a.reshape(TM // n, n, LRU_WIDTH)
    b = b.reshape(TM // n, n, LRU_WIDTH)
    row = lax.broadcasted_iota(jnp.int32, a.shape, 1)
    d = 1
    while d < n:
        shift, keep = (n - d, row < n - d) if reverse else (d, row >= d)
        a_prev = jnp.where(keep, pltpu.roll(a, shift, 1), 1.0)
        b_prev = jnp.where(keep, pltpu.roll(b, shift, 1), 0.0)
        b = a * b_prev + b
        a = a * a_prev
        d *= 2
    a_s[...] = a
    b_s[...] = b


def _carry_blocks(a_s, b_s, out_ref, state_ref, reverse):
    n = F32_SUBLANES
    nb = TM // n
    last = 0 if reverse else n - 1
    h_in = state_ref[...]
    for v in (reversed(range(nb)) if reverse else range(nb)):
        h = a_s[v] * h_in + b_s[v]
        out_ref[0, v * n:(v + 1) * n, :] = h
        h_in = h[last:last + 1]
    state_ref[...] = h_in


def _lru(rest, conv_w, conv_b, w_gate, b_gate, lam, sc_w):
    bsz, t_all, _ = rest.shape
    nt = t_all // TM
    hpt = TM // F32_SUBLANES
    nhb = t_all // F32_SUBLANES

    def rev(i):
        return jnp.where(i == 0, 0, nt - i)

    def tile(col, order):
        return pl.BlockSpec((1, TM, LRU_WIDTH), lambda b, i: (b, order(i), col))

    def prev_halo(col, order):
        return pl.BlockSpec((1, F32_SUBLANES, LRU_WIDTH),
                            lambda b, i: (b, jnp.maximum(order(i) * hpt - 1, 0), col))

    def next_halo(col, order):
        return pl.BlockSpec((1, F32_SUBLANES, LRU_WIDTH),
                            lambda b, i: (b, jnp.minimum((order(i) + 1) * hpt, nhb - 1), col))

    fwd = lambda i: i
    out_f32 = jax.ShapeDtypeStruct((bsz, t_all, LRU_WIDTH), F32)
    return pl.pallas_call(
        functools.partial(_lru_kernel, nt=nt),
        grid=(bsz, nt),
        in_specs=[tile(0, fwd), prev_halo(0, fwd), next_halo(0, fwd),
                  tile(0, rev), prev_halo(0, rev), next_halo(0, rev),
                  tile(2, fwd),
                  tile(3, fwd), prev_halo(3, fwd), next_halo(3, fwd),
                  _const_spec((LRU_CONV, LRU_WIDTH)),
                  _const_spec((1, LRU_WIDTH)),
                  _const_spec((LRU_WIDTH, 4 * LRU_WIDTH)),
                  _const_spec((1, 4 * LRU_WIDTH)),
                  _const_spec((2, LRU_WIDTH)),
                  _const_spec((3, SC_WIDTH))],
        out_specs=[pl.BlockSpec((1, TM, LRU_WIDTH), lambda b, i: (b, i, 0)),
                   pl.BlockSpec((1, TM, LRU_WIDTH), lambda b, i: (b, rev(i), 0)),
                   pl.BlockSpec((1, TM, SC_WIDTH), lambda b, i: (b, i, 0))],
        out_shape=[out_f32, out_f32, jax.ShapeDtypeStruct((bsz, t_all, SC_WIDTH), BF16)],
        scratch_shapes=([pltpu.VMEM((TM // F32_SUBLANES, F32_SUBLANES, LRU_WIDTH), F32)] * 4
                        + [pltpu.VMEM((1, LRU_WIDTH), F32)] * 2),
        compiler_params=_params(32),
        name="lru_scan",
    )(rest, rest, rest, rest, rest, rest, rest, rest, rest, rest,
      conv_w, conv_b, w_gate, b_gate, lam, sc_w)


def _mixer_kernel(qT_ref, k_ref, vT_ref, ctx_ref, x_ref, hf_ref, hb_ref, gg_ref, sc_ref, mod_ref, wo_ref,
                  gffn_ref, x1_ref, h2_ref, oT_ref, s_ref, *, tile_offset, n_key_tiles):
    i = pl.program_id(1) + tile_offset
    ones_rows = jnp.ones((BF16_SUBLANES, TM), BF16)

    def project_out():
        x = x_ref[0]
        if tile_offset == 0:
            x = jnp.where(i == 0, ctx_ref[0], x)
        mod = mod_ref[0, 0]
        gate_mix, shift, scale = mod[2:3], mod[3:4], mod[4:5]
        att = oT_ref[...].T.astype(BF16)
        lru = (gg_ref[0] * (hf_ref[0] + hb_ref[0])).astype(BF16)
        y = (jnp.dot(att, wo_ref[0:ATTN_WIDTH], preferred_element_type=F32)
             + jnp.dot(lru, wo_ref[ATTN_WIDTH:ATTN_WIDTH + LRU_WIDTH], preferred_element_type=F32)
             + jnp.dot(sc_ref[0], wo_ref[ATTN_WIDTH + LRU_WIDTH:], preferred_element_type=F32))
        x1 = x + gate_mix * y
        x1_ref[0] = x1
        h2 = x1 * _rms_scale(x1) * gffn_ref[...]
        h2_ref[0] = (h2 * (1.0 + scale) + shift).astype(BF16)

    def run(n_tiles):
        items = [(pr, c) for pr in range(N_HEADS // 2) for c in range(n_tiles)]
        qpads = {}

        def qpad_of(pr):
            if pr not in qpads:
                row0 = pr * 2 * HEAD_DIM
                q2 = jnp.concatenate([qT_ref[0, row0:row0 + HEAD_DIM, :],
                                      qT_ref[0, row0 + HEAD_DIM:row0 + 2 * HEAD_DIM, :]], axis=1)
                zero = jnp.zeros_like(q2)
                first_kv = (2 * pr) // HEADS_PER_KV == 0
                qpads[pr] = jnp.concatenate([q2, zero] if first_kv else [zero, q2], axis=0)
            return qpads[pr]

        def scores(t):
            pr, c = items[t]
            s = jnp.dot(k_ref[0, c * TM:(c + 1) * TM, :], qpad_of(pr), preferred_element_type=F32)
            s_ref[t % SCORE_SLOTS] = s
            return jnp.max(s, axis=0, keepdims=True)

        tile_max = [scores(t) for t in range(min(SCORE_LOOKAHEAD, len(items)))]
        m = acc = None
        for t, (pr, c) in enumerate(items):
            if t + SCORE_LOOKAHEAD < len(items):
                tile_max.append(scores(t + SCORE_LOOKAHEAD))
            if c == 0:
                m = jnp.full((1, 2 * TM), -jnp.inf, F32)
                acc = jnp.zeros((HEAD_DIM + BF16_SUBLANES, 2 * TM), F32)
            g = (2 * pr) // HEADS_PER_KV
            m_new = jnp.maximum(m, tile_max[t])
            alpha = jnp.exp2(m - m_new)
            p = jnp.exp2(s_ref[t % SCORE_SLOTS] - m_new).astype(BF16)
            vt = jnp.concatenate([vT_ref[0, g * HEAD_DIM:(g + 1) * HEAD_DIM, c * TM:(c + 1) * TM],
                                  ones_rows], axis=0)
            acc = alpha * acc + jnp.dot(vt, p, preferred_element_type=F32)
            m = m_new
            tile_max[t] = None
            if c == n_tiles - 1:
                o = acc[:HEAD_DIM] / acc[HEAD_DIM:HEAD_DIM + 1]
                row0 = pr * 2 * HEAD_DIM
                oT_ref[row0:row0 + HEAD_DIM, :] = o[:, :TM]
                oT_ref[row0 + HEAD_DIM:row0 + 2 * HEAD_DIM, :] = o[:, TM:]
        project_out()

    @pl.when(i == 0)
    def _():
        run(1)

    @pl.when(i != 0)
    def _():
        run(n_key_tiles)


def _mixer(qT, k, vT, ctx_arr, x_arr, x_base, hf, hb, rest, sc, mods, w_out, g_ffn, tile_offset):
    bsz, t_all, _ = k.shape
    nt = t_all // TM
    nq = nt - tile_offset
    off = lambda b, i: (b, i + tile_offset, 0)
    return pl.pallas_call(
        functools.partial(_mixer_kernel, tile_offset=tile_offset, n_key_tiles=nt),
        grid=(bsz, nq),
        in_specs=[pl.BlockSpec((1, ATTN_WIDTH, TM), lambda b, i: (b, 0, i + tile_offset)),
                  pl.BlockSpec((1, t_all, KV_WIDTH), lambda b, i: (b, 0, 0)),
                  pl.BlockSpec((1, KV_WIDTH, t_all), lambda b, i: (b, 0, 0))]
                 + _token_specs(x_base, tile_offset) + [
                  pl.BlockSpec((1, TM, LRU_WIDTH), off),
                  pl.BlockSpec((1, TM, LRU_WIDTH), off),
                  pl.BlockSpec((1, TM, LRU_WIDTH), lambda b, i: (b, i + tile_offset, 1)),
                  pl.BlockSpec((1, TM, SC_WIDTH), off),
                  pl.BlockSpec((1, 1, MOD_ROWS, D_MODEL),
                               lambda b, i: (b, jnp.minimum(i + tile_offset, 1), 0, 0)),
                  _const_spec((D_MODEL, D_MODEL)),
                  _const_spec((1, D_MODEL))],
        out_specs=[pl.BlockSpec((1, TM, D_MODEL), lambda b, i: (b, i, 0)),
                   pl.BlockSpec((1, TM, D_MODEL), lambda b, i: (b, i, 0))],
        out_shape=[jax.ShapeDtypeStruct((bsz, nq * TM, D_MODEL), F32),
                   jax.ShapeDtypeStruct((bsz, nq * TM, D_MODEL), BF16)],
        scratch_shapes=[pltpu.VMEM((ATTN_WIDTH, TM), F32), pltpu.VMEM((SCORE_SLOTS, TM, 2 * TM), F32)],
        compiler_params=_params(40),
        name="mixer",
    )(qT, k, vT, ctx_arr, x_arr, hf, hb, rest, sc, mods, w_out, g_ffn)


def _ffn_kernel(h_ref, hp_ref, hn_ref, x1_ref, mod_ref, wup_ref, cw_ref, wdn_ref, gfin_ref,
                o_ref, acc_ref, up_ref, *, ctx_tiles, nt, final_norm):
    i = pl.program_id(1)
    valid_prev = jnp.logical_and(i != 0, i != ctx_tiles)
    valid_next = jnp.logical_and(i != nt - 1, i != ctx_tiles - 1)
    hp = hp_ref[0] * valid_prev.astype(BF16)
    hn = hn_ref[0] * valid_next.astype(BF16)
    hcat = jnp.concatenate([hp, h_ref[0], hn], axis=0)
    rows = hcat.shape[0]

    def conv(t, w):
        lo, hi = BF16_SUBLANES, BF16_SUBLANES + TM
        return (w[0:1] * pltpu.roll(t, 1, 0)[lo:hi] + w[1:2] * t[lo:hi]
                + w[2:3] * pltpu.roll(t, rows - 1, 0)[lo:hi])

    n_chunks = D_FF // FF_CHUNK

    def up_project(c):
        cu = slice(c * FF_CHUNK, (c + 1) * FF_CHUNK)
        cg = slice(D_FF + c * FF_CHUNK, D_FF + (c + 1) * FF_CHUNK)
        up_ref[c % UP_SLOTS, :, :FF_CHUNK] = jnp.dot(hcat, wup_ref[:, cu], preferred_element_type=F32)
        up_ref[c % UP_SLOTS, :, FF_CHUNK:] = jnp.dot(hcat, wup_ref[:, cg], preferred_element_type=F32)

    for c in range(min(UP_LOOKAHEAD, n_chunks)):
        up_project(c)
    for c in range(n_chunks):
        if c + UP_LOOKAHEAD < n_chunks:
            up_project(c + UP_LOOKAHEAD)
        cu = slice(c * FF_CHUNK, (c + 1) * FF_CHUNK)
        cg = slice(D_FF + c * FF_CHUNK, D_FF + (c + 1) * FF_CHUNK)
        u = conv(up_ref[c % UP_SLOTS, :, :FF_CHUNK], cw_ref[:, cu])
        g = conv(up_ref[c % UP_SLOTS, :, FF_CHUNK:], cw_ref[:, cg])
        act = ((g * jax.nn.sigmoid(g)) * u).astype(BF16)
        part = jnp.dot(act, wdn_ref[cu, :], preferred_element_type=F32)
        if c == 0:
            acc_ref[...] = part
        else:
            acc_ref[...] += part

    gate_ffn = mod_ref[0, 0][5:6]
    x2 = x1_ref[0] + gate_ffn * acc_ref[...]
    if final_norm:
        x2 = x2 * _rms_scale(x2) * gfin_ref[...]
    o_ref[0] = x2


def _ffn(h2, x1, mods, w_up, conv_w, w_down, g_final, ctx_tiles, final_norm):
    bsz, t_len, _ = h2.shape
    nt = t_len // TM
    hpt = TM // BF16_SUBLANES
    nhb = t_len // BF16_SUBLANES
    return pl.pallas_call(
        functools.partial(_ffn_kernel, ctx_tiles=ctx_tiles, nt=nt, final_norm=final_norm),
        grid=(bsz, nt),
        in_specs=[pl.BlockSpec((1, TM, D_MODEL), lambda b, i: (b, i, 0)),
                  pl.BlockSpec((1, BF16_SUBLANES, D_MODEL),
                               lambda b, i: (b, jnp.maximum(i * hpt - 1, 0), 0)),
                  pl.BlockSpec((1, BF16_SUBLANES, D_MODEL),
                               lambda b, i: (b, jnp.minimum((i + 1) * hpt, nhb - 1), 0)),
                  pl.BlockSpec((1, TM, D_MODEL), lambda b, i: (b, i, 0)),
                  pl.BlockSpec((1, 1, MOD_ROWS, D_MODEL),
                               lambda b, i: (b, jnp.minimum(i + 1 - ctx_tiles, 1), 0, 0)),
                  _const_spec((D_MODEL, 2 * D_FF)),
                  _const_spec((3, 2 * D_FF)),
                  _const_spec((D_FF, D_MODEL)),
                  _const_spec((1, D_MODEL))],
        out_specs=pl.BlockSpec((1, TM, D_MODEL), lambda b, i: (b, i, 0)),
        out_shape=jax.ShapeDtypeStruct((bsz, t_len, D_MODEL), F32),
        scratch_shapes=[pltpu.VMEM((TM, D_MODEL), F32),
                        pltpu.VMEM((UP_SLOTS, TM + 2 * BF16_SUBLANES, 2 * FF_CHUNK), F32)],
        compiler_params=_params(48),
        name="conv_ffn",
    )(h2, h2, h2, x1, mods, w_up, conv_w, w_down, g_final)


def _rope_tables(seq, ctx_len):
    rows = seq // GRID_W
    pos_r = np.repeat(np.arange(rows, dtype=np.float32), GRID_W)
    pos_c = np.tile(np.arange(GRID_W, dtype=np.float32), rows)
    n_f = HEAD_DIM // 4
    inv = (np.float32(ROPE_THETA) ** (-np.arange(n_f, dtype=np.float32) / n_f)).astype(np.float32)
    ang = np.concatenate([pos_r[:, None] * inv, pos_c[:, None] * inv], axis=-1).astype(np.float32)
    cos = np.concatenate([np.ones((ctx_len, HEAD_DIM // 2)), np.cos(ang.astype(np.float64))], axis=0)
    sin = np.concatenate([np.zeros((ctx_len, HEAD_DIM // 2)), np.sin(ang.astype(np.float64))], axis=0)
    return jnp.asarray(cos.T, F32), jnp.asarray(sin.T, F32)


def _gate_weights(wa, ba, wi, bi):
    def dense(w):
        eye = jnp.eye(LRU_BLOCKS, dtype=w.dtype)
        return jnp.einsum("nde,nm->ndme", w, eye).reshape(LRU_WIDTH, LRU_WIDTH)
    w = jnp.concatenate([dense(wa[0]), dense(wi[0]), dense(wa[1]), dense(wi[1])], axis=1)
    b = jnp.concatenate([ba[0], bi[0], ba[1], bi[1]])[None, :]
    return w.astype(BF16), b


def kernel(x, c, ctx, c_ctx, w_mod, b_mod, g_mix, g_ffn, w_in, g_q, g_k, lru_conv_w, lru_conv_b,
           lru_wa, lru_ba, lru_wi, lru_bi, lru_lam, sc_conv_w, w_out, w_up, ffn_conv_w, w_down,
           g_final):
    bsz, seq, _ = x.shape
    ctx_len = ctx.shape[1]
    depth = w_mod.shape[0]
    assert ctx_len == TM and seq % TM == 0 and seq % GRID_W == 0

    cos_t, sin_t = _rope_tables(seq, ctx_len)
    ctx_arr, x_arr, x_base = ctx, x, 0
    cc_rows = -(-(bsz + 1) // F32_SUBLANES) * F32_SUBLANES
    cc = jnp.zeros((cc_rows, D_MODEL), F32).at[:bsz].set(c).at[bsz].set(c_ctx)

    mod_layers = _modulation(cc, w_mod, b_mod)

    out = None
    for l in range(depth):
        last = l == depth - 1
        mod_all = mod_layers[l].reshape(cc_rows, N_MOD, D_MODEL)
        mod_x = mod_all[:bsz]
        mod_c = jnp.broadcast_to(mod_all[bsz][None], (bsz, N_MOD, D_MODEL))
        mods = jnp.stack([mod_c, mod_x], axis=1)
        mods = jnp.pad(mods, ((0, 0), (0, 0), (0, MOD_ROWS - N_MOD), (0, 0)))

        gq = jnp.broadcast_to((g_q[l] * Q_SCALE)[:, None], (HEAD_DIM, TM))
        gk = jnp.broadcast_to(g_k[l][:, None], (HEAD_DIM, TM))
        qT, k, vT, rest = _inproj(ctx_arr, x_arr, x_base, mods, g_mix[l][None, :],
                                  w_in[l].astype(BF16), gq, gk, cos_t, sin_t)

        w_gate, b_gate = _gate_weights(lru_wa[l], lru_ba[l], lru_wi[l], lru_bi[l])
        hf, hb, sc = _lru(rest, lru_conv_w[l], lru_conv_b[l][None, :], w_gate, b_gate, lru_lam[l],
                          sc_conv_w[l])

        tile_offset = 1 if last else 0
        x1, h2 = _mixer(qT, k, vT, ctx_arr, x_arr, x_base, hf, hb, rest, sc, mods,
                        w_out[l].astype(BF16), g_ffn[l][None, :], tile_offset)
        out = _ffn(h2, x1, mods, w_up[l].astype(BF16), ffn_conv_w[l], w_down[l].astype(BF16),
                   g_final[None, :], ctx_tiles=1 - tile_offset, final_norm=last)
        ctx_arr, x_arr, x_base = out, out, 1
    return out
```

```python
import functools

import numpy as np
import jax
import jax.numpy as jnp
from jax import lax
from jax.experimental import pallas as pl
from jax.experimental.pallas import tpu as pltpu

F32 = jnp.float32
BF16 = jnp.bfloat16

D_MODEL = 1024
HEAD_DIM = 64
N_HEADS = 8
N_KV_HEADS = 2
HEADS_PER_KV = N_HEADS // N_KV_HEADS
ATTN_WIDTH = N_HEADS * HEAD_DIM
KV_WIDTH = N_KV_HEADS * HEAD_DIM
LRU_WIDTH = 256
LRU_BLOCKS = 4
LRU_BLOCK = LRU_WIDTH // LRU_BLOCKS
LRU_CONV = 4
LRU_C = 8.0
SC_WIDTH = 256
IN_WIDTH = 2048
D_FF = 2816
FF_CHUNK = 256
GRID_W = 64
ROPE_THETA = 10000.0
EPS = 1e-6
Q_SCALE = HEAD_DIM ** -0.5 * 1.4426950408889634

TM = 256
SCORE_LOOKAHEAD = 3
SCORE_SLOTS = SCORE_LOOKAHEAD + 1
UP_LOOKAHEAD = 3
UP_SLOTS = UP_LOOKAHEAD + 1
LANES = 128
F32_SUBLANES = 8
BF16_SUBLANES = 16
N_MOD = 6
MOD_ROWS = 8

OFF_Q = 0
OFF_K = OFF_Q + ATTN_WIDTH
OFF_V = OFF_K + KV_WIDTH
OFF_R = OFF_V + KV_WIDTH
OFF_G = OFF_R + LRU_WIDTH
OFF_B = OFF_G + LRU_WIDTH
OFF_C = OFF_B + SC_WIDTH
OFF_U = OFF_C + SC_WIDTH


def _const_spec(shape):
    nd = len(shape)
    return pl.BlockSpec(shape, lambda *_: (0,) * nd, pipeline_mode=pl.Buffered(1))


def _params(vmem_mb, n_grid_dims=2):
    return pltpu.CompilerParams(dimension_semantics=("arbitrary",) * n_grid_dims,
                                vmem_limit_bytes=vmem_mb * 1024 * 1024)


def _rms_scale(t):
    return lax.rsqrt(jnp.mean(t * t, axis=-1, keepdims=True) + EPS)


def _mod_kernel(c_ref, w_ref, b_ref, o_ref):
    c = c_ref[...]
    h = (c * jax.nn.sigmoid(c)).astype(BF16)
    o_ref[...] = jnp.dot(h, w_ref[...].astype(BF16), preferred_element_type=F32) + b_ref[...]


def _modulation(cc, w_mod, b_mod):
    rows = cc.shape[0]
    depth = w_mod.shape[0]
    return pl.pallas_call(
        _mod_kernel,
        grid=(depth, N_MOD),
        in_specs=[pl.BlockSpec((rows, D_MODEL), lambda l, j: (0, 0)),
                  pl.BlockSpec((None, D_MODEL, D_MODEL), lambda l, j: (l, 0, j)),
                  pl.BlockSpec((None, 1, D_MODEL), lambda l, j: (l, 0, j))],
        out_specs=pl.BlockSpec((None, rows, D_MODEL), lambda l, j: (l, 0, j)),
        out_shape=jax.ShapeDtypeStruct((depth, rows, N_MOD * D_MODEL), F32),
        name="modulation",
    )(cc, w_mod, b_mod[:, None, :])


def _norm_rope_transposed(t, gain, cos, sin):
    half = HEAD_DIM // 2
    out = []
    for h in range(t.shape[0] // HEAD_DIM):
        r = t[h * HEAD_DIM:(h + 1) * HEAD_DIM]
        r = r * lax.rsqrt(jnp.mean(r * r, axis=0, keepdims=True) + EPS) * gain
        x1, x2 = r[:half], r[half:]
        out += [x1 * cos - x2 * sin, x1 * sin + x2 * cos]
    return jnp.concatenate(out, axis=0)


def _inproj_kernel(ctx_ref, x_ref, mod_ref, gmix_ref, w_ref, gq_ref, gk_ref, cos_ref, sin_ref,
                   qT_ref, k_ref, vT_ref, rest_ref, y_even_ref, y_odd_ref, *, nt, n_tiles):
    s = pl.program_id(0)
    tile = jnp.minimum(s, n_tiles - 1) % nt

    def project(y_ref):
        x = jnp.where(tile == 0, ctx_ref[0], x_ref[0])
        mod = mod_ref[0, 0]
        shift, scale = mod[0:1], mod[1:2]
        h = x * _rms_scale(x) * gmix_ref[...]
        h = h * (1.0 + scale) + shift
        y_ref[...] = jnp.dot(h.astype(BF16), w_ref[...], preferred_element_type=F32)

    def finish(y_ref):
        cos, sin = cos_ref[...], sin_ref[...]
        q = _norm_rope_transposed(y_ref[:, OFF_Q:OFF_Q + ATTN_WIDTH].T, gq_ref[...], cos, sin)
        qT_ref[0] = q.astype(BF16)
        k = _norm_rope_transposed(y_ref[:, OFF_K:OFF_K + KV_WIDTH].T, gk_ref[...], cos, sin)
        k_ref[0] = k.T.astype(BF16)
        vT_ref[0] = y_ref[:, OFF_V:OFF_V + KV_WIDTH].T.astype(BF16)
        rest_ref[0, :, 0:LRU_WIDTH] = y_ref[:, OFF_R:OFF_R + LRU_WIDTH]
        rest_ref[0, :, LRU_WIDTH:2 * LRU_WIDTH] = jax.nn.gelu(y_ref[:, OFF_G:OFF_G + LRU_WIDTH])
        rest_ref[0, :, 2 * LRU_WIDTH:2 * LRU_WIDTH + SC_WIDTH] = y_ref[:, OFF_B:OFF_B + SC_WIDTH]
        rest_ref[0, :, 2 * LRU_WIDTH + SC_WIDTH:] = (y_ref[:, OFF_C:OFF_C + SC_WIDTH]
                                                     * y_ref[:, OFF_U:OFF_U + SC_WIDTH])

    @pl.when(s == 0)
    def _():
        y_odd_ref[...] = jnp.zeros_like(y_odd_ref)

    @pl.when(s % 2 == 0)
    def _():
        finish(y_odd_ref)
        project(y_even_ref)

    @pl.when(s % 2 == 1)
    def _():
        finish(y_even_ref)
        project(y_odd_ref)


def _inproj(ctx_arr, x_arr, x_base, mods, g_mix, w_in, gq, gk, cos_t, sin_t):
    bsz = x_arr.shape[0]
    nt = x_arr.shape[1] // TM - x_base + 1
    t_all = nt * TM
    n_tiles = bsz * nt
    rest_w = 2 * LRU_WIDTH + 2 * SC_WIDTH

    def cur(s):
        sc = jnp.minimum(s, n_tiles - 1)
        return sc // nt, sc % nt

    def prev(s):
        sp = jnp.maximum(s - 1, 0)
        return sp // nt, sp % nt

    rope_spec = pl.BlockSpec((HEAD_DIM // 2, TM), lambda s: (0, prev(s)[1]))
    return pl.pallas_call(
        functools.partial(_inproj_kernel, nt=nt, n_tiles=n_tiles),
        grid=(n_tiles + 1,),
        in_specs=[pl.BlockSpec((1, TM, D_MODEL), lambda s: (cur(s)[0], 0, 0)),
                  pl.BlockSpec((1, TM, D_MODEL),
                               lambda s: (cur(s)[0], x_base + jnp.maximum(cur(s)[1] - 1, 0), 0)),
                  pl.BlockSpec((1, 1, MOD_ROWS, D_MODEL),
                               lambda s: (cur(s)[0], jnp.minimum(cur(s)[1], 1), 0, 0)),
                  _const_spec((1, D_MODEL)),
                  _const_spec((D_MODEL, IN_WIDTH)),
                  _const_spec((HEAD_DIM, TM)),
                  _const_spec((HEAD_DIM, TM)),
                  rope_spec,
                  rope_spec],
        out_specs=[pl.BlockSpec((1, ATTN_WIDTH, TM), lambda s: (prev(s)[0], 0, prev(s)[1])),
                   pl.BlockSpec((1, TM, KV_WIDTH), lambda s: (prev(s)[0], prev(s)[1], 0)),
                   pl.BlockSpec((1, KV_WIDTH, TM), lambda s: (prev(s)[0], 0, prev(s)[1])),
                   pl.BlockSpec((1, TM, rest_w), lambda s: (prev(s)[0], prev(s)[1], 0))],
        out_shape=[jax.ShapeDtypeStruct((bsz, ATTN_WIDTH, t_all), BF16),
                   jax.ShapeDtypeStruct((bsz, t_all, KV_WIDTH), BF16),
                   jax.ShapeDtypeStruct((bsz, KV_WIDTH, t_all), BF16),
                   jax.ShapeDtypeStruct((bsz, t_all, rest_w), F32)],
        scratch_shapes=[pltpu.VMEM((TM, IN_WIDTH), F32)] * 2,
        compiler_params=_params(40, n_grid_dims=1),
        name="inproj",
    )(ctx_arr, x_arr, mods, g_mix, w_in, gq, gk, cos_t, sin_t)


def _with_halo(prev_ref, cur_ref, next_ref, valid_prev, valid_next):
    prev = prev_ref[0] * valid_prev.astype(F32)
    nxt = next_ref[0] * valid_next.astype(F32)
    return jnp.concatenate([prev, cur_ref[0], nxt], axis=0)


def _shifted(tcat, offset):
    rows = tcat.shape[0]
    if offset == 0:
        return tcat[F32_SUBLANES:F32_SUBLANES + TM]
    return pltpu.roll(tcat, (-offset) % rows, 0)[F32_SUBLANES:F32_SUBLANES + TM]


def _lru_coeffs(rcat, conv_w, conv_b, w_gate, b_gate, sp_lam):
    xc = conv_b
    for j in range(LRU_CONV):
        xc = xc + conv_w[j:j + 1] * _shifted(rcat, j - 2)
    z = jnp.dot(xc.astype(BF16), w_gate, preferred_element_type=F32) + b_gate
    r_gate = jax.nn.sigmoid(z[:, :LRU_WIDTH])
    i_gate = jax.nn.sigmoid(z[:, LRU_WIDTH:])
    log_a = (-LRU_C) * r_gate * sp_lam
    a = jnp.exp(log_a)
    one_minus_a2 = -jnp.tanh(log_a) * (1.0 + a * a)
    return a, jnp.sqrt(one_minus_a2) * (i_gate * xc)


def _lru_kernel(rf_ref, rfp_ref, rfn_ref, rb_ref, rbp_ref, rbn_ref, bg_ref, cu_ref, cup_ref, cun_ref,
                cw_ref, cb_ref, wg_ref, bgate_ref, lam_ref, scw_ref,
                hf_ref, hb_ref, sc_ref,
                af_s, bf_s, ab_s, bb_s, hf_state, hb_state, *, nt):
    i = pl.program_id(1)
    j = jnp.where(i == 0, 0, nt - i)

    @pl.when(i == 0)
    def _():
        hf_state[...] = jnp.zeros_like(hf_state)
        hb_state[...] = jnp.zeros_like(hb_state)

    def halo_valid(t):
        return jnp.logical_and(t != 0, t != 1), jnp.logical_and(t != 0, t != nt - 1)

    lam = lam_ref[...]
    sp_lam = jnp.maximum(-lam, 0.0) + jnp.log1p(jnp.exp(-jnp.abs(lam)))
    cw = cw_ref[...]
    cb = cb_ref[...]

    vp, vn = halo_valid(i)
    a, b = _lru_coeffs(_with_halo(rfp_ref, rf_ref, rfn_ref, vp, vn), cw, cb,
                       wg_ref[:, :2 * LRU_WIDTH], bgate_ref[:, :2 * LRU_WIDTH], sp_lam[0:1])
    _block_prefix(a, b, af_s, bf_s, reverse=False)
    cucat = _with_halo(cup_ref, cu_ref, cun_ref, vp, vn)
    scw = scw_ref[...]
    conv = scw[0:1] * _shifted(cucat, -1) + scw[1:2] * _shifted(cucat, 0) + scw[2:3] * _shifted(cucat, 1)
    sc_ref[0] = (bg_ref[0] * conv).astype(BF16)

    vp, vn = halo_valid(j)
    a, b = _lru_coeffs(_with_halo(rbp_ref, rb_ref, rbn_ref, vp, vn), cw, cb,
                       wg_ref[:, 2 * LRU_WIDTH:], bgate_ref[:, 2 * LRU_WIDTH:], sp_lam[1:2])
    _block_prefix(a, b, ab_s, bb_s, reverse=True)

    _carry_blocks(af_s, bf_s, hf_ref, hf_state, reverse=False)
    _carry_blocks(ab_s, bb_s, hb_ref, hb_state, reverse=True)


def _block_prefix(a, b, a_s, b_s, reverse):
    n = F32_SUBLANES
    a = a.reshape(TM // n, n, LRU_WIDTH)
    b = b.reshape(TM // n, n, LRU_WIDTH)
    row = lax.broadcasted_iota(jnp.int32, a.shape, 1)
    d = 1
    while d < n:
        shift, keep = (n - d, row < n - d) if reverse else (d, row >= d)
        a_prev = jnp.where(keep, pltpu.roll(a, shift, 1), 1.0)
        b_prev = jnp.where(keep, pltpu.roll(b, shift, 1), 0.0)
        b = a * b_prev + b
        a = a * a_prev
        d *= 2
    a_s[...] = a
    b_s[...] = b


def _carry_blocks(a_s, b_s, out_ref, state_ref, reverse):
    n = F32_SUBLANES
    nb = TM // n
    last = 0 if reverse else n - 1
    h_in = state_ref[...]
    for v in (reversed(range(nb)) if reverse else range(nb)):
        h = a_s[v] * h_in + b_s[v]
        out_ref[0, v * n:(v + 1) * n, :] = h
        h_in = h[last:last + 1]
    state_ref[...] = h_in


def _lru(rest, conv_w, conv_b, w_gate, b_gate, lam, sc_w):
    bsz, t_all, _ = rest.shape
    nt = t_all // TM
    hpt = TM // F32_SUBLANES
    nhb = t_all // F32_SUBLANES

    def rev(i):
        return jnp.where(i == 0, 0, nt - i)

    def tile(col, order):
        return pl.BlockSpec((1, TM, LRU_WIDTH), lambda b, i: (b, order(i), col))

    def prev_halo(col, order):
        return pl.BlockSpec((1, F32_SUBLANES, LRU_WIDTH),
                            lambda b, i: (b, jnp.maximum(order(i) * hpt - 1, 0), col))

    def next_halo(col, order):
        return pl.BlockSpec((1, F32_SUBLANES, LRU_WIDTH),
                            lambda b, i: (b, jnp.minimum((order(i) + 1) * hpt, nhb - 1), col))

    fwd = lambda i: i
    out_f32 = jax.ShapeDtypeStruct((bsz, t_all, LRU_WIDTH), F32)
    return pl.pallas_call(
        functools.partial(_lru_kernel, nt=nt),
        grid=(bsz, nt),
        in_specs=[tile(0, fwd), prev_halo(0, fwd), next_halo(0, fwd),
                  tile(0, rev), prev_halo(0, rev), next_halo(0, rev),
                  tile(2, fwd),
                  tile(3, fwd), prev_halo(3, fwd), next_halo(3, fwd),
                  _const_spec((LRU_CONV, LRU_WIDTH)),
                  _const_spec((1, LRU_WIDTH)),
                  _const_spec((LRU_WIDTH, 4 * LRU_WIDTH)),
                  _const_spec((1, 4 * LRU_WIDTH)),
                  _const_spec((2, LRU_WIDTH)),
                  _const_spec((3, SC_WIDTH))],
        out_specs=[pl.BlockSpec((1, TM, LRU_WIDTH), lambda b, i: (b, i, 0)),
                   pl.BlockSpec((1, TM, LRU_WIDTH), lambda b, i: (b, rev(i), 0)),
                   pl.BlockSpec((1, TM, SC_WIDTH), lambda b, i: (b, i, 0))],
        out_shape=[out_f32, out_f32, jax.ShapeDtypeStruct((bsz, t_all, SC_WIDTH), BF16)],
        scratch_shapes=([pltpu.VMEM((TM // F32_SUBLANES, F32_SUBLANES, LRU_WIDTH), F32)] * 4
                        + [pltpu.VMEM((1, LRU_WIDTH), F32)] * 2),
        compiler_params=_params(32),
        name="lru_scan",
    )(rest, rest, rest, rest, rest, rest, rest, rest, rest, rest,
      conv_w, conv_b, w_gate, b_gate, lam, sc_w)


def _mixer_kernel(qT_ref, k_ref, vT_ref, ctx_ref, x_ref, hf_ref, hb_ref, gg_ref, sc_ref, mod_ref, wo_ref,
                  gffn_ref, x1_ref, h2_ref, oT_ref, s_ref, *, tile_offset, nq, n_key_tiles, n_steps):
    step = pl.program_id(0)
    i = jnp.minimum(step, n_steps - 1) % nq + tile_offset
    i_prev = jnp.maximum(step - 1, 0) % nq + tile_offset
    ones_rows = jnp.ones((BF16_SUBLANES, TM), BF16)

    def project_prev():
        x = x_ref[0]
        if tile_offset == 0:
            x = jnp.where(i_prev == 0, ctx_ref[0], x)
        mod = mod_ref[0, 0]
        gate_mix, shift, scale = mod[2:3], mod[3:4], mod[4:5]
        lru = (gg_ref[0] * (hf_ref[0] + hb_ref[0])).astype(BF16)
        y = (jnp.dot(lru, wo_ref[ATTN_WIDTH:ATTN_WIDTH + LRU_WIDTH], preferred_element_type=F32)
             + jnp.dot(sc_ref[0], wo_ref[ATTN_WIDTH + LRU_WIDTH:], preferred_element_type=F32))
        att = oT_ref[...].T.astype(BF16)
        y = y + jnp.dot(att, wo_ref[0:ATTN_WIDTH], preferred_element_type=F32)
        x1 = x + gate_mix * y
        x1_ref[0] = x1
        h2 = x1 * _rms_scale(x1) * gffn_ref[...]
        h2_ref[0] = (h2 * (1.0 + scale) + shift).astype(BF16)

    def run(n_tiles):
        items = [(pr, c) for pr in range(N_HEADS // 2) for c in range(n_tiles)]
        qpads = {}

        def qpad_of(pr):
            if pr not in qpads:
                row0 = pr * 2 * HEAD_DIM
                q2 = jnp.concatenate([qT_ref[0, row0:row0 + HEAD_DIM, :],
                                      qT_ref[0, row0 + HEAD_DIM:row0 + 2 * HEAD_DIM, :]], axis=1)
                zero = jnp.zeros_like(q2)
                first_kv = (2 * pr) // HEADS_PER_KV == 0
                qpads[pr] = jnp.concatenate([q2, zero] if first_kv else [zero, q2], axis=0)
            return qpads[pr]

        def scores(t):
            pr, c = items[t]
            s = jnp.dot(k_ref[0, c * TM:(c + 1) * TM, :], qpad_of(pr), preferred_element_type=F32)
            s_ref[t % SCORE_SLOTS] = s
            return jnp.max(s, axis=0, keepdims=True)

        tile_max = [scores(t) for t in range(min(SCORE_LOOKAHEAD, len(items)))]
        project_prev()
        m = acc = None
        for t, (pr, c) in enumerate(items):
            if t + SCORE_LOOKAHEAD < len(items):
                tile_max.append(scores(t + SCORE_LOOKAHEAD))
            if c == 0:
                m = jnp.full((1, 2 * TM), -jnp.inf, F32)
                acc = jnp.zeros((HEAD_DIM + BF16_SUBLANES, 2 * TM), F32)
            g = (2 * pr) // HEADS_PER_KV
            m_new = jnp.maximum(m, tile_max[t])
            alpha = jnp.exp2(m - m_new)
            p = jnp.exp2(s_ref[t % SCORE_SLOTS] - m_new).astype(BF16)
            vt = jnp.concatenate([vT_ref[0, g * HEAD_DIM:(g + 1) * HEAD_DIM, c * TM:(c + 1) * TM],
                                  ones_rows], axis=0)
            acc = alpha * acc + jnp.dot(vt, p, preferred_element_type=F32)
            m = m_new
            tile_max[t] = None
            if c == n_tiles - 1:
                o = acc[:HEAD_DIM] / acc[HEAD_DIM:HEAD_DIM + 1]
                row0 = pr * 2 * HEAD_DIM
                oT_ref[row0:row0 + HEAD_DIM, :] = o[:, :TM]
                oT_ref[row0 + HEAD_DIM:row0 + 2 * HEAD_DIM, :] = o[:, TM:]

    @pl.when(step == 0)
    def _():
        oT_ref[...] = jnp.zeros_like(oT_ref)

    if tile_offset == 0:
        @pl.when(jnp.logical_and(step < n_steps, i == 0))
        def _():
            run(1)

    @pl.when(jnp.logical_and(step < n_steps, i != 0))
    def _():
        run(n_key_tiles)

    @pl.when(step == n_steps)
    def _():
        project_prev()


def _mixer(qT, k, vT, ctx_arr, x_arr, x_base, hf, hb, rest, sc, mods, w_out, g_ffn, tile_offset):
    bsz, t_all, _ = k.shape
    nt = t_all // TM
    nq = nt - tile_offset
    n_steps = bsz * nq

    def cur(s):
        sc = jnp.minimum(s, n_steps - 1)
        return sc // nq, sc % nq + tile_offset

    def prev(s):
        sp = jnp.maximum(s - 1, 0)
        return sp // nq, sp % nq + tile_offset

    prev_tile = lambda s: (prev(s)[0], prev(s)[1], 0)
    prev_out = lambda s: (prev(s)[0], prev(s)[1] - tile_offset, 0)
    return pl.pallas_call(
        functools.partial(_mixer_kernel, tile_offset=tile_offset, nq=nq, n_key_tiles=nt,
                          n_steps=n_steps),
        grid=(n_steps + 1,),
        in_specs=[pl.BlockSpec((1, ATTN_WIDTH, TM), lambda s: (cur(s)[0], 0, cur(s)[1])),
                  pl.BlockSpec((1, t_all, KV_WIDTH), lambda s: (cur(s)[0], 0, 0)),
                  pl.BlockSpec((1, KV_WIDTH, t_all), lambda s: (cur(s)[0], 0, 0)),
                  pl.BlockSpec((1, TM, D_MODEL), lambda s: (prev(s)[0], 0, 0)),
                  pl.BlockSpec((1, TM, D_MODEL),
                               lambda s: (prev(s)[0], x_base + jnp.maximum(prev(s)[1] - 1, 0), 0)),
                  pl.BlockSpec((1, TM, LRU_WIDTH), prev_tile),
                  pl.BlockSpec((1, TM, LRU_WIDTH), prev_tile),
                  pl.BlockSpec((1, TM, LRU_WIDTH), lambda s: (prev(s)[0], prev(s)[1], 1)),
                  pl.BlockSpec((1, TM, SC_WIDTH), prev_tile),
                  pl.BlockSpec((1, 1, MOD_ROWS, D_MODEL),
                               lambda s: (prev(s)[0], jnp.minimum(prev(s)[1], 1), 0, 0)),
                  _const_spec((D_MODEL, D_MODEL)),
                  _const_spec((1, D_MODEL))],
        out_specs=[pl.BlockSpec((1, TM, D_MODEL), prev_out),
                   pl.BlockSpec((1, TM, D_MODEL), prev_out)],
        out_shape=[jax.ShapeDtypeStruct((bsz, nq * TM, D_MODEL), F32),
                   jax.ShapeDtypeStruct((bsz, nq * TM, D_MODEL), BF16)],
        scratch_shapes=[pltpu.VMEM((ATTN_WIDTH, TM), F32), pltpu.VMEM((SCORE_SLOTS, TM, 2 * TM), F32)],
        compiler_params=_params(40, n_grid_dims=1),
        name="mixer",
    )(qT, k, vT, ctx_arr, x_arr, hf, hb, rest, sc, mods, w_out, g_ffn)


def _ffn_kernel(h_ref, hp_ref, hn_ref, x1_ref, mod_ref, wup_ref, cw_ref, wdn_ref, gfin_ref,
                o_ref, up_ref, *, ctx_tiles, nt, final_norm):
    i = pl.program_id(1)
    valid_prev = jnp.logical_and(i != 0, i != ctx_tiles)
    valid_next = jnp.logical_and(i != nt - 1, i != ctx_tiles - 1)
    hp = hp_ref[0] * valid_prev.astype(BF16)
    hn = hn_ref[0] * valid_next.astype(BF16)
    hcat = jnp.concatenate([hp, h_ref[0], hn], axis=0)
    rows = hcat.shape[0]

    def conv(t, w):
        lo, hi = BF16_SUBLANES, BF16_SUBLANES + TM
        return (w[0:1] * pltpu.roll(t, 1, 0)[lo:hi] + w[1:2] * t[lo:hi]
                + w[2:3] * pltpu.roll(t, rows - 1, 0)[lo:hi])

    n_chunks = D_FF // FF_CHUNK

    def up_project(c):
        cu = slice(c * FF_CHUNK, (c + 1) * FF_CHUNK)
        cg = slice(D_FF + c * FF_CHUNK, D_FF + (c + 1) * FF_CHUNK)
        up_ref[c % UP_SLOTS, :, :FF_CHUNK] = jnp.dot(hcat, wup_ref[:, cu], preferred_element_type=F32)
        up_ref[c % UP_SLOTS, :, FF_CHUNK:] = jnp.dot(hcat, wup_ref[:, cg], preferred_element_type=F32)

    for c in range(min(UP_LOOKAHEAD, n_chunks)):
        up_project(c)
    for c in range(n_chunks):
        if c + UP_LOOKAHEAD < n_chunks:
            up_project(c + UP_LOOKAHEAD)
        cu = slice(c * FF_CHUNK, (c + 1) * FF_CHUNK)
        cg = slice(D_FF + c * FF_CHUNK, D_FF + (c + 1) * FF_CHUNK)
        u = conv(up_ref[c % UP_SLOTS, :, :FF_CHUNK], cw_ref[:, cu])
        g = conv(up_ref[c % UP_SLOTS, :, FF_CHUNK:], cw_ref[:, cg])
        act = ((g * jax.nn.sigmoid(g)) * u).astype(BF16)
        part = jnp.dot(act, wdn_ref[cu, :], preferred_element_type=F32)
        acc = part if c == 0 else acc + part

    gate_ffn = mod_ref[0, 0][5:6]
    x2 = x1_ref[0] + gate_ffn * acc
    if final_norm:
        x2 = x2 * _rms_scale(x2) * gfin_ref[...]
    o_ref[0] = x2


def _ffn(h2, x1, mods, w_up, conv_w, w_down, g_final, ctx_tiles, final_norm):
    bsz, t_len, _ = h2.shape
    nt = t_len // TM
    hpt = TM // BF16_SUBLANES
    nhb = t_len // BF16_SUBLANES
    return pl.pallas_call(
        functools.partial(_ffn_kernel, ctx_tiles=ctx_tiles, nt=nt, final_norm=final_norm),
        grid=(bsz, nt),
        in_specs=[pl.BlockSpec((1, TM, D_MODEL), lambda b, i: (b, i, 0)),
                  pl.BlockSpec((1, BF16_SUBLANES, D_MODEL),
                               lambda b, i: (b, jnp.maximum(i * hpt - 1, 0), 0)),
                  pl.BlockSpec((1, BF16_SUBLANES, D_MODEL),
                               lambda b, i: (b, jnp.minimum((i + 1) * hpt, nhb - 1), 0)),
                  pl.BlockSpec((1, TM, D_MODEL), lambda b, i: (b, i, 0)),
                  pl.BlockSpec((1, 1, MOD_ROWS, D_MODEL),
                               lambda b, i: (b, jnp.minimum(i + 1 - ctx_tiles, 1), 0, 0)),
                  _const_spec((D_MODEL, 2 * D_FF)),
                  _const_spec((3, 2 * D_FF)),
                  _const_spec((D_FF, D_MODEL)),
                  _const_spec((1, D_MODEL))],
        out_specs=pl.BlockSpec((1, TM, D_MODEL), lambda b, i: (b, i, 0)),
        out_shape=jax.ShapeDtypeStruct((bsz, t_len, D_MODEL), F32),
        scratch_shapes=[pltpu.VMEM((UP_SLOTS, TM + 2 * BF16_SUBLANES, 2 * FF_CHUNK), F32)],
        compiler_params=_params(48),
        name="conv_ffn",
    )(h2, h2, h2, x1, mods, w_up, conv_w, w_down, g_final)


def _rope_tables(seq, ctx_len):
    rows = seq // GRID_W
    pos_r = np.repeat(np.arange(rows, dtype=np.float32), GRID_W)
    pos_c = np.tile(np.arange(GRID_W, dtype=np.float32), rows)
    n_f = HEAD_DIM // 4
    inv = (np.float32(ROPE_THETA) ** (-np.arange(n_f, dtype=np.float32) / n_f)).astype(np.float32)
    ang = np.concatenate([pos_r[:, None] * inv, pos_c[:, None] * inv], axis=-1).astype(np.float32)
    cos = np.concatenate([np.ones((ctx_len, HEAD_DIM // 2)), np.cos(ang.astype(np.float64))], axis=0)
    sin = np.concatenate([np.zeros((ctx_len, HEAD_DIM // 2)), np.sin(ang.astype(np.float64))], axis=0)
    return jnp.asarray(cos.T, F32), jnp.asarray(sin.T, F32)


def _gate_weights(wa, ba, wi, bi):
    def dense(w):
        eye = jnp.eye(LRU_BLOCKS, dtype=w.dtype)
        return jnp.einsum("nde,nm->ndme", w, eye).reshape(LRU_WIDTH, LRU_WIDTH)
    w = jnp.concatenate([dense(wa[0]), dense(wi[0]), dense(wa[1]), dense(wi[1])], axis=1)
    b = jnp.concatenate([ba[0], bi[0], ba[1], bi[1]])[None, :]
    return w.astype(BF16), b


def kernel(x, c, ctx, c_ctx, w_mod, b_mod, g_mix, g_ffn, w_in, g_q, g_k, lru_conv_w, lru_conv_b,
           lru_wa, lru_ba, lru_wi, lru_bi, lru_lam, sc_conv_w, w_out, w_up, ffn_conv_w, w_down,
           g_final):
    bsz, seq, _ = x.shape
    ctx_len = ctx.shape[1]
    depth = w_mod.shape[0]
    assert ctx_len == TM and seq % TM == 0 and seq % GRID_W == 0

    cos_t, sin_t = _rope_tables(seq, ctx_len)
    ctx_arr, x_arr, x_base = ctx, x, 0
    cc_rows = -(-(bsz + 1) // F32_SUBLANES) * F32_SUBLANES
    cc = jnp.zeros((cc_rows, D_MODEL), F32).at[:bsz].set(c).at[bsz].set(c_ctx)

    mod_layers = _modulation(cc, w_mod, b_mod)

    out = None
    for l in range(depth):
        last = l == depth - 1
        mod_all = mod_layers[l].reshape(cc_rows, N_MOD, D_MODEL)
        mod_x = mod_all[:bsz]
        mod_c = jnp.broadcast_to(mod_all[bsz][None], (bsz, N_MOD, D_MODEL))
        mods = jnp.stack([mod_c, mod_x], axis=1)
        mods = jnp.pad(mods, ((0, 0), (0, 0), (0, MOD_ROWS - N_MOD), (0, 0)))

        gq = jnp.broadcast_to((g_q[l] * Q_SCALE)[:, None], (HEAD_DIM, TM))
        gk = jnp.broadcast_to(g_k[l][:, None], (HEAD_DIM, TM))
        qT, k, vT, rest = _inproj(ctx_arr, x_arr, x_base, mods, g_mix[l][None, :],
                                  w_in[l].astype(BF16), gq, gk, cos_t, sin_t)

        w_gate, b_gate = _gate_weights(lru_wa[l], lru_ba[l], lru_wi[l], lru_bi[l])
        hf, hb, sc = _lru(rest, lru_conv_w[l], lru_conv_b[l][None, :], w_gate, b_gate, lru_lam[l],
                          sc_conv_w[l])

        tile_offset = 1 if last else 0
        x1, h2 = _mixer(qT, k, vT, ctx_arr, x_arr, x_base, hf, hb, rest, sc, mods,
                        w_out[l].astype(BF16), g_ffn[l][None, :], tile_offset)
        out = _ffn(h2, x1, mods, w_up[l].astype(BF16), ffn_conv_w[l], w_down[l].astype(BF16),
                   g_final[None, :], ctx_tiles=1 - tile_offset, final_norm=last)
        ctx_arr, x_arr, x_base = out, out, 1
    return out
```

```python
import functools

import numpy as np
import jax
import jax.numpy as jnp
from jax import lax
from jax.experimental import pallas as pl
from jax.experimental.pallas import tpu as pltpu

F32 = jnp.float32
BF16 = jnp.bfloat16

D_MODEL = 1024
HEAD_DIM = 64
N_HEADS = 8
N_KV_HEADS = 2
HEADS_PER_KV = N_HEADS // N_KV_HEADS
ATTN_WIDTH = N_HEADS * HEAD_DIM
KV_WIDTH = N_KV_HEADS * HEAD_DIM
LRU_WIDTH = 256
LRU_BLOCKS = 4
LRU_BLOCK = LRU_WIDTH // LRU_BLOCKS
LRU_CONV = 4
LRU_C = 8.0
SC_WIDTH = 256
IN_WIDTH = 2048
D_FF = 2816
FF_CHUNK = 256
GRID_W = 64
ROPE_THETA = 10000.0
EPS = 1e-6
Q_SCALE = HEAD_DIM ** -0.5 * 1.4426950408889634

TM = 256
SCORE_LOOKAHEAD = 3
SCORE_SLOTS = SCORE_LOOKAHEAD + 1
UP_LOOKAHEAD = 4
UP_SLOTS = UP_LOOKAHEAD + 1
LANES = 128
F32_SUBLANES = 8
BF16_SUBLANES = 16
N_MOD = 6
MOD_ROWS = 8

OFF_Q = 0
OFF_K = OFF_Q + ATTN_WIDTH
OFF_V = OFF_K + KV_WIDTH
OFF_R = OFF_V + KV_WIDTH
OFF_G = OFF_R + LRU_WIDTH
OFF_B = OFF_G + LRU_WIDTH
OFF_C = OFF_B + SC_WIDTH
OFF_U = OFF_C + SC_WIDTH


def _const_spec(shape):
    nd = len(shape)
    return pl.BlockSpec(shape, lambda *_: (0,) * nd, pipeline_mode=pl.Buffered(1))


def _params(vmem_mb, n_grid_dims=2):
    return pltpu.CompilerParams(dimension_semantics=("arbitrary",) * n_grid_dims,
                                vmem_limit_bytes=vmem_mb * 1024 * 1024)


def _rms_scale(t):
    return lax.rsqrt(jnp.mean(t * t, axis=-1, keepdims=True) + EPS)


def _mod_kernel(c_ref, w_ref, b_ref, o_ref):
    c = c_ref[...]
    h = (c * jax.nn.sigmoid(c)).astype(BF16)
    o_ref[...] = jnp.dot(h, w_ref[...].astype(BF16), preferred_element_type=F32) + b_ref[...]


def _modulation(cc, w_mod, b_mod):
    rows = cc.shape[0]
    depth = w_mod.shape[0]
    return pl.pallas_call(
        _mod_kernel,
        grid=(depth, N_MOD),
        in_specs=[pl.BlockSpec((rows, D_MODEL), lambda l, j: (0, 0)),
                  pl.BlockSpec((None, D_MODEL, D_MODEL), lambda l, j: (l, 0, j)),
                  pl.BlockSpec((None, 1, D_MODEL), lambda l, j: (l, 0, j))],
        out_specs=pl.BlockSpec((None, rows, D_MODEL), lambda l, j: (l, 0, j)),
        out_shape=jax.ShapeDtypeStruct((depth, rows, N_MOD * D_MODEL), F32),
        name="modulation",
    )(cc, w_mod, b_mod[:, None, :])


def _norm_rope_transposed(t, gain, cos, sin):
    half = HEAD_DIM // 2
    out = []
    for h in range(t.shape[0] // HEAD_DIM):
        r = t[h * HEAD_DIM:(h + 1) * HEAD_DIM]
        r = r * lax.rsqrt(jnp.mean(r * r, axis=0, keepdims=True) + EPS) * gain
        x1, x2 = r[:half], r[half:]
        out += [x1 * cos - x2 * sin, x1 * sin + x2 * cos]
    return jnp.concatenate(out, axis=0)


def _inproj_kernel(ctx_ref, x_ref, mod_ref, gmix_ref, w_ref, gq_ref, gk_ref, cos_ref, sin_ref,
                   qT_ref, k_ref, vT_ref, rest_ref, y_even_ref, y_odd_ref, *, nt, n_tiles):
    s = pl.program_id(0)
    tile = jnp.minimum(s, n_tiles - 1) % nt

    def project(y_ref):
        x = jnp.where(tile == 0, ctx_ref[0], x_ref[0])
        mod = mod_ref[0, 0]
        shift, scale = mod[0:1], mod[1:2]
        h = x * _rms_scale(x) * gmix_ref[...]
        h = h * (1.0 + scale) + shift
        y_ref[...] = jnp.dot(h.astype(BF16), w_ref[...], preferred_element_type=F32)

    def finish(y_ref):
        cos, sin = cos_ref[...], sin_ref[...]
        q = _norm_rope_transposed(y_ref[:, OFF_Q:OFF_Q + ATTN_WIDTH].T, gq_ref[...], cos, sin)
        qT_ref[0] = q.astype(BF16)
        k = _norm_rope_transposed(y_ref[:, OFF_K:OFF_K + KV_WIDTH].T, gk_ref[...], cos, sin)
        k_ref[0] = k.T.astype(BF16)
        vT_ref[0] = y_ref[:, OFF_V:OFF_V + KV_WIDTH].T.astype(BF16)
        rest_ref[0, :, 0:LRU_WIDTH] = y_ref[:, OFF_R:OFF_R + LRU_WIDTH]
        rest_ref[0, :, LRU_WIDTH:2 * LRU_WIDTH] = jax.nn.gelu(y_ref[:, OFF_G:OFF_G + LRU_WIDTH])
        rest_ref[0, :, 2 * LRU_WIDTH:2 * LRU_WIDTH + SC_WIDTH] = y_ref[:, OFF_B:OFF_B + SC_WIDTH]
        rest_ref[0, :, 2 * LRU_WIDTH + SC_WIDTH:] = (y_ref[:, OFF_C:OFF_C + SC_WIDTH]
                                                     * y_ref[:, OFF_U:OFF_U + SC_WIDTH])

    @pl.when(s == 0)
    def _():
        y_odd_ref[...] = jnp.zeros_like(y_odd_ref)

    @pl.when(s % 2 == 0)
    def _():
        finish(y_odd_ref)
        project(y_even_ref)

    @pl.when(s % 2 == 1)
    def _():
        finish(y_even_ref)
        project(y_odd_ref)


def _token_specs(x_base, tile_offset=0):
    return [pl.BlockSpec((1, TM, D_MODEL), lambda b, i: (b, 0, 0)),
            pl.BlockSpec((1, TM, D_MODEL),
                         lambda b, i: (b, x_base + jnp.maximum(i + tile_offset - 1, 0), 0))]


def _inproj(ctx_arr, x_arr, x_base, mods, g_mix, w_in, gq, gk, cos_t, sin_t):
    bsz = x_arr.shape[0]
    nt = x_arr.shape[1] // TM - x_base + 1
    t_all = nt * TM
    n_tiles = bsz * nt
    rest_w = 2 * LRU_WIDTH + 2 * SC_WIDTH

    def cur(s):
        sc = jnp.minimum(s, n_tiles - 1)
        return sc // nt, sc % nt

    def prev(s):
        sp = jnp.maximum(s - 1, 0)
        return sp // nt, sp % nt

    rope_spec = pl.BlockSpec((HEAD_DIM // 2, TM), lambda s: (0, prev(s)[1]))
    return pl.pallas_call(
        functools.partial(_inproj_kernel, nt=nt, n_tiles=n_tiles),
        grid=(n_tiles + 1,),
        in_specs=[pl.BlockSpec((1, TM, D_MODEL), lambda s: (cur(s)[0], 0, 0)),
                  pl.BlockSpec((1, TM, D_MODEL),
                               lambda s: (cur(s)[0], x_base + jnp.maximum(cur(s)[1] - 1, 0), 0)),
                  pl.BlockSpec((1, 1, MOD_ROWS, D_MODEL),
                               lambda s: (cur(s)[0], jnp.minimum(cur(s)[1], 1), 0, 0)),
                  _const_spec((1, D_MODEL)),
                  _const_spec((D_MODEL, IN_WIDTH)),
                  _const_spec((HEAD_DIM, TM)),
                  _const_spec((HEAD_DIM, TM)),
                  rope_spec,
                  rope_spec],
        out_specs=[pl.BlockSpec((1, ATTN_WIDTH, TM), lambda s: (prev(s)[0], 0, prev(s)[1])),
                   pl.BlockSpec((1, TM, KV_WIDTH), lambda s: (prev(s)[0], prev(s)[1], 0)),
                   pl.BlockSpec((1, KV_WIDTH, TM), lambda s: (prev(s)[0], 0, prev(s)[1])),
                   pl.BlockSpec((1, TM, rest_w), lambda s: (prev(s)[0], prev(s)[1], 0))],
        out_shape=[jax.ShapeDtypeStruct((bsz, ATTN_WIDTH, t_all), BF16),
                   jax.ShapeDtypeStruct((bsz, t_all, KV_WIDTH), BF16),
                   jax.ShapeDtypeStruct((bsz, KV_WIDTH, t_all), BF16),
                   jax.ShapeDtypeStruct((bsz, t_all, rest_w), F32)],
        scratch_shapes=[pltpu.VMEM((TM, IN_WIDTH), F32)] * 2,
        compiler_params=_params(40, n_grid_dims=1),
        name="inproj",
    )(ctx_arr, x_arr, mods, g_mix, w_in, gq, gk, cos_t, sin_t)


def _with_halo(prev_ref, cur_ref, next_ref, valid_prev, valid_next):
    prev = prev_ref[0] * valid_prev.astype(F32)
    nxt = next_ref[0] * valid_next.astype(F32)
    return jnp.concatenate([prev, cur_ref[0], nxt], axis=0)


def _shifted(tcat, offset):
    rows = tcat.shape[0]
    if offset == 0:
        return tcat[F32_SUBLANES:F32_SUBLANES + TM]
    return pltpu.roll(tcat, (-offset) % rows, 0)[F32_SUBLANES:F32_SUBLANES + TM]


def _lru_coeffs(rcat, conv_w, conv_b, w_gate, b_gate, sp_lam):
    xc = conv_b
    for j in range(LRU_CONV):
        xc = xc + conv_w[j:j + 1] * _shifted(rcat, j - 2)
    z = jnp.dot(xc.astype(BF16), w_gate, preferred_element_type=F32) + b_gate
    r_gate = jax.nn.sigmoid(z[:, :LRU_WIDTH])
    i_gate = jax.nn.sigmoid(z[:, LRU_WIDTH:])
    log_a = (-LRU_C) * r_gate * sp_lam
    a = jnp.exp(log_a)
    one_minus_a2 = -jnp.tanh(log_a) * (1.0 + a * a)
    return a, jnp.sqrt(one_minus_a2) * (i_gate * xc)


def _lru_kernel(rf_ref, rfp_ref, rfn_ref, rb_ref, rbp_ref, rbn_ref, bg_ref, cu_ref, cup_ref, cun_ref,
                cw_ref, cb_ref, wg_ref, bgate_ref, lam_ref, scw_ref,
                hf_ref, hb_ref, sc_ref,
                af_s, bf_s, ab_s, bb_s, hf_state, hb_state, *, nt):
    i = pl.program_id(1)
    j = jnp.where(i == 0, 0, nt - i)

    @pl.when(i == 0)
    def _():
        hf_state[...] = jnp.zeros_like(hf_state)
        hb_state[...] = jnp.zeros_like(hb_state)

    def halo_valid(t):
        return jnp.logical_and(t != 0, t != 1), jnp.logical_and(t != 0, t != nt - 1)

    lam = lam_ref[...]
    sp_lam = jnp.maximum(-lam, 0.0) + jnp.log1p(jnp.exp(-jnp.abs(lam)))
    cw = cw_ref[...]
    cb = cb_ref[...]

    vp, vn = halo_valid(i)
    a, b = _lru_coeffs(_with_halo(rfp_ref, rf_ref, rfn_ref, vp, vn), cw, cb,
                       wg_ref[:, :2 * LRU_WIDTH], bgate_ref[:, :2 * LRU_WIDTH], sp_lam[0:1])
    _block_prefix(a, b, af_s, bf_s, reverse=False)
    cucat = _with_halo(cup_ref, cu_ref, cun_ref, vp, vn)
    scw = scw_ref[...]
    conv = scw[0:1] * _shifted(cucat, -1) + scw[1:2] * _shifted(cucat, 0) + scw[2:3] * _shifted(cucat, 1)
    sc_ref[0] = (bg_ref[0] * conv).astype(BF16)

    vp, vn = halo_valid(j)
    a, b = _lru_coeffs(_with_halo(rbp_ref, rb_ref, rbn_ref, vp, vn), cw, cb,
                       wg_ref[:, 2 * LRU_WIDTH:], bgate_ref[:, 2 * LRU_WIDTH:], sp_lam[1:2])
    _block_prefix(a, b, ab_s, bb_s, reverse=True)

    _carry_blocks(af_s, bf_s, hf_ref, hf_state, reverse=False)
    _carry_blocks(ab_s, bb_s, hb_ref, hb_state, reverse=True)


def _block_prefix(a, b, a_s, b_s, reverse):
    n = F32_SUBLANES
    a = a.reshape(TM // n, n, LRU_WIDTH)
    b = b.reshape(TM // n, n, LRU_WIDTH)
    row = lax.broadcasted_iota(jnp.int32, a.shape, 1)
    d = 1
    while d < n:
        shift, keep = (n - d, row < n - d) if reverse else (d, row >= d)
        a_prev = jnp.where(keep, pltpu.roll(a, shift, 1), 1.0)
        b_prev = jnp.where(keep, pltpu.roll(b, shift, 1), 0.0)
        b = a * b_prev + b
        a = a * a_prev
        d *= 2
    a_s[...] = a
    b_s[...] = b


def _carry_blocks(a_s, b_s, out_ref, state_ref, reverse):
    n = F32_SUBLANES
    nb = TM // n
    last = 0 if reverse else n - 1
    h_in = state_ref[...]
    for v in (reversed(range(nb)) if reverse else range(nb)):
        h = a_s[v] * h_in + b_s[v]
        out_ref[0, v * n:(v + 1) * n, :] = h
        h_in = h[last:last + 1]
    state_ref[...] = h_in


def _lru(rest, conv_w, conv_b, w_gate, b_gate, lam, sc_w):
    bsz, t_all, _ = rest.shape
    nt = t_all // TM
    hpt = TM // F32_SUBLANES
    nhb = t_all // F32_SUBLANES

    def rev(i):
        return jnp.where(i == 0, 0, nt - i)

    def tile(col, order):
        return pl.BlockSpec((1, TM, LRU_WIDTH), lambda b, i: (b, order(i), col))

    def prev_halo(col, order):
        return pl.BlockSpec((1, F32_SUBLANES, LRU_WIDTH),
                            lambda b, i: (b, jnp.maximum(order(i) * hpt - 1, 0), col))

    def next_halo(col, order):
        return pl.BlockSpec((1, F32_SUBLANES, LRU_WIDTH),
                            lambda b, i: (b, jnp.minimum((order(i) + 1) * hpt, nhb - 1), col))

    fwd = lambda i: i
    out_f32 = jax.ShapeDtypeStruct((bsz, t_all, LRU_WIDTH), F32)
    return pl.pallas_call(
        functools.partial(_lru_kernel, nt=nt),
        grid=(bsz, nt),
        in_specs=[tile(0, fwd), prev_halo(0, fwd), next_halo(0, fwd),
                  tile(0, rev), prev_halo(0, rev), next_halo(0, rev),
                  tile(2, fwd),
                  tile(3, fwd), prev_halo(3, fwd), next_halo(3, fwd),
                  _const_spec((LRU_CONV, LRU_WIDTH)),
                  _const_spec((1, LRU_WIDTH)),
                  _const_spec((LRU_WIDTH, 4 * LRU_WIDTH)),
                  _const_spec((1, 4 * LRU_WIDTH)),
                  _const_spec((2, LRU_WIDTH)),
                  _const_spec((3, SC_WIDTH))],
        out_specs=[pl.BlockSpec((1, TM, LRU_WIDTH), lambda b, i: (b, i, 0)),
                   pl.BlockSpec((1, TM, LRU_WIDTH), lambda b, i: (b, rev(i), 0)),
                   pl.BlockSpec((1, TM, SC_WIDTH), lambda b, i: (b, i, 0))],
        out_shape=[out_f32, out_f32, jax.ShapeDtypeStruct((bsz, t_all, SC_WIDTH), BF16)],
        scratch_shapes=([pltpu.VMEM((TM // F32_SUBLANES, F32_SUBLANES, LRU_WIDTH), F32)] * 4
                        + [pltpu.VMEM((1, LRU_WIDTH), F32)] * 2),
        compiler_params=_params(32),
        name="lru_scan",
    )(rest, rest, rest, rest, rest, rest, rest, rest, rest, rest,
      conv_w, conv_b, w_gate, b_gate, lam, sc_w)


def _mixer_kernel(qT_ref, k_ref, vT_ref, ctx_ref, x_ref, hf_ref, hb_ref, gg_ref, sc_ref, mod_ref, wo_ref,
                  gffn_ref, x1_ref, h2_ref, oT_ref, s_ref, *, tile_offset, n_key_tiles):
    i = pl.program_id(1) + tile_offset
    ones_rows = jnp.ones((BF16_SUBLANES, TM), BF16)

    def project_out():
        x = x_ref[0]
        if tile_offset == 0:
            x = jnp.where(i == 0, ctx_ref[0], x)
        mod = mod_ref[0, 0]
        gate_mix, shift, scale = mod[2:3], mod[3:4], mod[4:5]
        att = oT_ref[...].T.astype(BF16)
        lru = (gg_ref[0] * (hf_ref[0] + hb_ref[0])).astype(BF16)
        y = (jnp.dot(att, wo_ref[0:ATTN_WIDTH], preferred_element_type=F32)
             + jnp.dot(lru, wo_ref[ATTN_WIDTH:ATTN_WIDTH + LRU_WIDTH], preferred_element_type=F32)
             + jnp.dot(sc_ref[0], wo_ref[ATTN_WIDTH + LRU_WIDTH:], preferred_element_type=F32))
        x1 = x + gate_mix * y
        x1_ref[0] = x1
        h2 = x1 * _rms_scale(x1) * gffn_ref[...]
        h2_ref[0] = (h2 * (1.0 + scale) + shift).astype(BF16)

    def run(n_tiles):
        items = [(pr, c) for pr in range(N_HEADS // 2) for c in range(n_tiles)]
        qpads = {}

        def qpad_of(pr):
            if pr not in qpads:
                row0 = pr * 2 * HEAD_DIM
                q2 = jnp.concatenate([qT_ref[0, row0:row0 + HEAD_DIM, :],
                                      qT_ref[0, row0 + HEAD_DIM:row0 + 2 * HEAD_DIM, :]], axis=1)
                zero = jnp.zeros_like(q2)
                first_kv = (2 * pr) // HEADS_PER_KV == 0
                qpads[pr] = jnp.concatenate([q2, zero] if first_kv else [zero, q2], axis=0)
            return qpads[pr]

        def scores(t):
            pr, c = items[t]
            s = jnp.dot(k_ref[0, c * TM:(c + 1) * TM, :], qpad_of(pr), preferred_element_type=F32)
            s_ref[t % SCORE_SLOTS] = s
            return jnp.max(s, axis=0, keepdims=True)

        tile_max = [scores(t) for t in range(min(SCORE_LOOKAHEAD, len(items)))]
        m = acc = None
        for t, (pr, c) in enumerate(items):
            if t + SCORE_LOOKAHEAD < len(items):
                tile_max.append(scores(t + SCORE_LOOKAHEAD))
            if c == 0:
                m = jnp.full((1, 2 * TM), -jnp.inf, F32)
                acc = jnp.zeros((HEAD_DIM + BF16_SUBLANES, 2 * TM), F32)
            g = (2 * pr) // HEADS_PER_KV
            m_new = jnp.maximum(m, tile_max[t])
            alpha = jnp.exp2(m - m_new)
            p = jnp.exp2(s_ref[t % SCORE_SLOTS] - m_new).astype(BF16)
            vt = jnp.concatenate([vT_ref[0, g * HEAD_DIM:(g + 1) * HEAD_DIM, c * TM:(c + 1) * TM],
                                  ones_rows], axis=0)
            acc = alpha * acc + jnp.dot(vt, p, preferred_element_type=F32)
            m = m_new
            tile_max[t] = None
            if c == n_tiles - 1:
                o = acc[:HEAD_DIM] / acc[HEAD_DIM:HEAD_DIM + 1]
                row0 = pr * 2 * HEAD_DIM
                oT_ref[row0:row0 + HEAD_DIM, :] = o[:, :TM]
                oT_ref[row0 + HEAD_DIM:row0 + 2 * HEAD_DIM, :] = o[:, TM:]
        project_out()

    @pl.when(i == 0)
    def _():
        run(1)

    @pl.when(i != 0)
    def _():
        run(n_key_tiles)


def _mixer(qT, k, vT, ctx_arr, x_arr, x_base, hf, hb, rest, sc, mods, w_out, g_ffn, tile_offset):
    bsz, t_all, _ = k.shape
    nt = t_all // TM
    nq = nt - tile_offset
    off = lambda b, i: (b, i + tile_offset, 0)
    return pl.pallas_call(
        functools.partial(_mixer_kernel, tile_offset=tile_offset, n_key_tiles=nt),
        grid=(bsz, nq),
        in_specs=[pl.BlockSpec((1, ATTN_WIDTH, TM), lambda b, i: (b, 0, i + tile_offset)),
                  pl.BlockSpec((1, t_all, KV_WIDTH), lambda b, i: (b, 0, 0)),
                  pl.BlockSpec((1, KV_WIDTH, t_all), lambda b, i: (b, 0, 0))]
                 + _token_specs(x_base, tile_offset) + [
                  pl.BlockSpec((1, TM, LRU_WIDTH), off),
                  pl.BlockSpec((1, TM, LRU_WIDTH), off),
                  pl.BlockSpec((1, TM, LRU_WIDTH), lambda b, i: (b, i + tile_offset, 1)),
                  pl.BlockSpec((1, TM, SC_WIDTH), off),
                  pl.BlockSpec((1, 1, MOD_ROWS, D_MODEL),
                               lambda b, i: (b, jnp.minimum(i + tile_offset, 1), 0, 0)),
                  _const_spec((D_MODEL, D_MODEL)),
                  _const_spec((1, D_MODEL))],
        out_specs=[pl.BlockSpec((1, TM, D_MODEL), lambda b, i: (b, i, 0)),
                   pl.BlockSpec((1, TM, D_MODEL), lambda b, i: (b, i, 0))],
        out_shape=[jax.ShapeDtypeStruct((bsz, nq * TM, D_MODEL), F32),
                   jax.ShapeDtypeStruct((bsz, nq * TM, D_MODEL), BF16)],
        scratch_shapes=[pltpu.VMEM((ATTN_WIDTH, TM), F32), pltpu.VMEM((SCORE_SLOTS, TM, 2 * TM), F32)],
        compiler_params=_params(40),
        name="mixer",
    )(qT, k, vT, ctx_arr, x_arr, hf, hb, rest, sc, mods, w_out, g_ffn)


def _ffn_kernel(h_ref, hp_ref, hn_ref, x1_ref, mod_ref, wup_ref, cw_ref, wdn_ref, gfin_ref,
                o_ref, acc_ref, up_ref, *, ctx_tiles, nt, final_norm):
    i = pl.program_id(1)
    valid_prev = jnp.logical_and(i != 0, i != ctx_tiles)
    valid_next = jnp.logical_and(i != nt - 1, i != ctx_tiles - 1)
    hp = hp_ref[0] * valid_prev.astype(BF16)
    hn = hn_ref[0] * valid_next.astype(BF16)
    hcat = jnp.concatenate([hp, h_ref[0], hn], axis=0)
    rows = hcat.shape[0]

    def conv(t, w):
        lo, hi = BF16_SUBLANES, BF16_SUBLANES + TM
        return (w[0:1] * pltpu.roll(t, 1, 0)[lo:hi] + w[1:2] * t[lo:hi]
                + w[2:3] * pltpu.roll(t, rows - 1, 0)[lo:hi])

    n_chunks = D_FF // FF_CHUNK

    def up_project(c):
        cu = slice(c * FF_CHUNK, (c + 1) * FF_CHUNK)
        cg = slice(D_FF + c * FF_CHUNK, D_FF + (c + 1) * FF_CHUNK)
        up_ref[c % UP_SLOTS, :, :FF_CHUNK] = jnp.dot(hcat, wup_ref[:, cu], preferred_element_type=F32)
        up_ref[c % UP_SLOTS, :, FF_CHUNK:] = jnp.dot(hcat, wup_ref[:, cg], preferred_element_type=F32)

    for c in range(min(UP_LOOKAHEAD, n_chunks)):
        up_project(c)
    for c in range(n_chunks):
        if c + UP_LOOKAHEAD < n_chunks:
            up_project(c + UP_LOOKAHEAD)
        cu = slice(c * FF_CHUNK, (c + 1) * FF_CHUNK)
        cg = slice(D_FF + c * FF_CHUNK, D_FF + (c + 1) * FF_CHUNK)
        u = conv(up_ref[c % UP_SLOTS, :, :FF_CHUNK], cw_ref[:, cu])
        g = conv(up_ref[c % UP_SLOTS, :, FF_CHUNK:], cw_ref[:, cg])
        act = ((g * jax.nn.sigmoid(g)) * u).astype(BF16)
        part = jnp.dot(act, wdn_ref[cu, :], preferred_element_type=F32)
        if c == 0:
            acc_ref[...] = part
        else:
            acc_ref[...] += part

    gate_ffn = mod_ref[0, 0][5:6]
    x2 = x1_ref[0] + gate_ffn * acc_ref[...]
    if final_norm:
        x2 = x2 * _rms_scale(x2) * gfin_ref[...]
    o_ref[0] = x2


def _ffn(h2, x1, mods, w_up, conv_w, w_down, g_final, ctx_tiles, final_norm):
    bsz, t_len, _ = h2.shape
    nt = t_len // TM
    hpt = TM // BF16_SUBLANES
    nhb = t_len // BF16_SUBLANES
    return pl.pallas_call(
        functools.partial(_ffn_kernel, ctx_tiles=ctx_tiles, nt=nt, final_norm=final_norm),
        grid=(bsz, nt),
        in_specs=[pl.BlockSpec((1, TM, D_MODEL), lambda b, i: (b, i, 0)),
                  pl.BlockSpec((1, BF16_SUBLANES, D_MODEL),
                               lambda b, i: (b, jnp.maximum(i * hpt - 1, 0), 0)),
                  pl.BlockSpec((1, BF16_SUBLANES, D_MODEL),
                               lambda b, i: (b, jnp.minimum((i + 1) * hpt, nhb - 1), 0)),
                  pl.BlockSpec((1, TM, D_MODEL), lambda b, i: (b, i, 0)),
                  pl.BlockSpec((1, 1, MOD_ROWS, D_MODEL),
                               lambda b, i: (b, jnp.minimum(i + 1 - ctx_tiles, 1), 0, 0)),
                  _const_spec((D_MODEL, 2 * D_FF)),
                  _const_spec((3, 2 * D_FF)),
                  _const_spec((D_FF, D_MODEL)),
                  _const_spec((1, D_MODEL))],
        out_specs=pl.BlockSpec((1, TM, D_MODEL), lambda b, i: (b, i, 0)),
        out_shape=jax.ShapeDtypeStruct((bsz, t_len, D_MODEL), F32),
        scratch_shapes=[pltpu.VMEM((TM, D_MODEL), F32),
                        pltpu.VMEM((UP_SLOTS, TM + 2 * BF16_SUBLANES, 2 * FF_CHUNK), F32)],
        compiler_params=_params(48),
        name="conv_ffn",
    )(h2, h2, h2, x1, mods, w_up, conv_w, w_down, g_final)


def _rope_tables(seq, ctx_len):
    rows = seq // GRID_W
    pos_r = np.repeat(np.arange(rows, dtype=np.float32), GRID_W)
    pos_c = np.tile(np.arange(GRID_W, dtype=np.float32), rows)
    n_f = HEAD_DIM // 4
    inv = (np.float32(ROPE_THETA) ** (-np.arange(n_f, dtype=np.float32) / n_f)).astype(np.float32)
    ang = np.concatenate([pos_r[:, None] * inv, pos_c[:, None] * inv], axis=-1).astype(np.float32)
    cos = np.concatenate([np.ones((ctx_len, HEAD_DIM // 2)), np.cos(ang.astype(np.float64))], axis=0)
    sin = np.concatenate([np.zeros((ctx_len, HEAD_DIM // 2)), np.sin(ang.astype(np.float64))], axis=0)
    return jnp.asarray(cos.T, F32), jnp.asarray(sin.T, F32)


def _gate_weights(wa, ba, wi, bi):
    def dense(w):
        eye = jnp.eye(LRU_BLOCKS, dtype=w.dtype)
        return jnp.einsum("nde,nm->ndme", w, eye).reshape(LRU_WIDTH, LRU_WIDTH)
    w = jnp.concatenate([dense(wa[0]), dense(wi[0]), dense(wa[1]), dense(wi[1])], axis=1)
    b = jnp.concatenate([ba[0], bi[0], ba[1], bi[1]])[None, :]
    return w.astype(BF16), b


def kernel(x, c, ctx, c_ctx, w_mod, b_mod, g_mix, g_ffn, w_in, g_q, g_k, lru_conv_w, lru_conv_b,
           lru_wa, lru_ba, lru_wi, lru_bi, lru_lam, sc_conv_w, w_out, w_up, ffn_conv_w, w_down,
           g_final):
    bsz, seq, _ = x.shape
    ctx_len = ctx.shape[1]
    depth = w_mod.shape[0]
    assert ctx_len == TM and seq % TM == 0 and seq % GRID_W == 0

    cos_t, sin_t = _rope_tables(seq, ctx_len)
    ctx_arr, x_arr, x_base = ctx, x, 0
    cc_rows = -(-(bsz + 1) // F32_SUBLANES) * F32_SUBLANES
    cc = jnp.zeros((cc_rows, D_MODEL), F32).at[:bsz].set(c).at[bsz].set(c_ctx)

    mod_layers = _modulation(cc, w_mod, b_mod)

    out = None
    for l in range(depth):
        last = l == depth - 1
        mod_all = mod_layers[l].reshape(cc_rows, N_MOD, D_MODEL)
        mod_x = mod_all[:bsz]
        mod_c = jnp.broadcast_to(mod_all[bsz][None], (bsz, N_MOD, D_MODEL))
        mods = jnp.stack([mod_c, mod_x], axis=1)
        mods = jnp.pad(mods, ((0, 0), (0, 0), (0, MOD_ROWS - N_MOD), (0, 0)))

        gq = jnp.broadcast_to((g_q[l] * Q_SCALE)[:, None], (HEAD_DIM, TM))
        gk = jnp.broadcast_to(g_k[l][:, None], (HEAD_DIM, TM))
        qT, k, vT, rest = _inproj(ctx_arr, x_arr, x_base, mods, g_mix[l][None, :],
                                  w_in[l].astype(BF16), gq, gk, cos_t, sin_t)

        w_gate, b_gate = _gate_weights(lru_wa[l], lru_ba[l], lru_wi[l], lru_bi[l])
        hf, hb, sc = _lru(rest, lru_conv_w[l], lru_conv_b[l][None, :], w_gate, b_gate, lru_lam[l],
                          sc_conv_w[l])

        tile_offset = 1 if last else 0
        x1, h2 = _mixer(qT, k, vT, ctx_arr, x_arr, x_base, hf, hb, rest, sc, mods,
                        w_out[l].astype(BF16), g_ffn[l][None, :], tile_offset)
        out = _ffn(h2, x1, mods, w_up[l].astype(BF16), ffn_conv_w[l], w_down[l].astype(BF16),
                   g_final[None, :], ctx_tiles=1 - tile_offset, final_norm=last)
        ctx_arr, x_arr, x_base = out, out, 1
    return out
```

```python
import functools

import numpy as np
import jax
import jax.numpy as jnp
from jax import lax
from jax.experimental import pallas as pl
from jax.experimental.pallas import tpu as pltpu

F32 = jnp.float32
BF16 = jnp.bfloat16

D_MODEL = 1024
HEAD_DIM = 64
N_HEADS = 8
N_KV_HEADS = 2
HEADS_PER_KV = N_HEADS // N_KV_HEADS
ATTN_WIDTH = N_HEADS * HEAD_DIM
KV_WIDTH = N_KV_HEADS * HEAD_DIM
LRU_WIDTH = 256
LRU_BLOCKS = 4
LRU_BLOCK = LRU_WIDTH // LRU_BLOCKS
LRU_CONV = 4
LRU_C = 8.0
SC_WIDTH = 256
IN_WIDTH = 2048
D_FF = 2816
FF_CHUNK = 256
GRID_W = 64
ROPE_THETA = 10000.0
EPS = 1e-6
Q_SCALE = HEAD_DIM ** -0.5 * 1.4426950408889634

TM = 256
FFN_TILE = 512
SCORE_LOOKAHEAD = 3
SCORE_SLOTS = SCORE_LOOKAHEAD + 1
UP_LOOKAHEAD = 3
UP_SLOTS = UP_LOOKAHEAD + 1
LANES = 128
F32_SUBLANES = 8
BF16_SUBLANES = 16
N_MOD = 6
MOD_ROWS = 8

OFF_Q = 0
OFF_K = OFF_Q + ATTN_WIDTH
OFF_V = OFF_K + KV_WIDTH
OFF_R = OFF_V + KV_WIDTH
OFF_G = OFF_R + LRU_WIDTH
OFF_B = OFF_G + LRU_WIDTH
OFF_C = OFF_B + SC_WIDTH
OFF_U = OFF_C + SC_WIDTH


def _const_spec(shape):
    nd = len(shape)
    return pl.BlockSpec(shape, lambda *_: (0,) * nd, pipeline_mode=pl.Buffered(1))


def _params(vmem_mb, n_grid_dims=2):
    return pltpu.CompilerParams(dimension_semantics=("arbitrary",) * n_grid_dims,
                                vmem_limit_bytes=vmem_mb * 1024 * 1024)


def _rms_scale(t):
    return lax.rsqrt(jnp.mean(t * t, axis=-1, keepdims=True) + EPS)


def _mod_kernel(c_ref, w_ref, b_ref, o_ref):
    c = c_ref[...]
    h = (c * jax.nn.sigmoid(c)).astype(BF16)
    o_ref[...] = jnp.dot(h, w_ref[...].astype(BF16), preferred_element_type=F32) + b_ref[...]


def _modulation(cc, w_mod, b_mod):
    rows = cc.shape[0]
    depth = w_mod.shape[0]
    return pl.pallas_call(
        _mod_kernel,
        grid=(depth, N_MOD),
        in_specs=[pl.BlockSpec((rows, D_MODEL), lambda l, j: (0, 0)),
                  pl.BlockSpec((None, D_MODEL, D_MODEL), lambda l, j: (l, 0, j)),
                  pl.BlockSpec((None, 1, D_MODEL), lambda l, j: (l, 0, j))],
        out_specs=pl.BlockSpec((None, rows, D_MODEL), lambda l, j: (l, 0, j)),
        out_shape=jax.ShapeDtypeStruct((depth, rows, N_MOD * D_MODEL), F32),
        name="modulation",
    )(cc, w_mod, b_mod[:, None, :])


def _norm_rope_transposed(t, gain, cos, sin):
    half = HEAD_DIM // 2
    out = []
    for h in range(t.shape[0] // HEAD_DIM):
        r = t[h * HEAD_DIM:(h + 1) * HEAD_DIM]
        r = r * lax.rsqrt(jnp.mean(r * r, axis=0, keepdims=True) + EPS) * gain
        x1, x2 = r[:half], r[half:]
        out += [x1 * cos - x2 * sin, x1 * sin + x2 * cos]
    return jnp.concatenate(out, axis=0)


def _inproj_kernel(ctx_ref, x_ref, mod_ref, gmix_ref, w_ref, gq_ref, gk_ref, cos_ref, sin_ref,
                   qT_ref, k_ref, vT_ref, rest_ref, y_even_ref, y_odd_ref, *, nt, n_tiles):
    s = pl.program_id(0)
    tile = jnp.minimum(s, n_tiles - 1) % nt

    def project(y_ref):
        x = jnp.where(tile == 0, ctx_ref[0], x_ref[0])
        mod = mod_ref[0, 0]
        shift, scale = mod[0:1], mod[1:2]
        h = x * _rms_scale(x) * gmix_ref[...]
        h = h * (1.0 + scale) + shift
        y_ref[...] = jnp.dot(h.astype(BF16), w_ref[...], preferred_element_type=F32)

    def finish(y_ref):
        cos, sin = cos_ref[...], sin_ref[...]
        q = _norm_rope_transposed(y_ref[:, OFF_Q:OFF_Q + ATTN_WIDTH].T, gq_ref[...], cos, sin)
        qT_ref[0] = q.astype(BF16)
        k = _norm_rope_transposed(y_ref[:, OFF_K:OFF_K + KV_WIDTH].T, gk_ref[...], cos, sin)
        k_ref[0] = k.T.astype(BF16)
        vT_ref[0] = y_ref[:, OFF_V:OFF_V + KV_WIDTH].T.astype(BF16)
        rest_ref[0, :, 0:LRU_WIDTH] = y_ref[:, OFF_R:OFF_R + LRU_WIDTH]
        rest_ref[0, :, LRU_WIDTH:2 * LRU_WIDTH] = jax.nn.gelu(y_ref[:, OFF_G:OFF_G + LRU_WIDTH])
        rest_ref[0, :, 2 * LRU_WIDTH:2 * LRU_WIDTH + SC_WIDTH] = y_ref[:, OFF_B:OFF_B + SC_WIDTH]
        rest_ref[0, :, 2 * LRU_WIDTH + SC_WIDTH:] = (y_ref[:, OFF_C:OFF_C + SC_WIDTH]
                                                     * y_ref[:, OFF_U:OFF_U + SC_WIDTH])

    @pl.when(s == 0)
    def _():
        y_odd_ref[...] = jnp.zeros_like(y_odd_ref)

    @pl.when(s % 2 == 0)
    def _():
        finish(y_odd_ref)
        project(y_even_ref)

    @pl.when(s % 2 == 1)
    def _():
        finish(y_even_ref)
        project(y_odd_ref)


def _token_specs(x_base, tile_offset=0):
    return [pl.BlockSpec((1, TM, D_MODEL), lambda b, i: (b, 0, 0)),
            pl.BlockSpec((1, TM, D_MODEL),
                         lambda b, i: (b, x_base + jnp.maximum(i + tile_offset - 1, 0), 0))]


def _inproj(ctx_arr, x_arr, x_base, mods, g_mix, w_in, gq, gk, cos_t, sin_t):
    bsz = x_arr.shape[0]
    nt = x_arr.shape[1] // TM - x_base + 1
    t_all = nt * TM
    n_tiles = bsz * nt
    rest_w = 2 * LRU_WIDTH + 2 * SC_WIDTH

    def cur(s):
        sc = jnp.minimum(s, n_tiles - 1)
        return sc // nt, sc % nt

    def prev(s):
        sp = jnp.maximum(s - 1, 0)
        return sp // nt, sp % nt

    rope_spec = pl.BlockSpec((HEAD_DIM // 2, TM), lambda s: (0, prev(s)[1]))
    return pl.pallas_call(
        functools.partial(_inproj_kernel, nt=nt, n_tiles=n_tiles),
        grid=(n_tiles + 1,),
        in_specs=[pl.BlockSpec((1, TM, D_MODEL), lambda s: (cur(s)[0], 0, 0)),
                  pl.BlockSpec((1, TM, D_MODEL),
                               lambda s: (cur(s)[0], x_base + jnp.maximum(cur(s)[1] - 1, 0), 0)),
                  pl.BlockSpec((1, 1, MOD_ROWS, D_MODEL),
                               lambda s: (cur(s)[0], jnp.minimum(cur(s)[1], 1), 0, 0)),
                  _const_spec((1, D_MODEL)),
                  _const_spec((D_MODEL, IN_WIDTH)),
                  _const_spec((HEAD_DIM, TM)),
                  _const_spec((HEAD_DIM, TM)),
                  rope_spec,
                  rope_spec],
        out_specs=[pl.BlockSpec((1, ATTN_WIDTH, TM), lambda s: (prev(s)[0], 0, prev(s)[1])),
                   pl.BlockSpec((1, TM, KV_WIDTH), lambda s: (prev(s)[0], prev(s)[1], 0)),
                   pl.BlockSpec((1, KV_WIDTH, TM), lambda s: (prev(s)[0], 0, prev(s)[1])),
                   pl.BlockSpec((1, TM, rest_w), lambda s: (prev(s)[0], prev(s)[1], 0))],
        out_shape=[jax.ShapeDtypeStruct((bsz, ATTN_WIDTH, t_all), BF16),
                   jax.ShapeDtypeStruct((bsz, t_all, KV_WIDTH), BF16),
                   jax.ShapeDtypeStruct((bsz, KV_WIDTH, t_all), BF16),
                   jax.ShapeDtypeStruct((bsz, t_all, rest_w), F32)],
        scratch_shapes=[pltpu.VMEM((TM, IN_WIDTH), F32)] * 2,
        compiler_params=_params(40, n_grid_dims=1),
        name="inproj",
    )(ctx_arr, x_arr, mods, g_mix, w_in, gq, gk, cos_t, sin_t)


def _with_halo(prev_ref, cur_ref, next_ref, valid_prev, valid_next):
    prev = prev_ref[0] * valid_prev.astype(F32)
    nxt = next_ref[0] * valid_next.astype(F32)
    return jnp.concatenate([prev, cur_ref[0], nxt], axis=0)


def _shifted(tcat, offset):
    rows = tcat.shape[0]
    if offset == 0:
        return tcat[F32_SUBLANES:F32_SUBLANES + TM]
    return pltpu.roll(tcat, (-offset) % rows, 0)[F32_SUBLANES:F32_SUBLANES + TM]


def _lru_coeffs(rcat, conv_w, conv_b, w_gate, b_gate, sp_lam):
    xc = conv_b
    for j in range(LRU_CONV):
        xc = xc + conv_w[j:j + 1] * _shifted(rcat, j - 2)
    z = jnp.dot(xc.astype(BF16), w_gate, preferred_element_type=F32) + b_gate
    r_gate = jax.nn.sigmoid(z[:, :LRU_WIDTH])
    i_gate = jax.nn.sigmoid(z[:, LRU_WIDTH:])
    log_a = (-LRU_C) * r_gate * sp_lam
    a = jnp.exp(log_a)
    one_minus_a2 = -jnp.tanh(log_a) * (1.0 + a * a)
    return a, jnp.sqrt(one_minus_a2) * (i_gate * xc)


def _lru_kernel(rf_ref, rfp_ref, rfn_ref, rb_ref, rbp_ref, rbn_ref, bg_ref, cu_ref, cup_ref, cun_ref,
                cw_ref, cb_ref, wg_ref, bgate_ref, lam_ref, scw_ref,
                hf_ref, hb_ref, sc_ref,
                af_s, bf_s, ab_s, bb_s, hf_state, hb_state, *, nt):
    i = pl.program_id(1)
    j = jnp.where(i == 0, 0, nt - i)

    @pl.when(i == 0)
    def _():
        hf_state[...] = jnp.zeros_like(hf_state)
        hb_state[...] = jnp.zeros_like(hb_state)

    def halo_valid(t):
        return jnp.logical_and(t != 0, t != 1), jnp.logical_and(t != 0, t != nt - 1)

    lam = lam_ref[...]
    sp_lam = jnp.maximum(-lam, 0.0) + jnp.log1p(jnp.exp(-jnp.abs(lam)))
    cw = cw_ref[...]
    cb = cb_ref[...]

    vp, vn = halo_valid(i)
    a, b = _lru_coeffs(_with_halo(rfp_ref, rf_ref, rfn_ref, vp, vn), cw, cb,
                       wg_ref[:, :2 * LRU_WIDTH], bgate_ref[:, :2 * LRU_WIDTH], sp_lam[0:1])
    _block_prefix(a, b, af_s, bf_s, reverse=False)
    cucat = _with_halo(cup_ref, cu_ref, cun_ref, vp, vn)
    scw = scw_ref[...]
    conv = scw[0:1] * _shifted(cucat, -1) + scw[1:2] * _shifted(cucat, 0) + scw[2:3] * _shifted(cucat, 1)
    sc_ref[0] = (bg_ref[0] * conv).astype(BF16)

    vp, vn = halo_valid(j)
    a, b = _lru_coeffs(_with_halo(rbp_ref, rb_ref, rbn_ref, vp, vn), cw, cb,
                       wg_ref[:, 2 * LRU_WIDTH:], bgate_ref[:, 2 * LRU_WIDTH:], sp_lam[1:2])
    _block_prefix(a, b, ab_s, bb_s, reverse=True)

    _carry_blocks(af_s, bf_s, hf_ref, hf_state, reverse=False)
    _carry_blocks(ab_s, bb_s, hb_ref, hb_state, reverse=True)


def _block_prefix(a, b, a_s, b_s, reverse):
    n = F32_SUBLANES
    a = a.reshape(TM // n, n, LRU_WIDTH)
    b = b.reshape(TM // n, n, LRU_WIDTH)
    row = lax.broadcasted_iota(jnp.int32, a.shape, 1)
    d = 1
    while d < n:
        shift, keep = (n - d, row < n - d) if reverse else (d, row >= d)
        a_prev = jnp.where(keep, pltpu.roll(a, shift, 1), 1.0)
        b_prev = jnp.where(keep, pltpu.roll(b, shift, 1), 0.0)
        b = a * b_prev + b
        a = a * a_prev
        d *= 2
    a_s[...] = a
    b_s[...] = b


def _carry_blocks(a_s, b_s, out_ref, state_ref, reverse):
    n = F32_SUBLANES
    nb = TM // n
    last = 0 if reverse else n - 1
    h_in = state_ref[...]
    for v in (reversed(range(nb)) if reverse else range(nb)):
        h = a_s[v] * h_in + b_s[v]
        out_ref[0, v * n:(v + 1) * n, :] = h
        h_in = h[last:last + 1]
    state_ref[...] = h_in


def _lru(rest, conv_w, conv_b, w_gate, b_gate, lam, sc_w):
    bsz, t_all, _ = rest.shape
    nt = t_all // TM
    hpt = TM // F32_SUBLANES
    nhb = t_all // F32_SUBLANES

    def rev(i):
        return jnp.where(i == 0, 0, nt - i)

    def tile(col, order):
        return pl.BlockSpec((1, TM, LRU_WIDTH), lambda b, i: (b, order(i), col))

    def prev_halo(col, order):
        return pl.BlockSpec((1, F32_SUBLANES, LRU_WIDTH),
                            lambda b, i: (b, jnp.maximum(order(i) * hpt - 1, 0), col))

    def next_halo(col, order):
        return pl.BlockSpec((1, F32_SUBLANES, LRU_WIDTH),
                            lambda b, i: (b, jnp.minimum((order(i) + 1) * hpt, nhb - 1), col))

    fwd = lambda i: i
    out_f32 = jax.ShapeDtypeStruct((bsz, t_all, LRU_WIDTH), F32)
    return pl.pallas_call(
        functools.partial(_lru_kernel, nt=nt),
        grid=(bsz, nt),
        in_specs=[tile(0, fwd), prev_halo(0, fwd), next_halo(0, fwd),
                  tile(0, rev), prev_halo(0, rev), next_halo(0, rev),
                  tile(2, fwd),
                  tile(3, fwd), prev_halo(3, fwd), next_halo(3, fwd),
                  _const_spec((LRU_CONV, LRU_WIDTH)),
                  _const_spec((1, LRU_WIDTH)),
                  _const_spec((LRU_WIDTH, 4 * LRU_WIDTH)),
                  _const_spec((1, 4 * LRU_WIDTH)),
                  _const_spec((2, LRU_WIDTH)),
                  _const_spec((3, SC_WIDTH))],
        out_specs=[pl.BlockSpec((1, TM, LRU_WIDTH), lambda b, i: (b, i, 0)),
                   pl.BlockSpec((1, TM, LRU_WIDTH), lambda b, i: (b, rev(i), 0)),
                   pl.BlockSpec((1, TM, SC_WIDTH), lambda b, i: (b, i, 0))],
        out_shape=[out_f32, out_f32, jax.ShapeDtypeStruct((bsz, t_all, SC_WIDTH), BF16)],
        scratch_shapes=([pltpu.VMEM((TM // F32_SUBLANES, F32_SUBLANES, LRU_WIDTH), F32)] * 4
                        + [pltpu.VMEM((1, LRU_WIDTH), F32)] * 2),
        compiler_params=_params(32),
        name="lru_scan",
    )(rest, rest, rest, rest, rest, rest, rest, rest, rest, rest,
      conv_w, conv_b, w_gate, b_gate, lam, sc_w)


def _mixer_kernel(qT_ref, k_ref, vT_ref, ctx_ref, x_ref, hf_ref, hb_ref, gg_ref, sc_ref, mod_ref, wo_ref,
                  gffn_ref, *out_and_scratch, tile_offset, n_key_tiles):
    i = pl.program_id(1) + tile_offset
    ones_rows = jnp.ones((BF16_SUBLANES, TM), BF16)

    *out_refs, oT_ref, s_ref = out_and_scratch
    latent_out = tuple(out_refs[-2:])
    context_out = tuple(out_refs[:2]) if tile_offset == 0 else None

    def project_out(x1_ref, h2_ref):
        x = x_ref[0]
        if tile_offset == 0:
            x = jnp.where(i == 0, ctx_ref[0], x)
        mod = mod_ref[0, 0]
        gate_mix, shift, scale = mod[2:3], mod[3:4], mod[4:5]
        att = oT_ref[...].T.astype(BF16)
        lru = (gg_ref[0] * (hf_ref[0] + hb_ref[0])).astype(BF16)
        y = (jnp.dot(att, wo_ref[0:ATTN_WIDTH], preferred_element_type=F32)
             + jnp.dot(lru, wo_ref[ATTN_WIDTH:ATTN_WIDTH + LRU_WIDTH], preferred_element_type=F32)
             + jnp.dot(sc_ref[0], wo_ref[ATTN_WIDTH + LRU_WIDTH:], preferred_element_type=F32))
        x1 = x + gate_mix * y
        x1_ref[0] = x1
        h2 = x1 * _rms_scale(x1) * gffn_ref[...]
        h2_ref[0] = (h2 * (1.0 + scale) + shift).astype(BF16)

    def run(n_tiles, out_pair):
        items = [(pr, c) for pr in range(N_HEADS // 2) for c in range(n_tiles)]
        qpads = {}

        def qpad_of(pr):
            if pr not in qpads:
                row0 = pr * 2 * HEAD_DIM
                q2 = jnp.concatenate([qT_ref[0, row0:row0 + HEAD_DIM, :],
                                      qT_ref[0, row0 + HEAD_DIM:row0 + 2 * HEAD_DIM, :]], axis=1)
                zero = jnp.zeros_like(q2)
                first_kv = (2 * pr) // HEADS_PER_KV == 0
                qpads[pr] = jnp.concatenate([q2, zero] if first_kv else [zero, q2], axis=0)
            return qpads[pr]

        def scores(t):
            pr, c = items[t]
            s = jnp.dot(k_ref[0, c * TM:(c + 1) * TM, :], qpad_of(pr), preferred_element_type=F32)
            s_ref[t % SCORE_SLOTS] = s
            return jnp.max(s, axis=0, keepdims=True)

        tile_max = [scores(t) for t in range(min(SCORE_LOOKAHEAD, len(items)))]
        m = acc = None
        for t, (pr, c) in enumerate(items):
            if t + SCORE_LOOKAHEAD < len(items):
                tile_max.append(scores(t + SCORE_LOOKAHEAD))
            if c == 0:
                m = jnp.full((1, 2 * TM), -jnp.inf, F32)
                acc = jnp.zeros((HEAD_DIM + BF16_SUBLANES, 2 * TM), F32)
            g = (2 * pr) // HEADS_PER_KV
            m_new = jnp.maximum(m, tile_max[t])
            alpha = jnp.exp2(m - m_new)
            p = jnp.exp2(s_ref[t % SCORE_SLOTS] - m_new).astype(BF16)
            vt = jnp.concatenate([vT_ref[0, g * HEAD_DIM:(g + 1) * HEAD_DIM, c * TM:(c + 1) * TM],
                                  ones_rows], axis=0)
            acc = alpha * acc + jnp.dot(vt, p, preferred_element_type=F32)
            m = m_new
            tile_max[t] = None
            if c == n_tiles - 1:
                o = acc[:HEAD_DIM] / acc[HEAD_DIM:HEAD_DIM + 1]
                row0 = pr * 2 * HEAD_DIM
                oT_ref[row0:row0 + HEAD_DIM, :] = o[:, :TM]
                oT_ref[row0 + HEAD_DIM:row0 + 2 * HEAD_DIM, :] = o[:, TM:]
        project_out(*out_pair)

    if tile_offset == 0:
        @pl.when(i == 0)
        def _():
            run(1, context_out)

    @pl.when(i != 0)
    def _():
        run(n_key_tiles, latent_out)


def _mixer(qT, k, vT, ctx_arr, x_arr, x_base, hf, hb, rest, sc, mods, w_out, g_ffn, tile_offset):
    bsz, t_all, _ = k.shape
    nt = t_all // TM
    nq = nt - tile_offset
    off = lambda b, i: (b, i + tile_offset, 0)
    latent_tile = lambda b, i: (b, jnp.maximum(i + tile_offset - 1, 0), 0)
    ctx_out_specs, ctx_out_shapes = [], []
    if tile_offset == 0:
        ctx_out_specs = [pl.BlockSpec((1, TM, D_MODEL), lambda b, i: (b, 0, 0))] * 2
        ctx_out_shapes = [jax.ShapeDtypeStruct((bsz, TM, D_MODEL), F32),
                          jax.ShapeDtypeStruct((bsz, TM, D_MODEL), BF16)]
    return pl.pallas_call(
        functools.partial(_mixer_kernel, tile_offset=tile_offset, n_key_tiles=nt),
        grid=(bsz, nq),
        in_specs=[pl.BlockSpec((1, ATTN_WIDTH, TM), lambda b, i: (b, 0, i + tile_offset)),
                  pl.BlockSpec((1, t_all, KV_WIDTH), lambda b, i: (b, 0, 0)),
                  pl.BlockSpec((1, KV_WIDTH, t_all), lambda b, i: (b, 0, 0))]
                 + _token_specs(x_base, tile_offset) + [
                  pl.BlockSpec((1, TM, LRU_WIDTH), off),
                  pl.BlockSpec((1, TM, LRU_WIDTH), off),
                  pl.BlockSpec((1, TM, LRU_WIDTH), lambda b, i: (b, i + tile_offset, 1)),
                  pl.BlockSpec((1, TM, SC_WIDTH), off),
                  pl.BlockSpec((1, 1, MOD_ROWS, D_MODEL),
                               lambda b, i: (b, jnp.minimum(i + tile_offset, 1), 0, 0)),
                  _const_spec((D_MODEL, D_MODEL)),
                  _const_spec((1, D_MODEL))],
        out_specs=ctx_out_specs + [pl.BlockSpec((1, TM, D_MODEL), latent_tile),
                                   pl.BlockSpec((1, TM, D_MODEL), latent_tile)],
        out_shape=ctx_out_shapes + [jax.ShapeDtypeStruct((bsz, (nt - 1) * TM, D_MODEL), F32),
                                    jax.ShapeDtypeStruct((bsz, (nt - 1) * TM, D_MODEL), BF16)],
        scratch_shapes=[pltpu.VMEM((ATTN_WIDTH, TM), F32), pltpu.VMEM((SCORE_SLOTS, TM, 2 * TM), F32)],
        compiler_params=_params(40),
        name="mixer",
    )(qT, k, vT, ctx_arr, x_arr, hf, hb, rest, sc, mods, w_out, g_ffn)


def _ffn_kernel(h_ref, hp_ref, hn_ref, x1_ref, mod_ref, wup_ref, cw_ref, wdn_ref, gfin_ref,
                o_ref, acc_ref, up_ref, *, nt, final_norm):
    i = pl.program_id(1)
    hp = hp_ref[0] * (i != 0).astype(BF16)
    hn = hn_ref[0] * (i != nt - 1).astype(BF16)
    hcat = jnp.concatenate([hp, h_ref[0], hn], axis=0)
    rows = hcat.shape[0]
    tile_rows = rows - 2 * BF16_SUBLANES

    def conv(t, w):
        lo, hi = BF16_SUBLANES, BF16_SUBLANES + tile_rows
        return (w[0:1] * pltpu.roll(t, 1, 0)[lo:hi] + w[1:2] * t[lo:hi]
                + w[2:3] * pltpu.roll(t, rows - 1, 0)[lo:hi])

    n_chunks = D_FF // FF_CHUNK

    def up_project(c):
        cu = slice(c * FF_CHUNK, (c + 1) * FF_CHUNK)
        cg = slice(D_FF + c * FF_CHUNK, D_FF + (c + 1) * FF_CHUNK)
        up_ref[c % UP_SLOTS, :, :FF_CHUNK] = jnp.dot(hcat, wup_ref[:, cu], preferred_element_type=F32)
        up_ref[c % UP_SLOTS, :, FF_CHUNK:] = jnp.dot(hcat, wup_ref[:, cg], preferred_element_type=F32)

    for c in range(min(UP_LOOKAHEAD, n_chunks)):
        up_project(c)
    for c in range(n_chunks):
        if c + UP_LOOKAHEAD < n_chunks:
            up_project(c + UP_LOOKAHEAD)
        cu = slice(c * FF_CHUNK, (c + 1) * FF_CHUNK)
        cg = slice(D_FF + c * FF_CHUNK, D_FF + (c + 1) * FF_CHUNK)
        u = conv(up_ref[c % UP_SLOTS, :, :FF_CHUNK], cw_ref[:, cu])
        g = conv(up_ref[c % UP_SLOTS, :, FF_CHUNK:], cw_ref[:, cg])
        act = ((g * jax.nn.sigmoid(g)) * u).astype(BF16)
        part = jnp.dot(act, wdn_ref[cu, :], preferred_element_type=F32)
        if c == 0:
            acc_ref[...] = part
        else:
            acc_ref[...] += part

    gate_ffn = mod_ref[0, 0][5:6]
    x2 = x1_ref[0] + gate_ffn * acc_ref[...]
    if final_norm:
        x2 = x2 * _rms_scale(x2) * gfin_ref[...]
    o_ref[0] = x2


def _ffn(h2, x1, mods, mod_slot, w_up, conv_w, w_down, g_final, final_norm):
    bsz, t_len, _ = h2.shape
    tile = min(FFN_TILE, t_len)
    nt = t_len // tile
    hpt = tile // BF16_SUBLANES
    nhb = t_len // BF16_SUBLANES
    return pl.pallas_call(
        functools.partial(_ffn_kernel, nt=nt, final_norm=final_norm),
        grid=(bsz, nt),
        in_specs=[pl.BlockSpec((1, tile, D_MODEL), lambda b, i: (b, i, 0)),
                  pl.BlockSpec((1, BF16_SUBLANES, D_MODEL),
                               lambda b, i: (b, jnp.maximum(i * hpt - 1, 0), 0)),
                  pl.BlockSpec((1, BF16_SUBLANES, D_MODEL),
                               lambda b, i: (b, jnp.minimum((i + 1) * hpt, nhb - 1), 0)),
                  pl.BlockSpec((1, tile, D_MODEL), lambda b, i: (b, i, 0)),
                  pl.BlockSpec((1, 1, MOD_ROWS, D_MODEL), lambda b, i: (b, mod_slot, 0, 0)),
                  _const_spec((D_MODEL, 2 * D_FF)),
                  _const_spec((3, 2 * D_FF)),
                  _const_spec((D_FF, D_MODEL)),
                  _const_spec((1, D_MODEL))],
        out_specs=pl.BlockSpec((1, tile, D_MODEL), lambda b, i: (b, i, 0)),
        out_shape=jax.ShapeDtypeStruct((bsz, t_len, D_MODEL), F32),
        scratch_shapes=[pltpu.VMEM((tile, D_MODEL), F32),
                        pltpu.VMEM((UP_SLOTS, tile + 2 * BF16_SUBLANES, 2 * FF_CHUNK), F32)],
        compiler_params=_params(52),
        name="conv_ffn",
    )(h2, h2, h2, x1, mods, w_up, conv_w, w_down, g_final)


def _rope_tables(seq, ctx_len):
    rows = seq // GRID_W
    pos_r = np.repeat(np.arange(rows, dtype=np.float32), GRID_W)
    pos_c = np.tile(np.arange(GRID_W, dtype=np.float32), rows)
    n_f = HEAD_DIM // 4
    inv = (np.float32(ROPE_THETA) ** (-np.arange(n_f, dtype=np.float32) / n_f)).astype(np.float32)
    ang = np.concatenate([pos_r[:, None] * inv, pos_c[:, None] * inv], axis=-1).astype(np.float32)
    cos = np.concatenate([np.ones((ctx_len, HEAD_DIM // 2)), np.cos(ang.astype(np.float64))], axis=0)
    sin = np.concatenate([np.zeros((ctx_len, HEAD_DIM // 2)), np.sin(ang.astype(np.float64))], axis=0)
    return jnp.asarray(cos.T, F32), jnp.asarray(sin.T, F32)


def _gate_weights(wa, ba, wi, bi):
    def dense(w):
        eye = jnp.eye(LRU_BLOCKS, dtype=w.dtype)
        return jnp.einsum("nde,nm->ndme", w, eye).reshape(LRU_WIDTH, LRU_WIDTH)
    w = jnp.concatenate([dense(wa[0]), dense(wi[0]), dense(wa[1]), dense(wi[1])], axis=1)
    b = jnp.concatenate([ba[0], bi[0], ba[1], bi[1]])[None, :]
    return w.astype(BF16), b


def kernel(x, c, ctx, c_ctx, w_mod, b_mod, g_mix, g_ffn, w_in, g_q, g_k, lru_conv_w, lru_conv_b,
           lru_wa, lru_ba, lru_wi, lru_bi, lru_lam, sc_conv_w, w_out, w_up, ffn_conv_w, w_down,
           g_final):
    bsz, seq, _ = x.shape
    ctx_len = ctx.shape[1]
    depth = w_mod.shape[0]
    assert ctx_len == TM and seq % TM == 0 and seq % GRID_W == 0

    cos_t, sin_t = _rope_tables(seq, ctx_len)
    ctx_arr, x_arr, x_base = ctx, x, 0
    cc_rows = -(-(bsz + 1) // F32_SUBLANES) * F32_SUBLANES
    cc = jnp.zeros((cc_rows, D_MODEL), F32).at[:bsz].set(c).at[bsz].set(c_ctx)

    mod_layers = _modulation(cc, w_mod, b_mod)

    out = None
    for l in range(depth):
        last = l == depth - 1
        mod_all = mod_layers[l].reshape(cc_rows, N_MOD, D_MODEL)
        mod_x = mod_all[:bsz]
        mod_c = jnp.broadcast_to(mod_all[bsz][None], (bsz, N_MOD, D_MODEL))
        mods = jnp.stack([mod_c, mod_x], axis=1)
        mods = jnp.pad(mods, ((0, 0), (0, 0), (0, MOD_ROWS - N_MOD), (0, 0)))

        gq = jnp.broadcast_to((g_q[l] * Q_SCALE)[:, None], (HEAD_DIM, TM))
        gk = jnp.broadcast_to(g_k[l][:, None], (HEAD_DIM, TM))
        qT, k, vT, rest = _inproj(ctx_arr, x_arr, x_base, mods, g_mix[l][None, :],
                                  w_in[l].astype(BF16), gq, gk, cos_t, sin_t)

        w_gate, b_gate = _gate_weights(lru_wa[l], lru_ba[l], lru_wi[l], lru_bi[l])
        hf, hb, sc = _lru(rest, lru_conv_w[l], lru_conv_b[l][None, :], w_gate, b_gate, lru_lam[l],
                          sc_conv_w[l])

        tile_offset = 1 if last else 0
        mixed = _mixer(qT, k, vT, ctx_arr, x_arr, x_base, hf, hb, rest, sc, mods,
                       w_out[l].astype(BF16), g_ffn[l][None, :], tile_offset)
        ffn_weights = (w_up[l].astype(BF16), ffn_conv_w[l], w_down[l].astype(BF16), g_final[None, :])
        x1, h2 = mixed[-2:]
        out = _ffn(h2, x1, mods, 1, *ffn_weights, final_norm=last)
        if not last:
            x1_ctx, h2_ctx = mixed[:2]
            ctx_arr = _ffn(h2_ctx, x1_ctx, mods, 0, *ffn_weights, final_norm=False)
            x_arr = out
    return out
```

```python
import functools

import numpy as np
import jax
import jax.numpy as jnp
from jax import lax
from jax.experimental import pallas as pl
from jax.experimental.pallas import tpu as pltpu

F32 = jnp.float32
BF16 = jnp.bfloat16

D_MODEL = 1024
HEAD_DIM = 64
N_HEADS = 8
N_KV_HEADS = 2
HEADS_PER_KV = N_HEADS // N_KV_HEADS
ATTN_WIDTH = N_HEADS * HEAD_DIM
KV_WIDTH = N_KV_HEADS * HEAD_DIM
LRU_WIDTH = 256
LRU_BLOCKS = 4
LRU_BLOCK = LRU_WIDTH // LRU_BLOCKS
LRU_CONV = 4
LRU_C = 8.0
SC_WIDTH = 256
IN_WIDTH = 2048
D_FF = 2816
FF_CHUNK = 512
GRID_W = 64
ROPE_THETA = 10000.0
EPS = 1e-6
Q_SCALE = HEAD_DIM ** -0.5 * 1.4426950408889634

TM = 256
SCORE_LOOKAHEAD = 3
SCORE_SLOTS = SCORE_LOOKAHEAD + 1
UP_LOOKAHEAD = 2
UP_SLOTS = UP_LOOKAHEAD + 1
LANES = 128
F32_SUBLANES = 8
BF16_SUBLANES = 16
N_MOD = 6
MOD_ROWS = 8

OFF_Q = 0
OFF_K = OFF_Q + ATTN_WIDTH
OFF_V = OFF_K + KV_WIDTH
OFF_R = OFF_V + KV_WIDTH
OFF_G = OFF_R + LRU_WIDTH
OFF_B = OFF_G + LRU_WIDTH
OFF_C = OFF_B + SC_WIDTH
OFF_U = OFF_C + SC_WIDTH


def _const_spec(shape):
    nd = len(shape)
    return pl.BlockSpec(shape, lambda *_: (0,) * nd, pipeline_mode=pl.Buffered(1))


def _params(vmem_mb, n_grid_dims=2):
    return pltpu.CompilerParams(dimension_semantics=("arbitrary",) * n_grid_dims,
                                vmem_limit_bytes=vmem_mb * 1024 * 1024)


def _rms_scale(t):
    return lax.rsqrt(jnp.mean(t * t, axis=-1, keepdims=True) + EPS)


def _mod_kernel(c_ref, w_ref, b_ref, o_ref):
    c = c_ref[...]
    h = (c * jax.nn.sigmoid(c)).astype(BF16)
    o_ref[...] = jnp.dot(h, w_ref[...].astype(BF16), preferred_element_type=F32) + b_ref[...]


def _modulation(cc, w_mod, b_mod):
    rows = cc.shape[0]
    depth = w_mod.shape[0]
    return pl.pallas_call(
        _mod_kernel,
        grid=(depth, N_MOD),
        in_specs=[pl.BlockSpec((rows, D_MODEL), lambda l, j: (0, 0)),
                  pl.BlockSpec((None, D_MODEL, D_MODEL), lambda l, j: (l, 0, j)),
                  pl.BlockSpec((None, 1, D_MODEL), lambda l, j: (l, 0, j))],
        out_specs=pl.BlockSpec((None, rows, D_MODEL), lambda l, j: (l, 0, j)),
        out_shape=jax.ShapeDtypeStruct((depth, rows, N_MOD * D_MODEL), F32),
        name="modulation",
    )(cc, w_mod, b_mod[:, None, :])


def _norm_rope_transposed(t, gain, cos, sin):
    half = HEAD_DIM // 2
    out = []
    for h in range(t.shape[0] // HEAD_DIM):
        r = t[h * HEAD_DIM:(h + 1) * HEAD_DIM]
        r = r * lax.rsqrt(jnp.mean(r * r, axis=0, keepdims=True) + EPS) * gain
        x1, x2 = r[:half], r[half:]
        out += [x1 * cos - x2 * sin, x1 * sin + x2 * cos]
    return jnp.concatenate(out, axis=0)


def _inproj_kernel(ctx_ref, x_ref, mod_ref, gmix_ref, w_ref, gq_ref, gk_ref, cos_ref, sin_ref,
                   qT_ref, k_ref, vT_ref, rest_ref, y_even_ref, y_odd_ref, *, nt, n_tiles):
    s = pl.program_id(0)
    tile = jnp.minimum(s, n_tiles - 1) % nt

    def project(y_ref):
        x = jnp.where(tile == 0, ctx_ref[0], x_ref[0])
        mod = mod_ref[0, 0]
        shift, scale = mod[0:1], mod[1:2]
        h = x * _rms_scale(x) * gmix_ref[...]
        h = h * (1.0 + scale) + shift
        y_ref[...] = jnp.dot(h.astype(BF16), w_ref[...], preferred_element_type=F32)

    def finish(y_ref):
        cos, sin = cos_ref[...], sin_ref[...]
        q = _norm_rope_transposed(y_ref[:, OFF_Q:OFF_Q + ATTN_WIDTH].T, gq_ref[...], cos, sin)
        qT_ref[0] = q.astype(BF16)
        k = _norm_rope_transposed(y_ref[:, OFF_K:OFF_K + KV_WIDTH].T, gk_ref[...], cos, sin)
        k_ref[0] = k.T.astype(BF16)
        vT_ref[0] = y_ref[:, OFF_V:OFF_V + KV_WIDTH].T.astype(BF16)
        rest_ref[0, :, 0:LRU_WIDTH] = y_ref[:, OFF_R:OFF_R + LRU_WIDTH]
        rest_ref[0, :, LRU_WIDTH:2 * LRU_WIDTH] = jax.nn.gelu(y_ref[:, OFF_G:OFF_G + LRU_WIDTH])
        rest_ref[0, :, 2 * LRU_WIDTH:2 * LRU_WIDTH + SC_WIDTH] = y_ref[:, OFF_B:OFF_B + SC_WIDTH]
        rest_ref[0, :, 2 * LRU_WIDTH + SC_WIDTH:] = (y_ref[:, OFF_C:OFF_C + SC_WIDTH]
                                                     * y_ref[:, OFF_U:OFF_U + SC_WIDTH])

    @pl.when(s == 0)
    def _():
        y_odd_ref[...] = jnp.zeros_like(y_odd_ref)

    @pl.when(s % 2 == 0)
    def _():
        finish(y_odd_ref)
        project(y_even_ref)

    @pl.when(s % 2 == 1)
    def _():
        finish(y_even_ref)
        project(y_odd_ref)


def _token_specs(x_base, tile_offset=0):
    return [pl.BlockSpec((1, TM, D_MODEL), lambda b, i: (b, 0, 0)),
            pl.BlockSpec((1, TM, D_MODEL),
                         lambda b, i: (b, x_base + jnp.maximum(i + tile_offset - 1, 0), 0))]


def _inproj(ctx_arr, x_arr, x_base, mods, g_mix, w_in, gq, gk, cos_t, sin_t):
    bsz = x_arr.shape[0]
    nt = x_arr.shape[1] // TM - x_base + 1
    t_all = nt * TM
    n_tiles = bsz * nt
    rest_w = 2 * LRU_WIDTH + 2 * SC_WIDTH

    def cur(s):
        sc = jnp.minimum(s, n_tiles - 1)
        return sc // nt, sc % nt

    def prev(s):
        sp = jnp.maximum(s - 1, 0)
        return sp // nt, sp % nt

    rope_spec = pl.BlockSpec((HEAD_DIM // 2, TM), lambda s: (0, prev(s)[1]))
    return pl.pallas_call(
        functools.partial(_inproj_kernel, nt=nt, n_tiles=n_tiles),
        grid=(n_tiles + 1,),
        in_specs=[pl.BlockSpec((1, TM, D_MODEL), lambda s: (cur(s)[0], 0, 0)),
                  pl.BlockSpec((1, TM, D_MODEL),
                               lambda s: (cur(s)[0], x_base + jnp.maximum(cur(s)[1] - 1, 0), 0)),
                  pl.BlockSpec((1, 1, MOD_ROWS, D_MODEL),
                               lambda s: (cur(s)[0], jnp.minimum(cur(s)[1], 1), 0, 0)),
                  _const_spec((1, D_MODEL)),
                  _const_spec((D_MODEL, IN_WIDTH)),
                  _const_spec((HEAD_DIM, TM)),
                  _const_spec((HEAD_DIM, TM)),
                  rope_spec,
                  rope_spec],
        out_specs=[pl.BlockSpec((1, ATTN_WIDTH, TM), lambda s: (prev(s)[0], 0, prev(s)[1])),
                   pl.BlockSpec((1, TM, KV_WIDTH), lambda s: (prev(s)[0], prev(s)[1], 0)),
                   pl.BlockSpec((1, KV_WIDTH, TM), lambda s: (prev(s)[0], 0, prev(s)[1])),
                   pl.BlockSpec((1, TM, rest_w), lambda s: (prev(s)[0], prev(s)[1], 0))],
        out_shape=[jax.ShapeDtypeStruct((bsz, ATTN_WIDTH, t_all), BF16),
                   jax.ShapeDtypeStruct((bsz, t_all, KV_WIDTH), BF16),
                   jax.ShapeDtypeStruct((bsz, KV_WIDTH, t_all), BF16),
                   jax.ShapeDtypeStruct((bsz, t_all, rest_w), F32)],
        scratch_shapes=[pltpu.VMEM((TM, IN_WIDTH), F32)] * 2,
        compiler_params=_params(40, n_grid_dims=1),
        name="inproj",
    )(ctx_arr, x_arr, mods, g_mix, w_in, gq, gk, cos_t, sin_t)


def _with_halo(prev_ref, cur_ref, next_ref, valid_prev, valid_next):
    prev = prev_ref[0] * valid_prev.astype(F32)
    nxt = next_ref[0] * valid_next.astype(F32)
    return jnp.concatenate([prev, cur_ref[0], nxt], axis=0)


def _shifted(tcat, offset):
    rows = tcat.shape[0]
    if offset == 0:
        return tcat[F32_SUBLANES:F32_SUBLANES + TM]
    return pltpu.roll(tcat, (-offset) % rows, 0)[F32_SUBLANES:F32_SUBLANES + TM]


def _lru_coeffs(rcat, conv_w, conv_b, w_gate, b_gate, sp_lam):
    xc = conv_b
    for j in range(LRU_CONV):
        xc = xc + conv_w[j:j + 1] * _shifted(rcat, j - 2)
    z = jnp.dot(xc.astype(BF16), w_gate, preferred_element_type=F32) + b_gate
    r_gate = jax.nn.sigmoid(z[:, :LRU_WIDTH])
    i_gate = jax.nn.sigmoid(z[:, LRU_WIDTH:])
    log_a = (-LRU_C) * r_gate * sp_lam
    a = jnp.exp(log_a)
    one_minus_a2 = -jnp.tanh(log_a) * (1.0 + a * a)
    return a, jnp.sqrt(one_minus_a2) * (i_gate * xc)


def _lru_kernel(rf_ref, rfp_ref, rfn_ref, rb_ref, rbp_ref, rbn_ref, bg_ref, cu_ref, cup_ref, cun_ref,
                cw_ref, cb_ref, wg_ref, bgate_ref, lam_ref, scw_ref,
                hf_ref, hb_ref, sc_ref,
                af_s, bf_s, ab_s, bb_s, hf_state, hb_state, *, nt):
    i = pl.program_id(1)
    j = jnp.where(i == 0, 0, nt - i)

    @pl.when(i == 0)
    def _():
        hf_state[...] = jnp.zeros_like(hf_state)
        hb_state[...] = jnp.zeros_like(hb_state)

    def halo_valid(t):
        return jnp.logical_and(t != 0, t != 1), jnp.logical_and(t != 0, t != nt - 1)

    lam = lam_ref[...]
    sp_lam = jnp.maximum(-lam, 0.0) + jnp.log1p(jnp.exp(-jnp.abs(lam)))
    cw = cw_ref[...]
    cb = cb_ref[...]

    vp, vn = halo_valid(i)
    a, b = _lru_coeffs(_with_halo(rfp_ref, rf_ref, rfn_ref, vp, vn), cw, cb,
                       wg_ref[:, :2 * LRU_WIDTH], bgate_ref[:, :2 * LRU_WIDTH], sp_lam[0:1])
    _block_prefix(a, b, af_s, bf_s, reverse=False)
    cucat = _with_halo(cup_ref, cu_ref, cun_ref, vp, vn)
    scw = scw_ref[...]
    conv = scw[0:1] * _shifted(cucat, -1) + scw[1:2] * _shifted(cucat, 0) + scw[2:3] * _shifted(cucat, 1)
    sc_ref[0] = (bg_ref[0] * conv).astype(BF16)

    vp, vn = halo_valid(j)
    a, b = _lru_coeffs(_with_halo(rbp_ref, rb_ref, rbn_ref, vp, vn), cw, cb,
                       wg_ref[:, 2 * LRU_WIDTH:], bgate_ref[:, 2 * LRU_WIDTH:], sp_lam[1:2])
    _block_prefix(a, b, ab_s, bb_s, reverse=True)

    _carry_blocks(af_s, bf_s, hf_ref, hf_state, reverse=False)
    _carry_blocks(ab_s, bb_s, hb_ref, hb_state, reverse=True)


def _block_prefix(a, b, a_s, b_s, reverse):
    n = F32_SUBLANES
    a = a.reshape(TM // n, n, LRU_WIDTH)
    b = b.reshape(TM // n, n, LRU_WIDTH)
    row = lax.broadcasted_iota(jnp.int32, a.shape, 1)
    d = 1
    while d < n:
        shift, keep = (n - d, row < n - d) if reverse else (d, row >= d)
        a_prev = jnp.where(keep, pltpu.roll(a, shift, 1), 1.0)
        b_prev = jnp.where(keep, pltpu.roll(b, shift, 1), 0.0)
        b = a * b_prev + b
        a = a * a_prev
        d *= 2
    a_s[...] = a
    b_s[...] = b


def _carry_blocks(a_s, b_s, out_ref, state_ref, reverse):
    n = F32_SUBLANES
    nb = TM // n
    last = 0 if reverse else n - 1
    h_in = state_ref[...]
    for v in (reversed(range(nb)) if reverse else range(nb)):
        h = a_s[v] * h_in + b_s[v]
        out_ref[0, v * n:(v + 1) * n, :] = h
        h_in = h[last:last + 1]
    state_ref[...] = h_in


def _lru(rest, conv_w, conv_b, w_gate, b_gate, lam, sc_w):
    bsz, t_all, _ = rest.shape
    nt = t_all // TM
    hpt = TM // F32_SUBLANES
    nhb = t_all // F32_SUBLANES

    def rev(i):
        return jnp.where(i == 0, 0, nt - i)

    def tile(col, order):
        return pl.BlockSpec((1, TM, LRU_WIDTH), lambda b, i: (b, order(i), col))

    def prev_halo(col, order):
        return pl.BlockSpec((1, F32_SUBLANES, LRU_WIDTH),
                            lambda b, i: (b, jnp.maximum(order(i) * hpt - 1, 0), col))

    def next_halo(col, order):
        return pl.BlockSpec((1, F32_SUBLANES, LRU_WIDTH),
                            lambda b, i: (b, jnp.minimum((order(i) + 1) * hpt, nhb - 1), col))

    fwd = lambda i: i
    out_f32 = jax.ShapeDtypeStruct((bsz, t_all, LRU_WIDTH), F32)
    return pl.pallas_call(
        functools.partial(_lru_kernel, nt=nt),
        grid=(bsz, nt),
        in_specs=[tile(0, fwd), prev_halo(0, fwd), next_halo(0, fwd),
                  tile(0, rev), prev_halo(0, rev), next_halo(0, rev),
                  tile(2, fwd),
                  tile(3, fwd), prev_halo(3, fwd), next_halo(3, fwd),
                  _const_spec((LRU_CONV, LRU_WIDTH)),
                  _const_spec((1, LRU_WIDTH)),
                  _const_spec((LRU_WIDTH, 4 * LRU_WIDTH)),
                  _const_spec((1, 4 * LRU_WIDTH)),
                  _const_spec((2, LRU_WIDTH)),
                  _const_spec((3, SC_WIDTH))],
        out_specs=[pl.BlockSpec((1, TM, LRU_WIDTH), lambda b, i: (b, i, 0)),
                   pl.BlockSpec((1, TM, LRU_WIDTH), lambda b, i: (b, rev(i), 0)),
                   pl.BlockSpec((1, TM, SC_WIDTH), lambda b, i: (b, i, 0))],
        out_shape=[out_f32, out_f32, jax.ShapeDtypeStruct((bsz, t_all, SC_WIDTH), BF16)],
        scratch_shapes=([pltpu.VMEM((TM // F32_SUBLANES, F32_SUBLANES, LRU_WIDTH), F32)] * 4
                        + [pltpu.VMEM((1, LRU_WIDTH), F32)] * 2),
        compiler_params=_params(32),
        name="lru_scan",
    )(rest, rest, rest, rest, rest, rest, rest, rest, rest, rest,
      conv_w, conv_b, w_gate, b_gate, lam, sc_w)


def _mixer_kernel(qT_ref, k_ref, vT_ref, ctx_ref, x_ref, hf_ref, hb_ref, gg_ref, sc_ref, mod_ref, wo_ref,
                  gffn_ref, x1_ref, h2_ref, oT_ref, s_ref, *, tile_offset, n_key_tiles):
    i = pl.program_id(1) + tile_offset
    ones_rows = jnp.ones((BF16_SUBLANES, TM), BF16)

    def project_out():
        x = x_ref[0]
        if tile_offset == 0:
            x = jnp.where(i == 0, ctx_ref[0], x)
        mod = mod_ref[0, 0]
        gate_mix, shift, scale = mod[2:3], mod[3:4], mod[4:5]
        att = oT_ref[...].T.astype(BF16)
        lru = (gg_ref[0] * (hf_ref[0] + hb_ref[0])).astype(BF16)
        y = (jnp.dot(att, wo_ref[0:ATTN_WIDTH], preferred_element_type=F32)
             + jnp.dot(lru, wo_ref[ATTN_WIDTH:ATTN_WIDTH + LRU_WIDTH], preferred_element_type=F32)
             + jnp.dot(sc_ref[0], wo_ref[ATTN_WIDTH + LRU_WIDTH:], preferred_element_type=F32))
        x1 = x + gate_mix * y
        x1_ref[0] = x1
        h2 = x1 * _rms_scale(x1) * gffn_ref[...]
        h2_ref[0] = (h2 * (1.0 + scale) + shift).astype(BF16)

    def run(n_tiles):
        items = [(pr, c) for pr in range(N_HEADS // 2) for c in range(n_tiles)]
        qpads = {}

        def qpad_of(pr):
            if pr not in qpads:
                row0 = pr * 2 * HEAD_DIM
                q2 = jnp.concatenate([qT_ref[0, row0:row0 + HEAD_DIM, :],
                                      qT_ref[0, row0 + HEAD_DIM:row0 + 2 * HEAD_DIM, :]], axis=1)
                zero = jnp.zeros_like(q2)
                first_kv = (2 * pr) // HEADS_PER_KV == 0
                qpads[pr] = jnp.concatenate([q2, zero] if first_kv else [zero, q2], axis=0)
            return qpads[pr]

        def scores(t):
            pr, c = items[t]
            s = jnp.dot(k_ref[0, c * TM:(c + 1) * TM, :], qpad_of(pr), preferred_element_type=F32)
            s_ref[t % SCORE_SLOTS] = s
            return jnp.max(s, axis=0, keepdims=True)

        tile_max = [scores(t) for t in range(min(SCORE_LOOKAHEAD, len(items)))]
        m = acc = None
        for t, (pr, c) in enumerate(items):
            if t + SCORE_LOOKAHEAD < len(items):
                tile_max.append(scores(t + SCORE_LOOKAHEAD))
            if c == 0:
                m = jnp.full((1, 2 * TM), -jnp.inf, F32)
                acc = jnp.zeros((HEAD_DIM + BF16_SUBLANES, 2 * TM), F32)
            g = (2 * pr) // HEADS_PER_KV
            m_new = jnp.maximum(m, tile_max[t])
            alpha = jnp.exp2(m - m_new)
            p = jnp.exp2(s_ref[t % SCORE_SLOTS] - m_new).astype(BF16)
            vt = jnp.concatenate([vT_ref[0, g * HEAD_DIM:(g + 1) * HEAD_DIM, c * TM:(c + 1) * TM],
                                  ones_rows], axis=0)
            acc = alpha * acc + jnp.dot(vt, p, preferred_element_type=F32)
            m = m_new
            tile_max[t] = None
            if c == n_tiles - 1:
                o = acc[:HEAD_DIM] / acc[HEAD_DIM:HEAD_DIM + 1]
                row0 = pr * 2 * HEAD_DIM
                oT_ref[row0:row0 + HEAD_DIM, :] = o[:, :TM]
                oT_ref[row0 + HEAD_DIM:row0 + 2 * HEAD_DIM, :] = o[:, TM:]
        project_out()

    @pl.when(i == 0)
    def _():
        run(1)

    @pl.when(i != 0)
    def _():
        run(n_key_tiles)


def _mixer(qT, k, vT, ctx_arr, x_arr, x_base, hf, hb, rest, sc, mods, w_out, g_ffn, tile_offset):
    bsz, t_all, _ = k.shape
    nt = t_all // TM
    nq = nt - tile_offset
    off = lambda b, i: (b, i + tile_offset, 0)
    return pl.pallas_call(
        functools.partial(_mixer_kernel, tile_offset=tile_offset, n_key_tiles=nt),
        grid=(bsz, nq),
        in_specs=[pl.BlockSpec((1, ATTN_WIDTH, TM), lambda b, i: (b, 0, i + tile_offset)),
                  pl.BlockSpec((1, t_all, KV_WIDTH), lambda b, i: (b, 0, 0)),
                  pl.BlockSpec((1, KV_WIDTH, t_all), lambda b, i: (b, 0, 0))]
                 + _token_specs(x_base, tile_offset) + [
                  pl.BlockSpec((1, TM, LRU_WIDTH), off),
                  pl.BlockSpec((1, TM, LRU_WIDTH), off),
                  pl.BlockSpec((1, TM, LRU_WIDTH), lambda b, i: (b, i + tile_offset, 1)),
                  pl.BlockSpec((1, TM, SC_WIDTH), off),
                  pl.BlockSpec((1, 1, MOD_ROWS, D_MODEL),
                               lambda b, i: (b, jnp.minimum(i + tile_offset, 1), 0, 0)),
                  _const_spec((D_MODEL, D_MODEL)),
                  _const_spec((1, D_MODEL))],
        out_specs=[pl.BlockSpec((1, TM, D_MODEL), lambda b, i: (b, i, 0)),
                   pl.BlockSpec((1, TM, D_MODEL), lambda b, i: (b, i, 0))],
        out_shape=[jax.ShapeDtypeStruct((bsz, nq * TM, D_MODEL), F32),
                   jax.ShapeDtypeStruct((bsz, nq * TM, D_MODEL), BF16)],
        scratch_shapes=[pltpu.VMEM((ATTN_WIDTH, TM), F32), pltpu.VMEM((SCORE_SLOTS, TM, 2 * TM), F32)],
        compiler_params=_params(40),
        name="mixer",
    )(qT, k, vT, ctx_arr, x_arr, hf, hb, rest, sc, mods, w_out, g_ffn)


def _ffn_kernel(h_ref, hp_ref, hn_ref, x1_ref, mod_ref, wup_ref, cw_ref, wdn_ref, gfin_ref,
                o_ref, acc_ref, up_ref, *, ctx_tiles, nt, final_norm):
    i = pl.program_id(1)
    valid_prev = jnp.logical_and(i != 0, i != ctx_tiles)
    valid_next = jnp.logical_and(i != nt - 1, i != ctx_tiles - 1)
    hp = hp_ref[0] * valid_prev.astype(BF16)
    hn = hn_ref[0] * valid_next.astype(BF16)
    hcat = jnp.concatenate([hp, h_ref[0], hn], axis=0)
    rows = hcat.shape[0]

    def conv(t, w):
        lo, hi = BF16_SUBLANES, BF16_SUBLANES + TM
        return (w[0:1] * pltpu.roll(t, 1, 0)[lo:hi] + w[1:2] * t[lo:hi]
                + w[2:3] * pltpu.roll(t, rows - 1, 0)[lo:hi])

    starts = list(range(0, D_FF, FF_CHUNK))
    chunks = [(c0, min(FF_CHUNK, D_FF - c0)) for c0 in starts]
    n_chunks = len(chunks)

    def up_project(c):
        c0, w = chunks[c]
        cu, cg = slice(c0, c0 + w), slice(D_FF + c0, D_FF + c0 + w)
        up_ref[c % UP_SLOTS, :, :w] = jnp.dot(hcat, wup_ref[:, cu], preferred_element_type=F32)
        up_ref[c % UP_SLOTS, :, FF_CHUNK:FF_CHUNK + w] = jnp.dot(hcat, wup_ref[:, cg],
                                                                 preferred_element_type=F32)

    for c in range(min(UP_LOOKAHEAD, n_chunks)):
        up_project(c)
    for c in range(n_chunks):
        if c + UP_LOOKAHEAD < n_chunks:
            up_project(c + UP_LOOKAHEAD)
        c0, w = chunks[c]
        cu, cg = slice(c0, c0 + w), slice(D_FF + c0, D_FF + c0 + w)
        u = conv(up_ref[c % UP_SLOTS, :, :w], cw_ref[:, cu])
        g = conv(up_ref[c % UP_SLOTS, :, FF_CHUNK:FF_CHUNK + w], cw_ref[:, cg])
        act = ((g * jax.nn.sigmoid(g)) * u).astype(BF16)
        part = jnp.dot(act, wdn_ref[cu, :], preferred_element_type=F32)
        if c == 0:
            acc_ref[...] = part
        else:
            acc_ref[...] += part

    gate_ffn = mod_ref[0, 0][5:6]
    x2 = x1_ref[0] + gate_ffn * acc_ref[...]
    if final_norm:
        x2 = x2 * _rms_scale(x2) * gfin_ref[...]
    o_ref[0] = x2


def _ffn(h2, x1, mods, w_up, conv_w, w_down, g_final, ctx_tiles, final_norm):
    bsz, t_len, _ = h2.shape
    nt = t_len // TM
    hpt = TM // BF16_SUBLANES
    nhb = t_len // BF16_SUBLANES
    return pl.pallas_call(
        functools.partial(_ffn_kernel, ctx_tiles=ctx_tiles, nt=nt, final_norm=final_norm),
        grid=(bsz, nt),
        in_specs=[pl.BlockSpec((1, TM, D_MODEL), lambda b, i: (b, i, 0)),
                  pl.BlockSpec((1, BF16_SUBLANES, D_MODEL),
                               lambda b, i: (b, jnp.maximum(i * hpt - 1, 0), 0)),
                  pl.BlockSpec((1, BF16_SUBLANES, D_MODEL),
                               lambda b, i: (b, jnp.minimum((i + 1) * hpt, nhb - 1), 0)),
                  pl.BlockSpec((1, TM, D_MODEL), lambda b, i: (b, i, 0)),
                  pl.BlockSpec((1, 1, MOD_ROWS, D_MODEL),
                               lambda b, i: (b, jnp.minimum(i + 1 - ctx_tiles, 1), 0, 0)),
                  _const_spec((D_MODEL, 2 * D_FF)),
                  _const_spec((3, 2 * D_FF)),
                  _const_spec((D_FF, D_MODEL)),
                  _const_spec((1, D_MODEL))],
        out_specs=pl.BlockSpec((1, TM, D_MODEL), lambda b, i: (b, i, 0)),
        out_shape=jax.ShapeDtypeStruct((bsz, t_len, D_MODEL), F32),
        scratch_shapes=[pltpu.VMEM((TM, D_MODEL), F32),
                        pltpu.VMEM((UP_SLOTS, TM + 2 * BF16_SUBLANES, 2 * FF_CHUNK), F32)],
        compiler_params=_params(48),
        name="conv_ffn",
    )(h2, h2, h2, x1, mods, w_up, conv_w, w_down, g_final)


def _rope_tables(seq, ctx_len):
    rows = seq // GRID_W
    pos_r = np.repeat(np.arange(rows, dtype=np.float32), GRID_W)
    pos_c = np.tile(np.arange(GRID_W, dtype=np.float32), rows)
    n_f = HEAD_DIM // 4
    inv = (np.float32(ROPE_THETA) ** (-np.arange(n_f, dtype=np.float32) / n_f)).astype(np.float32)
    ang = np.concatenate([pos_r[:, None] * inv, pos_c[:, None] * inv], axis=-1).astype(np.float32)
    cos = np.concatenate([np.ones((ctx_len, HEAD_DIM // 2)), np.cos(ang.astype(np.float64))], axis=0)
    sin = np.concatenate([np.zeros((ctx_len, HEAD_DIM // 2)), np.sin(ang.astype(np.float64))], axis=0)
    return jnp.asarray(cos.T, F32), jnp.asarray(sin.T, F32)


def _gate_weights(wa, ba, wi, bi):
    def dense(w):
        eye = jnp.eye(LRU_BLOCKS, dtype=w.dtype)
        return jnp.einsum("nde,nm->ndme", w, eye).reshape(LRU_WIDTH, LRU_WIDTH)
    w = jnp.concatenate([dense(wa[0]), dense(wi[0]), dense(wa[1]), dense(wi[1])], axis=1)
    b = jnp.concatenate([ba[0], bi[0], ba[1], bi[1]])[None, :]
    return w.astype(BF16), b


def kernel(x, c, ctx, c_ctx, w_mod, b_mod, g_mix, g_ffn, w_in, g_q, g_k, lru_conv_w, lru_conv_b,
           lru_wa, lru_ba, lru_wi, lru_bi, lru_lam, sc_conv_w, w_out, w_up, ffn_conv_w, w_down,
           g_final):
    bsz, seq, _ = x.shape
    ctx_len = ctx.shape[1]
    depth = w_mod.shape[0]
    assert ctx_len == TM and seq % TM == 0 and seq % GRID_W == 0

    cos_t, sin_t = _rope_tables(seq, ctx_len)
    ctx_arr, x_arr, x_base = ctx, x, 0
    cc_rows = -(-(bsz + 1) // F32_SUBLANES) * F32_SUBLANES
    cc = jnp.zeros((cc_rows, D_MODEL), F32).at[:bsz].set(c).at[bsz].set(c_ctx)

    mod_layers = _modulation(cc, w_mod, b_mod)

    out = None
    for l in range(depth):
        last = l == depth - 1
        mod_all = mod_layers[l].reshape(cc_rows, N_MOD, D_MODEL)
        mod_x = mod_all[:bsz]
        mod_c = jnp.broadcast_to(mod_all[bsz][None], (bsz, N_MOD, D_MODEL))
        mods = jnp.stack([mod_c, mod_x], axis=1)
        mods = jnp.pad(mods, ((0, 0), (0, 0), (0, MOD_ROWS - N_MOD), (0, 0)))

        gq = jnp.broadcast_to((g_q[l] * Q_SCALE)[:, None], (HEAD_DIM, TM))
        gk = jnp.broadcast_to(g_k[l][:, None], (HEAD_DIM, TM))
        qT, k, vT, rest = _inproj(ctx_arr, x_arr, x_base, mods, g_mix[l][None, :],
                                  w_in[l].astype(BF16), gq, gk, cos_t, sin_t)

        w_gate, b_gate = _gate_weights(lru_wa[l], lru_ba[l], lru_wi[l], lru_bi[l])
        hf, hb, sc = _lru(rest, lru_conv_w[l], lru_conv_b[l][None, :], w_gate, b_gate, lru_lam[l],
                          sc_conv_w[l])

        tile_offset = 1 if last else 0
        x1, h2 = _mixer(qT, k, vT, ctx_arr, x_arr, x_base, hf, hb, rest, sc, mods,
                        w_out[l].astype(BF16), g_ffn[l][None, :], tile_offset)
        out = _ffn(h2, x1, mods, w_up[l].astype(BF16), ffn_conv_w[l], w_down[l].astype(BF16),
                   g_final[None, :], ctx_tiles=1 - tile_offset, final_norm=last)
        ctx_arr, x_arr, x_base = out, out, 1
    return out
```

```python
import functools

import numpy as np
import jax
import jax.numpy as jnp
from jax import lax
from jax.experimental import pallas as pl
from jax.experimental.pallas import tpu as pltpu

F32 = jnp.float32
BF16 = jnp.bfloat16

D_MODEL = 1024
HEAD_DIM = 64
N_HEADS = 8
N_KV_HEADS = 2
HEADS_PER_KV = N_HEADS // N_KV_HEADS
ATTN_WIDTH = N_HEADS * HEAD_DIM
KV_WIDTH = N_KV_HEADS * HEAD_DIM
LRU_WIDTH = 256
LRU_BLOCKS = 4
LRU_BLOCK = LRU_WIDTH // LRU_BLOCKS
LRU_CONV = 4
LRU_C = 8.0
SC_WIDTH = 256
IN_WIDTH = 2048
D_FF = 2816
FF_CHUNK = 256
GRID_W = 64
ROPE_THETA = 10000.0
EPS = 1e-6
Q_SCALE = HEAD_DIM ** -0.5 * 1.4426950408889634

TM = 256
SCORE_LOOKAHEAD = 3
SCORE_SLOTS = SCORE_LOOKAHEAD + 1
UP_LOOKAHEAD = 3
UP_SLOTS = UP_LOOKAHEAD + 1
LANES = 128
F32_SUBLANES = 8
BF16_SUBLANES = 16
N_MOD = 6
MOD_ROWS = 8

OFF_Q = 0
OFF_K = OFF_Q + ATTN_WIDTH
OFF_V = OFF_K + KV_WIDTH
OFF_R = OFF_V + KV_WIDTH
OFF_G = OFF_R + LRU_WIDTH
OFF_B = OFF_G + LRU_WIDTH
OFF_C = OFF_B + SC_WIDTH
OFF_U = OFF_C + SC_WIDTH


def _const_spec(shape):
    nd = len(shape)
    return pl.BlockSpec(shape, lambda *_: (0,) * nd, pipeline_mode=pl.Buffered(1))


def _params(vmem_mb, n_grid_dims=2):
    return pltpu.CompilerParams(dimension_semantics=("arbitrary",) * n_grid_dims,
                                vmem_limit_bytes=vmem_mb * 1024 * 1024)


def _rms_scale(t):
    return lax.rsqrt(jnp.mean(t * t, axis=-1, keepdims=True) + EPS)


def _mod_kernel(c_ref, w_ref, b_ref, o_ref):
    c = c_ref[...]
    h = (c * jax.nn.sigmoid(c)).astype(BF16)
    o_ref[...] = jnp.dot(h, w_ref[...].astype(BF16), preferred_element_type=F32) + b_ref[...]


def _modulation(cc, w_mod, b_mod):
    rows = cc.shape[0]
    depth = w_mod.shape[0]
    return pl.pallas_call(
        _mod_kernel,
        grid=(depth, N_MOD),
        in_specs=[pl.BlockSpec((rows, D_MODEL), lambda l, j: (0, 0)),
                  pl.BlockSpec((None, D_MODEL, D_MODEL), lambda l, j: (l, 0, j)),
                  pl.BlockSpec((None, 1, D_MODEL), lambda l, j: (l, 0, j))],
        out_specs=pl.BlockSpec((None, rows, D_MODEL), lambda l, j: (l, 0, j)),
        out_shape=jax.ShapeDtypeStruct((depth, rows, N_MOD * D_MODEL), F32),
        name="modulation",
    )(cc, w_mod, b_mod[:, None, :])


def _norm_rope_transposed(t, gain, cos, sin):
    half = HEAD_DIM // 2
    out = []
    for h in range(t.shape[0] // HEAD_DIM):
        r = t[h * HEAD_DIM:(h + 1) * HEAD_DIM]
        r = r * lax.rsqrt(jnp.mean(r * r, axis=0, keepdims=True) + EPS) * gain
        x1, x2 = r[:half], r[half:]
        out += [x1 * cos - x2 * sin, x1 * sin + x2 * cos]
    return jnp.concatenate(out, axis=0)


def _inproj_kernel(ctx_ref, x_ref, mod_ref, gmix_ref, w_ref, gq_ref, gk_ref, cos_ref, sin_ref,
                   qT_ref, k_ref, vT_ref, rest_ref, h_even_ref, h_odd_ref, y_even_ref, y_odd_ref,
                   *, nt, n_tiles):
    s = pl.program_id(0)
    tile = jnp.minimum(s, n_tiles - 1) % nt

    def normalise(h_ref, rows):
        x = jnp.where(tile == 0, ctx_ref[0, rows, :], x_ref[0, rows, :])
        mod = mod_ref[0, 0]
        shift, scale = mod[0:1], mod[1:2]
        h = x * _rms_scale(x) * gmix_ref[...]
        h_ref[rows, :] = (h * (1.0 + scale) + shift).astype(BF16)

    def project(h_ref, y_ref, cols):
        y_ref[:, cols] = jnp.dot(h_ref[...], w_ref[:, cols], preferred_element_type=F32)

    def finish_q(y_ref):
        q = _norm_rope_transposed(y_ref[:, OFF_Q:OFF_Q + ATTN_WIDTH].T, gq_ref[...],
                                  cos_ref[...], sin_ref[...])
        qT_ref[0] = q.astype(BF16)

    def finish_kv(y_ref):
        k = _norm_rope_transposed(y_ref[:, OFF_K:OFF_K + KV_WIDTH].T, gk_ref[...],
                                  cos_ref[...], sin_ref[...])
        k_ref[0] = k.T.astype(BF16)
        vT_ref[0] = y_ref[:, OFF_V:OFF_V + KV_WIDTH].T.astype(BF16)

    def finish_rest(y_ref):
        rest_ref[0, :, 0:LRU_WIDTH] = y_ref[:, OFF_R:OFF_R + LRU_WIDTH]
        rest_ref[0, :, LRU_WIDTH:2 * LRU_WIDTH] = jax.nn.gelu(y_ref[:, OFF_G:OFF_G + LRU_WIDTH])
        rest_ref[0, :, 2 * LRU_WIDTH:2 * LRU_WIDTH + SC_WIDTH] = y_ref[:, OFF_B:OFF_B + SC_WIDTH]
        rest_ref[0, :, 2 * LRU_WIDTH + SC_WIDTH:] = (y_ref[:, OFF_C:OFF_C + SC_WIDTH]
                                                     * y_ref[:, OFF_U:OFF_U + SC_WIDTH])

    def stages(h_in, y_out, h_out, y_in):
        quarter = IN_WIDTH // 4
        half = TM // 2
        project(h_in, y_out, slice(0, quarter))
        finish_q(y_in)
        project(h_in, y_out, slice(quarter, 2 * quarter))
        normalise(h_out, slice(0, half))
        project(h_in, y_out, slice(2 * quarter, 3 * quarter))
        finish_kv(y_in)
        finish_rest(y_in)
        project(h_in, y_out, slice(3 * quarter, IN_WIDTH))
        normalise(h_out, slice(half, TM))

    @pl.when(s == 0)
    def _():
        h_odd_ref[...] = jnp.zeros_like(h_odd_ref)
        y_even_ref[...] = jnp.zeros_like(y_even_ref)

    @pl.when(s % 2 == 0)
    def _():
        stages(h_odd_ref, y_odd_ref, h_even_ref, y_even_ref)

    @pl.when(s % 2 == 1)
    def _():
        stages(h_even_ref, y_even_ref, h_odd_ref, y_odd_ref)


def _token_specs(x_base, tile_offset=0):
    return [pl.BlockSpec((1, TM, D_MODEL), lambda b, i: (b, 0, 0)),
            pl.BlockSpec((1, TM, D_MODEL),
                         lambda b, i: (b, x_base + jnp.maximum(i + tile_offset - 1, 0), 0))]


def _inproj(ctx_arr, x_arr, x_base, mods, g_mix, w_in, gq, gk, cos_t, sin_t):
    bsz = x_arr.shape[0]
    nt = x_arr.shape[1] // TM - x_base + 1
    t_all = nt * TM
    n_tiles = bsz * nt
    rest_w = 2 * LRU_WIDTH + 2 * SC_WIDTH

    def cur(s):
        sc = jnp.minimum(s, n_tiles - 1)
        return sc // nt, sc % nt

    def prev(s):
        sp = jnp.maximum(s - 2, 0)
        return sp // nt, sp % nt

    rope_spec = pl.BlockSpec((HEAD_DIM // 2, TM), lambda s: (0, prev(s)[1]))
    return pl.pallas_call(
        functools.partial(_inproj_kernel, nt=nt, n_tiles=n_tiles),
        grid=(n_tiles + 2,),
        in_specs=[pl.BlockSpec((1, TM, D_MODEL), lambda s: (cur(s)[0], 0, 0)),
                  pl.BlockSpec((1, TM, D_MODEL),
                               lambda s: (cur(s)[0], x_base + jnp.maximum(cur(s)[1] - 1, 0), 0)),
                  pl.BlockSpec((1, 1, MOD_ROWS, D_MODEL),
                               lambda s: (cur(s)[0], jnp.minimum(cur(s)[1], 1), 0, 0)),
                  _const_spec((1, D_MODEL)),
                  _const_spec((D_MODEL, IN_WIDTH)),
                  _const_spec((HEAD_DIM, TM)),
                  _const_spec((HEAD_DIM, TM)),
                  rope_spec,
                  rope_spec],
        out_specs=[pl.BlockSpec((1, ATTN_WIDTH, TM), lambda s: (prev(s)[0], 0, prev(s)[1])),
                   pl.BlockSpec((1, TM, KV_WIDTH), lambda s: (prev(s)[0], prev(s)[1], 0)),
                   pl.BlockSpec((1, KV_WIDTH, TM), lambda s: (prev(s)[0], 0, prev(s)[1])),
                   pl.BlockSpec((1, TM, rest_w), lambda s: (prev(s)[0], prev(s)[1], 0))],
        out_shape=[jax.ShapeDtypeStruct((bsz, ATTN_WIDTH, t_all), BF16),
                   jax.ShapeDtypeStruct((bsz, t_all, KV_WIDTH), BF16),
                   jax.ShapeDtypeStruct((bsz, KV_WIDTH, t_all), BF16),
                   jax.ShapeDtypeStruct((bsz, t_all, rest_w), F32)],
        scratch_shapes=[pltpu.VMEM((TM, D_MODEL), BF16)] * 2 + [pltpu.VMEM((TM, IN_WIDTH), F32)] * 2,
        compiler_params=_params(40, n_grid_dims=1),
        name="inproj",
    )(ctx_arr, x_arr, mods, g_mix, w_in, gq, gk, cos_t, sin_t)


def _with_halo(prev_ref, cur_ref, next_ref, valid_prev, valid_next):
    prev = prev_ref[0] * valid_prev.astype(F32)
    nxt = next_ref[0] * valid_next.astype(F32)
    return jnp.concatenate([prev, cur_ref[0], nxt], axis=0)


def _shifted(tcat, offset):
    rows = tcat.shape[0]
    if offset == 0:
        return tcat[F32_SUBLANES:F32_SUBLANES + TM]
    return pltpu.roll(tcat, (-offset) % rows, 0)[F32_SUBLANES:F32_SUBLANES + TM]


def _lru_coeffs(rcat, conv_w, conv_b, w_gate, b_gate, sp_lam):
    xc = conv_b
    for j in range(LRU_CONV):
        xc = xc + conv_w[j:j + 1] * _shifted(rcat, j - 2)
    z = jnp.dot(xc.astype(BF16), w_gate, preferred_element_type=F32) + b_gate
    r_gate = jax.nn.sigmoid(z[:, :LRU_WIDTH])
    i_gate = jax.nn.sigmoid(z[:, LRU_WIDTH:])
    log_a = (-LRU_C) * r_gate * sp_lam
    a = jnp.exp(log_a)
    one_minus_a2 = -jnp.tanh(log_a) * (1.0 + a * a)
    return a, jnp.sqrt(one_minus_a2) * (i_gate * xc)


def _lru_kernel(rf_ref, rfp_ref, rfn_ref, rb_ref, rbp_ref, rbn_ref, bg_ref, cu_ref, cup_ref, cun_ref,
                cw_ref, cb_ref, wg_ref, bgate_ref, lam_ref, scw_ref,
                hf_ref, hb_ref, sc_ref,
                af_s, bf_s, ab_s, bb_s, hf_state, hb_state, *, nt):
    i = pl.program_id(1)
    j = jnp.where(i == 0, 0, nt - i)

    @pl.when(i == 0)
    def _():
        hf_state[...] = jnp.zeros_like(hf_state)
        hb_state[...] = jnp.zeros_like(hb_state)

    def halo_valid(t):
        return jnp.logical_and(t != 0, t != 1), jnp.logical_and(t != 0, t != nt - 1)

    lam = lam_ref[...]
    sp_lam = jnp.maximum(-lam, 0.0) + jnp.log1p(jnp.exp(-jnp.abs(lam)))
    cw = cw_ref[...]
    cb = cb_ref[...]

    vp, vn = halo_valid(i)
    a, b = _lru_coeffs(_with_halo(rfp_ref, rf_ref, rfn_ref, vp, vn), cw, cb,
                       wg_ref[:, :2 * LRU_WIDTH], bgate_ref[:, :2 * LRU_WIDTH], sp_lam[0:1])
    _block_prefix(a, b, af_s, bf_s, reverse=False)
    cucat = _with_halo(cup_ref, cu_ref, cun_ref, vp, vn)
    scw = scw_ref[...]
    conv = scw[0:1] * _shifted(cucat, -1) + scw[1:2] * _shifted(cucat, 0) + scw[2:3] * _shifted(cucat, 1)
    sc_ref[0] = (bg_ref[0] * conv).astype(BF16)

    vp, vn = halo_valid(j)
    a, b = _lru_coeffs(_with_halo(rbp_ref, rb_ref, rbn_ref, vp, vn), cw, cb,
                       wg_ref[:, 2 * LRU_WIDTH:], bgate_ref[:, 2 * LRU_WIDTH:], sp_lam[1:2])
    _block_prefix(a, b, ab_s, bb_s, reverse=True)

    _carry_blocks(af_s, bf_s, hf_ref, hf_state, reverse=False)
    _carry_blocks(ab_s, bb_s, hb_ref, hb_state, reverse=True)


def _block_prefix(a, b, a_s, b_s, reverse):
    n = F32_SUBLANES
    a = a.reshape(TM // n, n, LRU_WIDTH)
    b = b.reshape(TM // n, n, LRU_WIDTH)
    row = lax.broadcasted_iota(jnp.int32, a.shape, 1)
    d = 1
    while d < n:
        shift, keep = (n - d, row < n - d) if reverse else (d, row >= d)
        a_prev = jnp.where(keep, pltpu.roll(a, shift, 1), 1.0)
        b_prev = jnp.where(keep, pltpu.roll(b, shift, 1), 0.0)
        b = a * b_prev + b
        a = a * a_prev
        d *= 2
    a_s[...] = a
    b_s[...] = b


def _carry_blocks(a_s, b_s, out_ref, state_ref, reverse):
    n = F32_SUBLANES
    nb = TM // n
    last = 0 if reverse else n - 1
    h_in = state_ref[...]
    for v in (reversed(range(nb)) if reverse else range(nb)):
        h = a_s[v] * h_in + b_s[v]
        out_ref[0, v * n:(v + 1) * n, :] = h
        h_in = h[last:last + 1]
    state_ref[...] = h_in


def _lru(rest, conv_w, conv_b, w_gate, b_gate, lam, sc_w):
    bsz, t_all, _ = rest.shape
    nt = t_all // TM
    hpt = TM // F32_SUBLANES
    nhb = t_all // F32_SUBLANES

    def rev(i):
        return jnp.where(i == 0, 0, nt - i)

    def tile(col, order):
        return pl.BlockSpec((1, TM, LRU_WIDTH), lambda b, i: (b, order(i), col))

    def prev_halo(col, order):
        return pl.BlockSpec((1, F32_SUBLANES, LRU_WIDTH),
                            lambda b, i: (b, jnp.maximum(order(i) * hpt - 1, 0), col))

    def next_halo(col, order):
        return pl.BlockSpec((1, F32_SUBLANES, LRU_WIDTH),
                            lambda b, i: (b, jnp.minimum((order(i) + 1) * hpt, nhb - 1), col))

    fwd = lambda i: i
    out_f32 = jax.ShapeDtypeStruct((bsz, t_all, LRU_WIDTH), F32)
    return pl.pallas_call(
        functools.partial(_lru_kernel, nt=nt),
        grid=(bsz, nt),
        in_specs=[tile(0, fwd), prev_halo(0, fwd), next_halo(0, fwd),
                  tile(0, rev), prev_halo(0, rev), next_halo(0, rev),
                  tile(2, fwd),
                  tile(3, fwd), prev_halo(3, fwd), next_halo(3, fwd),
                  _const_spec((LRU_CONV, LRU_WIDTH)),
                  _const_spec((1, LRU_WIDTH)),
                  _const_spec((LRU_WIDTH, 4 * LRU_WIDTH)),
                  _const_spec((1, 4 * LRU_WIDTH)),
                  _const_spec((2, LRU_WIDTH)),
                  _const_spec((3, SC_WIDTH))],
        out_specs=[pl.BlockSpec((1, TM, LRU_WIDTH), lambda b, i: (b, i, 0)),
                   pl.BlockSpec((1, TM, LRU_WIDTH), lambda b, i: (b, rev(i), 0)),
                   pl.BlockSpec((1, TM, SC_WIDTH), lambda b, i: (b, i, 0))],
        out_shape=[out_f32, out_f32, jax.ShapeDtypeStruct((bsz, t_all, SC_WIDTH), BF16)],
        scratch_shapes=([pltpu.VMEM((TM // F32_SUBLANES, F32_SUBLANES, LRU_WIDTH), F32)] * 4
                        + [pltpu.VMEM((1, LRU_WIDTH), F32)] * 2),
        compiler_params=_params(32),
        name="lru_scan",
    )(rest, rest, rest, rest, rest, rest, rest, rest, rest, rest,
      conv_w, conv_b, w_gate, b_gate, lam, sc_w)


def _mixer_kernel(qT_ref, k_ref, vT_ref, ctx_ref, x_ref, hf_ref, hb_ref, gg_ref, sc_ref, mod_ref, wo_ref,
                  gffn_ref, x1_ref, h2_ref, oT_ref, s_ref, *, tile_offset, n_key_tiles):
    i = pl.program_id(1) + tile_offset
    ones_rows = jnp.ones((BF16_SUBLANES, TM), BF16)

    def project_out():
        x = x_ref[0]
        if tile_offset == 0:
            x = jnp.where(i == 0, ctx_ref[0], x)
        mod = mod_ref[0, 0]
        gate_mix, shift, scale = mod[2:3], mod[3:4], mod[4:5]
        att = oT_ref[...].T.astype(BF16)
        lru = (gg_ref[0] * (hf_ref[0] + hb_ref[0])).astype(BF16)
        y = (jnp.dot(att, wo_ref[0:ATTN_WIDTH], preferred_element_type=F32)
             + jnp.dot(lru, wo_ref[ATTN_WIDTH:ATTN_WIDTH + LRU_WIDTH], preferred_element_type=F32)
             + jnp.dot(sc_ref[0], wo_ref[ATTN_WIDTH + LRU_WIDTH:], preferred_element_type=F32))
        x1 = x + gate_mix * y
        x1_ref[0] = x1
        h2 = x1 * _rms_scale(x1) * gffn_ref[...]
        h2_ref[0] = (h2 * (1.0 + scale) + shift).astype(BF16)

    def run(n_tiles):
        items = [(pr, c) for pr in range(N_HEADS // 2) for c in range(n_tiles)]
        qpads = {}

        def qpad_of(pr):
            if pr not in qpads:
                row0 = pr * 2 * HEAD_DIM
                q2 = jnp.concatenate([qT_ref[0, row0:row0 + HEAD_DIM, :],
                                      qT_ref[0, row0 + HEAD_DIM:row0 + 2 * HEAD_DIM, :]], axis=1)
                zero = jnp.zeros_like(q2)
                first_kv = (2 * pr) // HEADS_PER_KV == 0
                qpads[pr] = jnp.concatenate([q2, zero] if first_kv else [zero, q2], axis=0)
            return qpads[pr]

        def scores(t):
            pr, c = items[t]
            s = jnp.dot(k_ref[0, c * TM:(c + 1) * TM, :], qpad_of(pr), preferred_element_type=F32)
            s_ref[t % SCORE_SLOTS] = s
            return jnp.max(s, axis=0, keepdims=True)

        tile_max = [scores(t) for t in range(min(SCORE_LOOKAHEAD, len(items)))]
        m = acc = None
        for t, (pr, c) in enumerate(items):
            if t + SCORE_LOOKAHEAD < len(items):
                tile_max.append(scores(t + SCORE_LOOKAHEAD))
            if c == 0:
                m = jnp.full((1, 2 * TM), -jnp.inf, F32)
                acc = jnp.zeros((HEAD_DIM + BF16_SUBLANES, 2 * TM), F32)
            g = (2 * pr) // HEADS_PER_KV
            m_new = jnp.maximum(m, tile_max[t])
            alpha = jnp.exp2(m - m_new)
            p = jnp.exp2(s_ref[t % SCORE_SLOTS] - m_new).astype(BF16)
            vt = jnp.concatenate([vT_ref[0, g * HEAD_DIM:(g + 1) * HEAD_DIM, c * TM:(c + 1) * TM],
                                  ones_rows], axis=0)
            acc = alpha * acc + jnp.dot(vt, p, preferred_element_type=F32)
            m = m_new
            tile_max[t] = None
            if c == n_tiles - 1:
                o = acc[:HEAD_DIM] / acc[HEAD_DIM:HEAD_DIM + 1]
                row0 = pr * 2 * HEAD_DIM
                oT_ref[row0:row0 + HEAD_DIM, :] = o[:, :TM]
                oT_ref[row0 + HEAD_DIM:row0 + 2 * HEAD_DIM, :] = o[:, TM:]
        project_out()

    @pl.when(i == 0)
    def _():
        run(1)

    @pl.when(i != 0)
    def _():
        run(n_key_tiles)


def _mixer(qT, k, vT, ctx_arr, x_arr, x_base, hf, hb, rest, sc, mods, w_out, g_ffn, tile_offset):
    bsz, t_all, _ = k.shape
    nt = t_all // TM
    nq = nt - tile_offset
    off = lambda b, i: (b, i + tile_offset, 0)
    return pl.pallas_call(
        functools.partial(_mixer_kernel, tile_offset=tile_offset, n_key_tiles=nt),
        grid=(bsz, nq),
        in_specs=[pl.BlockSpec((1, ATTN_WIDTH, TM), lambda b, i: (b, 0, i + tile_offset)),
                  pl.BlockSpec((1, t_all, KV_WIDTH), lambda b, i: (b, 0, 0)),
                  pl.BlockSpec((1, KV_WIDTH, t_all), lambda b, i: (b, 0, 0))]
                 + _token_specs(x_base, tile_offset) + [
                  pl.BlockSpec((1, TM, LRU_WIDTH), off),
                  pl.BlockSpec((1, TM, LRU_WIDTH), off),
                  pl.BlockSpec((1, TM, LRU_WIDTH), lambda b, i: (b, i + tile_offset, 1)),
                  pl.BlockSpec((1, TM, SC_WIDTH), off),
                  pl.BlockSpec((1, 1, MOD_ROWS, D_MODEL),
                               lambda b, i: (b, jnp.minimum(i + tile_offset, 1), 0, 0)),
                  _const_spec((D_MODEL, D_MODEL)),
                  _const_spec((1, D_MODEL))],
        out_specs=[pl.BlockSpec((1, TM, D_MODEL), lambda b, i: (b, i, 0)),
                   pl.BlockSpec((1, TM, D_MODEL), lambda b, i: (b, i, 0))],
        out_shape=[jax.ShapeDtypeStruct((bsz, nq * TM, D_MODEL), F32),
                   jax.ShapeDtypeStruct((bsz, nq * TM, D_MODEL), BF16)],
        scratch_shapes=[pltpu.VMEM((ATTN_WIDTH, TM), F32), pltpu.VMEM((SCORE_SLOTS, TM, 2 * TM), F32)],
        compiler_params=_params(40),
        name="mixer",
    )(qT, k, vT, ctx_arr, x_arr, hf, hb, rest, sc, mods, w_out, g_ffn)


def _ffn_kernel(h_ref, hp_ref, hn_ref, x1_ref, mod_ref, wup_ref, cw_ref, wdn_ref, gfin_ref,
                o_ref, acc_ref, up_ref, *, ctx_tiles, nt, final_norm):
    i = pl.program_id(1)
    valid_prev = jnp.logical_and(i != 0, i != ctx_tiles)
    valid_next = jnp.logical_and(i != nt - 1, i != ctx_tiles - 1)
    hp = hp_ref[0] * valid_prev.astype(BF16)
    hn = hn_ref[0] * valid_next.astype(BF16)
    hcat = jnp.concatenate([hp, h_ref[0], hn], axis=0)
    rows = hcat.shape[0]

    def conv(t, w):
        lo, hi = BF16_SUBLANES, BF16_SUBLANES + TM
        return (w[0:1] * pltpu.roll(t, 1, 0)[lo:hi] + w[1:2] * t[lo:hi]
                + w[2:3] * pltpu.roll(t, rows - 1, 0)[lo:hi])

    n_chunks = D_FF // FF_CHUNK

    def up_project(c):
        cu = slice(c * FF_CHUNK, (c + 1) * FF_CHUNK)
        cg = slice(D_FF + c * FF_CHUNK, D_FF + (c + 1) * FF_CHUNK)
        up_ref[c % UP_SLOTS, :, :FF_CHUNK] = jnp.dot(hcat, wup_ref[:, cu], preferred_element_type=F32)
        up_ref[c % UP_SLOTS, :, FF_CHUNK:] = jnp.dot(hcat, wup_ref[:, cg], preferred_element_type=F32)

    for c in range(min(UP_LOOKAHEAD, n_chunks)):
        up_project(c)
    for c in range(n_chunks):
        if c + UP_LOOKAHEAD < n_chunks:
            up_project(c + UP_LOOKAHEAD)
        cu = slice(c * FF_CHUNK, (c + 1) * FF_CHUNK)
        cg = slice(D_FF + c * FF_CHUNK, D_FF + (c + 1) * FF_CHUNK)
        u = conv(up_ref[c % UP_SLOTS, :, :FF_CHUNK], cw_ref[:, cu])
        g = conv(up_ref[c % UP_SLOTS, :, FF_CHUNK:], cw_ref[:, cg])
        act = ((g * jax.nn.sigmoid(g)) * u).astype(BF16)
        part = jnp.dot(act, wdn_ref[cu, :], preferred_element_type=F32)
        if c == 0:
            acc_ref[...] = part
        else:
            acc_ref[...] += part

    gate_ffn = mod_ref[0, 0][5:6]
    x2 = x1_ref[0] + gate_ffn * acc_ref[...]
    if final_norm:
        x2 = x2 * _rms_scale(x2) * gfin_ref[...]
    o_ref[0] = x2


def _ffn(h2, x1, mods, w_up, conv_w, w_down, g_final, ctx_tiles, final_norm):
    bsz, t_len, _ = h2.shape
    nt = t_len // TM
    hpt = TM // BF16_SUBLANES
    nhb = t_len // BF16_SUBLANES
    return pl.pallas_call(
        functools.partial(_ffn_kernel, ctx_tiles=ctx_tiles, nt=nt, final_norm=final_norm),
        grid=(bsz, nt),
        in_specs=[pl.BlockSpec((1, TM, D_MODEL), lambda b, i: (b, i, 0)),
                  pl.BlockSpec((1, BF16_SUBLANES, D_MODEL),
                               lambda b, i: (b, jnp.maximum(i * hpt - 1, 0), 0)),
                  pl.BlockSpec((1, BF16_SUBLANES, D_MODEL),
                               lambda b, i: (b, jnp.minimum((i + 1) * hpt, nhb - 1), 0)),
                  pl.BlockSpec((1, TM, D_MODEL), lambda b, i: (b, i, 0)),
                  pl.BlockSpec((1, 1, MOD_ROWS, D_MODEL),
                               lambda b, i: (b, jnp.minimum(i + 1 - ctx_tiles, 1), 0, 0)),
                  _const_spec((D_MODEL, 2 * D_FF)),
                  _const_spec((3, 2 * D_FF)),
                  _const_spec((D_FF, D_MODEL)),
                  _const_spec((1, D_MODEL))],
        out_specs=pl.BlockSpec((1, TM, D_MODEL), lambda b, i: (b, i, 0)),
        out_shape=jax.ShapeDtypeStruct((bsz, t_len, D_MODEL), F32),
        scratch_shapes=[pltpu.VMEM((TM, D_MODEL), F32),
                        pltpu.VMEM((UP_SLOTS, TM + 2 * BF16_SUBLANES, 2 * FF_CHUNK), F32)],
        compiler_params=_params(48),
        name="conv_ffn",
    )(h2, h2, h2, x1, mods, w_up, conv_w, w_down, g_final)


def _rope_tables(seq, ctx_len):
    rows = seq // GRID_W
    pos_r = np.repeat(np.arange(rows, dtype=np.float32), GRID_W)
    pos_c = np.tile(np.arange(GRID_W, dtype=np.float32), rows)
    n_f = HEAD_DIM // 4
    inv = (np.float32(ROPE_THETA) ** (-np.arange(n_f, dtype=np.float32) / n_f)).astype(np.float32)
    ang = np.concatenate([pos_r[:, None] * inv, pos_c[:, None] * inv], axis=-1).astype(np.float32)
    cos = np.concatenate([np.ones((ctx_len, HEAD_DIM // 2)), np.cos(ang.astype(np.float64))], axis=0)
    sin = np.concatenate([np.zeros((ctx_len, HEAD_DIM // 2)), np.sin(ang.astype(np.float64))], axis=0)
    return jnp.asarray(cos.T, F32), jnp.asarray(sin.T, F32)


def _gate_weights(wa, ba, wi, bi):
    def dense(w):
        eye = jnp.eye(LRU_BLOCKS, dtype=w.dtype)
        return jnp.einsum("nde,nm->ndme", w, eye).reshape(LRU_WIDTH, LRU_WIDTH)
    w = jnp.concatenate([dense(wa[0]), dense(wi[0]), dense(wa[1]), dense(wi[1])], axis=1)
    b = jnp.concatenate([ba[0], bi[0], ba[1], bi[1]])[None, :]
    return w.astype(BF16), b


def kernel(x, c, ctx, c_ctx, w_mod, b_mod, g_mix, g_ffn, w_in, g_q, g_k, lru_conv_w, lru_conv_b,
           lru_wa, lru_ba, lru_wi, lru_bi, lru_lam, sc_conv_w, w_out, w_up, ffn_conv_w, w_down,
           g_final):
    bsz, seq, _ = x.shape
    ctx_len = ctx.shape[1]
    depth = w_mod.shape[0]
    assert ctx_len == TM and seq % TM == 0 and seq % GRID_W == 0

    cos_t, sin_t = _rope_tables(seq, ctx_len)
    ctx_arr, x_arr, x_base = ctx, x, 0
    cc_rows = -(-(bsz + 1) // F32_SUBLANES) * F32_SUBLANES
    cc = jnp.zeros((cc_rows, D_MODEL), F32).at[:bsz].set(c).at[bsz].set(c_ctx)

    mod_layers = _modulation(cc, w_mod, b_mod)

    out = None
    for l in range(depth):
        last = l == depth - 1
        mod_all = mod_layers[l].reshape(cc_rows, N_MOD, D_MODEL)
        mod_x = mod_all[:bsz]
        mod_c = jnp.broadcast_to(mod_all[bsz][None], (bsz, N_MOD, D_MODEL))
        mods = jnp.stack([mod_c, mod_x], axis=1)
        mods = jnp.pad(mods, ((0, 0), (0, 0), (0, MOD_ROWS - N_MOD), (0, 0)))

        gq = jnp.broadcast_to((g_q[l] * Q_SCALE)[:, None], (HEAD_DIM, TM))
        gk = jnp.broadcast_to(g_k[l][:, None], (HEAD_DIM, TM))
        qT, k, vT, rest = _inproj(ctx_arr, x_arr, x_base, mods, g_mix[l][None, :],
                                  w_in[l].astype(BF16), gq, gk, cos_t, sin_t)

        w_gate, b_gate = _gate_weights(lru_wa[l], lru_ba[l], lru_wi[l], lru_bi[l])
        hf, hb, sc = _lru(rest, lru_conv_w[l], lru_conv_b[l][None, :], w_gate, b_gate, lru_lam[l],
                          sc_conv_w[l])

        tile_offset = 1 if last else 0
        x1, h2 = _mixer(qT, k, vT, ctx_arr, x_arr, x_base, hf, hb, rest, sc, mods,
                        w_out[l].astype(BF16), g_ffn[l][None, :], tile_offset)
        out = _ffn(h2, x1, mods, w_up[l].astype(BF16), ffn_conv_w[l], w_down[l].astype(BF16),
                   g_final[None, :], ctx_tiles=1 - tile_offset, final_norm=last)
        ctx_arr, x_arr, x_base = out, out, 1
    return out
```

```python
import functools

import numpy as np
import jax
import jax.numpy as jnp
from jax import lax
from jax.experimental import pallas as pl
from jax.experimental.pallas import tpu as pltpu

F32 = jnp.float32
BF16 = jnp.bfloat16

D_MODEL = 1024
HEAD_DIM = 64
N_HEADS = 8
N_KV_HEADS = 2
HEADS_PER_KV = N_HEADS // N_KV_HEADS
ATTN_WIDTH = N_HEADS * HEAD_DIM
KV_WIDTH = N_KV_HEADS * HEAD_DIM
LRU_WIDTH = 256
LRU_BLOCKS = 4
LRU_BLOCK = LRU_WIDTH // LRU_BLOCKS
LRU_CONV = 4
LRU_C = 8.0
SC_WIDTH = 256
IN_WIDTH = 2048
D_FF = 2816
FF_CHUNK = 256
GRID_W = 64
ROPE_THETA = 10000.0
EPS = 1e-6
Q_SCALE = HEAD_DIM ** -0.5 * 1.4426950408889634

TM = 256
SCORE_LOOKAHEAD = 3
SCORE_SLOTS = SCORE_LOOKAHEAD + 2
UP_LOOKAHEAD = 3
UP_SLOTS = UP_LOOKAHEAD + 2
LANES = 128
F32_SUBLANES = 8
BF16_SUBLANES = 16
N_MOD = 6
MOD_ROWS = 8

OFF_Q = 0
OFF_K = OFF_Q + ATTN_WIDTH
OFF_V = OFF_K + KV_WIDTH
OFF_R = OFF_V + KV_WIDTH
OFF_G = OFF_R + LRU_WIDTH
OFF_B = OFF_G + LRU_WIDTH
OFF_C = OFF_B + SC_WIDTH
OFF_U = OFF_C + SC_WIDTH


def _const_spec(shape):
    nd = len(shape)
    return pl.BlockSpec(shape, lambda *_: (0,) * nd, pipeline_mode=pl.Buffered(1))


def _params(vmem_mb, n_grid_dims=2):
    return pltpu.CompilerParams(dimension_semantics=("arbitrary",) * n_grid_dims,
                                vmem_limit_bytes=vmem_mb * 1024 * 1024)


def _rms_scale(t):
    return lax.rsqrt(jnp.mean(t * t, axis=-1, keepdims=True) + EPS)


def _mod_kernel(c_ref, w_ref, b_ref, o_ref):
    c = c_ref[...]
    h = (c * jax.nn.sigmoid(c)).astype(BF16)
    o_ref[...] = jnp.dot(h, w_ref[...].astype(BF16), preferred_element_type=F32) + b_ref[...]


def _modulation(cc, w_mod, b_mod):
    rows = cc.shape[0]
    depth = w_mod.shape[0]
    return pl.pallas_call(
        _mod_kernel,
        grid=(depth, N_MOD),
        in_specs=[pl.BlockSpec((rows, D_MODEL), lambda l, j: (0, 0)),
                  pl.BlockSpec((None, D_MODEL, D_MODEL), lambda l, j: (l, 0, j)),
                  pl.BlockSpec((None, 1, D_MODEL), lambda l, j: (l, 0, j))],
        out_specs=pl.BlockSpec((None, rows, D_MODEL), lambda l, j: (l, 0, j)),
        out_shape=jax.ShapeDtypeStruct((depth, rows, N_MOD * D_MODEL), F32),
        name="modulation",
    )(cc, w_mod, b_mod[:, None, :])


def _norm_rope_transposed(t, gain, cos, sin):
    half = HEAD_DIM // 2
    out = []
    for h in range(t.shape[0] // HEAD_DIM):
        r = t[h * HEAD_DIM:(h + 1) * HEAD_DIM]
        r = r * lax.rsqrt(jnp.mean(r * r, axis=0, keepdims=True) + EPS) * gain
        x1, x2 = r[:half], r[half:]
        out += [x1 * cos - x2 * sin, x1 * sin + x2 * cos]
    return jnp.concatenate(out, axis=0)


def _inproj_kernel(ctx_ref, x_ref, mod_ref, gmix_ref, w_ref, gq_ref, gk_ref, cos_ref, sin_ref,
                   qT_ref, k_ref, vT_ref, rest_ref, y_even_ref, y_odd_ref, *, nt, n_tiles):
    s = pl.program_id(0)
    tile = jnp.minimum(s, n_tiles - 1) % nt

    def project(y_ref):
        x = jnp.where(tile == 0, ctx_ref[0], x_ref[0])
        mod = mod_ref[0, 0]
        shift, scale = mod[0:1], mod[1:2]
        h = x * _rms_scale(x) * gmix_ref[...]
        h = h * (1.0 + scale) + shift
        y_ref[...] = jnp.dot(h.astype(BF16), w_ref[...], preferred_element_type=F32)

    def finish(y_ref):
        cos, sin = cos_ref[...], sin_ref[...]
        q = _norm_rope_transposed(y_ref[:, OFF_Q:OFF_Q + ATTN_WIDTH].T, gq_ref[...], cos, sin)
        qT_ref[0] = q.astype(BF16)
        k = _norm_rope_transposed(y_ref[:, OFF_K:OFF_K + KV_WIDTH].T, gk_ref[...], cos, sin)
        k_ref[0] = k.T.astype(BF16)
        vT_ref[0] = y_ref[:, OFF_V:OFF_V + KV_WIDTH].T.astype(BF16)
        rest_ref[0, :, 0:LRU_WIDTH] = y_ref[:, OFF_R:OFF_R + LRU_WIDTH]
        rest_ref[0, :, LRU_WIDTH:2 * LRU_WIDTH] = jax.nn.gelu(y_ref[:, OFF_G:OFF_G + LRU_WIDTH])
        rest_ref[0, :, 2 * LRU_WIDTH:2 * LRU_WIDTH + SC_WIDTH] = y_ref[:, OFF_B:OFF_B + SC_WIDTH]
        rest_ref[0, :, 2 * LRU_WIDTH + SC_WIDTH:] = (y_ref[:, OFF_C:OFF_C + SC_WIDTH]
                                                     * y_ref[:, OFF_U:OFF_U + SC_WIDTH])

    @pl.when(s == 0)
    def _():
        y_odd_ref[...] = jnp.zeros_like(y_odd_ref)

    @pl.when(s % 2 == 0)
    def _():
        finish(y_odd_ref)
        project(y_even_ref)

    @pl.when(s % 2 == 1)
    def _():
        finish(y_even_ref)
        project(y_odd_ref)


def _token_specs(x_base, tile_offset=0):
    return [pl.BlockSpec((1, TM, D_MODEL), lambda b, i: (b, 0, 0)),
            pl.BlockSpec((1, TM, D_MODEL),
                         lambda b, i: (b, x_base + jnp.maximum(i + tile_offset - 1, 0), 0))]


def _inproj(ctx_arr, x_arr, x_base, mods, g_mix, w_in, gq, gk, cos_t, sin_t):
    bsz = x_arr.shape[0]
    nt = x_arr.shape[1] // TM - x_base + 1
    t_all = nt * TM
    n_tiles = bsz * nt
    rest_w = 2 * LRU_WIDTH + 2 * SC_WIDTH

    def cur(s):
        sc = jnp.minimum(s, n_tiles - 1)
        return sc // nt, sc % nt

    def prev(s):
        sp = jnp.maximum(s - 1, 0)
        return sp // nt, sp % nt

    rope_spec = pl.BlockSpec((HEAD_DIM // 2, TM), lambda s: (0, prev(s)[1]))
    return pl.pallas_call(
        functools.partial(_inproj_kernel, nt=nt, n_tiles=n_tiles),
        grid=(n_tiles + 1,),
        in_specs=[pl.BlockSpec((1, TM, D_MODEL), lambda s: (cur(s)[0], 0, 0)),
                  pl.BlockSpec((1, TM, D_MODEL),
                               lambda s: (cur(s)[0], x_base + jnp.maximum(cur(s)[1] - 1, 0), 0)),
                  pl.BlockSpec((1, 1, MOD_ROWS, D_MODEL),
                               lambda s: (cur(s)[0], jnp.minimum(cur(s)[1], 1), 0, 0)),
                  _const_spec((1, D_MODEL)),
                  _const_spec((D_MODEL, IN_WIDTH)),
                  _const_spec((HEAD_DIM, TM)),
                  _const_spec((HEAD_DIM, TM)),
                  rope_spec,
                  rope_spec],
        out_specs=[pl.BlockSpec((1, ATTN_WIDTH, TM), lambda s: (prev(s)[0], 0, prev(s)[1])),
                   pl.BlockSpec((1, TM, KV_WIDTH), lambda s: (prev(s)[0], prev(s)[1], 0)),
                   pl.BlockSpec((1, KV_WIDTH, TM), lambda s: (prev(s)[0], 0, prev(s)[1])),
                   pl.BlockSpec((1, TM, rest_w), lambda s: (prev(s)[0], prev(s)[1], 0))],
        out_shape=[jax.ShapeDtypeStruct((bsz, ATTN_WIDTH, t_all), BF16),
                   jax.ShapeDtypeStruct((bsz, t_all, KV_WIDTH), BF16),
                   jax.ShapeDtypeStruct((bsz, KV_WIDTH, t_all), BF16),
                   jax.ShapeDtypeStruct((bsz, t_all, rest_w), F32)],
        scratch_shapes=[pltpu.VMEM((TM, IN_WIDTH), F32)] * 2,
        compiler_params=_params(40, n_grid_dims=1),
        name="inproj",
    )(ctx_arr, x_arr, mods, g_mix, w_in, gq, gk, cos_t, sin_t)


def _with_halo(prev_ref, cur_ref, next_ref, valid_prev, valid_next):
    prev = prev_ref[0] * valid_prev.astype(F32)
    nxt = next_ref[0] * valid_next.astype(F32)
    return jnp.concatenate([prev, cur_ref[0], nxt], axis=0)


def _shifted(tcat, offset):
    rows = tcat.shape[0]
    if offset == 0:
        return tcat[F32_SUBLANES:F32_SUBLANES + TM]
    return pltpu.roll(tcat, (-offset) % rows, 0)[F32_SUBLANES:F32_SUBLANES + TM]


def _lru_coeffs(rcat, conv_w, conv_b, w_gate, b_gate, sp_lam):
    xc = conv_b
    for j in range(LRU_CONV):
        xc = xc + conv_w[j:j + 1] * _shifted(rcat, j - 2)
    z = jnp.dot(xc.astype(BF16), w_gate, preferred_element_type=F32) + b_gate
    r_gate = jax.nn.sigmoid(z[:, :LRU_WIDTH])
    i_gate = jax.nn.sigmoid(z[:, LRU_WIDTH:])
    log_a = (-LRU_C) * r_gate * sp_lam
    a = jnp.exp(log_a)
    one_minus_a2 = -jnp.tanh(log_a) * (1.0 + a * a)
    return a, jnp.sqrt(one_minus_a2) * (i_gate * xc)


def _lru_kernel(rf_ref, rfp_ref, rfn_ref, rb_ref, rbp_ref, rbn_ref, bg_ref, cu_ref, cup_ref, cun_ref,
                cw_ref, cb_ref, wg_ref, bgate_ref, lam_ref, scw_ref,
                hf_ref, hb_ref, sc_ref,
                af_s, bf_s, ab_s, bb_s, hf_state, hb_state, *, nt):
    i = pl.program_id(1)
    j = jnp.where(i == 0, 0, nt - i)

    @pl.when(i == 0)
    def _():
        hf_state[...] = jnp.zeros_like(hf_state)
        hb_state[...] = jnp.zeros_like(hb_state)

    def halo_valid(t):
        return jnp.logical_and(t != 0, t != 1), jnp.logical_and(t != 0, t != nt - 1)

    lam = lam_ref[...]
    sp_lam = jnp.maximum(-lam, 0.0) + jnp.log1p(jnp.exp(-jnp.abs(lam)))
    cw = cw_ref[...]
    cb = cb_ref[...]

    vp, vn = halo_valid(i)
    a, b = _lru_coeffs(_with_halo(rfp_ref, rf_ref, rfn_ref, vp, vn), cw, cb,
                       wg_ref[:, :2 * LRU_WIDTH], bgate_ref[:, :2 * LRU_WIDTH], sp_lam[0:1])
    _block_prefix(a, b, af_s, bf_s, reverse=False)
    cucat = _with_halo(cup_ref, cu_ref, cun_ref, vp, vn)
    scw = scw_ref[...]
    conv = scw[0:1] * _shifted(cucat, -1) + scw[1:2] * _shifted(cucat, 0) + scw[2:3] * _shifted(cucat, 1)
    sc_ref[0] = (bg_ref[0] * conv).astype(BF16)

    vp, vn = halo_valid(j)
    a, b = _lru_coeffs(_with_halo(rbp_ref, rb_ref, rbn_ref, vp, vn), cw, cb,
                       wg_ref[:, 2 * LRU_WIDTH:], bgate_ref[:, 2 * LRU_WIDTH:], sp_lam[1:2])
    _block_prefix(a, b, ab_s, bb_s, reverse=True)

    _carry_blocks(af_s, bf_s, hf_ref, hf_state, reverse=False)
    _carry_blocks(ab_s, bb_s, hb_ref, hb_state, reverse=True)


def _block_prefix(a, b, a_s, b_s, reverse):
    n = F32_SUBLANES
    a = a.reshape(TM // n, n, LRU_WIDTH)
    b = b.reshape(TM // n, n, LRU_WIDTH)
    row = lax.broadcasted_iota(jnp.int32, a.shape, 1)
    d = 1
    while d < n:
        shift, keep = (n - d, row < n - d) if reverse else (d, row >= d)
        a_prev = jnp.where(keep, pltpu.roll(a, shift, 1), 1.0)
        b_prev = jnp.where(keep, pltpu.roll(b, shift, 1), 0.0)
        b = a * b_prev + b
        a = a * a_prev
        d *= 2
    a_s[...] = a
    b_s[...] = b


def _carry_blocks(a_s, b_s, out_ref, state_ref, reverse):
    n = F32_SUBLANES
    nb = TM // n
    last = 0 if reverse else n - 1
    h_in = state_ref[...]
    for v in (reversed(range(nb)) if reverse else range(nb)):
        h = a_s[v] * h_in + b_s[v]
        out_ref[0, v * n:(v + 1) * n, :] = h
        h_in = h[last:last + 1]
    state_ref[...] = h_in


def _lru(rest, conv_w, conv_b, w_gate, b_gate, lam, sc_w):
    bsz, t_all, _ = rest.shape
    nt = t_all // TM
    hpt = TM // F32_SUBLANES
    nhb = t_all // F32_SUBLANES

    def rev(i):
        return jnp.where(i == 0, 0, nt - i)

    def tile(col, order):
        return pl.BlockSpec((1, TM, LRU_WIDTH), lambda b, i: (b, order(i), col))

    def prev_halo(col, order):
        return pl.BlockSpec((1, F32_SUBLANES, LRU_WIDTH),
                            lambda b, i: (b, jnp.maximum(order(i) * hpt - 1, 0), col))

    def next_halo(col, order):
        return pl.BlockSpec((1, F32_SUBLANES, LRU_WIDTH),
                            lambda b, i: (b, jnp.minimum((order(i) + 1) * hpt, nhb - 1), col))

    fwd = lambda i: i
    out_f32 = jax.ShapeDtypeStruct((bsz, t_all, LRU_WIDTH), F32)
    return pl.pallas_call(
        functools.partial(_lru_kernel, nt=nt),
        grid=(bsz, nt),
        in_specs=[tile(0, fwd), prev_halo(0, fwd), next_halo(0, fwd),
                  tile(0, rev), prev_halo(0, rev), next_halo(0, rev),
                  tile(2, fwd),
                  tile(3, fwd), prev_halo(3, fwd), next_halo(3, fwd),
                  _const_spec((LRU_CONV, LRU_WIDTH)),
                  _const_spec((1, LRU_WIDTH)),
                  _const_spec((LRU_WIDTH, 4 * LRU_WIDTH)),
                  _const_spec((1, 4 * LRU_WIDTH)),
                  _const_spec((2, LRU_WIDTH)),
                  _const_spec((3, SC_WIDTH))],
        out_specs=[pl.BlockSpec((1, TM, LRU_WIDTH), lambda b, i: (b, i, 0)),
                   pl.BlockSpec((1, TM, LRU_WIDTH), lambda b, i: (b, rev(i), 0)),
                   pl.BlockSpec((1, TM, SC_WIDTH), lambda b, i: (b, i, 0))],
        out_shape=[out_f32, out_f32, jax.ShapeDtypeStruct((bsz, t_all, SC_WIDTH), BF16)],
        scratch_shapes=([pltpu.VMEM((TM // F32_SUBLANES, F32_SUBLANES, LRU_WIDTH), F32)] * 4
                        + [pltpu.VMEM((1, LRU_WIDTH), F32)] * 2),
        compiler_params=_params(32),
        name="lru_scan",
    )(rest, rest, rest, rest, rest, rest, rest, rest, rest, rest,
      conv_w, conv_b, w_gate, b_gate, lam, sc_w)


def _mixer_kernel(qT_ref, k_ref, vT_ref, ctx_ref, x_ref, hf_ref, hb_ref, gg_ref, sc_ref, mod_ref, wo_ref,
                  gffn_ref, x1_ref, h2_ref, oT_ref, s_ref, *, tile_offset, n_key_tiles):
    i = pl.program_id(1) + tile_offset
    ones_rows = jnp.ones((BF16_SUBLANES, TM), BF16)

    def project_out():
        x = x_ref[0]
        if tile_offset == 0:
            x = jnp.where(i == 0, ctx_ref[0], x)
        mod = mod_ref[0, 0]
        gate_mix, shift, scale = mod[2:3], mod[3:4], mod[4:5]
        att = oT_ref[...].T.astype(BF16)
        lru = (gg_ref[0] * (hf_ref[0] + hb_ref[0])).astype(BF16)
        y = (jnp.dot(att, wo_ref[0:ATTN_WIDTH], preferred_element_type=F32)
             + jnp.dot(lru, wo_ref[ATTN_WIDTH:ATTN_WIDTH + LRU_WIDTH], preferred_element_type=F32)
             + jnp.dot(sc_ref[0], wo_ref[ATTN_WIDTH + LRU_WIDTH:], preferred_element_type=F32))
        x1 = x + gate_mix * y
        x1_ref[0] = x1
        h2 = x1 * _rms_scale(x1) * gffn_ref[...]
        h2_ref[0] = (h2 * (1.0 + scale) + shift).astype(BF16)

    def run(n_tiles):
        items = [(pr, c) for pr in range(N_HEADS // 2) for c in range(n_tiles)]
        qpads = {}

        def qpad_of(pr):
            if pr not in qpads:
                row0 = pr * 2 * HEAD_DIM
                q2 = jnp.concatenate([qT_ref[0, row0:row0 + HEAD_DIM, :],
                                      qT_ref[0, row0 + HEAD_DIM:row0 + 2 * HEAD_DIM, :]], axis=1)
                zero = jnp.zeros_like(q2)
                first_kv = (2 * pr) // HEADS_PER_KV == 0
                qpads[pr] = jnp.concatenate([q2, zero] if first_kv else [zero, q2], axis=0)
            return qpads[pr]

        def scores(t):
            pr, c = items[t]
            s = jnp.dot(k_ref[0, c * TM:(c + 1) * TM, :], qpad_of(pr), preferred_element_type=F32)
            s_ref[t % SCORE_SLOTS] = s
            return jnp.max(s, axis=0, keepdims=True)

        tile_max = [scores(t) for t in range(min(SCORE_LOOKAHEAD, len(items)))]
        m = acc = None
        for t, (pr, c) in enumerate(items):
            if t + SCORE_LOOKAHEAD < len(items):
                tile_max.append(scores(t + SCORE_LOOKAHEAD))
            if c == 0:
                m = jnp.full((1, 2 * TM), -jnp.inf, F32)
                acc = jnp.zeros((HEAD_DIM + BF16_SUBLANES, 2 * TM), F32)
            g = (2 * pr) // HEADS_PER_KV
            m_new = jnp.maximum(m, tile_max[t])
            alpha = jnp.exp2(m - m_new)
            p = jnp.exp2(s_ref[t % SCORE_SLOTS] - m_new).astype(BF16)
            vt = jnp.concatenate([vT_ref[0, g * HEAD_DIM:(g + 1) * HEAD_DIM, c * TM:(c + 1) * TM],
                                  ones_rows], axis=0)
            acc = alpha * acc + jnp.dot(vt, p, preferred_element_type=F32)
            m = m_new
            tile_max[t] = None
            if c == n_tiles - 1:
                o = acc[:HEAD_DIM] / acc[HEAD_DIM:HEAD_DIM + 1]
                row0 = pr * 2 * HEAD_DIM
                oT_ref[row0:row0 + HEAD_DIM, :] = o[:, :TM]
                oT_ref[row0 + HEAD_DIM:row0 + 2 * HEAD_DIM, :] = o[:, TM:]
        project_out()

    @pl.when(i == 0)
    def _():
        run(1)

    @pl.when(i != 0)
    def _():
        run(n_key_tiles)


def _mixer(qT, k, vT, ctx_arr, x_arr, x_base, hf, hb, rest, sc, mods, w_out, g_ffn, tile_offset):
    bsz, t_all, _ = k.shape
    nt = t_all // TM
    nq = nt - tile_offset
    off = lambda b, i: (b, i + tile_offset, 0)
    return pl.pallas_call(
        functools.partial(_mixer_kernel, tile_offset=tile_offset, n_key_tiles=nt),
        grid=(bsz, nq),
        in_specs=[pl.BlockSpec((1, ATTN_WIDTH, TM), lambda b, i: (b, 0, i + tile_offset)),
                  pl.BlockSpec((1, t_all, KV_WIDTH), lambda b, i: (b, 0, 0)),
                  pl.BlockSpec((1, KV_WIDTH, t_all), lambda b, i: (b, 0, 0))]
                 + _token_specs(x_base, tile_offset) + [
                  pl.BlockSpec((1, TM, LRU_WIDTH), off),
                  pl.BlockSpec((1, TM, LRU_WIDTH), off),
                  pl.BlockSpec((1, TM, LRU_WIDTH), lambda b, i: (b, i + tile_offset, 1)),
                  pl.BlockSpec((1, TM, SC_WIDTH), off),
                  pl.BlockSpec((1, 1, MOD_ROWS, D_MODEL),
                               lambda b, i: (b, jnp.minimum(i + tile_offset, 1), 0, 0)),
                  _const_spec((D_MODEL, D_MODEL)),
                  _const_spec((1, D_MODEL))],
        out_specs=[pl.BlockSpec((1, TM, D_MODEL), lambda b, i: (b, i, 0)),
                   pl.BlockSpec((1, TM, D_MODEL), lambda b, i: (b, i, 0))],
        out_shape=[jax.ShapeDtypeStruct((bsz, nq * TM, D_MODEL), F32),
                   jax.ShapeDtypeStruct((bsz, nq * TM, D_MODEL), BF16)],
        scratch_shapes=[pltpu.VMEM((ATTN_WIDTH, TM), F32), pltpu.VMEM((SCORE_SLOTS, TM, 2 * TM), F32)],
        compiler_params=_params(40),
        name="mixer",
    )(qT, k, vT, ctx_arr, x_arr, hf, hb, rest, sc, mods, w_out, g_ffn)


def _ffn_kernel(h_ref, hp_ref, hn_ref, x1_ref, mod_ref, wup_ref, cw_ref, wdn_ref, gfin_ref,
                o_ref, acc_ref, up_ref, *, ctx_tiles, nt, final_norm):
    i = pl.program_id(1)
    valid_prev = jnp.logical_and(i != 0, i != ctx_tiles)
    valid_next = jnp.logical_and(i != nt - 1, i != ctx_tiles - 1)
    hp = hp_ref[0] * valid_prev.astype(BF16)
    hn = hn_ref[0] * valid_next.astype(BF16)
    hcat = jnp.concatenate([hp, h_ref[0], hn], axis=0)
    rows = hcat.shape[0]

    def conv(t, w):
        lo, hi = BF16_SUBLANES, BF16_SUBLANES + TM
        return (w[0:1] * pltpu.roll(t, 1, 0)[lo:hi] + w[1:2] * t[lo:hi]
                + w[2:3] * pltpu.roll(t, rows - 1, 0)[lo:hi])

    n_chunks = D_FF // FF_CHUNK

    def up_project(c):
        cu = slice(c * FF_CHUNK, (c + 1) * FF_CHUNK)
        cg = slice(D_FF + c * FF_CHUNK, D_FF + (c + 1) * FF_CHUNK)
        up_ref[c % UP_SLOTS, :, :FF_CHUNK] = jnp.dot(hcat, wup_ref[:, cu], preferred_element_type=F32)
        up_ref[c % UP_SLOTS, :, FF_CHUNK:] = jnp.dot(hcat, wup_ref[:, cg], preferred_element_type=F32)

    for c in range(min(UP_LOOKAHEAD, n_chunks)):
        up_project(c)
    for c in range(n_chunks):
        if c + UP_LOOKAHEAD < n_chunks:
            up_project(c + UP_LOOKAHEAD)
        cu = slice(c * FF_CHUNK, (c + 1) * FF_CHUNK)
        cg = slice(D_FF + c * FF_CHUNK, D_FF + (c + 1) * FF_CHUNK)
        u = conv(up_ref[c % UP_SLOTS, :, :FF_CHUNK], cw_ref[:, cu])
        g = conv(up_ref[c % UP_SLOTS, :, FF_CHUNK:], cw_ref[:, cg])
        act = ((g * jax.nn.sigmoid(g)) * u).astype(BF16)
        part = jnp.dot(act, wdn_ref[cu, :], preferred_element_type=F32)
        if c == 0:
            acc_ref[...] = part
        else:
            acc_ref[...] += part

    gate_ffn = mod_ref[0, 0][5:6]
    x2 = x1_ref[0] + gate_ffn * acc_ref[...]
    if final_norm:
        x2 = x2 * _rms_scale(x2) * gfin_ref[...]
    o_ref[0] = x2


def _ffn(h2, x1, mods, w_up, conv_w, w_down, g_final, ctx_tiles, final_norm):
    bsz, t_len, _ = h2.shape
    nt = t_len // TM
    hpt = TM // BF16_SUBLANES
    nhb = t_len // BF16_SUBLANES
    return pl.pallas_call(
        functools.partial(_ffn_kernel, ctx_tiles=ctx_tiles, nt=nt, final_norm=final_norm),
        grid=(bsz, nt),
        in_specs=[pl.BlockSpec((1, TM, D_MODEL), lambda b, i: (b, i, 0)),
                  pl.BlockSpec((1, BF16_SUBLANES, D_MODEL),
                               lambda b, i: (b, jnp.maximum(i * hpt - 1, 0), 0)),
                  pl.BlockSpec((1, BF16_SUBLANES, D_MODEL),
                               lambda b, i: (b, jnp.minimum((i + 1) * hpt, nhb - 1), 0)),
                  pl.BlockSpec((1, TM, D_MODEL), lambda b, i: (b, i, 0)),
                  pl.BlockSpec((1, 1, MOD_ROWS, D_MODEL),
                               lambda b, i: (b, jnp.minimum(i + 1 - ctx_tiles, 1), 0, 0)),
                  _const_spec((D_MODEL, 2 * D_FF)),
                  _const_spec((3, 2 * D_FF)),
                  _const_spec((D_FF, D_MODEL)),
                  _const_spec((1, D_MODEL))],
        out_specs=pl.BlockSpec((1, TM, D_MODEL), lambda b, i: (b, i, 0)),
        out_shape=jax.ShapeDtypeStruct((bsz, t_len, D_MODEL), F32),
        scratch_shapes=[pltpu.VMEM((TM, D_MODEL), F32),
                        pltpu.VMEM((UP_SLOTS, TM + 2 * BF16_SUBLANES, 2 * FF_CHUNK), F32)],
        compiler_params=_params(48),
        name="conv_ffn",
    )(h2, h2, h2, x1, mods, w_up, conv_w, w_down, g_final)


def _rope_tables(seq, ctx_len):
    rows = seq // GRID_W
    pos_r = np.repeat(np.arange(rows, dtype=np.float32), GRID_W)
    pos_c = np.tile(np.arange(GRID_W, dtype=np.float32), rows)
    n_f = HEAD_DIM // 4
    inv = (np.float32(ROPE_THETA) ** (-np.arange(n_f, dtype=np.float32) / n_f)).astype(np.float32)
    ang = np.concatenate([pos_r[:, None] * inv, pos_c[:, None] * inv], axis=-1).astype(np.float32)
    cos = np.concatenate([np.ones((ctx_len, HEAD_DIM // 2)), np.cos(ang.astype(np.float64))], axis=0)
    sin = np.concatenate([np.zeros((ctx_len, HEAD_DIM // 2)), np.sin(ang.astype(np.float64))], axis=0)
    return jnp.asarray(cos.T, F32), jnp.asarray(sin.T, F32)


def _gate_weights(wa, ba, wi, bi):
    def dense(w):
        eye = jnp.eye(LRU_BLOCKS, dtype=w.dtype)
        return jnp.einsum("nde,nm->ndme", w, eye).reshape(LRU_WIDTH, LRU_WIDTH)
    w = jnp.concatenate([dense(wa[0]), dense(wi[0]), dense(wa[1]), dense(wi[1])], axis=1)
    b = jnp.concatenate([ba[0], bi[0], ba[1], bi[1]])[None, :]
    return w.astype(BF16), b


def kernel(x, c, ctx, c_ctx, w_mod, b_mod, g_mix, g_ffn, w_in, g_q, g_k, lru_conv_w, lru_conv_b,
           lru_wa, lru_ba, lru_wi, lru_bi, lru_lam, sc_conv_w, w_out, w_up, ffn_conv_w, w_down,
           g_final):
    bsz, seq, _ = x.shape
    ctx_len = ctx.shape[1]
    depth = w_mod.shape[0]
    assert ctx_len == TM and seq % TM == 0 and seq % GRID_W == 0

    cos_t, sin_t = _rope_tables(seq, ctx_len)
    ctx_arr, x_arr, x_base = ctx, x, 0
    cc_rows = -(-(bsz + 1) // F32_SUBLANES) * F32_SUBLANES
    cc = jnp.zeros((cc_rows, D_MODEL), F32).at[:bsz].set(c).at[bsz].set(c_ctx)

    mod_layers = _modulation(cc, w_mod, b_mod)

    out = None
    for l in range(depth):
        last = l == depth - 1
        mod_all = mod_layers[l].reshape(cc_rows, N_MOD, D_MODEL)
        mod_x = mod_all[:bsz]
        mod_c = jnp.broadcast_to(mod_all[bsz][None], (bsz, N_MOD, D_MODEL))
        mods = jnp.stack([mod_c, mod_x], axis=1)
        mods = jnp.pad(mods, ((0, 0), (0, 0), (0, MOD_ROWS - N_MOD), (0, 0)))

        gq = jnp.broadcast_to((g_q[l] * Q_SCALE)[:, None], (HEAD_DIM, TM))
        gk = jnp.broadcast_to(g_k[l][:, None], (HEAD_DIM, TM))
        qT, k, vT, rest = _inproj(ctx_arr, x_arr, x_base, mods, g_mix[l][None, :],
                                  w_in[l].astype(BF16), gq, gk, cos_t, sin_t)

        w_gate, b_gate = _gate_weights(lru_wa[l], lru_ba[l], lru_wi[l], lru_bi[l])
        hf, hb, sc = _lru(rest, lru_conv_w[l], lru_conv_b[l][None, :], w_gate, b_gate, lru_lam[l],
                          sc_conv_w[l])

        tile_offset = 1 if last else 0
        x1, h2 = _mixer(qT, k, vT, ctx_arr, x_arr, x_base, hf, hb, rest, sc, mods,
                        w_out[l].astype(BF16), g_ffn[l][None, :], tile_offset)
        out = _ffn(h2, x1, mods, w_up[l].astype(BF16), ffn_conv_w[l], w_down[l].astype(BF16),
                   g_final[None, :], ctx_tiles=1 - tile_offset, final_norm=last)
        ctx_arr, x_arr, x_base = out, out, 1
    return out
```

```python
import functools

import numpy as np
import jax
import jax.numpy as jnp
from jax import lax
from jax.experimental import pallas as pl
from jax.experimental.pallas import tpu as pltpu

F32 = jnp.float32
BF16 = jnp.bfloat16

D_MODEL = 1024
HEAD_DIM = 64
N_HEADS = 8
N_KV_HEADS = 2
HEADS_PER_KV = N_HEADS // N_KV_HEADS
ATTN_WIDTH = N_HEADS * HEAD_DIM
KV_WIDTH = N_KV_HEADS * HEAD_DIM
LRU_WIDTH = 256
LRU_BLOCKS = 4
LRU_BLOCK = LRU_WIDTH // LRU_BLOCKS
LRU_CONV = 4
LRU_C = 8.0
SC_WIDTH = 256
IN_WIDTH = 2048
D_FF = 2816
FF_CHUNK = 256
EDGE_COLS = 512
GRID_W = 64
ROPE_THETA = 10000.0
EPS = 1e-6
Q_SCALE = HEAD_DIM ** -0.5 * 1.4426950408889634

TM = 256
SCORE_LOOKAHEAD = 3
SCORE_SLOTS = SCORE_LOOKAHEAD + 1
UP_LOOKAHEAD = 3
UP_SLOTS = UP_LOOKAHEAD + 1
LANES = 128
F32_SUBLANES = 8
BF16_SUBLANES = 16
N_MOD = 6
MOD_ROWS = 8

OFF_Q = 0
OFF_K = OFF_Q + ATTN_WIDTH
OFF_V = OFF_K + KV_WIDTH
OFF_R = OFF_V + KV_WIDTH
OFF_G = OFF_R + LRU_WIDTH
OFF_B = OFF_G + LRU_WIDTH
OFF_C = OFF_B + SC_WIDTH
OFF_U = OFF_C + SC_WIDTH


def _const_spec(shape):
    nd = len(shape)
    return pl.BlockSpec(shape, lambda *_: (0,) * nd, pipeline_mode=pl.Buffered(1))


def _params(vmem_mb, n_grid_dims=2):
    return pltpu.CompilerParams(dimension_semantics=("arbitrary",) * n_grid_dims,
                                vmem_limit_bytes=vmem_mb * 1024 * 1024)


def _rms_scale(t):
    return lax.rsqrt(jnp.mean(t * t, axis=-1, keepdims=True) + EPS)


def _mod_kernel(c_ref, w_ref, b_ref, o_ref):
    c = c_ref[...]
    h = (c * jax.nn.sigmoid(c)).astype(BF16)
    o_ref[...] = jnp.dot(h, w_ref[...].astype(BF16), preferred_element_type=F32) + b_ref[...]


def _modulation(cc, w_mod, b_mod):
    rows = cc.shape[0]
    depth = w_mod.shape[0]
    return pl.pallas_call(
        _mod_kernel,
        grid=(depth, N_MOD),
        in_specs=[pl.BlockSpec((rows, D_MODEL), lambda l, j: (0, 0)),
                  pl.BlockSpec((None, D_MODEL, D_MODEL), lambda l, j: (l, 0, j)),
                  pl.BlockSpec((None, 1, D_MODEL), lambda l, j: (l, 0, j))],
        out_specs=pl.BlockSpec((None, rows, D_MODEL), lambda l, j: (l, 0, j)),
        out_shape=jax.ShapeDtypeStruct((depth, rows, N_MOD * D_MODEL), F32),
        name="modulation",
    )(cc, w_mod, b_mod[:, None, :])


def _norm_rope_transposed(t, gain, cos, sin):
    half = HEAD_DIM // 2
    out = []
    for h in range(t.shape[0] // HEAD_DIM):
        r = t[h * HEAD_DIM:(h + 1) * HEAD_DIM]
        r = r * lax.rsqrt(jnp.mean(r * r, axis=0, keepdims=True) + EPS) * gain
        x1, x2 = r[:half], r[half:]
        out += [x1 * cos - x2 * sin, x1 * sin + x2 * cos]
    return jnp.concatenate(out, axis=0)


def _inproj_kernel(ctx_ref, x_ref, mod_ref, gmix_ref, w_ref, gq_ref, gk_ref, cos_ref, sin_ref,
                   qT_ref, k_ref, vT_ref, rest_ref, y_even_ref, y_odd_ref, *, nt, n_tiles):
    s = pl.program_id(0)
    tile = jnp.minimum(s, n_tiles - 1) % nt

    def project(y_ref):
        x = jnp.where(tile == 0, ctx_ref[0], x_ref[0])
        mod = mod_ref[0, 0]
        shift, scale = mod[0:1], mod[1:2]
        h = x * _rms_scale(x) * gmix_ref[...]
        h = h * (1.0 + scale) + shift
        y_ref[...] = jnp.dot(h.astype(BF16), w_ref[...], preferred_element_type=F32)

    def finish(y_ref):
        cos, sin = cos_ref[...], sin_ref[...]
        q = _norm_rope_transposed(y_ref[:, OFF_Q:OFF_Q + ATTN_WIDTH].T, gq_ref[...], cos, sin)
        qT_ref[0] = q.astype(BF16)
        k = _norm_rope_transposed(y_ref[:, OFF_K:OFF_K + KV_WIDTH].T, gk_ref[...], cos, sin)
        k_ref[0] = k.T.astype(BF16)
        vT_ref[0] = y_ref[:, OFF_V:OFF_V + KV_WIDTH].T.astype(BF16)
        rest_ref[0, :, 0:LRU_WIDTH] = y_ref[:, OFF_R:OFF_R + LRU_WIDTH]
        rest_ref[0, :, LRU_WIDTH:2 * LRU_WIDTH] = jax.nn.gelu(y_ref[:, OFF_G:OFF_G + LRU_WIDTH])
        rest_ref[0, :, 2 * LRU_WIDTH:2 * LRU_WIDTH + SC_WIDTH] = y_ref[:, OFF_B:OFF_B + SC_WIDTH]
        rest_ref[0, :, 2 * LRU_WIDTH + SC_WIDTH:] = (y_ref[:, OFF_C:OFF_C + SC_WIDTH]
                                                     * y_ref[:, OFF_U:OFF_U + SC_WIDTH])

    @pl.when(s == 0)
    def _():
        y_odd_ref[...] = jnp.zeros_like(y_odd_ref)

    @pl.when(s % 2 == 0)
    def _():
        finish(y_odd_ref)
        project(y_even_ref)

    @pl.when(s % 2 == 1)
    def _():
        finish(y_even_ref)
        project(y_odd_ref)


def _token_specs(x_base, tile_offset=0):
    return [pl.BlockSpec((1, TM, D_MODEL), lambda b, i: (b, 0, 0)),
            pl.BlockSpec((1, TM, D_MODEL),
                         lambda b, i: (b, x_base + jnp.maximum(i + tile_offset - 1, 0), 0))]


def _inproj(ctx_arr, x_arr, x_base, mods, g_mix, w_in, gq, gk, cos_t, sin_t):
    bsz = x_arr.shape[0]
    nt = x_arr.shape[1] // TM - x_base + 1
    t_all = nt * TM
    n_tiles = bsz * nt
    rest_w = 2 * LRU_WIDTH + 2 * SC_WIDTH

    def cur(s):
        sc = jnp.minimum(s, n_tiles - 1)
        return sc // nt, sc % nt

    def prev(s):
        sp = jnp.maximum(s - 1, 0)
        return sp // nt, sp % nt

    rope_spec = pl.BlockSpec((HEAD_DIM // 2, TM), lambda s: (0, prev(s)[1]))
    return pl.pallas_call(
        functools.partial(_inproj_kernel, nt=nt, n_tiles=n_tiles),
        grid=(n_tiles + 1,),
        in_specs=[pl.BlockSpec((1, TM, D_MODEL), lambda s: (cur(s)[0], 0, 0)),
                  pl.BlockSpec((1, TM, D_MODEL),
                               lambda s: (cur(s)[0], x_base + jnp.maximum(cur(s)[1] - 1, 0), 0)),
                  pl.BlockSpec((1, 1, MOD_ROWS, D_MODEL),
                               lambda s: (cur(s)[0], jnp.minimum(cur(s)[1], 1), 0, 0)),
                  _const_spec((1, D_MODEL)),
                  _const_spec((D_MODEL, IN_WIDTH)),
                  _const_spec((HEAD_DIM, TM)),
                  _const_spec((HEAD_DIM, TM)),
                  rope_spec,
                  rope_spec],
        out_specs=[pl.BlockSpec((1, ATTN_WIDTH, TM), lambda s: (prev(s)[0], 0, prev(s)[1])),
                   pl.BlockSpec((1, TM, KV_WIDTH), lambda s: (prev(s)[0], prev(s)[1], 0)),
                   pl.BlockSpec((1, KV_WIDTH, TM), lambda s: (prev(s)[0], 0, prev(s)[1])),
                   pl.BlockSpec((1, TM, rest_w), lambda s: (prev(s)[0], prev(s)[1], 0))],
        out_shape=[jax.ShapeDtypeStruct((bsz, ATTN_WIDTH, t_all), BF16),
                   jax.ShapeDtypeStruct((bsz, t_all, KV_WIDTH), BF16),
                   jax.ShapeDtypeStruct((bsz, KV_WIDTH, t_all), BF16),
                   jax.ShapeDtypeStruct((bsz, t_all, rest_w), F32)],
        scratch_shapes=[pltpu.VMEM((TM, IN_WIDTH), F32)] * 2,
        compiler_params=_params(40, n_grid_dims=1),
        name="inproj",
    )(ctx_arr, x_arr, mods, g_mix, w_in, gq, gk, cos_t, sin_t)


def _with_halo(prev_ref, cur_ref, next_ref, valid_prev, valid_next):
    prev = prev_ref[0] * valid_prev.astype(F32)
    nxt = next_ref[0] * valid_next.astype(F32)
    return jnp.concatenate([prev, cur_ref[0], nxt], axis=0)


def _shifted(tcat, offset):
    rows = tcat.shape[0]
    if offset == 0:
        return tcat[F32_SUBLANES:F32_SUBLANES + TM]
    return pltpu.roll(tcat, (-offset) % rows, 0)[F32_SUBLANES:F32_SUBLANES + TM]


def _lru_coeffs(rcat, conv_w, conv_b, w_gate, b_gate, sp_lam):
    xc = conv_b
    for j in range(LRU_CONV):
        xc = xc + conv_w[j:j + 1] * _shifted(rcat, j - 2)
    z = jnp.dot(xc.astype(BF16), w_gate, preferred_element_type=F32) + b_gate
    r_gate = jax.nn.sigmoid(z[:, :LRU_WIDTH])
    i_gate = jax.nn.sigmoid(z[:, LRU_WIDTH:])
    log_a = (-LRU_C) * r_gate * sp_lam
    a = jnp.exp(log_a)
    one_minus_a2 = -jnp.tanh(log_a) * (1.0 + a * a)
    return a, jnp.sqrt(one_minus_a2) * (i_gate * xc)


def _lru_kernel(rf_ref, rfp_ref, rfn_ref, rb_ref, rbp_ref, rbn_ref, bg_ref, cu_ref, cup_ref, cun_ref,
                cw_ref, cb_ref, wg_ref, bgate_ref, lam_ref, scw_ref,
                hf_ref, hb_ref, sc_ref,
                af_s, bf_s, ab_s, bb_s, hf_state, hb_state, *, nt):
    i = pl.program_id(1)
    j = jnp.where(i == 0, 0, nt - i)

    @pl.when(i == 0)
    def _():
        hf_state[...] = jnp.zeros_like(hf_state)
        hb_state[...] = jnp.zeros_like(hb_state)

    def halo_valid(t):
        return jnp.logical_and(t != 0, t != 1), jnp.logical_and(t != 0, t != nt - 1)

    lam = lam_ref[...]
    sp_lam = jnp.maximum(-lam, 0.0) + jnp.log1p(jnp.exp(-jnp.abs(lam)))
    cw = cw_ref[...]
    cb = cb_ref[...]

    vp, vn = halo_valid(i)
    a, b = _lru_coeffs(_with_halo(rfp_ref, rf_ref, rfn_ref, vp, vn), cw, cb,
                       wg_ref[:, :2 * LRU_WIDTH], bgate_ref[:, :2 * LRU_WIDTH], sp_lam[0:1])
    _block_prefix(a, b, af_s, bf_s, reverse=False)
    cucat = _with_halo(cup_ref, cu_ref, cun_ref, vp, vn)
    scw = scw_ref[...]
    conv = scw[0:1] * _shifted(cucat, -1) + scw[1:2] * _shifted(cucat, 0) + scw[2:3] * _shifted(cucat, 1)
    sc_ref[0] = (bg_ref[0] * conv).astype(BF16)

    vp, vn = halo_valid(j)
    a, b = _lru_coeffs(_with_halo(rbp_ref, rb_ref, rbn_ref, vp, vn), cw, cb,
                       wg_ref[:, 2 * LRU_WIDTH:], bgate_ref[:, 2 * LRU_WIDTH:], sp_lam[1:2])
    _block_prefix(a, b, ab_s, bb_s, reverse=True)

    _carry_blocks(af_s, bf_s, hf_ref, hf_state, reverse=False)
    _carry_blocks(ab_s, bb_s, hb_ref, hb_state, reverse=True)


def _block_prefix(a, b, a_s, b_s, reverse):
    n = F32_SUBLANES
    a = a.reshape(TM // n, n, LRU_WIDTH)
    b = b.reshape(TM // n, n, LRU_WIDTH)
    row = lax.broadcasted_iota(jnp.int32, a.shape, 1)
    d = 1
    while d < n:
        shift, keep = (n - d, row < n - d) if reverse else (d, row >= d)
        a_prev = jnp.where(keep, pltpu.roll(a, shift, 1), 1.0)
        b_prev = jnp.where(keep, pltpu.roll(b, shift, 1), 0.0)
        b = a * b_prev + b
        a = a * a_prev
        d *= 2
    a_s[...] = a
    b_s[...] = b


def _carry_blocks(a_s, b_s, out_ref, state_ref, reverse):
    n = F32_SUBLANES
    nb = TM // n
    last = 0 if reverse else n - 1
    h_in = state_ref[...]
    for v in (reversed(range(nb)) if reverse else range(nb)):
        h = a_s[v] * h_in + b_s[v]
        out_ref[0, v * n:(v + 1) * n, :] = h
        h_in = h[last:last + 1]
    state_ref[...] = h_in


def _lru(rest, conv_w, conv_b, w_gate, b_gate, lam, sc_w):
    bsz, t_all, _ = rest.shape
    nt = t_all // TM
    hpt = TM // F32_SUBLANES
    nhb = t_all // F32_SUBLANES

    def rev(i):
        return jnp.where(i == 0, 0, nt - i)

    def tile(col, order):
        return pl.BlockSpec((1, TM, LRU_WIDTH), lambda b, i: (b, order(i), col))

    def prev_halo(col, order):
        return pl.BlockSpec((1, F32_SUBLANES, LRU_WIDTH),
                            lambda b, i: (b, jnp.maximum(order(i) * hpt - 1, 0), col))

    def next_halo(col, order):
        return pl.BlockSpec((1, F32_SUBLANES, LRU_WIDTH),
                            lambda b, i: (b, jnp.minimum((order(i) + 1) * hpt, nhb - 1), col))

    fwd = lambda i: i
    out_f32 = jax.ShapeDtypeStruct((bsz, t_all, LRU_WIDTH), F32)
    return pl.pallas_call(
        functools.partial(_lru_kernel, nt=nt),
        grid=(bsz, nt),
        in_specs=[tile(0, fwd), prev_halo(0, fwd), next_halo(0, fwd),
                  tile(0, rev), prev_halo(0, rev), next_halo(0, rev),
                  tile(2, fwd),
                  tile(3, fwd), prev_halo(3, fwd), next_halo(3, fwd),
                  _const_spec((LRU_CONV, LRU_WIDTH)),
                  _const_spec((1, LRU_WIDTH)),
                  _const_spec((LRU_WIDTH, 4 * LRU_WIDTH)),
                  _const_spec((1, 4 * LRU_WIDTH)),
                  _const_spec((2, LRU_WIDTH)),
                  _const_spec((3, SC_WIDTH))],
        out_specs=[pl.BlockSpec((1, TM, LRU_WIDTH), lambda b, i: (b, i, 0)),
                   pl.BlockSpec((1, TM, LRU_WIDTH), lambda b, i: (b, rev(i), 0)),
                   pl.BlockSpec((1, TM, SC_WIDTH), lambda b, i: (b, i, 0))],
        out_shape=[out_f32, out_f32, jax.ShapeDtypeStruct((bsz, t_all, SC_WIDTH), BF16)],
        scratch_shapes=([pltpu.VMEM((TM // F32_SUBLANES, F32_SUBLANES, LRU_WIDTH), F32)] * 4
                        + [pltpu.VMEM((1, LRU_WIDTH), F32)] * 2),
        compiler_params=_params(32),
        name="lru_scan",
    )(rest, rest, rest, rest, rest, rest, rest, rest, rest, rest,
      conv_w, conv_b, w_gate, b_gate, lam, sc_w)


def _mixer_kernel(qT_ref, k_ref, vT_ref, ctx_ref, x_ref, hf_ref, hb_ref, gg_ref, sc_ref, mod_ref, wo_ref,
                  gffn_ref, x1_ref, h2_ref, oT_ref, s_ref, *, tile_offset, n_key_tiles):
    i = pl.program_id(1) + tile_offset
    ones_rows = jnp.ones((BF16_SUBLANES, TM), BF16)

    def project_out():
        x = x_ref[0]
        if tile_offset == 0:
            x = jnp.where(i == 0, ctx_ref[0], x)
        mod = mod_ref[0, 0]
        gate_mix, shift, scale = mod[2:3], mod[3:4], mod[4:5]
        att = oT_ref[...].T.astype(BF16)
        lru = (gg_ref[0] * (hf_ref[0] + hb_ref[0])).astype(BF16)
        y = (jnp.dot(att, wo_ref[0:ATTN_WIDTH], preferred_element_type=F32)
             + jnp.dot(lru, wo_ref[ATTN_WIDTH:ATTN_WIDTH + LRU_WIDTH], preferred_element_type=F32)
             + jnp.dot(sc_ref[0], wo_ref[ATTN_WIDTH + LRU_WIDTH:], preferred_element_type=F32))
        x1 = x + gate_mix * y
        x1_ref[0] = x1
        h2 = x1 * _rms_scale(x1) * gffn_ref[...]
        h2_ref[0] = (h2 * (1.0 + scale) + shift).astype(BF16)

    def run(n_tiles):
        items = [(pr, c) for pr in range(N_HEADS // 2) for c in range(n_tiles)]
        qpads = {}

        def qpad_of(pr):
            if pr not in qpads:
                row0 = pr * 2 * HEAD_DIM
                q2 = jnp.concatenate([qT_ref[0, row0:row0 + HEAD_DIM, :],
                                      qT_ref[0, row0 + HEAD_DIM:row0 + 2 * HEAD_DIM, :]], axis=1)
                zero = jnp.zeros_like(q2)
                first_kv = (2 * pr) // HEADS_PER_KV == 0
                qpads[pr] = jnp.concatenate([q2, zero] if first_kv else [zero, q2], axis=0)
            return qpads[pr]

        def scores(t):
            pr, c = items[t]
            s = jnp.dot(k_ref[0, c * TM:(c + 1) * TM, :], qpad_of(pr), preferred_element_type=F32)
            s_ref[t % SCORE_SLOTS] = s
            return jnp.max(s, axis=0, keepdims=True)

        tile_max = [scores(t) for t in range(min(SCORE_LOOKAHEAD, len(items)))]
        m = acc = None
        for t, (pr, c) in enumerate(items):
            if t + SCORE_LOOKAHEAD < len(items):
                tile_max.append(scores(t + SCORE_LOOKAHEAD))
            if c == 0:
                m = jnp.full((1, 2 * TM), -jnp.inf, F32)
                acc = jnp.zeros((HEAD_DIM + BF16_SUBLANES, 2 * TM), F32)
            g = (2 * pr) // HEADS_PER_KV
            m_new = jnp.maximum(m, tile_max[t])
            alpha = jnp.exp2(m - m_new)
            p = jnp.exp2(s_ref[t % SCORE_SLOTS] - m_new).astype(BF16)
            vt = jnp.concatenate([vT_ref[0, g * HEAD_DIM:(g + 1) * HEAD_DIM, c * TM:(c + 1) * TM],
                                  ones_rows], axis=0)
            acc = alpha * acc + jnp.dot(vt, p, preferred_element_type=F32)
            m = m_new
            tile_max[t] = None
            if c == n_tiles - 1:
                o = acc[:HEAD_DIM] / acc[HEAD_DIM:HEAD_DIM + 1]
                row0 = pr * 2 * HEAD_DIM
                oT_ref[row0:row0 + HEAD_DIM, :] = o[:, :TM]
                oT_ref[row0 + HEAD_DIM:row0 + 2 * HEAD_DIM, :] = o[:, TM:]
        project_out()

    @pl.when(i == 0)
    def _():
        run(1)

    @pl.when(i != 0)
    def _():
        run(n_key_tiles)


def _mixer(qT, k, vT, ctx_arr, x_arr, x_base, hf, hb, rest, sc, mods, w_out, g_ffn, tile_offset):
    bsz, t_all, _ = k.shape
    nt = t_all // TM
    nq = nt - tile_offset
    off = lambda b, i: (b, i + tile_offset, 0)
    return pl.pallas_call(
        functools.partial(_mixer_kernel, tile_offset=tile_offset, n_key_tiles=nt),
        grid=(bsz, nq),
        in_specs=[pl.BlockSpec((1, ATTN_WIDTH, TM), lambda b, i: (b, 0, i + tile_offset)),
                  pl.BlockSpec((1, t_all, KV_WIDTH), lambda b, i: (b, 0, 0)),
                  pl.BlockSpec((1, KV_WIDTH, t_all), lambda b, i: (b, 0, 0))]
                 + _token_specs(x_base, tile_offset) + [
                  pl.BlockSpec((1, TM, LRU_WIDTH), off),
                  pl.BlockSpec((1, TM, LRU_WIDTH), off),
                  pl.BlockSpec((1, TM, LRU_WIDTH), lambda b, i: (b, i + tile_offset, 1)),
                  pl.BlockSpec((1, TM, SC_WIDTH), off),
                  pl.BlockSpec((1, 1, MOD_ROWS, D_MODEL),
                               lambda b, i: (b, jnp.minimum(i + tile_offset, 1), 0, 0)),
                  _const_spec((D_MODEL, D_MODEL)),
                  _const_spec((1, D_MODEL))],
        out_specs=[pl.BlockSpec((1, TM, D_MODEL), lambda b, i: (b, i, 0)),
                   pl.BlockSpec((1, TM, D_MODEL), lambda b, i: (b, i, 0))],
        out_shape=[jax.ShapeDtypeStruct((bsz, nq * TM, D_MODEL), F32),
                   jax.ShapeDtypeStruct((bsz, nq * TM, D_MODEL), BF16)],
        scratch_shapes=[pltpu.VMEM((ATTN_WIDTH, TM), F32), pltpu.VMEM((SCORE_SLOTS, TM, 2 * TM), F32)],
        compiler_params=_params(40),
        name="mixer",
    )(qT, k, vT, ctx_arr, x_arr, hf, hb, rest, sc, mods, w_out, g_ffn)


def _ffn_kernel(h_ref, edge_ref, x1_ref, mod_ref, wup_ref, cw_ref, wdn_ref, gfin_ref,
                o_ref, acc_ref, up_ref, *, final_norm):
    hcat = h_ref[0]
    row = lax.broadcasted_iota(jnp.int32, (TM, FF_CHUNK), 0)

    def conv(t, w, cols):
        before = jnp.where(row == 0, edge_ref[0, 0, 0:1, cols], pltpu.roll(t, 1, 0))
        after = jnp.where(row == TM - 1, edge_ref[0, 0, 1:2, cols], pltpu.roll(t, TM - 1, 0))
        return w[0:1] * before + w[1:2] * t + w[2:3] * after

    n_chunks = D_FF // FF_CHUNK

    def up_project(c):
        cu = slice(c * FF_CHUNK, (c + 1) * FF_CHUNK)
        cg = slice(D_FF + c * FF_CHUNK, D_FF + (c + 1) * FF_CHUNK)
        up_ref[c % UP_SLOTS, :, :FF_CHUNK] = jnp.dot(hcat, wup_ref[:, cu], preferred_element_type=F32)
        up_ref[c % UP_SLOTS, :, FF_CHUNK:] = jnp.dot(hcat, wup_ref[:, cg], preferred_element_type=F32)

    for c in range(min(UP_LOOKAHEAD, n_chunks)):
        up_project(c)
    for c in range(n_chunks):
        if c + UP_LOOKAHEAD < n_chunks:
            up_project(c + UP_LOOKAHEAD)
        cu = slice(c * FF_CHUNK, (c + 1) * FF_CHUNK)
        cg = slice(D_FF + c * FF_CHUNK, D_FF + (c + 1) * FF_CHUNK)
        u = conv(up_ref[c % UP_SLOTS, :, :FF_CHUNK], cw_ref[:, cu], cu)
        g = conv(up_ref[c % UP_SLOTS, :, FF_CHUNK:], cw_ref[:, cg], cg)
        act = ((g * jax.nn.sigmoid(g)) * u).astype(BF16)
        part = jnp.dot(act, wdn_ref[cu, :], preferred_element_type=F32)
        if c == 0:
            acc_ref[...] = part
        else:
            acc_ref[...] += part

    gate_ffn = mod_ref[0, 0][5:6]
    x2 = x1_ref[0] + gate_ffn * acc_ref[...]
    if final_norm:
        x2 = x2 * _rms_scale(x2) * gfin_ref[...]
    o_ref[0] = x2


def _edge_up_kernel(h_ref, w_ref, o_ref):
    o_ref[...] = jnp.dot(h_ref[...], w_ref[...], preferred_element_type=F32)


def _edge_up(h2, w_up, ctx_tiles):
    bsz, t_len, _ = h2.shape
    nt = t_len // TM
    tiles = h2.reshape(bsz, nt, TM, D_MODEL)
    first, last = tiles[:, :, 0], tiles[:, :, TM - 1]
    t_idx = jnp.arange(nt)
    has_before = jnp.logical_and(t_idx != 0, t_idx != ctx_tiles)[None, :, None]
    has_after = jnp.logical_and(t_idx != nt - 1, t_idx != ctx_tiles - 1)[None, :, None]
    before = jnp.where(has_before, jnp.roll(last, 1, axis=1), 0)
    after = jnp.where(has_after, jnp.roll(first, -1, axis=1), 0)
    rows = jnp.stack([before, after], axis=2).reshape(bsz * nt * 2, D_MODEL)
    n_rows = rows.shape[0]
    padded = -(-n_rows // BF16_SUBLANES) * BF16_SUBLANES
    rows = jnp.pad(rows, ((0, padded - n_rows), (0, 0)))
    n_blk = 2 * D_FF // EDGE_COLS
    up = pl.pallas_call(
        _edge_up_kernel,
        grid=(n_blk,),
        in_specs=[pl.BlockSpec((padded, D_MODEL), lambda j: (0, 0)),
                  pl.BlockSpec((D_MODEL, EDGE_COLS), lambda j: (0, j))],
        out_specs=pl.BlockSpec((padded, EDGE_COLS), lambda j: (0, j)),
        out_shape=jax.ShapeDtypeStruct((padded, 2 * D_FF), F32),
        name="edge_up",
    )(rows, w_up)
    return up[:n_rows].reshape(bsz, nt, 2, 2 * D_FF)


def _ffn(h2, x1, mods, w_up, conv_w, w_down, g_final, ctx_tiles, final_norm):
    bsz, t_len, _ = h2.shape
    nt = t_len // TM
    edges = _edge_up(h2, w_up, ctx_tiles)
    return pl.pallas_call(
        functools.partial(_ffn_kernel, final_norm=final_norm),
        grid=(bsz, nt),
        in_specs=[pl.BlockSpec((1, TM, D_MODEL), lambda b, i: (b, i, 0)),
                  pl.BlockSpec((1, 1, 2, 2 * D_FF), lambda b, i: (b, i, 0, 0)),
                  pl.BlockSpec((1, TM, D_MODEL), lambda b, i: (b, i, 0)),
                  pl.BlockSpec((1, 1, MOD_ROWS, D_MODEL),
                               lambda b, i: (b, jnp.minimum(i + 1 - ctx_tiles, 1), 0, 0)),
                  _const_spec((D_MODEL, 2 * D_FF)),
                  _const_spec((3, 2 * D_FF)),
                  _const_spec((D_FF, D_MODEL)),
                  _const_spec((1, D_MODEL))],
        out_specs=pl.BlockSpec((1, TM, D_MODEL), lambda b, i: (b, i, 0)),
        out_shape=jax.ShapeDtypeStruct((bsz, t_len, D_MODEL), F32),
        scratch_shapes=[pltpu.VMEM((TM, D_MODEL), F32),
                        pltpu.VMEM((UP_SLOTS, TM, 2 * FF_CHUNK), F32)],
        compiler_params=_params(48),
        name="conv_ffn",
    )(h2, edges, x1, mods, w_up, conv_w, w_down, g_final)


def _rope_tables(seq, ctx_len):
    rows = seq // GRID_W
    pos_r = np.repeat(np.arange(rows, dtype=np.float32), GRID_W)
    pos_c = np.tile(np.arange(GRID_W, dtype=np.float32), rows)
    n_f = HEAD_DIM // 4
    inv = (np.float32(ROPE_THETA) ** (-np.arange(n_f, dtype=np.float32) / n_f)).astype(np.float32)
    ang = np.concatenate([pos_r[:, None] * inv, pos_c[:, None] * inv], axis=-1).astype(np.float32)
    cos = np.concatenate([np.ones((ctx_len, HEAD_DIM // 2)), np.cos(ang.astype(np.float64))], axis=0)
    sin = np.concatenate([np.zeros((ctx_len, HEAD_DIM // 2)), np.sin(ang.astype(np.float64))], axis=0)
    return jnp.asarray(cos.T, F32), jnp.asarray(sin.T, F32)


def _gate_weights(wa, ba, wi, bi):
    def dense(w):
        eye = jnp.eye(LRU_BLOCKS, dtype=w.dtype)
        return jnp.einsum("nde,nm->ndme", w, eye).reshape(LRU_WIDTH, LRU_WIDTH)
    w = jnp.concatenate([dense(wa[0]), dense(wi[0]), dense(wa[1]), dense(wi[1])], axis=1)
    b = jnp.concatenate([ba[0], bi[0], ba[1], bi[1]])[None, :]
    return w.astype(BF16), b


def kernel(x, c, ctx, c_ctx, w_mod, b_mod, g_mix, g_ffn, w_in, g_q, g_k, lru_conv_w, lru_conv_b,
           lru_wa, lru_ba, lru_wi, lru_bi, lru_lam, sc_conv_w, w_out, w_up, ffn_conv_w, w_down,
           g_final):
    bsz, seq, _ = x.shape
    ctx_len = ctx.shape[1]
    depth = w_mod.shape[0]
    assert ctx_len == TM and seq % TM == 0 and seq % GRID_W == 0

    cos_t, sin_t = _rope_tables(seq, ctx_len)
    ctx_arr, x_arr, x_base = ctx, x, 0
    cc_rows = -(-(bsz + 1) // F32_SUBLANES) * F32_SUBLANES
    cc = jnp.zeros((cc_rows, D_MODEL), F32).at[:bsz].set(c).at[bsz].set(c_ctx)

    mod_layers = _modulation(cc, w_mod, b_mod)

    out = None
    for l in range(depth):
        last = l == depth - 1
        mod_all = mod_layers[l].reshape(cc_rows, N_MOD, D_MODEL)
        mod_x = mod_all[:bsz]
        mod_c = jnp.broadcast_to(mod_all[bsz][None], (bsz, N_MOD, D_MODEL))
        mods = jnp.stack([mod_c, mod_x], axis=1)
        mods = jnp.pad(mods, ((0, 0), (0, 0), (0, MOD_ROWS - N_MOD), (0, 0)))

        gq = jnp.broadcast_to((g_q[l] * Q_SCALE)[:, None], (HEAD_DIM, TM))
        gk = jnp.broadcast_to(g_k[l][:, None], (HEAD_DIM, TM))
        qT, k, vT, rest = _inproj(ctx_arr, x_arr, x_base, mods, g_mix[l][None, :],
                                  w_in[l].astype(BF16), gq, gk, cos_t, sin_t)

        w_gate, b_gate = _gate_weights(lru_wa[l], lru_ba[l], lru_wi[l], lru_bi[l])
        hf, hb, sc = _lru(rest, lru_conv_w[l], lru_conv_b[l][None, :], w_gate, b_gate, lru_lam[l],
                          sc_conv_w[l])

        tile_offset = 1 if last else 0
        x1, h2 = _mixer(qT, k, vT, ctx_arr, x_arr, x_base, hf, hb, rest, sc, mods,
                        w_out[l].astype(BF16), g_ffn[l][None, :], tile_offset)
        out = _ffn(h2, x1, mods, w_up[l].astype(BF16), ffn_conv_w[l], w_down[l].astype(BF16),
                   g_final[None, :], ctx_tiles=1 - tile_offset, final_norm=last)
        ctx_arr, x_arr, x_base = out, out, 1
    return out
```

```python
import functools

import numpy as np
import jax
import jax.numpy as jnp
from jax import lax
from jax.experimental import pallas as pl
from jax.experimental.pallas import tpu as pltpu

F32 = jnp.float32
BF16 = jnp.bfloat16

D_MODEL = 1024
HEAD_DIM = 64
N_HEADS = 8
N_KV_HEADS = 2
HEADS_PER_KV = N_HEADS // N_KV_HEADS
ATTN_WIDTH = N_HEADS * HEAD_DIM
KV_WIDTH = N_KV_HEADS * HEAD_DIM
LRU_WIDTH = 256
LRU_BLOCKS = 4
LRU_BLOCK = LRU_WIDTH // LRU_BLOCKS
LRU_CONV = 4
LRU_C = 8.0
SC_WIDTH = 256
IN_WIDTH = 2048
D_FF = 2816
FF_CHUNK = 256
EDGE_COLS = 1408
GRID_W = 64
ROPE_THETA = 10000.0
EPS = 1e-6
Q_SCALE = HEAD_DIM ** -0.5 * 1.4426950408889634

TM = 256
SCORE_LOOKAHEAD = 3
SCORE_SLOTS = SCORE_LOOKAHEAD + 1
UP_LOOKAHEAD = 3
UP_SLOTS = UP_LOOKAHEAD + 1
LANES = 128
F32_SUBLANES = 8
BF16_SUBLANES = 16
N_MOD = 6
MOD_ROWS = 8

OFF_Q = 0
OFF_K = OFF_Q + ATTN_WIDTH
OFF_V = OFF_K + KV_WIDTH
OFF_R = OFF_V + KV_WIDTH
OFF_G = OFF_R + LRU_WIDTH
OFF_B = OFF_G + LRU_WIDTH
OFF_C = OFF_B + SC_WIDTH
OFF_U = OFF_C + SC_WIDTH


def _const_spec(shape):
    nd = len(shape)
    return pl.BlockSpec(shape, lambda *_: (0,) * nd, pipeline_mode=pl.Buffered(1))


def _params(vmem_mb, n_grid_dims=2):
    return pltpu.CompilerParams(dimension_semantics=("arbitrary",) * n_grid_dims,
                                vmem_limit_bytes=vmem_mb * 1024 * 1024)


def _rms_scale(t):
    return lax.rsqrt(jnp.mean(t * t, axis=-1, keepdims=True) + EPS)


def _mod_kernel(c_ref, w_ref, b_ref, o_ref):
    c = c_ref[...]
    h = (c * jax.nn.sigmoid(c)).astype(BF16)
    o_ref[...] = jnp.dot(h, w_ref[...].astype(BF16), preferred_element_type=F32) + b_ref[...]


def _modulation(cc, w_mod, b_mod):
    rows = cc.shape[0]
    depth = w_mod.shape[0]
    return pl.pallas_call(
        _mod_kernel,
        grid=(depth, N_MOD),
        in_specs=[pl.BlockSpec((rows, D_MODEL), lambda l, j: (0, 0)),
                  pl.BlockSpec((None, D_MODEL, D_MODEL), lambda l, j: (l, 0, j)),
                  pl.BlockSpec((None, 1, D_MODEL), lambda l, j: (l, 0, j))],
        out_specs=pl.BlockSpec((None, rows, D_MODEL), lambda l, j: (l, 0, j)),
        out_shape=jax.ShapeDtypeStruct((depth, rows, N_MOD * D_MODEL), F32),
        name="modulation",
    )(cc, w_mod, b_mod[:, None, :])


def _norm_rope_transposed(t, gain, cos, sin):
    half = HEAD_DIM // 2
    out = []
    for h in range(t.shape[0] // HEAD_DIM):
        r = t[h * HEAD_DIM:(h + 1) * HEAD_DIM]
        r = r * lax.rsqrt(jnp.mean(r * r, axis=0, keepdims=True) + EPS) * gain
        x1, x2 = r[:half], r[half:]
        out += [x1 * cos - x2 * sin, x1 * sin + x2 * cos]
    return jnp.concatenate(out, axis=0)


def _inproj_kernel(ctx_ref, x_ref, mod_ref, gmix_ref, w_ref, gq_ref, gk_ref, cos_ref, sin_ref,
                   qT_ref, k_ref, vT_ref, rest_ref, y_even_ref, y_odd_ref, *, nt, n_tiles):
    s = pl.program_id(0)
    tile = jnp.minimum(s, n_tiles - 1) % nt

    def project(y_ref):
        x = jnp.where(tile == 0, ctx_ref[0], x_ref[0])
        mod = mod_ref[0, 0]
        shift, scale = mod[0:1], mod[1:2]
        h = x * _rms_scale(x) * gmix_ref[...]
        h = h * (1.0 + scale) + shift
        y_ref[...] = jnp.dot(h.astype(BF16), w_ref[...], preferred_element_type=F32)

    def finish(y_ref):
        cos, sin = cos_ref[...], sin_ref[...]
        q = _norm_rope_transposed(y_ref[:, OFF_Q:OFF_Q + ATTN_WIDTH].T, gq_ref[...], cos, sin)
        qT_ref[0] = q.astype(BF16)
        k = _norm_rope_transposed(y_ref[:, OFF_K:OFF_K + KV_WIDTH].T, gk_ref[...], cos, sin)
        k_ref[0] = k.T.astype(BF16)
        vT_ref[0] = y_ref[:, OFF_V:OFF_V + KV_WIDTH].T.astype(BF16)
        rest_ref[0, :, 0:LRU_WIDTH] = y_ref[:, OFF_R:OFF_R + LRU_WIDTH]
        rest_ref[0, :, LRU_WIDTH:2 * LRU_WIDTH] = jax.nn.gelu(y_ref[:, OFF_G:OFF_G + LRU_WIDTH])
        rest_ref[0, :, 2 * LRU_WIDTH:2 * LRU_WIDTH + SC_WIDTH] = y_ref[:, OFF_B:OFF_B + SC_WIDTH]
        rest_ref[0, :, 2 * LRU_WIDTH + SC_WIDTH:] = (y_ref[:, OFF_C:OFF_C + SC_WIDTH]
                                                     * y_ref[:, OFF_U:OFF_U + SC_WIDTH])

    @pl.when(s == 0)
    def _():
        y_odd_ref[...] = jnp.zeros_like(y_odd_ref)

    @pl.when(s % 2 == 0)
    def _():
        finish(y_odd_ref)
        project(y_even_ref)

    @pl.when(s % 2 == 1)
    def _():
        finish(y_even_ref)
        project(y_odd_ref)


def _token_specs(x_base, tile_offset=0):
    return [pl.BlockSpec((1, TM, D_MODEL), lambda b, i: (b, 0, 0)),
            pl.BlockSpec((1, TM, D_MODEL),
                         lambda b, i: (b, x_base + jnp.maximum(i + tile_offset - 1, 0), 0))]


def _inproj(ctx_arr, x_arr, x_base, mods, g_mix, w_in, gq, gk, cos_t, sin_t):
    bsz = x_arr.shape[0]
    nt = x_arr.shape[1] // TM - x_base + 1
    t_all = nt * TM
    n_tiles = bsz * nt
    rest_w = 2 * LRU_WIDTH + 2 * SC_WIDTH

    def cur(s):
        sc = jnp.minimum(s, n_tiles - 1)
        return sc // nt, sc % nt

    def prev(s):
        sp = jnp.maximum(s - 1, 0)
        return sp // nt, sp % nt

    rope_spec = pl.BlockSpec((HEAD_DIM // 2, TM), lambda s: (0, prev(s)[1]))
    return pl.pallas_call(
        functools.partial(_inproj_kernel, nt=nt, n_tiles=n_tiles),
        grid=(n_tiles + 1,),
        in_specs=[pl.BlockSpec((1, TM, D_MODEL), lambda s: (cur(s)[0], 0, 0)),
                  pl.BlockSpec((1, TM, D_MODEL),
                               lambda s: (cur(s)[0], x_base + jnp.maximum(cur(s)[1] - 1, 0), 0)),
                  pl.BlockSpec((1, 1, MOD_ROWS, D_MODEL),
                               lambda s: (cur(s)[0], jnp.minimum(cur(s)[1], 1), 0, 0)),
                  _const_spec((1, D_MODEL)),
                  _const_spec((D_MODEL, IN_WIDTH)),
                  _const_spec((HEAD_DIM, TM)),
                  _const_spec((HEAD_DIM, TM)),
                  rope_spec,
                  rope_spec],
        out_specs=[pl.BlockSpec((1, ATTN_WIDTH, TM), lambda s: (prev(s)[0], 0, prev(s)[1])),
                   pl.BlockSpec((1, TM, KV_WIDTH), lambda s: (prev(s)[0], prev(s)[1], 0)),
                   pl.BlockSpec((1, KV_WIDTH, TM), lambda s: (prev(s)[0], 0, prev(s)[1])),
                   pl.BlockSpec((1, TM, rest_w), lambda s: (prev(s)[0], prev(s)[1], 0))],
        out_shape=[jax.ShapeDtypeStruct((bsz, ATTN_WIDTH, t_all), BF16),
                   jax.ShapeDtypeStruct((bsz, t_all, KV_WIDTH), BF16),
                   jax.ShapeDtypeStruct((bsz, KV_WIDTH, t_all), BF16),
                   jax.ShapeDtypeStruct((bsz, t_all, rest_w), F32)],
        scratch_shapes=[pltpu.VMEM((TM, IN_WIDTH), F32)] * 2,
        compiler_params=_params(40, n_grid_dims=1),
        name="inproj",
    )(ctx_arr, x_arr, mods, g_mix, w_in, gq, gk, cos_t, sin_t)


def _with_halo(prev_ref, cur_ref, next_ref, valid_prev, valid_next):
    prev = prev_ref[0] * valid_prev.astype(F32)
    nxt = next_ref[0] * valid_next.astype(F32)
    return jnp.concatenate([prev, cur_ref[0], nxt], axis=0)


def _shifted(tcat, offset):
    rows = tcat.shape[0]
    if offset == 0:
        return tcat[F32_SUBLANES:F32_SUBLANES + TM]
    return pltpu.roll(tcat, (-offset) % rows, 0)[F32_SUBLANES:F32_SUBLANES + TM]


def _lru_coeffs(rcat, conv_w, conv_b, w_gate, b_gate, sp_lam):
    xc = conv_b
    for j in range(LRU_CONV):
        xc = xc + conv_w[j:j + 1] * _shifted(rcat, j - 2)
    z = jnp.dot(xc.astype(BF16), w_gate, preferred_element_type=F32) + b_gate
    r_gate = jax.nn.sigmoid(z[:, :LRU_WIDTH])
    i_gate = jax.nn.sigmoid(z[:, LRU_WIDTH:])
    log_a = (-LRU_C) * r_gate * sp_lam
    a = jnp.exp(log_a)
    one_minus_a2 = -jnp.tanh(log_a) * (1.0 + a * a)
    return a, jnp.sqrt(one_minus_a2) * (i_gate * xc)


def _lru_kernel(rf_ref, rfp_ref, rfn_ref, rb_ref, rbp_ref, rbn_ref, bg_ref, cu_ref, cup_ref, cun_ref,
                cw_ref, cb_ref, wg_ref, bgate_ref, lam_ref, scw_ref,
                hf_ref, hb_ref, sc_ref,
                af_s, bf_s, ab_s, bb_s, hf_state, hb_state, *, nt):
    i = pl.program_id(1)
    j = jnp.where(i == 0, 0, nt - i)

    @pl.when(i == 0)
    def _():
        hf_state[...] = jnp.zeros_like(hf_state)
        hb_state[...] = jnp.zeros_like(hb_state)

    def halo_valid(t):
        return jnp.logical_and(t != 0, t != 1), jnp.logical_and(t != 0, t != nt - 1)

    lam = lam_ref[...]
    sp_lam = jnp.maximum(-lam, 0.0) + jnp.log1p(jnp.exp(-jnp.abs(lam)))
    cw = cw_ref[...]
    cb = cb_ref[...]

    vp, vn = halo_valid(i)
    a, b = _lru_coeffs(_with_halo(rfp_ref, rf_ref, rfn_ref, vp, vn), cw, cb,
                       wg_ref[:, :2 * LRU_WIDTH], bgate_ref[:, :2 * LRU_WIDTH], sp_lam[0:1])
    _block_prefix(a, b, af_s, bf_s, reverse=False)
    cucat = _with_halo(cup_ref, cu_ref, cun_ref, vp, vn)
    scw = scw_ref[...]
    conv = scw[0:1] * _shifted(cucat, -1) + scw[1:2] * _shifted(cucat, 0) + scw[2:3] * _shifted(cucat, 1)
    sc_ref[0] = (bg_ref[0] * conv).astype(BF16)

    vp, vn = halo_valid(j)
    a, b = _lru_coeffs(_with_halo(rbp_ref, rb_ref, rbn_ref, vp, vn), cw, cb,
                       wg_ref[:, 2 * LRU_WIDTH:], bgate_ref[:, 2 * LRU_WIDTH:], sp_lam[1:2])
    _block_prefix(a, b, ab_s, bb_s, reverse=True)

    _carry_blocks(af_s, bf_s, hf_ref, hf_state, reverse=False)
    _carry_blocks(ab_s, bb_s, hb_ref, hb_state, reverse=True)


def _block_prefix(a, b, a_s, b_s, reverse):
    n = F32_SUBLANES
    a = a.reshape(TM // n, n, LRU_WIDTH)
    b = b.reshape(TM // n, n, LRU_WIDTH)
    row = lax.broadcasted_iota(jnp.int32, a.shape, 1)
    d = 1
    while d < n:
        shift, keep = (n - d, row < n - d) if reverse else (d, row >= d)
        a_prev = jnp.where(keep, pltpu.roll(a, shift, 1), 1.0)
        b_prev = jnp.where(keep, pltpu.roll(b, shift, 1), 0.0)
        b = a * b_prev + b
        a = a * a_prev
        d *= 2
    a_s[...] = a
    b_s[...] = b


def _carry_blocks(a_s, b_s, out_ref, state_ref, reverse):
    n = F32_SUBLANES
    nb = TM // n
    last = 0 if reverse else n - 1
    h_in = state_ref[...]
    for v in (reversed(range(nb)) if reverse else range(nb)):
        h = a_s[v] * h_in + b_s[v]
        out_ref[0, v * n:(v + 1) * n, :] = h
        h_in = h[last:last + 1]
    state_ref[...] = h_in


def _lru(rest, conv_w, conv_b, w_gate, b_gate, lam, sc_w):
    bsz, t_all, _ = rest.shape
    nt = t_all // TM
    hpt = TM // F32_SUBLANES
    nhb = t_all // F32_SUBLANES

    def rev(i):
        return jnp.where(i == 0, 0, nt - i)

    def tile(col, order):
        return pl.BlockSpec((1, TM, LRU_WIDTH), lambda b, i: (b, order(i), col))

    def prev_halo(col, order):
        return pl.BlockSpec((1, F32_SUBLANES, LRU_WIDTH),
                            lambda b, i: (b, jnp.maximum(order(i) * hpt - 1, 0), col))

    def next_halo(col, order):
        return pl.BlockSpec((1, F32_SUBLANES, LRU_WIDTH),
                            lambda b, i: (b, jnp.minimum((order(i) + 1) * hpt, nhb - 1), col))

    fwd = lambda i: i
    out_f32 = jax.ShapeDtypeStruct((bsz, t_all, LRU_WIDTH), F32)
    return pl.pallas_call(
        functools.partial(_lru_kernel, nt=nt),
        grid=(bsz, nt),
        in_specs=[tile(0, fwd), prev_halo(0, fwd), next_halo(0, fwd),
                  tile(0, rev), prev_halo(0, rev), next_halo(0, rev),
                  tile(2, fwd),
                  tile(3, fwd), prev_halo(3, fwd), next_halo(3, fwd),
                  _const_spec((LRU_CONV, LRU_WIDTH)),
                  _const_spec((1, LRU_WIDTH)),
                  _const_spec((LRU_WIDTH, 4 * LRU_WIDTH)),
                  _const_spec((1, 4 * LRU_WIDTH)),
                  _const_spec((2, LRU_WIDTH)),
                  _const_spec((3, SC_WIDTH))],
        out_specs=[pl.BlockSpec((1, TM, LRU_WIDTH), lambda b, i: (b, i, 0)),
                   pl.BlockSpec((1, TM, LRU_WIDTH), lambda b, i: (b, rev(i), 0)),
                   pl.BlockSpec((1, TM, SC_WIDTH), lambda b, i: (b, i, 0))],
        out_shape=[out_f32, out_f32, jax.ShapeDtypeStruct((bsz, t_all, SC_WIDTH), BF16)],
        scratch_shapes=([pltpu.VMEM((TM // F32_SUBLANES, F32_SUBLANES, LRU_WIDTH), F32)] * 4
                        + [pltpu.VMEM((1, LRU_WIDTH), F32)] * 2),
        compiler_params=_params(32),
        name="lru_scan",
    )(rest, rest, rest, rest, rest, rest, rest, rest, rest, rest,
      conv_w, conv_b, w_gate, b_gate, lam, sc_w)


def _mixer_kernel(qT_ref, k_ref, vT_ref, ctx_ref, x_ref, hf_ref, hb_ref, gg_ref, sc_ref, mod_ref, wo_ref,
                  gffn_ref, x1_ref, h2_ref, oT_ref, s_ref, *, tile_offset, n_key_tiles):
    i = pl.program_id(1) + tile_offset
    ones_rows = jnp.ones((BF16_SUBLANES, TM), BF16)

    def project_out():
        x = x_ref[0]
        if tile_offset == 0:
            x = jnp.where(i == 0, ctx_ref[0], x)
        mod = mod_ref[0, 0]
        gate_mix, shift, scale = mod[2:3], mod[3:4], mod[4:5]
        att = oT_ref[...].T.astype(BF16)
        lru = (gg_ref[0] * (hf_ref[0] + hb_ref[0])).astype(BF16)
        y = (jnp.dot(att, wo_ref[0:ATTN_WIDTH], preferred_element_type=F32)
             + jnp.dot(lru, wo_ref[ATTN_WIDTH:ATTN_WIDTH + LRU_WIDTH], preferred_element_type=F32)
             + jnp.dot(sc_ref[0], wo_ref[ATTN_WIDTH + LRU_WIDTH:], preferred_element_type=F32))
        x1 = x + gate_mix * y
        x1_ref[0] = x1
        h2 = x1 * _rms_scale(x1) * gffn_ref[...]
        h2_ref[0] = (h2 * (1.0 + scale) + shift).astype(BF16)

    def run(n_tiles):
        items = [(pr, c) for pr in range(N_HEADS // 2) for c in range(n_tiles)]
        qpads = {}

        def qpad_of(pr):
            if pr not in qpads:
                row0 = pr * 2 * HEAD_DIM
                q2 = jnp.concatenate([qT_ref[0, row0:row0 + HEAD_DIM, :],
                                      qT_ref[0, row0 + HEAD_DIM:row0 + 2 * HEAD_DIM, :]], axis=1)
                zero = jnp.zeros_like(q2)
                first_kv = (2 * pr) // HEADS_PER_KV == 0
                qpads[pr] = jnp.concatenate([q2, zero] if first_kv else [zero, q2], axis=0)
            return qpads[pr]

        def scores(t):
            pr, c = items[t]
            s = jnp.dot(k_ref[0, c * TM:(c + 1) * TM, :], qpad_of(pr), preferred_element_type=F32)
            s_ref[t % SCORE_SLOTS] = s
            return jnp.max(s, axis=0, keepdims=True)

        tile_max = [scores(t) for t in range(min(SCORE_LOOKAHEAD, len(items)))]
        m = acc = None
        for t, (pr, c) in enumerate(items):
            if t + SCORE_LOOKAHEAD < len(items):
                tile_max.append(scores(t + SCORE_LOOKAHEAD))
            if c == 0:
                m = jnp.full((1, 2 * TM), -jnp.inf, F32)
                acc = jnp.zeros((HEAD_DIM + BF16_SUBLANES, 2 * TM), F32)
            g = (2 * pr) // HEADS_PER_KV
            m_new = jnp.maximum(m, tile_max[t])
            alpha = jnp.exp2(m - m_new)
            p = jnp.exp2(s_ref[t % SCORE_SLOTS] - m_new).astype(BF16)
            vt = jnp.concatenate([vT_ref[0, g * HEAD_DIM:(g + 1) * HEAD_DIM, c * TM:(c + 1) * TM],
                                  ones_rows], axis=0)
            acc = alpha * acc + jnp.dot(vt, p, preferred_element_type=F32)
            m = m_new
            tile_max[t] = None
            if c == n_tiles - 1:
                o = acc[:HEAD_DIM] / acc[HEAD_DIM:HEAD_DIM + 1]
                row0 = pr * 2 * HEAD_DIM
                oT_ref[row0:row0 + HEAD_DIM, :] = o[:, :TM]
                oT_ref[row0 + HEAD_DIM:row0 + 2 * HEAD_DIM, :] = o[:, TM:]
        project_out()

    @pl.when(i == 0)
    def _():
        run(1)

    @pl.when(i != 0)
    def _():
        run(n_key_tiles)


def _mixer(qT, k, vT, ctx_arr, x_arr, x_base, hf, hb, rest, sc, mods, w_out, g_ffn, tile_offset):
    bsz, t_all, _ = k.shape
    nt = t_all // TM
    nq = nt - tile_offset
    off = lambda b, i: (b, i + tile_offset, 0)
    return pl.pallas_call(
        functools.partial(_mixer_kernel, tile_offset=tile_offset, n_key_tiles=nt),
        grid=(bsz, nq),
        in_specs=[pl.BlockSpec((1, ATTN_WIDTH, TM), lambda b, i: (b, 0, i + tile_offset)),
                  pl.BlockSpec((1, t_all, KV_WIDTH), lambda b, i: (b, 0, 0)),
                  pl.BlockSpec((1, KV_WIDTH, t_all), lambda b, i: (b, 0, 0))]
                 + _token_specs(x_base, tile_offset) + [
                  pl.BlockSpec((1, TM, LRU_WIDTH), off),
                  pl.BlockSpec((1, TM, LRU_WIDTH), off),
                  pl.BlockSpec((1, TM, LRU_WIDTH), lambda b, i: (b, i + tile_offset, 1)),
                  pl.BlockSpec((1, TM, SC_WIDTH), off),
                  pl.BlockSpec((1, 1, MOD_ROWS, D_MODEL),
                               lambda b, i: (b, jnp.minimum(i + tile_offset, 1), 0, 0)),
                  _const_spec((D_MODEL, D_MODEL)),
                  _const_spec((1, D_MODEL))],
        out_specs=[pl.BlockSpec((1, TM, D_MODEL), lambda b, i: (b, i, 0)),
                   pl.BlockSpec((1, TM, D_MODEL), lambda b, i: (b, i, 0))],
        out_shape=[jax.ShapeDtypeStruct((bsz, nq * TM, D_MODEL), F32),
                   jax.ShapeDtypeStruct((bsz, nq * TM, D_MODEL), BF16)],
        scratch_shapes=[pltpu.VMEM((ATTN_WIDTH, TM), F32), pltpu.VMEM((SCORE_SLOTS, TM, 2 * TM), F32)],
        compiler_params=_params(40),
        name="mixer",
    )(qT, k, vT, ctx_arr, x_arr, hf, hb, rest, sc, mods, w_out, g_ffn)


def _ffn_kernel(h_ref, edge_ref, x1_ref, mod_ref, wup_ref, cw_ref, wdn_ref, gfin_ref,
                o_ref, acc_ref, up_ref, *, final_norm):
    hcat = h_ref[0]
    row = lax.broadcasted_iota(jnp.int32, (TM, FF_CHUNK), 0)

    def conv(t, w, cols):
        before = jnp.where(row == 0, edge_ref[0, 0, 0:1, cols], pltpu.roll(t, 1, 0))
        after = jnp.where(row == TM - 1, edge_ref[0, 0, 1:2, cols], pltpu.roll(t, TM - 1, 0))
        return w[0:1] * before + w[1:2] * t + w[2:3] * after

    n_chunks = D_FF // FF_CHUNK

    def up_project(c):
        cu = slice(c * FF_CHUNK, (c + 1) * FF_CHUNK)
        cg = slice(D_FF + c * FF_CHUNK, D_FF + (c + 1) * FF_CHUNK)
        up_ref[c % UP_SLOTS, :, :FF_CHUNK] = jnp.dot(hcat, wup_ref[:, cu], preferred_element_type=F32)
        up_ref[c % UP_SLOTS, :, FF_CHUNK:] = jnp.dot(hcat, wup_ref[:, cg], preferred_element_type=F32)

    for c in range(min(UP_LOOKAHEAD, n_chunks)):
        up_project(c)
    for c in range(n_chunks):
        if c + UP_LOOKAHEAD < n_chunks:
            up_project(c + UP_LOOKAHEAD)
        cu = slice(c * FF_CHUNK, (c + 1) * FF_CHUNK)
        cg = slice(D_FF + c * FF_CHUNK, D_FF + (c + 1) * FF_CHUNK)
        u = conv(up_ref[c % UP_SLOTS, :, :FF_CHUNK], cw_ref[:, cu], cu)
        g = conv(up_ref[c % UP_SLOTS, :, FF_CHUNK:], cw_ref[:, cg], cg)
        act = ((g * jax.nn.sigmoid(g)) * u).astype(BF16)
        part = jnp.dot(act, wdn_ref[cu, :], preferred_element_type=F32)
        if c == 0:
            acc_ref[...] = part
        elif c < n_chunks - 1:
            acc_ref[...] += part

    gate_ffn = mod_ref[0, 0][5:6]
    x2 = x1_ref[0] + gate_ffn * (acc_ref[...] + part)
    if final_norm:
        x2 = x2 * _rms_scale(x2) * gfin_ref[...]
    o_ref[0] = x2


def _edge_up_kernel(h_ref, w_ref, o_ref):
    o_ref[...] = jnp.dot(h_ref[...], w_ref[...], preferred_element_type=F32)


def _edge_up(h2, w_up, ctx_tiles):
    bsz, t_len, _ = h2.shape
    nt = t_len // TM
    first, last = h2[:, 0::TM], h2[:, TM - 1::TM]
    t_idx = jnp.arange(nt)
    has_before = jnp.logical_and(t_idx != 0, t_idx != ctx_tiles)[None, :, None]
    has_after = jnp.logical_and(t_idx != nt - 1, t_idx != ctx_tiles - 1)[None, :, None]
    before = jnp.where(has_before, jnp.roll(last, 1, axis=1), 0)
    after = jnp.where(has_after, jnp.roll(first, -1, axis=1), 0)
    rows = jnp.stack([before, after], axis=2).reshape(bsz * nt * 2, D_MODEL)
    n_rows = rows.shape[0]
    padded = -(-n_rows // BF16_SUBLANES) * BF16_SUBLANES
    rows = jnp.pad(rows, ((0, padded - n_rows), (0, 0)))
    n_blk = 2 * D_FF // EDGE_COLS
    up = pl.pallas_call(
        _edge_up_kernel,
        grid=(n_blk,),
        in_specs=[pl.BlockSpec((padded, D_MODEL), lambda j: (0, 0)),
                  pl.BlockSpec((D_MODEL, EDGE_COLS), lambda j: (0, j))],
        out_specs=pl.BlockSpec((padded, EDGE_COLS), lambda j: (0, j)),
        out_shape=jax.ShapeDtypeStruct((padded, 2 * D_FF), F32),
        name="edge_up",
    )(rows, w_up)
    return up[:n_rows].reshape(bsz, nt, 2, 2 * D_FF)


def _ffn(h2, x1, mods, w_up, conv_w, w_down, g_final, ctx_tiles, final_norm):
    bsz, t_len, _ = h2.shape
    nt = t_len // TM
    edges = _edge_up(h2, w_up, ctx_tiles)
    return pl.pallas_call(
        functools.partial(_ffn_kernel, final_norm=final_norm),
        grid=(bsz, nt),
        in_specs=[pl.BlockSpec((1, TM, D_MODEL), lambda b, i: (b, i, 0)),
                  pl.BlockSpec((1, 1, 2, 2 * D_FF), lambda b, i: (b, i, 0, 0)),
                  pl.BlockSpec((1, TM, D_MODEL), lambda b, i: (b, i, 0)),
                  pl.BlockSpec((1, 1, MOD_ROWS, D_MODEL),
                               lambda b, i: (b, jnp.minimum(i + 1 - ctx_tiles, 1), 0, 0)),
                  _const_spec((D_MODEL, 2 * D_FF)),
                  _const_spec((3, 2 * D_FF)),
                  _const_spec((D_FF, D_MODEL)),
                  _const_spec((1, D_MODEL))],
        out_specs=pl.BlockSpec((1, TM, D_MODEL), lambda b, i: (b, i, 0)),
        out_shape=jax.ShapeDtypeStruct((bsz, t_len, D_MODEL), F32),
        scratch_shapes=[pltpu.VMEM((TM, D_MODEL), F32),
                        pltpu.VMEM((UP_SLOTS, TM, 2 * FF_CHUNK), F32)],
        compiler_params=_params(48),
        name="conv_ffn",
    )(h2, edges, x1, mods, w_up, conv_w, w_down, g_final)


def _rope_tables(seq, ctx_len):
    rows = seq // GRID_W
    pos_r = np.repeat(np.arange(rows, dtype=np.float32), GRID_W)
    pos_c = np.tile(np.arange(GRID_W, dtype=np.float32), rows)
    n_f = HEAD_DIM // 4
    inv = (np.float32(ROPE_THETA) ** (-np.arange(n_f, dtype=np.float32) / n_f)).astype(np.float32)
    ang = np.concatenate([pos_r[:, None] * inv, pos_c[:, None] * inv], axis=-1).astype(np.float32)
    cos = np.concatenate([np.ones((ctx_len, HEAD_DIM // 2)), np.cos(ang.astype(np.float64))], axis=0)
    sin = np.concatenate([np.zeros((ctx_len, HEAD_DIM // 2)), np.sin(ang.astype(np.float64))], axis=0)
    return jnp.asarray(cos.T, F32), jnp.asarray(sin.T, F32)


def _gate_weights(wa, ba, wi, bi):
    def dense(w):
        eye = jnp.eye(LRU_BLOCKS, dtype=w.dtype)
        return jnp.einsum("nde,nm->ndme", w, eye).reshape(LRU_WIDTH, LRU_WIDTH)
    w = jnp.concatenate([dense(wa[0]), dense(wi[0]), dense(wa[1]), dense(wi[1])], axis=1)
    b = jnp.concatenate([ba[0], bi[0], ba[1], bi[1]])[None, :]
    return w.astype(BF16), b


def kernel(x, c, ctx, c_ctx, w_mod, b_mod, g_mix, g_ffn, w_in, g_q, g_k, lru_conv_w, lru_conv_b,
           lru_wa, lru_ba, lru_wi, lru_bi, lru_lam, sc_conv_w, w_out, w_up, ffn_conv_w, w_down,
           g_final):
    bsz, seq, _ = x.shape
    ctx_len = ctx.shape[1]
    depth = w_mod.shape[0]
    assert ctx_len == TM and seq % TM == 0 and seq % GRID_W == 0

    cos_t, sin_t = _rope_tables(seq, ctx_len)
    ctx_arr, x_arr, x_base = ctx, x, 0
    cc_rows = -(-(bsz + 1) // F32_SUBLANES) * F32_SUBLANES
    cc = jnp.zeros((cc_rows, D_MODEL), F32).at[:bsz].set(c).at[bsz].set(c_ctx)

    mod_layers = _modulation(cc, w_mod, b_mod)

    out = None
    for l in range(depth):
        last = l == depth - 1
        mod_all = mod_layers[l].reshape(cc_rows, N_MOD, D_MODEL)
        mod_x = mod_all[:bsz]
        mod_c = jnp.broadcast_to(mod_all[bsz][None], (bsz, N_MOD, D_MODEL))
        mods = jnp.stack([mod_c, mod_x], axis=1)
        mods = jnp.pad(mods, ((0, 0), (0, 0), (0, MOD_ROWS - N_MOD), (0, 0)))

        gq = jnp.broadcast_to((g_q[l] * Q_SCALE)[:, None], (HEAD_DIM, TM))
        gk = jnp.broadcast_to(g_k[l][:, None], (HEAD_DIM, TM))
        qT, k, vT, rest = _inproj(ctx_arr, x_arr, x_base, mods, g_mix[l][None, :],
                                  w_in[l].astype(BF16), gq, gk, cos_t, sin_t)

        w_gate, b_gate = _gate_weights(lru_wa[l], lru_ba[l], lru_wi[l], lru_bi[l])
        hf, hb, sc = _lru(rest, lru_conv_w[l], lru_conv_b[l][None, :], w_gate, b_gate, lru_lam[l],
                          sc_conv_w[l])

        tile_offset = 1 if last else 0
        x1, h2 = _mixer(qT, k, vT, ctx_arr, x_arr, x_base, hf, hb, rest, sc, mods,
                        w_out[l].astype(BF16), g_ffn[l][None, :], tile_offset)
        out = _ffn(h2, x1, mods, w_up[l].astype(BF16), ffn_conv_w[l], w_down[l].astype(BF16),
                   g_final[None, :], ctx_tiles=1 - tile_offset, final_norm=last)
        ctx_arr, x_arr, x_base = out, out, 1
    return out
```

```python
import functools

import numpy as np
import jax
import jax.numpy as jnp
from jax import lax
from jax.experimental import pallas as pl
from jax.experimental.pallas import tpu as pltpu

F32 = jnp.float32
BF16 = jnp.bfloat16

D_MODEL = 1024
HEAD_DIM = 64
N_HEADS = 8
N_KV_HEADS = 2
HEADS_PER_KV = N_HEADS // N_KV_HEADS
ATTN_WIDTH = N_HEADS * HEAD_DIM
KV_WIDTH = N_KV_HEADS * HEAD_DIM
LRU_WIDTH = 256
LRU_BLOCKS = 4
LRU_BLOCK = LRU_WIDTH // LRU_BLOCKS
LRU_CONV = 4
LRU_C = 8.0
SC_WIDTH = 256
IN_WIDTH = 2048
D_FF = 2816
FF_CHUNK = 256
EDGE_COLS = 1408
GRID_W = 64
ROPE_THETA = 10000.0
EPS = 1e-6
Q_SCALE = HEAD_DIM ** -0.5 * 1.4426950408889634

TM = 256
SCORE_LOOKAHEAD = 3
SCORE_SLOTS = SCORE_LOOKAHEAD + 1
UP_LOOKAHEAD = 3
UP_SLOTS = UP_LOOKAHEAD + 1
LANES = 128
F32_SUBLANES = 8
BF16_SUBLANES = 16
N_MOD = 6
MOD_ROWS = 8

OFF_Q = 0
OFF_K = OFF_Q + ATTN_WIDTH
OFF_V = OFF_K + KV_WIDTH
OFF_R = OFF_V + KV_WIDTH
OFF_G = OFF_R + LRU_WIDTH
OFF_B = OFF_G + LRU_WIDTH
OFF_C = OFF_B + SC_WIDTH
OFF_U = OFF_C + SC_WIDTH


def _const_spec(shape):
    nd = len(shape)
    return pl.BlockSpec(shape, lambda *_: (0,) * nd, pipeline_mode=pl.Buffered(1))


def _params(vmem_mb, n_grid_dims=2):
    return pltpu.CompilerParams(dimension_semantics=("arbitrary",) * n_grid_dims,
                                vmem_limit_bytes=vmem_mb * 1024 * 1024)


def _rms_scale(t):
    return lax.rsqrt(jnp.mean(t * t, axis=-1, keepdims=True) + EPS)


def _mod_kernel(c_ref, w_ref, b_ref, o_ref):
    c = c_ref[...]
    h = (c * jax.nn.sigmoid(c)).astype(BF16)
    o_ref[...] = jnp.dot(h, w_ref[...].astype(BF16), preferred_element_type=F32) + b_ref[...]


def _modulation(cc, w_mod, b_mod):
    rows = cc.shape[0]
    depth = w_mod.shape[0]
    return pl.pallas_call(
        _mod_kernel,
        grid=(depth, N_MOD),
        in_specs=[pl.BlockSpec((rows, D_MODEL), lambda l, j: (0, 0)),
                  pl.BlockSpec((None, D_MODEL, D_MODEL), lambda l, j: (l, 0, j)),
                  pl.BlockSpec((None, 1, D_MODEL), lambda l, j: (l, 0, j))],
        out_specs=pl.BlockSpec((None, rows, D_MODEL), lambda l, j: (l, 0, j)),
        out_shape=jax.ShapeDtypeStruct((depth, rows, N_MOD * D_MODEL), F32),
        name="modulation",
    )(cc, w_mod, b_mod[:, None, :])


def _norm_rope_transposed(t, gain, cos, sin):
    half = HEAD_DIM // 2
    out = []
    for h in range(t.shape[0] // HEAD_DIM):
        r = t[h * HEAD_DIM:(h + 1) * HEAD_DIM]
        r = r * lax.rsqrt(jnp.mean(r * r, axis=0, keepdims=True) + EPS) * gain
        x1, x2 = r[:half], r[half:]
        out += [x1 * cos - x2 * sin, x1 * sin + x2 * cos]
    return jnp.concatenate(out, axis=0)


def _inproj_kernel(ctx_ref, x_ref, mod_ref, gmix_ref, w_ref, gq_ref, gk_ref, cos_ref, sin_ref,
                   qT_ref, k_ref, vT_ref, rest_ref, y_even_ref, y_odd_ref, *, nt, n_tiles):
    s = pl.program_id(0)
    tile = jnp.minimum(s, n_tiles - 1) % nt

    def project(y_ref):
        x = jnp.where(tile == 0, ctx_ref[0], x_ref[0])
        mod = mod_ref[0, 0]
        shift, scale = mod[0:1], mod[1:2]
        h = x * _rms_scale(x) * gmix_ref[...]
        h = h * (1.0 + scale) + shift
        y_ref[...] = jnp.dot(h.astype(BF16), w_ref[...], preferred_element_type=F32)

    def finish(y_ref):
        cos, sin = cos_ref[...], sin_ref[...]
        q = _norm_rope_transposed(y_ref[:, OFF_Q:OFF_Q + ATTN_WIDTH].T, gq_ref[...], cos, sin)
        qT_ref[0] = q.astype(BF16)
        k = _norm_rope_transposed(y_ref[:, OFF_K:OFF_K + KV_WIDTH].T, gk_ref[...], cos, sin)
        k_ref[0] = k.T.astype(BF16)
        vT_ref[0] = y_ref[:, OFF_V:OFF_V + KV_WIDTH].T.astype(BF16)
        rest_ref[0, :, 0:LRU_WIDTH] = y_ref[:, OFF_R:OFF_R + LRU_WIDTH]
        rest_ref[0, :, LRU_WIDTH:2 * LRU_WIDTH] = jax.nn.gelu(y_ref[:, OFF_G:OFF_G + LRU_WIDTH])
        rest_ref[0, :, 2 * LRU_WIDTH:2 * LRU_WIDTH + SC_WIDTH] = y_ref[:, OFF_B:OFF_B + SC_WIDTH]
        rest_ref[0, :, 2 * LRU_WIDTH + SC_WIDTH:] = (y_ref[:, OFF_C:OFF_C + SC_WIDTH]
                                                     * y_ref[:, OFF_U:OFF_U + SC_WIDTH])

    @pl.when(s == 0)
    def _():
        y_odd_ref[...] = jnp.zeros_like(y_odd_ref)

    @pl.when(s % 2 == 0)
    def _():
        finish(y_odd_ref)
        project(y_even_ref)

    @pl.when(s % 2 == 1)
    def _():
        finish(y_even_ref)
        project(y_odd_ref)


def _token_specs(x_base, tile_offset=0):
    return [pl.BlockSpec((1, TM, D_MODEL), lambda b, i: (b, 0, 0)),
            pl.BlockSpec((1, TM, D_MODEL),
                         lambda b, i: (b, x_base + jnp.maximum(i + tile_offset - 1, 0), 0))]


def _inproj(ctx_arr, x_arr, x_base, mods, g_mix, w_in, gq, gk, cos_t, sin_t):
    bsz = x_arr.shape[0]
    nt = x_arr.shape[1] // TM - x_base + 1
    t_all = nt * TM
    n_tiles = bsz * nt
    rest_w = 2 * LRU_WIDTH + 2 * SC_WIDTH

    def cur(s):
        sc = jnp.minimum(s, n_tiles - 1)
        return sc // nt, sc % nt

    def prev(s):
        sp = jnp.maximum(s - 1, 0)
        return sp // nt, sp % nt

    rope_spec = pl.BlockSpec((HEAD_DIM // 2, TM), lambda s: (0, prev(s)[1]))
    return pl.pallas_call(
        functools.partial(_inproj_kernel, nt=nt, n_tiles=n_tiles),
        grid=(n_tiles + 1,),
        in_specs=[pl.BlockSpec((1, TM, D_MODEL), lambda s: (cur(s)[0], 0, 0)),
                  pl.BlockSpec((1, TM, D_MODEL),
                               lambda s: (cur(s)[0], x_base + jnp.maximum(cur(s)[1] - 1, 0), 0)),
                  pl.BlockSpec((1, 1, MOD_ROWS, D_MODEL),
                               lambda s: (cur(s)[0], jnp.minimum(cur(s)[1], 1), 0, 0)),
                  _const_spec((1, D_MODEL)),
                  _const_spec((D_MODEL, IN_WIDTH)),
                  _const_spec((HEAD_DIM, TM)),
                  _const_spec((HEAD_DIM, TM)),
                  rope_spec,
                  rope_spec],
        out_specs=[pl.BlockSpec((1, ATTN_WIDTH, TM), lambda s: (prev(s)[0], 0, prev(s)[1])),
                   pl.BlockSpec((1, TM, KV_WIDTH), lambda s: (prev(s)[0], prev(s)[1], 0)),
                   pl.BlockSpec((1, KV_WIDTH, TM), lambda s: (prev(s)[0], 0, prev(s)[1])),
                   pl.BlockSpec((1, TM, rest_w), lambda s: (prev(s)[0], prev(s)[1], 0))],
        out_shape=[jax.ShapeDtypeStruct((bsz, ATTN_WIDTH, t_all), BF16),
                   jax.ShapeDtypeStruct((bsz, t_all, KV_WIDTH), BF16),
                   jax.ShapeDtypeStruct((bsz, KV_WIDTH, t_all), BF16),
                   jax.ShapeDtypeStruct((bsz, t_all, rest_w), F32)],
        scratch_shapes=[pltpu.VMEM((TM, IN_WIDTH), F32)] * 2,
        compiler_params=_params(40, n_grid_dims=1),
        name="inproj",
    )(ctx_arr, x_arr, mods, g_mix, w_in, gq, gk, cos_t, sin_t)


def _with_halo(prev_ref, cur_ref, next_ref, valid_prev, valid_next):
    prev = prev_ref[0] * valid_prev.astype(F32)
    nxt = next_ref[0] * valid_next.astype(F32)
    return jnp.concatenate([prev, cur_ref[0], nxt], axis=0)


def _shifted(tcat, offset):
    rows = tcat.shape[0]
    if offset == 0:
        return tcat[F32_SUBLANES:F32_SUBLANES + TM]
    return pltpu.roll(tcat, (-offset) % rows, 0)[F32_SUBLANES:F32_SUBLANES + TM]


def _lru_coeffs(rcat, conv_w, conv_b, w_gate, b_gate, sp_lam):
    xc = conv_b
    for j in range(LRU_CONV):
        xc = xc + conv_w[j:j + 1] * _shifted(rcat, j - 2)
    z = jnp.dot(xc.astype(BF16), w_gate, preferred_element_type=F32) + b_gate
    r_gate = jax.nn.sigmoid(z[:, :LRU_WIDTH])
    i_gate = jax.nn.sigmoid(z[:, LRU_WIDTH:])
    log_a = (-LRU_C) * r_gate * sp_lam
    a = jnp.exp(log_a)
    one_minus_a2 = -jnp.tanh(log_a) * (1.0 + a * a)
    return a, jnp.sqrt(one_minus_a2) * (i_gate * xc)


def _lru_kernel(rf_ref, rfp_ref, rfn_ref, rb_ref, rbp_ref, rbn_ref, bg_ref, cu_ref, cup_ref, cun_ref,
                cw_ref, cb_ref, wg_ref, bgate_ref, lam_ref, scw_ref,
                hf_ref, hb_ref, sc_ref,
                af_s, bf_s, ab_s, bb_s, hf_state, hb_state, *, nt):
    i = pl.program_id(1)
    j = jnp.where(i == 0, 0, nt - i)

    @pl.when(i == 0)
    def _():
        hf_state[...] = jnp.zeros_like(hf_state)
        hb_state[...] = jnp.zeros_like(hb_state)

    def halo_valid(t):
        return jnp.logical_and(t != 0, t != 1), jnp.logical_and(t != 0, t != nt - 1)

    lam = lam_ref[...]
    sp_lam = jnp.maximum(-lam, 0.0) + jnp.log1p(jnp.exp(-jnp.abs(lam)))
    cw = cw_ref[...]
    cb = cb_ref[...]

    vp, vn = halo_valid(i)
    a, b = _lru_coeffs(_with_halo(rfp_ref, rf_ref, rfn_ref, vp, vn), cw, cb,
                       wg_ref[:, :2 * LRU_WIDTH], bgate_ref[:, :2 * LRU_WIDTH], sp_lam[0:1])
    _block_prefix(a, b, af_s, bf_s, reverse=False)
    cucat = _with_halo(cup_ref, cu_ref, cun_ref, vp, vn)
    scw = scw_ref[...]
    conv = scw[0:1] * _shifted(cucat, -1) + scw[1:2] * _shifted(cucat, 0) + scw[2:3] * _shifted(cucat, 1)
    sc_ref[0] = (bg_ref[0] * conv).astype(BF16)

    vp, vn = halo_valid(j)
    a, b = _lru_coeffs(_with_halo(rbp_ref, rb_ref, rbn_ref, vp, vn), cw, cb,
                       wg_ref[:, 2 * LRU_WIDTH:], bgate_ref[:, 2 * LRU_WIDTH:], sp_lam[1:2])
    _block_prefix(a, b, ab_s, bb_s, reverse=True)

    _carry_blocks(af_s, bf_s, hf_ref, hf_state, reverse=False)
    _carry_blocks(ab_s, bb_s, hb_ref, hb_state, reverse=True)


def _block_prefix(a, b, a_s, b_s, reverse):
    n = F32_SUBLANES
    a = a.reshape(TM // n, n, LRU_WIDTH)
    b = b.reshape(TM // n, n, LRU_WIDTH)
    row = lax.broadcasted_iota(jnp.int32, a.shape, 1)
    d = 1
    while d < n:
        shift, keep = (n - d, row < n - d) if reverse else (d, row >= d)
        a_prev = jnp.where(keep, pltpu.roll(a, shift, 1), 1.0)
        b_prev = jnp.where(keep, pltpu.roll(b, shift, 1), 0.0)
        b = a * b_prev + b
        a = a * a_prev
        d *= 2
    a_s[...] = a
    b_s[...] = b


def _carry_blocks(a_s, b_s, out_ref, state_ref, reverse):
    n = F32_SUBLANES
    nb = TM // n
    last = 0 if reverse else n - 1
    h_in = state_ref[...]
    for v in (reversed(range(nb)) if reverse else range(nb)):
        h = a_s[v] * h_in + b_s[v]
        out_ref[0, v * n:(v + 1) * n, :] = h
        h_in = h[last:last + 1]
    state_ref[...] = h_in


def _lru(rest, conv_w, conv_b, w_gate, b_gate, lam, sc_w):
    bsz, t_all, _ = rest.shape
    nt = t_all // TM
    hpt = TM // F32_SUBLANES
    nhb = t_all // F32_SUBLANES

    def rev(i):
        return jnp.where(i == 0, 0, nt - i)

    def tile(col, order):
        return pl.BlockSpec((1, TM, LRU_WIDTH), lambda b, i: (b, order(i), col))

    def prev_halo(col, order):
        return pl.BlockSpec((1, F32_SUBLANES, LRU_WIDTH),
                            lambda b, i: (b, jnp.maximum(order(i) * hpt - 1, 0), col))

    def next_halo(col, order):
        return pl.BlockSpec((1, F32_SUBLANES, LRU_WIDTH),
                            lambda b, i: (b, jnp.minimum((order(i) + 1) * hpt, nhb - 1), col))

    fwd = lambda i: i
    out_f32 = jax.ShapeDtypeStruct((bsz, t_all, LRU_WIDTH), F32)
    return pl.pallas_call(
        functools.partial(_lru_kernel, nt=nt),
        grid=(bsz, nt),
        in_specs=[tile(0, fwd), prev_halo(0, fwd), next_halo(0, fwd),
                  tile(0, rev), prev_halo(0, rev), next_halo(0, rev),
                  tile(2, fwd),
                  tile(3, fwd), prev_halo(3, fwd), next_halo(3, fwd),
                  _const_spec((LRU_CONV, LRU_WIDTH)),
                  _const_spec((1, LRU_WIDTH)),
                  _const_spec((LRU_WIDTH, 4 * LRU_WIDTH)),
                  _const_spec((1, 4 * LRU_WIDTH)),
                  _const_spec((2, LRU_WIDTH)),
                  _const_spec((3, SC_WIDTH))],
        out_specs=[pl.BlockSpec((1, TM, LRU_WIDTH), lambda b, i: (b, i, 0)),
                   pl.BlockSpec((1, TM, LRU_WIDTH), lambda b, i: (b, rev(i), 0)),
                   pl.BlockSpec((1, TM, SC_WIDTH), lambda b, i: (b, i, 0))],
        out_shape=[out_f32, out_f32, jax.ShapeDtypeStruct((bsz, t_all, SC_WIDTH), BF16)],
        scratch_shapes=([pltpu.VMEM((TM // F32_SUBLANES, F32_SUBLANES, LRU_WIDTH), F32)] * 4
                        + [pltpu.VMEM((1, LRU_WIDTH), F32)] * 2),
        compiler_params=_params(32),
        name="lru_scan",
    )(rest, rest, rest, rest, rest, rest, rest, rest, rest, rest,
      conv_w, conv_b, w_gate, b_gate, lam, sc_w)


def _mixer_kernel(qT_ref, k_ref, vT_ref, ctx_ref, x_ref, hf_ref, hb_ref, gg_ref, sc_ref, mod_ref, wo_ref,
                  gffn_ref, x1_ref, h2_ref, edge_ref, oT_ref, s_ref, *, tile_offset, n_key_tiles):
    i = pl.program_id(1) + tile_offset
    ones_rows = jnp.ones((BF16_SUBLANES, TM), BF16)

    def project_out():
        x = x_ref[0]
        if tile_offset == 0:
            x = jnp.where(i == 0, ctx_ref[0], x)
        mod = mod_ref[0, 0]
        gate_mix, shift, scale = mod[2:3], mod[3:4], mod[4:5]
        att = oT_ref[...].T.astype(BF16)
        lru = (gg_ref[0] * (hf_ref[0] + hb_ref[0])).astype(BF16)
        y = (jnp.dot(att, wo_ref[0:ATTN_WIDTH], preferred_element_type=F32)
             + jnp.dot(lru, wo_ref[ATTN_WIDTH:ATTN_WIDTH + LRU_WIDTH], preferred_element_type=F32)
             + jnp.dot(sc_ref[0], wo_ref[ATTN_WIDTH + LRU_WIDTH:], preferred_element_type=F32))
        x1 = x + gate_mix * y
        x1_ref[0] = x1
        h2 = x1 * _rms_scale(x1) * gffn_ref[...]
        h2 = h2 * (1.0 + scale) + shift
        h2_ref[0] = h2.astype(BF16)
        edge_ref[0, 0] = jnp.concatenate(
            [h2[0:1], h2[TM - 1:TM], jnp.zeros((F32_SUBLANES - 2, D_MODEL), F32)], axis=0)

    def run(n_tiles):
        items = [(pr, c) for pr in range(N_HEADS // 2) for c in range(n_tiles)]
        qpads = {}

        def qpad_of(pr):
            if pr not in qpads:
                row0 = pr * 2 * HEAD_DIM
                q2 = jnp.concatenate([qT_ref[0, row0:row0 + HEAD_DIM, :],
                                      qT_ref[0, row0 + HEAD_DIM:row0 + 2 * HEAD_DIM, :]], axis=1)
                zero = jnp.zeros_like(q2)
                first_kv = (2 * pr) // HEADS_PER_KV == 0
                qpads[pr] = jnp.concatenate([q2, zero] if first_kv else [zero, q2], axis=0)
            return qpads[pr]

        def scores(t):
            pr, c = items[t]
            s = jnp.dot(k_ref[0, c * TM:(c + 1) * TM, :], qpad_of(pr), preferred_element_type=F32)
            s_ref[t % SCORE_SLOTS] = s
            return jnp.max(s, axis=0, keepdims=True)

        tile_max = [scores(t) for t in range(min(SCORE_LOOKAHEAD, len(items)))]
        m = acc = None
        for t, (pr, c) in enumerate(items):
            if t + SCORE_LOOKAHEAD < len(items):
                tile_max.append(scores(t + SCORE_LOOKAHEAD))
            if c == 0:
                m = jnp.full((1, 2 * TM), -jnp.inf, F32)
                acc = jnp.zeros((HEAD_DIM + BF16_SUBLANES, 2 * TM), F32)
            g = (2 * pr) // HEADS_PER_KV
            m_new = jnp.maximum(m, tile_max[t])
            alpha = jnp.exp2(m - m_new)
            p = jnp.exp2(s_ref[t % SCORE_SLOTS] - m_new).astype(BF16)
            vt = jnp.concatenate([vT_ref[0, g * HEAD_DIM:(g + 1) * HEAD_DIM, c * TM:(c + 1) * TM],
                                  ones_rows], axis=0)
            acc = alpha * acc + jnp.dot(vt, p, preferred_element_type=F32)
            m = m_new
            tile_max[t] = None
            if c == n_tiles - 1:
                o = acc[:HEAD_DIM] / acc[HEAD_DIM:HEAD_DIM + 1]
                row0 = pr * 2 * HEAD_DIM
                oT_ref[row0:row0 + HEAD_DIM, :] = o[:, :TM]
                oT_ref[row0 + HEAD_DIM:row0 + 2 * HEAD_DIM, :] = o[:, TM:]
        project_out()

    @pl.when(i == 0)
    def _():
        run(1)

    @pl.when(i != 0)
    def _():
        run(n_key_tiles)


def _mixer(qT, k, vT, ctx_arr, x_arr, x_base, hf, hb, rest, sc, mods, w_out, g_ffn, tile_offset):
    bsz, t_all, _ = k.shape
    nt = t_all // TM
    nq = nt - tile_offset
    off = lambda b, i: (b, i + tile_offset, 0)
    return pl.pallas_call(
        functools.partial(_mixer_kernel, tile_offset=tile_offset, n_key_tiles=nt),
        grid=(bsz, nq),
        in_specs=[pl.BlockSpec((1, ATTN_WIDTH, TM), lambda b, i: (b, 0, i + tile_offset)),
                  pl.BlockSpec((1, t_all, KV_WIDTH), lambda b, i: (b, 0, 0)),
                  pl.BlockSpec((1, KV_WIDTH, t_all), lambda b, i: (b, 0, 0))]
                 + _token_specs(x_base, tile_offset) + [
                  pl.BlockSpec((1, TM, LRU_WIDTH), off),
                  pl.BlockSpec((1, TM, LRU_WIDTH), off),
                  pl.BlockSpec((1, TM, LRU_WIDTH), lambda b, i: (b, i + tile_offset, 1)),
                  pl.BlockSpec((1, TM, SC_WIDTH), off),
                  pl.BlockSpec((1, 1, MOD_ROWS, D_MODEL),
                               lambda b, i: (b, jnp.minimum(i + tile_offset, 1), 0, 0)),
                  _const_spec((D_MODEL, D_MODEL)),
                  _const_spec((1, D_MODEL))],
        out_specs=[pl.BlockSpec((1, TM, D_MODEL), lambda b, i: (b, i, 0)),
                   pl.BlockSpec((1, TM, D_MODEL), lambda b, i: (b, i, 0)),
                   pl.BlockSpec((1, 1, F32_SUBLANES, D_MODEL), lambda b, i: (b, i, 0, 0))],
        out_shape=[jax.ShapeDtypeStruct((bsz, nq * TM, D_MODEL), F32),
                   jax.ShapeDtypeStruct((bsz, nq * TM, D_MODEL), BF16),
                   jax.ShapeDtypeStruct((bsz, nq, F32_SUBLANES, D_MODEL), F32)],
        scratch_shapes=[pltpu.VMEM((ATTN_WIDTH, TM), F32), pltpu.VMEM((SCORE_SLOTS, TM, 2 * TM), F32)],
        compiler_params=_params(40),
        name="mixer",
    )(qT, k, vT, ctx_arr, x_arr, hf, hb, rest, sc, mods, w_out, g_ffn)


def _ffn_kernel(h_ref, edge_ref, x1_ref, mod_ref, wup_ref, cw_ref, wdn_ref, gfin_ref,
                o_ref, acc_ref, up_ref, *, nt, final_norm):
    hcat = h_ref[0]
    row = lax.broadcasted_iota(jnp.int32, (TM, FF_CHUNK), 0)
    tile = pl.program_id(0) * nt + pl.program_id(1)
    edge_row = (tile % (F32_SUBLANES // 2)) * 2

    def conv(t, w, cols):
        before = jnp.where(row == 0, edge_ref[pl.ds(edge_row, 1), cols], pltpu.roll(t, 1, 0))
        after = jnp.where(row == TM - 1, edge_ref[pl.ds(edge_row + 1, 1), cols],
                          pltpu.roll(t, TM - 1, 0))
        return w[0:1] * before + w[1:2] * t + w[2:3] * after

    n_chunks = D_FF // FF_CHUNK

    def up_project(c):
        cu = slice(c * FF_CHUNK, (c + 1) * FF_CHUNK)
        cg = slice(D_FF + c * FF_CHUNK, D_FF + (c + 1) * FF_CHUNK)
        up_ref[c % UP_SLOTS, :, :FF_CHUNK] = jnp.dot(hcat, wup_ref[:, cu], preferred_element_type=F32)
        up_ref[c % UP_SLOTS, :, FF_CHUNK:] = jnp.dot(hcat, wup_ref[:, cg], preferred_element_type=F32)

    for c in range(min(UP_LOOKAHEAD, n_chunks)):
        up_project(c)
    for c in range(n_chunks):
        if c + UP_LOOKAHEAD < n_chunks:
            up_project(c + UP_LOOKAHEAD)
        cu = slice(c * FF_CHUNK, (c + 1) * FF_CHUNK)
        cg = slice(D_FF + c * FF_CHUNK, D_FF + (c + 1) * FF_CHUNK)
        u = conv(up_ref[c % UP_SLOTS, :, :FF_CHUNK], cw_ref[:, cu], cu)
        g = conv(up_ref[c % UP_SLOTS, :, FF_CHUNK:], cw_ref[:, cg], cg)
        act = ((g * jax.nn.sigmoid(g)) * u).astype(BF16)
        part = jnp.dot(act, wdn_ref[cu, :], preferred_element_type=F32)
        if c == 0:
            acc_ref[...] = part
        elif c < n_chunks - 1:
            acc_ref[...] += part

    gate_ffn = mod_ref[0, 0][5:6]
    x2 = x1_ref[0] + gate_ffn * (acc_ref[...] + part)
    if final_norm:
        x2 = x2 * _rms_scale(x2) * gfin_ref[...]
    o_ref[0] = x2


def _edge_up_kernel(h_ref, w_ref, o_ref):
    o_ref[...] = jnp.dot(h_ref[...], w_ref[...], preferred_element_type=F32)


def _edge_up(tile_edges, w_up, ctx_tiles):
    bsz, nt = tile_edges.shape[:2]
    first, last = tile_edges[:, :, 0].astype(BF16), tile_edges[:, :, 1].astype(BF16)
    t_idx = jnp.arange(nt)
    has_before = jnp.logical_and(t_idx != 0, t_idx != ctx_tiles)[None, :, None]
    has_after = jnp.logical_and(t_idx != nt - 1, t_idx != ctx_tiles - 1)[None, :, None]
    before = jnp.where(has_before, jnp.roll(last, 1, axis=1), 0)
    after = jnp.where(has_after, jnp.roll(first, -1, axis=1), 0)
    rows = jnp.stack([before, after], axis=2).reshape(bsz * nt * 2, D_MODEL)
    n_rows = rows.shape[0]
    padded = -(-n_rows // BF16_SUBLANES) * BF16_SUBLANES
    rows = jnp.pad(rows, ((0, padded - n_rows), (0, 0)))
    n_blk = 2 * D_FF // EDGE_COLS
    return pl.pallas_call(
        _edge_up_kernel,
        grid=(n_blk,),
        in_specs=[pl.BlockSpec((padded, D_MODEL), lambda j: (0, 0)),
                  pl.BlockSpec((D_MODEL, EDGE_COLS), lambda j: (0, j))],
        out_specs=pl.BlockSpec((padded, EDGE_COLS), lambda j: (0, j)),
        out_shape=jax.ShapeDtypeStruct((padded, 2 * D_FF), F32),
        name="edge_up",
    )(rows, w_up)


def _ffn(h2, tile_edges, x1, mods, w_up, conv_w, w_down, g_final, ctx_tiles, final_norm):
    bsz, t_len, _ = h2.shape
    nt = t_len // TM
    edges = _edge_up(tile_edges, w_up, ctx_tiles)
    tiles_per_block = F32_SUBLANES // 2
    return pl.pallas_call(
        functools.partial(_ffn_kernel, nt=nt, final_norm=final_norm),
        grid=(bsz, nt),
        in_specs=[pl.BlockSpec((1, TM, D_MODEL), lambda b, i: (b, i, 0)),
                  pl.BlockSpec((F32_SUBLANES, 2 * D_FF),
                               lambda b, i: ((b * nt + i) // tiles_per_block, 0)),
                  pl.BlockSpec((1, TM, D_MODEL), lambda b, i: (b, i, 0)),
                  pl.BlockSpec((1, 1, MOD_ROWS, D_MODEL),
                               lambda b, i: (b, jnp.minimum(i + 1 - ctx_tiles, 1), 0, 0)),
                  _const_spec((D_MODEL, 2 * D_FF)),
                  _const_spec((3, 2 * D_FF)),
                  _const_spec((D_FF, D_MODEL)),
                  _const_spec((1, D_MODEL))],
        out_specs=pl.BlockSpec((1, TM, D_MODEL), lambda b, i: (b, i, 0)),
        out_shape=jax.ShapeDtypeStruct((bsz, t_len, D_MODEL), F32),
        scratch_shapes=[pltpu.VMEM((TM, D_MODEL), F32),
                        pltpu.VMEM((UP_SLOTS, TM, 2 * FF_CHUNK), F32)],
        compiler_params=_params(48),
        name="conv_ffn",
    )(h2, edges, x1, mods, w_up, conv_w, w_down, g_final)


def _rope_tables(seq, ctx_len):
    rows = seq // GRID_W
    pos_r = np.repeat(np.arange(rows, dtype=np.float32), GRID_W)
    pos_c = np.tile(np.arange(GRID_W, dtype=np.float32), rows)
    n_f = HEAD_DIM // 4
    inv = (np.float32(ROPE_THETA) ** (-np.arange(n_f, dtype=np.float32) / n_f)).astype(np.float32)
    ang = np.concatenate([pos_r[:, None] * inv, pos_c[:, None] * inv], axis=-1).astype(np.float32)
    cos = np.concatenate([np.ones((ctx_len, HEAD_DIM // 2)), np.cos(ang.astype(np.float64))], axis=0)
    sin = np.concatenate([np.zeros((ctx_len, HEAD_DIM // 2)), np.sin(ang.astype(np.float64))], axis=0)
    return jnp.asarray(cos.T, F32), jnp.asarray(sin.T, F32)


def _gate_weights(wa, ba, wi, bi):
    def dense(w):
        eye = jnp.eye(LRU_BLOCKS, dtype=w.dtype)
        return jnp.einsum("nde,nm->ndme", w, eye).reshape(LRU_WIDTH, LRU_WIDTH)
    w = jnp.concatenate([dense(wa[0]), dense(wi[0]), dense(wa[1]), dense(wi[1])], axis=1)
    b = jnp.concatenate([ba[0], bi[0], ba[1], bi[1]])[None, :]
    return w.astype(BF16), b


def kernel(x, c, ctx, c_ctx, w_mod, b_mod, g_mix, g_ffn, w_in, g_q, g_k, lru_conv_w, lru_conv_b,
           lru_wa, lru_ba, lru_wi, lru_bi, lru_lam, sc_conv_w, w_out, w_up, ffn_conv_w, w_down,
           g_final):
    bsz, seq, _ = x.shape
    ctx_len = ctx.shape[1]
    depth = w_mod.shape[0]
    assert ctx_len == TM and seq % TM == 0 and seq % GRID_W == 0

    cos_t, sin_t = _rope_tables(seq, ctx_len)
    ctx_arr, x_arr, x_base = ctx, x, 0
    cc_rows = -(-(bsz + 1) // F32_SUBLANES) * F32_SUBLANES
    cc = jnp.zeros((cc_rows, D_MODEL), F32).at[:bsz].set(c).at[bsz].set(c_ctx)

    mod_layers = _modulation(cc, w_mod, b_mod)

    out = None
    for l in range(depth):
        last = l == depth - 1
        mod_all = mod_layers[l].reshape(cc_rows, N_MOD, D_MODEL)
        mod_x = mod_all[:bsz]
        mod_c = jnp.broadcast_to(mod_all[bsz][None], (bsz, N_MOD, D_MODEL))
        mods = jnp.stack([mod_c, mod_x], axis=1)
        mods = jnp.pad(mods, ((0, 0), (0, 0), (0, MOD_ROWS - N_MOD), (0, 0)))

        gq = jnp.broadcast_to((g_q[l] * Q_SCALE)[:, None], (HEAD_DIM, TM))
        gk = jnp.broadcast_to(g_k[l][:, None], (HEAD_DIM, TM))
        qT, k, vT, rest = _inproj(ctx_arr, x_arr, x_base, mods, g_mix[l][None, :],
                                  w_in[l].astype(BF16), gq, gk, cos_t, sin_t)

        w_gate, b_gate = _gate_weights(lru_wa[l], lru_ba[l], lru_wi[l], lru_bi[l])
        hf, hb, sc = _lru(rest, lru_conv_w[l], lru_conv_b[l][None, :], w_gate, b_gate, lru_lam[l],
                          sc_conv_w[l])

        tile_offset = 1 if last else 0
        x1, h2, tile_edges = _mixer(qT, k, vT, ctx_arr, x_arr, x_base, hf, hb, rest, sc, mods,
                                    w_out[l].astype(BF16), g_ffn[l][None, :], tile_offset)
        out = _ffn(h2, tile_edges, x1, mods, w_up[l].astype(BF16), ffn_conv_w[l],
                   w_down[l].astype(BF16), g_final[None, :], ctx_tiles=1 - tile_offset,
                   final_norm=last)
        ctx_arr, x_arr, x_base = out, out, 1
    return out
```

```python
import functools

import numpy as np
import jax
import jax.numpy as jnp
from jax import lax
from jax.experimental import pallas as pl
from jax.experimental.pallas import tpu as pltpu

F32 = jnp.float32
BF16 = jnp.bfloat16

D_MODEL = 1024
HEAD_DIM = 64
N_HEADS = 8
N_KV_HEADS = 2
HEADS_PER_KV = N_HEADS // N_KV_HEADS
ATTN_WIDTH = N_HEADS * HEAD_DIM
KV_WIDTH = N_KV_HEADS * HEAD_DIM
LRU_WIDTH = 256
LRU_BLOCKS = 4
LRU_BLOCK = LRU_WIDTH // LRU_BLOCKS
LRU_CONV = 4
LRU_C = 8.0
SC_WIDTH = 256
IN_WIDTH = 2048
D_FF = 2816
FF_CHUNK = 256
EDGE_COLS = 1408
GRID_W = 64
ROPE_THETA = 10000.0
EPS = 1e-6
Q_SCALE = HEAD_DIM ** -0.5 * 1.4426950408889634

TM = 256
SCORE_LOOKAHEAD = 3
SCORE_SLOTS = SCORE_LOOKAHEAD + 1
UP_LOOKAHEAD = 4
UP_SLOTS = UP_LOOKAHEAD + 1
LANES = 128
F32_SUBLANES = 8
BF16_SUBLANES = 16
N_MOD = 6
MOD_ROWS = 8

OFF_Q = 0
OFF_K = OFF_Q + ATTN_WIDTH
OFF_V = OFF_K + KV_WIDTH
OFF_R = OFF_V + KV_WIDTH
OFF_G = OFF_R + LRU_WIDTH
OFF_B = OFF_G + LRU_WIDTH
OFF_C = OFF_B + SC_WIDTH
OFF_U = OFF_C + SC_WIDTH


def _const_spec(shape):
    nd = len(shape)
    return pl.BlockSpec(shape, lambda *_: (0,) * nd, pipeline_mode=pl.Buffered(1))


def _params(vmem_mb, n_grid_dims=2):
    return pltpu.CompilerParams(dimension_semantics=("arbitrary",) * n_grid_dims,
                                vmem_limit_bytes=vmem_mb * 1024 * 1024)


def _rms_scale(t):
    return lax.rsqrt(jnp.mean(t * t, axis=-1, keepdims=True) + EPS)


def _mod_kernel(c_ref, w_ref, b_ref, o_ref):
    c = c_ref[...]
    h = (c * jax.nn.sigmoid(c)).astype(BF16)
    o_ref[...] = jnp.dot(h, w_ref[...].astype(BF16), preferred_element_type=F32) + b_ref[...]


def _modulation(cc, w_mod, b_mod):
    rows = cc.shape[0]
    depth = w_mod.shape[0]
    return pl.pallas_call(
        _mod_kernel,
        grid=(depth, N_MOD),
        in_specs=[pl.BlockSpec((rows, D_MODEL), lambda l, j: (0, 0)),
                  pl.BlockSpec((None, D_MODEL, D_MODEL), lambda l, j: (l, 0, j)),
                  pl.BlockSpec((None, 1, D_MODEL), lambda l, j: (l, 0, j))],
        out_specs=pl.BlockSpec((None, rows, D_MODEL), lambda l, j: (l, 0, j)),
        out_shape=jax.ShapeDtypeStruct((depth, rows, N_MOD * D_MODEL), F32),
        name="modulation",
    )(cc, w_mod, b_mod[:, None, :])


def _norm_rope_transposed(t, gain, cos, sin):
    half = HEAD_DIM // 2
    out = []
    for h in range(t.shape[0] // HEAD_DIM):
        r = t[h * HEAD_DIM:(h + 1) * HEAD_DIM]
        r = r * lax.rsqrt(jnp.mean(r * r, axis=0, keepdims=True) + EPS) * gain
        x1, x2 = r[:half], r[half:]
        out += [x1 * cos - x2 * sin, x1 * sin + x2 * cos]
    return jnp.concatenate(out, axis=0)


def _inproj_kernel(ctx_ref, x_ref, mod_ref, gmix_ref, w_ref, gq_ref, gk_ref, cos_ref, sin_ref,
                   qT_ref, k_ref, vT_ref, rest_ref, y_even_ref, y_odd_ref, *, nt, n_tiles):
    s = pl.program_id(0)
    tile = jnp.minimum(s, n_tiles - 1) % nt

    def project(y_ref):
        x = jnp.where(tile == 0, ctx_ref[0], x_ref[0])
        mod = mod_ref[0, 0]
        shift, scale = mod[0:1], mod[1:2]
        h = x * _rms_scale(x) * gmix_ref[...]
        h = h * (1.0 + scale) + shift
        y_ref[...] = jnp.dot(h.astype(BF16), w_ref[...], preferred_element_type=F32)

    def finish(y_ref):
        cos, sin = cos_ref[...], sin_ref[...]
        q = _norm_rope_transposed(y_ref[:, OFF_Q:OFF_Q + ATTN_WIDTH].T, gq_ref[...], cos, sin)
        qT_ref[0] = q.astype(BF16)
        k = _norm_rope_transposed(y_ref[:, OFF_K:OFF_K + KV_WIDTH].T, gk_ref[...], cos, sin)
        k_ref[0] = k.T.astype(BF16)
        vT_ref[0] = y_ref[:, OFF_V:OFF_V + KV_WIDTH].T.astype(BF16)
        rest_ref[0, :, 0:LRU_WIDTH] = y_ref[:, OFF_R:OFF_R + LRU_WIDTH]
        rest_ref[0, :, LRU_WIDTH:2 * LRU_WIDTH] = jax.nn.gelu(y_ref[:, OFF_G:OFF_G + LRU_WIDTH])
        rest_ref[0, :, 2 * LRU_WIDTH:2 * LRU_WIDTH + SC_WIDTH] = y_ref[:, OFF_B:OFF_B + SC_WIDTH]
        rest_ref[0, :, 2 * LRU_WIDTH + SC_WIDTH:] = (y_ref[:, OFF_C:OFF_C + SC_WIDTH]
                                                     * y_ref[:, OFF_U:OFF_U + SC_WIDTH])

    @pl.when(s == 0)
    def _():
        y_odd_ref[...] = jnp.zeros_like(y_odd_ref)

    @pl.when(s % 2 == 0)
    def _():
        finish(y_odd_ref)
        project(y_even_ref)

    @pl.when(s % 2 == 1)
    def _():
        finish(y_even_ref)
        project(y_odd_ref)


def _token_specs(x_base, tile_offset=0):
    return [pl.BlockSpec((1, TM, D_MODEL), lambda b, i: (b, 0, 0)),
            pl.BlockSpec((1, TM, D_MODEL),
                         lambda b, i: (b, x_base + jnp.maximum(i + tile_offset - 1, 0), 0))]


def _inproj(ctx_arr, x_arr, x_base, mods, g_mix, w_in, gq, gk, cos_t, sin_t):
    bsz = x_arr.shape[0]
    nt = x_arr.shape[1] // TM - x_base + 1
    t_all = nt * TM
    n_tiles = bsz * nt
    rest_w = 2 * LRU_WIDTH + 2 * SC_WIDTH

    def cur(s):
        sc = jnp.minimum(s, n_tiles - 1)
        return sc // nt, sc % nt

    def prev(s):
        sp = jnp.maximum(s - 1, 0)
        return sp // nt, sp % nt

    rope_spec = pl.BlockSpec((HEAD_DIM // 2, TM), lambda s: (0, prev(s)[1]))
    return pl.pallas_call(
        functools.partial(_inproj_kernel, nt=nt, n_tiles=n_tiles),
        grid=(n_tiles + 1,),
        in_specs=[pl.BlockSpec((1, TM, D_MODEL), lambda s: (cur(s)[0], 0, 0)),
                  pl.BlockSpec((1, TM, D_MODEL),
                               lambda s: (cur(s)[0], x_base + jnp.maximum(cur(s)[1] - 1, 0), 0)),
                  pl.BlockSpec((1, 1, MOD_ROWS, D_MODEL),
                               lambda s: (cur(s)[0], jnp.minimum(cur(s)[1], 1), 0, 0)),
                  _const_spec((1, D_MODEL)),
                  _const_spec((D_MODEL, IN_WIDTH)),
                  _const_spec((HEAD_DIM, TM)),
                  _const_spec((HEAD_DIM, TM)),
                  rope_spec,
                  rope_spec],
        out_specs=[pl.BlockSpec((1, ATTN_WIDTH, TM), lambda s: (prev(s)[0], 0, prev(s)[1])),
                   pl.BlockSpec((1, TM, KV_WIDTH), lambda s: (prev(s)[0], prev(s)[1], 0)),
                   pl.BlockSpec((1, KV_WIDTH, TM), lambda s: (prev(s)[0], 0, prev(s)[1])),
                   pl.BlockSpec((1, TM, rest_w), lambda s: (prev(s)[0], prev(s)[1], 0))],
        out_shape=[jax.ShapeDtypeStruct((bsz, ATTN_WIDTH, t_all), BF16),
                   jax.ShapeDtypeStruct((bsz, t_all, KV_WIDTH), BF16),
                   jax.ShapeDtypeStruct((bsz, KV_WIDTH, t_all), BF16),
                   jax.ShapeDtypeStruct((bsz, t_all, rest_w), F32)],
        scratch_shapes=[pltpu.VMEM((TM, IN_WIDTH), F32)] * 2,
        compiler_params=_params(40, n_grid_dims=1),
        name="inproj",
    )(ctx_arr, x_arr, mods, g_mix, w_in, gq, gk, cos_t, sin_t)


def _with_halo(prev_ref, cur_ref, next_ref, valid_prev, valid_next):
    prev = prev_ref[0] * valid_prev.astype(F32)
    nxt = next_ref[0] * valid_next.astype(F32)
    return jnp.concatenate([prev, cur_ref[0], nxt], axis=0)


def _shifted(tcat, offset):
    rows = tcat.shape[0]
    if offset == 0:
        return tcat[F32_SUBLANES:F32_SUBLANES + TM]
    return pltpu.roll(tcat, (-offset) % rows, 0)[F32_SUBLANES:F32_SUBLANES + TM]


def _lru_coeffs(rcat, conv_w, conv_b, w_gate, b_gate, sp_lam):
    xc = conv_b
    for j in range(LRU_CONV):
        xc = xc + conv_w[j:j + 1] * _shifted(rcat, j - 2)
    z = jnp.dot(xc.astype(BF16), w_gate, preferred_element_type=F32) + b_gate
    r_gate = jax.nn.sigmoid(z[:, :LRU_WIDTH])
    i_gate = jax.nn.sigmoid(z[:, LRU_WIDTH:])
    log_a = (-LRU_C) * r_gate * sp_lam
    a = jnp.exp(log_a)
    one_minus_a2 = -jnp.tanh(log_a) * (1.0 + a * a)
    return a, jnp.sqrt(one_minus_a2) * (i_gate * xc)


def _lru_kernel(rf_ref, rfp_ref, rfn_ref, rb_ref, rbp_ref, rbn_ref, bg_ref, cu_ref, cup_ref, cun_ref,
                cw_ref, cb_ref, wg_ref, bgate_ref, lam_ref, scw_ref,
                hf_ref, hb_ref, sc_ref,
                af_s, bf_s, ab_s, bb_s, hf_state, hb_state, *, nt):
    i = pl.program_id(1)
    j = jnp.where(i == 0, 0, nt - i)

    @pl.when(i == 0)
    def _():
        hf_state[...] = jnp.zeros_like(hf_state)
        hb_state[...] = jnp.zeros_like(hb_state)

    def halo_valid(t):
        return jnp.logical_and(t != 0, t != 1), jnp.logical_and(t != 0, t != nt - 1)

    lam = lam_ref[...]
    sp_lam = jnp.maximum(-lam, 0.0) + jnp.log1p(jnp.exp(-jnp.abs(lam)))
    cw = cw_ref[...]
    cb = cb_ref[...]

    vp, vn = halo_valid(i)
    a, b = _lru_coeffs(_with_halo(rfp_ref, rf_ref, rfn_ref, vp, vn), cw, cb,
                       wg_ref[:, :2 * LRU_WIDTH], bgate_ref[:, :2 * LRU_WIDTH], sp_lam[0:1])
    _block_prefix(a, b, af_s, bf_s, reverse=False)
    cucat = _with_halo(cup_ref, cu_ref, cun_ref, vp, vn)
    scw = scw_ref[...]
    conv = scw[0:1] * _shifted(cucat, -1) + scw[1:2] * _shifted(cucat, 0) + scw[2:3] * _shifted(cucat, 1)
    sc_ref[0] = (bg_ref[0] * conv).astype(BF16)

    vp, vn = halo_valid(j)
    a, b = _lru_coeffs(_with_halo(rbp_ref, rb_ref, rbn_ref, vp, vn), cw, cb,
                       wg_ref[:, 2 * LRU_WIDTH:], bgate_ref[:, 2 * LRU_WIDTH:], sp_lam[1:2])
    _block_prefix(a, b, ab_s, bb_s, reverse=True)

    _carry_blocks(af_s, bf_s, hf_ref, hf_state, reverse=False)
    _carry_blocks(ab_s, bb_s, hb_ref, hb_state, reverse=True)


def _block_prefix(a, b, a_s, b_s, reverse):
    n = F32_SUBLANES
    a = a.reshape(TM // n, n, LRU_WIDTH)
    b = b.reshape(TM // n, n, LRU_WIDTH)
    row = lax.broadcasted_iota(jnp.int32, a.shape, 1)
    d = 1
    while d < n:
        shift, keep = (n - d, row < n - d) if reverse else (d, row >= d)
        a_prev = jnp.where(keep, pltpu.roll(a, shift, 1), 1.0)
        b_prev = jnp.where(keep, pltpu.roll(b, shift, 1), 0.0)
        b = a * b_prev + b
        a = a * a_prev
        d *= 2
    a_s[...] = a
    b_s[...] = b


def _carry_blocks(a_s, b_s, out_ref, state_ref, reverse):
    n = F32_SUBLANES
    nb = TM // n
    last = 0 if reverse else n - 1
    h_in = state_ref[...]
    for v in (reversed(range(nb)) if reverse else range(nb)):
        h = a_s[v] * h_in + b_s[v]
        out_ref[0, v * n:(v + 1) * n, :] = h
        h_in = h[last:last + 1]
    state_ref[...] = h_in


def _lru(rest, conv_w, conv_b, w_gate, b_gate, lam, sc_w):
    bsz, t_all, _ = rest.shape
    nt = t_all // TM
    hpt = TM // F32_SUBLANES
    nhb = t_all // F32_SUBLANES

    def rev(i):
        return jnp.where(i == 0, 0, nt - i)

    def tile(col, order):
        return pl.BlockSpec((1, TM, LRU_WIDTH), lambda b, i: (b, order(i), col))

    def prev_halo(col, order):
        return pl.BlockSpec((1, F32_SUBLANES, LRU_WIDTH),
                            lambda b, i: (b, jnp.maximum(order(i) * hpt - 1, 0), col))

    def next_halo(col, order):
        return pl.BlockSpec((1, F32_SUBLANES, LRU_WIDTH),
                            lambda b, i: (b, jnp.minimum((order(i) + 1) * hpt, nhb - 1), col))

    fwd = lambda i: i
    out_f32 = jax.ShapeDtypeStruct((bsz, t_all, LRU_WIDTH), F32)
    return pl.pallas_call(
        functools.partial(_lru_kernel, nt=nt),
        grid=(bsz, nt),
        in_specs=[tile(0, fwd), prev_halo(0, fwd), next_halo(0, fwd),
                  tile(0, rev), prev_halo(0, rev), next_halo(0, rev),
                  tile(2, fwd),
                  tile(3, fwd), prev_halo(3, fwd), next_halo(3, fwd),
                  _const_spec((LRU_CONV, LRU_WIDTH)),
                  _const_spec((1, LRU_WIDTH)),
                  _const_spec((LRU_WIDTH, 4 * LRU_WIDTH)),
                  _const_spec((1, 4 * LRU_WIDTH)),
                  _const_spec((2, LRU_WIDTH)),
                  _const_spec((3, SC_WIDTH))],
        out_specs=[pl.BlockSpec((1, TM, LRU_WIDTH), lambda b, i: (b, i, 0)),
                   pl.BlockSpec((1, TM, LRU_WIDTH), lambda b, i: (b, rev(i), 0)),
                   pl.BlockSpec((1, TM, SC_WIDTH), lambda b, i: (b, i, 0))],
        out_shape=[out_f32, out_f32, jax.ShapeDtypeStruct((bsz, t_all, SC_WIDTH), BF16)],
        scratch_shapes=([pltpu.VMEM((TM // F32_SUBLANES, F32_SUBLANES, LRU_WIDTH), F32)] * 4
                        + [pltpu.VMEM((1, LRU_WIDTH), F32)] * 2),
        compiler_params=_params(32),
        name="lru_scan",
    )(rest, rest, rest, rest, rest, rest, rest, rest, rest, rest,
      conv_w, conv_b, w_gate, b_gate, lam, sc_w)


def _mixer_kernel(qT_ref, k_ref, vT_ref, ctx_ref, x_ref, hf_ref, hb_ref, gg_ref, sc_ref, mod_ref, wo_ref,
                  gffn_ref, x1_ref, h2_ref, edge_ref, oT_ref, s_ref, *, tile_offset, n_key_tiles):
    i = pl.program_id(1) + tile_offset
    ones_rows = jnp.ones((BF16_SUBLANES, TM), BF16)

    def project_out():
        x = x_ref[0]
        if tile_offset == 0:
            x = jnp.where(i == 0, ctx_ref[0], x)
        mod = mod_ref[0, 0]
        gate_mix, shift, scale = mod[2:3], mod[3:4], mod[4:5]
        att = oT_ref[...].T.astype(BF16)
        lru = (gg_ref[0] * (hf_ref[0] + hb_ref[0])).astype(BF16)
        y = (jnp.dot(att, wo_ref[0:ATTN_WIDTH], preferred_element_type=F32)
             + jnp.dot(lru, wo_ref[ATTN_WIDTH:ATTN_WIDTH + LRU_WIDTH], preferred_element_type=F32)
             + jnp.dot(sc_ref[0], wo_ref[ATTN_WIDTH + LRU_WIDTH:], preferred_element_type=F32))
        x1 = x + gate_mix * y
        x1_ref[0] = x1
        h2 = x1 * _rms_scale(x1) * gffn_ref[...]
        h2 = h2 * (1.0 + scale) + shift
        h2_ref[0] = h2.astype(BF16)
        edge_ref[0, 0] = jnp.concatenate(
            [h2[0:1], h2[TM - 1:TM], jnp.zeros((F32_SUBLANES - 2, D_MODEL), F32)], axis=0)

    def run(n_tiles):
        items = [(pr, c) for pr in range(N_HEADS // 2) for c in range(n_tiles)]
        qpads = {}

        def qpad_of(pr):
            if pr not in qpads:
                row0 = pr * 2 * HEAD_DIM
                q2 = jnp.concatenate([qT_ref[0, row0:row0 + HEAD_DIM, :],
                                      qT_ref[0, row0 + HEAD_DIM:row0 + 2 * HEAD_DIM, :]], axis=1)
                zero = jnp.zeros_like(q2)
                first_kv = (2 * pr) // HEADS_PER_KV == 0
                qpads[pr] = jnp.concatenate([q2, zero] if first_kv else [zero, q2], axis=0)
            return qpads[pr]

        def scores(t):
            pr, c = items[t]
            s = jnp.dot(k_ref[0, c * TM:(c + 1) * TM, :], qpad_of(pr), preferred_element_type=F32)
            s_ref[t % SCORE_SLOTS] = s
            return jnp.max(s, axis=0, keepdims=True)

        tile_max = [scores(t) for t in range(min(SCORE_LOOKAHEAD, len(items)))]
        m = acc = None
        for t, (pr, c) in enumerate(items):
            if t + SCORE_LOOKAHEAD < len(items):
                tile_max.append(scores(t + SCORE_LOOKAHEAD))
            if c == 0:
                m = jnp.full((1, 2 * TM), -jnp.inf, F32)
                acc = jnp.zeros((HEAD_DIM + BF16_SUBLANES, 2 * TM), F32)
            g = (2 * pr) // HEADS_PER_KV
            m_new = jnp.maximum(m, tile_max[t])
            alpha = jnp.exp2(m - m_new)
            p = jnp.exp2(s_ref[t % SCORE_SLOTS] - m_new).astype(BF16)
            vt = jnp.concatenate([vT_ref[0, g * HEAD_DIM:(g + 1) * HEAD_DIM, c * TM:(c + 1) * TM],
                                  ones_rows], axis=0)
            acc = alpha * acc + jnp.dot(vt, p, preferred_element_type=F32)
            m = m_new
            tile_max[t] = None
            if c == n_tiles - 1:
                o = acc[:HEAD_DIM] / acc[HEAD_DIM:HEAD_DIM + 1]
                row0 = pr * 2 * HEAD_DIM
                oT_ref[row0:row0 + HEAD_DIM, :] = o[:, :TM]
                oT_ref[row0 + HEAD_DIM:row0 + 2 * HEAD_DIM, :] = o[:, TM:]
        project_out()

    @pl.when(i == 0)
    def _():
        run(1)

    @pl.when(i != 0)
    def _():
        run(n_key_tiles)


def _mixer(qT, k, vT, ctx_arr, x_arr, x_base, hf, hb, rest, sc, mods, w_out, g_ffn, tile_offset):
    bsz, t_all, _ = k.shape
    nt = t_all // TM
    nq = nt - tile_offset
    off = lambda b, i: (b, i + tile_offset, 0)
    return pl.pallas_call(
        functools.partial(_mixer_kernel, tile_offset=tile_offset, n_key_tiles=nt),
        grid=(bsz, nq),
        in_specs=[pl.BlockSpec((1, ATTN_WIDTH, TM), lambda b, i: (b, 0, i + tile_offset)),
                  pl.BlockSpec((1, t_all, KV_WIDTH), lambda b, i: (b, 0, 0)),
                  pl.BlockSpec((1, KV_WIDTH, t_all), lambda b, i: (b, 0, 0))]
                 + _token_specs(x_base, tile_offset) + [
                  pl.BlockSpec((1, TM, LRU_WIDTH), off),
                  pl.BlockSpec((1, TM, LRU_WIDTH), off),
                  pl.BlockSpec((1, TM, LRU_WIDTH), lambda b, i: (b, i + tile_offset, 1)),
                  pl.BlockSpec((1, TM, SC_WIDTH), off),
                  pl.BlockSpec((1, 1, MOD_ROWS, D_MODEL),
                               lambda b, i: (b, jnp.minimum(i + tile_offset, 1), 0, 0)),
                  _const_spec((D_MODEL, D_MODEL)),
                  _const_spec((1, D_MODEL))],
        out_specs=[pl.BlockSpec((1, TM, D_MODEL), lambda b, i: (b, i, 0)),
                   pl.BlockSpec((1, TM, D_MODEL), lambda b, i: (b, i, 0)),
                   pl.BlockSpec((1, 1, F32_SUBLANES, D_MODEL), lambda b, i: (b, i, 0, 0))],
        out_shape=[jax.ShapeDtypeStruct((bsz, nq * TM, D_MODEL), F32),
                   jax.ShapeDtypeStruct((bsz, nq * TM, D_MODEL), BF16),
                   jax.ShapeDtypeStruct((bsz, nq, F32_SUBLANES, D_MODEL), F32)],
        scratch_shapes=[pltpu.VMEM((ATTN_WIDTH, TM), F32), pltpu.VMEM((SCORE_SLOTS, TM, 2 * TM), F32)],
        compiler_params=_params(40),
        name="mixer",
    )(qT, k, vT, ctx_arr, x_arr, hf, hb, rest, sc, mods, w_out, g_ffn)


def _ffn_kernel(h_ref, edge_ref, x1_ref, mod_ref, wup_ref, cw_ref, wdn_ref, gfin_ref,
                o_ref, acc_ref, up_ref, *, nt, final_norm):
    hcat = h_ref[0]
    row = lax.broadcasted_iota(jnp.int32, (TM, FF_CHUNK), 0)
    tile = pl.program_id(0) * nt + pl.program_id(1)
    edge_row = (tile % (F32_SUBLANES // 2)) * 2

    def conv(t, w, cols):
        before = jnp.where(row == 0, edge_ref[pl.ds(edge_row, 1), cols], pltpu.roll(t, 1, 0))
        after = jnp.where(row == TM - 1, edge_ref[pl.ds(edge_row + 1, 1), cols],
                          pltpu.roll(t, TM - 1, 0))
        return w[0:1] * before + w[1:2] * t + w[2:3] * after

    n_chunks = D_FF // FF_CHUNK

    def up_project(c):
        cu = slice(c * FF_CHUNK, (c + 1) * FF_CHUNK)
        cg = slice(D_FF + c * FF_CHUNK, D_FF + (c + 1) * FF_CHUNK)
        up_ref[c % UP_SLOTS, :, :FF_CHUNK] = jnp.dot(hcat, wup_ref[:, cu], preferred_element_type=F32)
        up_ref[c % UP_SLOTS, :, FF_CHUNK:] = jnp.dot(hcat, wup_ref[:, cg], preferred_element_type=F32)

    for c in range(min(UP_LOOKAHEAD, n_chunks)):
        up_project(c)
    for c in range(n_chunks):
        if c + UP_LOOKAHEAD < n_chunks:
            up_project(c + UP_LOOKAHEAD)
        cu = slice(c * FF_CHUNK, (c + 1) * FF_CHUNK)
        cg = slice(D_FF + c * FF_CHUNK, D_FF + (c + 1) * FF_CHUNK)
        u = conv(up_ref[c % UP_SLOTS, :, :FF_CHUNK], cw_ref[:, cu], cu)
        g = conv(up_ref[c % UP_SLOTS, :, FF_CHUNK:], cw_ref[:, cg], cg)
        act = ((g * jax.nn.sigmoid(g)) * u).astype(BF16)
        part = jnp.dot(act, wdn_ref[cu, :], preferred_element_type=F32)
        if c == 0:
            acc_ref[...] = part
        elif c < n_chunks - 1:
            acc_ref[...] += part

    gate_ffn = mod_ref[0, 0][5:6]
    x2 = x1_ref[0] + gate_ffn * (acc_ref[...] + part)
    if final_norm:
        x2 = x2 * _rms_scale(x2) * gfin_ref[...]
    o_ref[0] = x2


def _edge_up_kernel(h_ref, w_ref, o_ref):
    o_ref[...] = jnp.dot(h_ref[...], w_ref[...], preferred_element_type=F32)


def _edge_up(tile_edges, w_up, ctx_tiles):
    bsz, nt = tile_edges.shape[:2]
    first, last = tile_edges[:, :, 0].astype(BF16), tile_edges[:, :, 1].astype(BF16)
    t_idx = jnp.arange(nt)
    has_before = jnp.logical_and(t_idx != 0, t_idx != ctx_tiles)[None, :, None]
    has_after = jnp.logical_and(t_idx != nt - 1, t_idx != ctx_tiles - 1)[None, :, None]
    before = jnp.where(has_before, jnp.roll(last, 1, axis=1), 0)
    after = jnp.where(has_after, jnp.roll(first, -1, axis=1), 0)
    rows = jnp.stack([before, after], axis=2).reshape(bsz * nt * 2, D_MODEL)
    n_rows = rows.shape[0]
    padded = -(-n_rows // BF16_SUBLANES) * BF16_SUBLANES
    rows = jnp.pad(rows, ((0, padded - n_rows), (0, 0)))
    n_blk = 2 * D_FF // EDGE_COLS
    return pl.pallas_call(
        _edge_up_kernel,
        grid=(n_blk,),
        in_specs=[pl.BlockSpec((padded, D_MODEL), lambda j: (0, 0)),
                  pl.BlockSpec((D_MODEL, EDGE_COLS), lambda j: (0, j))],
        out_specs=pl.BlockSpec((padded, EDGE_COLS), lambda j: (0, j)),
        out_shape=jax.ShapeDtypeStruct((padded, 2 * D_FF), F32),
        name="edge_up",
    )(rows, w_up)


def _ffn(h2, tile_edges, x1, mods, w_up, conv_w, w_down, g_final, ctx_tiles, final_norm):
    bsz, t_len, _ = h2.shape
    nt = t_len // TM
    edges = _edge_up(tile_edges, w_up, ctx_tiles)
    tiles_per_block = F32_SUBLANES // 2
    return pl.pallas_call(
        functools.partial(_ffn_kernel, nt=nt, final_norm=final_norm),
        grid=(bsz, nt),
        in_specs=[pl.BlockSpec((1, TM, D_MODEL), lambda b, i: (b, i, 0)),
                  pl.BlockSpec((F32_SUBLANES, 2 * D_FF),
                               lambda b, i: ((b * nt + i) // tiles_per_block, 0)),
                  pl.BlockSpec((1, TM, D_MODEL), lambda b, i: (b, i, 0)),
                  pl.BlockSpec((1, 1, MOD_ROWS, D_MODEL),
                               lambda b, i: (b, jnp.minimum(i + 1 - ctx_tiles, 1), 0, 0)),
                  _const_spec((D_MODEL, 2 * D_FF)),
                  _const_spec((3, 2 * D_FF)),
                  _const_spec((D_FF, D_MODEL)),
                  _const_spec((1, D_MODEL))],
        out_specs=pl.BlockSpec((1, TM, D_MODEL), lambda b, i: (b, i, 0)),
        out_shape=jax.ShapeDtypeStruct((bsz, t_len, D_MODEL), F32),
        scratch_shapes=[pltpu.VMEM((TM, D_MODEL), F32),
                        pltpu.VMEM((UP_SLOTS, TM, 2 * FF_CHUNK), F32)],
        compiler_params=_params(48),
        name="conv_ffn",
    )(h2, edges, x1, mods, w_up, conv_w, w_down, g_final)


def _rope_tables(seq, ctx_len):
    rows = seq // GRID_W
    pos_r = np.repeat(np.arange(rows, dtype=np.float32), GRID_W)
    pos_c = np.tile(np.arange(GRID_W, dtype=np.float32), rows)
    n_f = HEAD_DIM // 4
    inv = (np.float32(ROPE_THETA) ** (-np.arange(n_f, dtype=np.float32) / n_f)).astype(np.float32)
    ang = np.concatenate([pos_r[:, None] * inv, pos_c[:, None] * inv], axis=-1).astype(np.float32)
    cos = np.concatenate([np.ones((ctx_len, HEAD_DIM // 2)), np.cos(ang.astype(np.float64))], axis=0)
    sin = np.concatenate([np.zeros((ctx_len, HEAD_DIM // 2)), np.sin(ang.astype(np.float64))], axis=0)
    return jnp.asarray(cos.T, F32), jnp.asarray(sin.T, F32)


def _gate_weights(wa, ba, wi, bi):
    def dense(w):
        eye = jnp.eye(LRU_BLOCKS, dtype=w.dtype)
        return jnp.einsum("nde,nm->ndme", w, eye).reshape(LRU_WIDTH, LRU_WIDTH)
    w = jnp.concatenate([dense(wa[0]), dense(wi[0]), dense(wa[1]), dense(wi[1])], axis=1)
    b = jnp.concatenate([ba[0], bi[0], ba[1], bi[1]])[None, :]
    return w.astype(BF16), b


def kernel(x, c, ctx, c_ctx, w_mod, b_mod, g_mix, g_ffn, w_in, g_q, g_k, lru_conv_w, lru_conv_b,
           lru_wa, lru_ba, lru_wi, lru_bi, lru_lam, sc_conv_w, w_out, w_up, ffn_conv_w, w_down,
           g_final):
    bsz, seq, _ = x.shape
    ctx_len = ctx.shape[1]
    depth = w_mod.shape[0]
    assert ctx_len == TM and seq % TM == 0 and seq % GRID_W == 0

    cos_t, sin_t = _rope_tables(seq, ctx_len)
    ctx_arr, x_arr, x_base = ctx, x, 0
    cc_rows = -(-(bsz + 1) // F32_SUBLANES) * F32_SUBLANES
    cc = jnp.zeros((cc_rows, D_MODEL), F32).at[:bsz].set(c).at[bsz].set(c_ctx)

    mod_layers = _modulation(cc, w_mod, b_mod)

    out = None
    for l in range(depth):
        last = l == depth - 1
        mod_all = mod_layers[l].reshape(cc_rows, N_MOD, D_MODEL)
        mod_x = mod_all[:bsz]
        mod_c = jnp.broadcast_to(mod_all[bsz][None], (bsz, N_MOD, D_MODEL))
        mods = jnp.stack([mod_c, mod_x], axis=1)
        mods = jnp.pad(mods, ((0, 0), (0, 0), (0, MOD_ROWS - N_MOD), (0, 0)))

        gq = jnp.broadcast_to((g_q[l] * Q_SCALE)[:, None], (HEAD_DIM, TM))
        gk = jnp.broadcast_to(g_k[l][:, None], (HEAD_DIM, TM))
        qT, k, vT, rest = _inproj(ctx_arr, x_arr, x_base, mods, g_mix[l][None, :],
                                  w_in[l].astype(BF16), gq, gk, cos_t, sin_t)

        w_gate, b_gate = _gate_weights(lru_wa[l], lru_ba[l], lru_wi[l], lru_bi[l])
        hf, hb, sc = _lru(rest, lru_conv_w[l], lru_conv_b[l][None, :], w_gate, b_gate, lru_lam[l],
                          sc_conv_w[l])

        tile_offset = 1 if last else 0
        x1, h2, tile_edges = _mixer(qT, k, vT, ctx_arr, x_arr, x_base, hf, hb, rest, sc, mods,
                                    w_out[l].astype(BF16), g_ffn[l][None, :], tile_offset)
        out = _ffn(h2, tile_edges, x1, mods, w_up[l].astype(BF16), ffn_conv_w[l],
                   w_down[l].astype(BF16), g_final[None, :], ctx_tiles=1 - tile_offset,
                   final_norm=last)
        ctx_arr, x_arr, x_base = out, out, 1
    return out
```

```python
import functools

import numpy as np
import jax
import jax.numpy as jnp
from jax import lax
from jax.experimental import pallas as pl
from jax.experimental.pallas import tpu as pltpu

F32 = jnp.float32
BF16 = jnp.bfloat16

D_MODEL = 1024
HEAD_DIM = 64
N_HEADS = 8
N_KV_HEADS = 2
HEADS_PER_KV = N_HEADS // N_KV_HEADS
ATTN_WIDTH = N_HEADS * HEAD_DIM
KV_WIDTH = N_KV_HEADS * HEAD_DIM
LRU_WIDTH = 256
LRU_BLOCKS = 4
LRU_BLOCK = LRU_WIDTH // LRU_BLOCKS
LRU_CONV = 4
LRU_C = 8.0
SC_WIDTH = 256
IN_WIDTH = 2048
D_FF = 2816
FF_CHUNK = 256
EDGE_COLS = 1408
GRID_W = 64
ROPE_THETA = 10000.0
EPS = 1e-6
Q_SCALE = HEAD_DIM ** -0.5 * 1.4426950408889634

TM = 256
SCORE_LOOKAHEAD = 2
SCORE_SLOTS = SCORE_LOOKAHEAD + 1
KEY_CHUNK = 512
UP_LOOKAHEAD = 3
UP_SLOTS = UP_LOOKAHEAD + 1
LANES = 128
F32_SUBLANES = 8
BF16_SUBLANES = 16
N_MOD = 6
MOD_ROWS = 8

OFF_Q = 0
OFF_K = OFF_Q + ATTN_WIDTH
OFF_V = OFF_K + KV_WIDTH
OFF_R = OFF_V + KV_WIDTH
OFF_G = OFF_R + LRU_WIDTH
OFF_B = OFF_G + LRU_WIDTH
OFF_C = OFF_B + SC_WIDTH
OFF_U = OFF_C + SC_WIDTH


def _const_spec(shape):
    nd = len(shape)
    return pl.BlockSpec(shape, lambda *_: (0,) * nd, pipeline_mode=pl.Buffered(1))


def _params(vmem_mb, n_grid_dims=2):
    return pltpu.CompilerParams(dimension_semantics=("arbitrary",) * n_grid_dims,
                                vmem_limit_bytes=vmem_mb * 1024 * 1024)


def _rms_scale(t):
    return lax.rsqrt(jnp.mean(t * t, axis=-1, keepdims=True) + EPS)


def _mod_kernel(c_ref, w_ref, b_ref, o_ref):
    c = c_ref[...]
    h = (c * jax.nn.sigmoid(c)).astype(BF16)
    o_ref[...] = jnp.dot(h, w_ref[...].astype(BF16), preferred_element_type=F32) + b_ref[...]


def _modulation(cc, w_mod, b_mod):
    rows = cc.shape[0]
    depth = w_mod.shape[0]
    return pl.pallas_call(
        _mod_kernel,
        grid=(depth, N_MOD),
        in_specs=[pl.BlockSpec((rows, D_MODEL), lambda l, j: (0, 0)),
                  pl.BlockSpec((None, D_MODEL, D_MODEL), lambda l, j: (l, 0, j)),
                  pl.BlockSpec((None, 1, D_MODEL), lambda l, j: (l, 0, j))],
        out_specs=pl.BlockSpec((None, rows, D_MODEL), lambda l, j: (l, 0, j)),
        out_shape=jax.ShapeDtypeStruct((depth, rows, N_MOD * D_MODEL), F32),
        name="modulation",
    )(cc, w_mod, b_mod[:, None, :])


def _norm_rope_transposed(t, gain, cos, sin):
    half = HEAD_DIM // 2
    out = []
    for h in range(t.shape[0] // HEAD_DIM):
        r = t[h * HEAD_DIM:(h + 1) * HEAD_DIM]
        r = r * lax.rsqrt(jnp.mean(r * r, axis=0, keepdims=True) + EPS) * gain
        x1, x2 = r[:half], r[half:]
        out += [x1 * cos - x2 * sin, x1 * sin + x2 * cos]
    return jnp.concatenate(out, axis=0)


def _inproj_kernel(ctx_ref, x_ref, mod_ref, gmix_ref, w_ref, gq_ref, gk_ref, cos_ref, sin_ref,
                   qT_ref, k_ref, vT_ref, rest_ref, y_even_ref, y_odd_ref, *, nt, n_tiles):
    s = pl.program_id(0)
    tile = jnp.minimum(s, n_tiles - 1) % nt

    def project(y_ref):
        x = jnp.where(tile == 0, ctx_ref[0], x_ref[0])
        mod = mod_ref[0, 0]
        shift, scale = mod[0:1], mod[1:2]
        h = x * _rms_scale(x) * gmix_ref[...]
        h = h * (1.0 + scale) + shift
        y_ref[...] = jnp.dot(h.astype(BF16), w_ref[...], preferred_element_type=F32)

    def finish(y_ref):
        cos, sin = cos_ref[...], sin_ref[...]
        q = _norm_rope_transposed(y_ref[:, OFF_Q:OFF_Q + ATTN_WIDTH].T, gq_ref[...], cos, sin)
        qT_ref[0] = q.astype(BF16)
        k = _norm_rope_transposed(y_ref[:, OFF_K:OFF_K + KV_WIDTH].T, gk_ref[...], cos, sin)
        k_ref[0] = k.T.astype(BF16)
        vT_ref[0] = y_ref[:, OFF_V:OFF_V + KV_WIDTH].T.astype(BF16)
        rest_ref[0, :, 0:LRU_WIDTH] = y_ref[:, OFF_R:OFF_R + LRU_WIDTH]
        rest_ref[0, :, LRU_WIDTH:2 * LRU_WIDTH] = jax.nn.gelu(y_ref[:, OFF_G:OFF_G + LRU_WIDTH])
        rest_ref[0, :, 2 * LRU_WIDTH:2 * LRU_WIDTH + SC_WIDTH] = y_ref[:, OFF_B:OFF_B + SC_WIDTH]
        rest_ref[0, :, 2 * LRU_WIDTH + SC_WIDTH:] = (y_ref[:, OFF_C:OFF_C + SC_WIDTH]
                                                     * y_ref[:, OFF_U:OFF_U + SC_WIDTH])

    @pl.when(s == 0)
    def _():
        y_odd_ref[...] = jnp.zeros_like(y_odd_ref)

    @pl.when(s % 2 == 0)
    def _():
        finish(y_odd_ref)
        project(y_even_ref)

    @pl.when(s % 2 == 1)
    def _():
        finish(y_even_ref)
        project(y_odd_ref)


def _token_specs(x_base, tile_offset=0):
    return [pl.BlockSpec((1, TM, D_MODEL), lambda b, i: (b, 0, 0)),
            pl.BlockSpec((1, TM, D_MODEL),
                         lambda b, i: (b, x_base + jnp.maximum(i + tile_offset - 1, 0), 0))]


def _inproj(ctx_arr, x_arr, x_base, mods, g_mix, w_in, gq, gk, cos_t, sin_t):
    bsz = x_arr.shape[0]
    nt = x_arr.shape[1] // TM - x_base + 1
    t_all = nt * TM
    n_tiles = bsz * nt
    rest_w = 2 * LRU_WIDTH + 2 * SC_WIDTH

    def cur(s):
        sc = jnp.minimum(s, n_tiles - 1)
        return sc // nt, sc % nt

    def prev(s):
        sp = jnp.maximum(s - 1, 0)
        return sp // nt, sp % nt

    rope_spec = pl.BlockSpec((HEAD_DIM // 2, TM), lambda s: (0, prev(s)[1]))
    return pl.pallas_call(
        functools.partial(_inproj_kernel, nt=nt, n_tiles=n_tiles),
        grid=(n_tiles + 1,),
        in_specs=[pl.BlockSpec((1, TM, D_MODEL), lambda s: (cur(s)[0], 0, 0)),
                  pl.BlockSpec((1, TM, D_MODEL),
                               lambda s: (cur(s)[0], x_base + jnp.maximum(cur(s)[1] - 1, 0), 0)),
                  pl.BlockSpec((1, 1, MOD_ROWS, D_MODEL),
                               lambda s: (cur(s)[0], jnp.minimum(cur(s)[1], 1), 0, 0)),
                  _const_spec((1, D_MODEL)),
                  _const_spec((D_MODEL, IN_WIDTH)),
                  _const_spec((HEAD_DIM, TM)),
                  _const_spec((HEAD_DIM, TM)),
                  rope_spec,
                  rope_spec],
        out_specs=[pl.BlockSpec((1, ATTN_WIDTH, TM), lambda s: (prev(s)[0], 0, prev(s)[1])),
                   pl.BlockSpec((1, TM, KV_WIDTH), lambda s: (prev(s)[0], prev(s)[1], 0)),
                   pl.BlockSpec((1, KV_WIDTH, TM), lambda s: (prev(s)[0], 0, prev(s)[1])),
                   pl.BlockSpec((1, TM, rest_w), lambda s: (prev(s)[0], prev(s)[1], 0))],
        out_shape=[jax.ShapeDtypeStruct((bsz, ATTN_WIDTH, t_all), BF16),
                   jax.ShapeDtypeStruct((bsz, t_all, KV_WIDTH), BF16),
                   jax.ShapeDtypeStruct((bsz, KV_WIDTH, t_all), BF16),
                   jax.ShapeDtypeStruct((bsz, t_all, rest_w), F32)],
        scratch_shapes=[pltpu.VMEM((TM, IN_WIDTH), F32)] * 2,
        compiler_params=_params(40, n_grid_dims=1),
        name="inproj",
    )(ctx_arr, x_arr, mods, g_mix, w_in, gq, gk, cos_t, sin_t)


def _with_halo(prev_ref, cur_ref, next_ref, valid_prev, valid_next):
    prev = prev_ref[0] * valid_prev.astype(F32)
    nxt = next_ref[0] * valid_next.astype(F32)
    return jnp.concatenate([prev, cur_ref[0], nxt], axis=0)


def _shifted(tcat, offset):
    rows = tcat.shape[0]
    if offset == 0:
        return tcat[F32_SUBLANES:F32_SUBLANES + TM]
    return pltpu.roll(tcat, (-offset) % rows, 0)[F32_SUBLANES:F32_SUBLANES + TM]


def _lru_coeffs(rcat, conv_w, conv_b, w_gate, b_gate, sp_lam):
    xc = conv_b
    for j in range(LRU_CONV):
        xc = xc + conv_w[j:j + 1] * _shifted(rcat, j - 2)
    z = jnp.dot(xc.astype(BF16), w_gate, preferred_element_type=F32) + b_gate
    r_gate = jax.nn.sigmoid(z[:, :LRU_WIDTH])
    i_gate = jax.nn.sigmoid(z[:, LRU_WIDTH:])
    log_a = (-LRU_C) * r_gate * sp_lam
    a = jnp.exp(log_a)
    one_minus_a2 = -jnp.tanh(log_a) * (1.0 + a * a)
    return a, jnp.sqrt(one_minus_a2) * (i_gate * xc)


def _lru_kernel(rf_ref, rfp_ref, rfn_ref, rb_ref, rbp_ref, rbn_ref, bg_ref, cu_ref, cup_ref, cun_ref,
                cw_ref, cb_ref, wg_ref, bgate_ref, lam_ref, scw_ref,
                hf_ref, hb_ref, sc_ref,
                af_s, bf_s, ab_s, bb_s, hf_state, hb_state, *, nt):
    i = pl.program_id(1)
    j = jnp.where(i == 0, 0, nt - i)

    @pl.when(i == 0)
    def _():
        hf_state[...] = jnp.zeros_like(hf_state)
        hb_state[...] = jnp.zeros_like(hb_state)

    def halo_valid(t):
        return jnp.logical_and(t != 0, t != 1), jnp.logical_and(t != 0, t != nt - 1)

    lam = lam_ref[...]
    sp_lam = jnp.maximum(-lam, 0.0) + jnp.log1p(jnp.exp(-jnp.abs(lam)))
    cw = cw_ref[...]
    cb = cb_ref[...]

    vp, vn = halo_valid(i)
    a, b = _lru_coeffs(_with_halo(rfp_ref, rf_ref, rfn_ref, vp, vn), cw, cb,
                       wg_ref[:, :2 * LRU_WIDTH], bgate_ref[:, :2 * LRU_WIDTH], sp_lam[0:1])
    _block_prefix(a, b, af_s, bf_s, reverse=False)
    cucat = _with_halo(cup_ref, cu_ref, cun_ref, vp, vn)
    scw = scw_ref[...]
    conv = scw[0:1] * _shifted(cucat, -1) + scw[1:2] * _shifted(cucat, 0) + scw[2:3] * _shifted(cucat, 1)
    sc_ref[0] = (bg_ref[0] * conv).astype(BF16)

    vp, vn = halo_valid(j)
    a, b = _lru_coeffs(_with_halo(rbp_ref, rb_ref, rbn_ref, vp, vn), cw, cb,
                       wg_ref[:, 2 * LRU_WIDTH:], bgate_ref[:, 2 * LRU_WIDTH:], sp_lam[1:2])
    _block_prefix(a, b, ab_s, bb_s, reverse=True)

    _carry_blocks(af_s, bf_s, hf_ref, hf_state, reverse=False)
    _carry_blocks(ab_s, bb_s, hb_ref, hb_state, reverse=True)


def _block_prefix(a, b, a_s, b_s, reverse):
    n = F32_SUBLANES
    a = a.reshape(TM // n, n, LRU_WIDTH)
    b = b.reshape(TM // n, n, LRU_WIDTH)
    row = lax.broadcasted_iota(jnp.int32, a.shape, 1)
    d = 1
    while d < n:
        shift, keep = (n - d, row < n - d) if reverse else (d, row >= d)
        a_prev = jnp.where(keep, pltpu.roll(a, shift, 1), 1.0)
        b_prev = jnp.where(keep, pltpu.roll(b, shift, 1), 0.0)
        b = a * b_prev + b
        a = a * a_prev
        d *= 2
    a_s[...] = a
    b_s[...] = b


def _carry_blocks(a_s, b_s, out_ref, state_ref, reverse):
    n = F32_SUBLANES
    nb = TM // n
    last = 0 if reverse else n - 1
    h_in = state_ref[...]
    for v in (reversed(range(nb)) if reverse else range(nb)):
        h = a_s[v] * h_in + b_s[v]
        out_ref[0, v * n:(v + 1) * n, :] = h
        h_in = h[last:last + 1]
    state_ref[...] = h_in


def _lru(rest, conv_w, conv_b, w_gate, b_gate, lam, sc_w):
    bsz, t_all, _ = rest.shape
    nt = t_all // TM
    hpt = TM // F32_SUBLANES
    nhb = t_all // F32_SUBLANES

    def rev(i):
        return jnp.where(i == 0, 0, nt - i)

    def tile(col, order):
        return pl.BlockSpec((1, TM, LRU_WIDTH), lambda b, i: (b, order(i), col))

    def prev_halo(col, order):
        return pl.BlockSpec((1, F32_SUBLANES, LRU_WIDTH),
                            lambda b, i: (b, jnp.maximum(order(i) * hpt - 1, 0), col))

    def next_halo(col, order):
        return pl.BlockSpec((1, F32_SUBLANES, LRU_WIDTH),
                            lambda b, i: (b, jnp.minimum((order(i) + 1) * hpt, nhb - 1), col))

    fwd = lambda i: i
    out_f32 = jax.ShapeDtypeStruct((bsz, t_all, LRU_WIDTH), F32)
    return pl.pallas_call(
        functools.partial(_lru_kernel, nt=nt),
        grid=(bsz, nt),
        in_specs=[tile(0, fwd), prev_halo(0, fwd), next_halo(0, fwd),
                  tile(0, rev), prev_halo(0, rev), next_halo(0, rev),
                  tile(2, fwd),
                  tile(3, fwd), prev_halo(3, fwd), next_halo(3, fwd),
                  _const_spec((LRU_CONV, LRU_WIDTH)),
                  _const_spec((1, LRU_WIDTH)),
                  _const_spec((LRU_WIDTH, 4 * LRU_WIDTH)),
                  _const_spec((1, 4 * LRU_WIDTH)),
                  _const_spec((2, LRU_WIDTH)),
                  _const_spec((3, SC_WIDTH))],
        out_specs=[pl.BlockSpec((1, TM, LRU_WIDTH), lambda b, i: (b, i, 0)),
                   pl.BlockSpec((1, TM, LRU_WIDTH), lambda b, i: (b, rev(i), 0)),
                   pl.BlockSpec((1, TM, SC_WIDTH), lambda b, i: (b, i, 0))],
        out_shape=[out_f32, out_f32, jax.ShapeDtypeStruct((bsz, t_all, SC_WIDTH), BF16)],
        scratch_shapes=([pltpu.VMEM((TM // F32_SUBLANES, F32_SUBLANES, LRU_WIDTH), F32)] * 4
                        + [pltpu.VMEM((1, LRU_WIDTH), F32)] * 2),
        compiler_params=_params(32),
        name="lru_scan",
    )(rest, rest, rest, rest, rest, rest, rest, rest, rest, rest,
      conv_w, conv_b, w_gate, b_gate, lam, sc_w)


def _mixer_kernel(qT_ref, k_ref, vT_ref, ctx_ref, x_ref, hf_ref, hb_ref, gg_ref, sc_ref, mod_ref, wo_ref,
                  gffn_ref, x1_ref, h2_ref, edge_ref, oT_ref, s_ref, *, tile_offset, n_key_tiles):
    i = pl.program_id(1) + tile_offset

    def project_out():
        x = x_ref[0]
        if tile_offset == 0:
            x = jnp.where(i == 0, ctx_ref[0], x)
        mod = mod_ref[0, 0]
        gate_mix, shift, scale = mod[2:3], mod[3:4], mod[4:5]
        att = oT_ref[...].T.astype(BF16)
        lru = (gg_ref[0] * (hf_ref[0] + hb_ref[0])).astype(BF16)
        y = (jnp.dot(att, wo_ref[0:ATTN_WIDTH], preferred_element_type=F32)
             + jnp.dot(lru, wo_ref[ATTN_WIDTH:ATTN_WIDTH + LRU_WIDTH], preferred_element_type=F32)
             + jnp.dot(sc_ref[0], wo_ref[ATTN_WIDTH + LRU_WIDTH:], preferred_element_type=F32))
        x1 = x + gate_mix * y
        x1_ref[0] = x1
        h2 = x1 * _rms_scale(x1) * gffn_ref[...]
        h2 = h2 * (1.0 + scale) + shift
        h2_ref[0] = h2.astype(BF16)
        edge_ref[0, 0] = jnp.concatenate(
            [h2[0:1], h2[TM - 1:TM], jnp.zeros((F32_SUBLANES - 2, D_MODEL), F32)], axis=0)

    def run(n_tiles):
        chunks = [(0, TM)] + [(k0, KEY_CHUNK) for k0 in range(TM, n_tiles * TM, KEY_CHUNK)]
        items = [(pr, c) for pr in range(N_HEADS // 2) for c in range(len(chunks))]
        qpads = {}

        def qpad_of(pr):
            if pr not in qpads:
                row0 = pr * 2 * HEAD_DIM
                q2 = jnp.concatenate([qT_ref[0, row0:row0 + HEAD_DIM, :],
                                      qT_ref[0, row0 + HEAD_DIM:row0 + 2 * HEAD_DIM, :]], axis=1)
                zero = jnp.zeros_like(q2)
                first_kv = (2 * pr) // HEADS_PER_KV == 0
                qpads[pr] = jnp.concatenate([q2, zero] if first_kv else [zero, q2], axis=0)
            return qpads[pr]

        def scores(t):
            pr, c = items[t]
            k0, kn = chunks[c]
            s = jnp.dot(k_ref[0, k0:k0 + kn, :], qpad_of(pr), preferred_element_type=F32)
            s_ref[t % SCORE_SLOTS, 0:kn, :] = s
            return jnp.max(s, axis=0, keepdims=True)

        tile_max = [scores(t) for t in range(min(SCORE_LOOKAHEAD, len(items)))]
        m = acc = None
        for t, (pr, c) in enumerate(items):
            if t + SCORE_LOOKAHEAD < len(items):
                tile_max.append(scores(t + SCORE_LOOKAHEAD))
            if c == 0:
                m = jnp.full((1, 2 * TM), -jnp.inf, F32)
                acc = jnp.zeros((HEAD_DIM + BF16_SUBLANES, 2 * TM), F32)
            g = (2 * pr) // HEADS_PER_KV
            k0, kn = chunks[c]
            m_new = jnp.maximum(m, tile_max[t])
            alpha = jnp.exp2(m - m_new)
            p = jnp.exp2(s_ref[t % SCORE_SLOTS, 0:kn, :] - m_new).astype(BF16)
            vt = jnp.concatenate([vT_ref[0, g * HEAD_DIM:(g + 1) * HEAD_DIM, k0:k0 + kn],
                                  jnp.ones((BF16_SUBLANES, kn), BF16)], axis=0)
            acc = alpha * acc + jnp.dot(vt, p, preferred_element_type=F32)
            m = m_new
            tile_max[t] = None
            if c == len(chunks) - 1:
                o = acc[:HEAD_DIM] / acc[HEAD_DIM:HEAD_DIM + 1]
                row0 = pr * 2 * HEAD_DIM
                oT_ref[row0:row0 + HEAD_DIM, :] = o[:, :TM]
                oT_ref[row0 + HEAD_DIM:row0 + 2 * HEAD_DIM, :] = o[:, TM:]
        project_out()

    @pl.when(i == 0)
    def _():
        run(1)

    @pl.when(i != 0)
    def _():
        run(n_key_tiles)


def _mixer(qT, k, vT, ctx_arr, x_arr, x_base, hf, hb, rest, sc, mods, w_out, g_ffn, tile_offset):
    bsz, t_all, _ = k.shape
    nt = t_all // TM
    nq = nt - tile_offset
    off = lambda b, i: (b, i + tile_offset, 0)
    return pl.pallas_call(
        functools.partial(_mixer_kernel, tile_offset=tile_offset, n_key_tiles=nt),
        grid=(bsz, nq),
        in_specs=[pl.BlockSpec((1, ATTN_WIDTH, TM), lambda b, i: (b, 0, i + tile_offset)),
                  pl.BlockSpec((1, t_all, KV_WIDTH), lambda b, i: (b, 0, 0)),
                  pl.BlockSpec((1, KV_WIDTH, t_all), lambda b, i: (b, 0, 0))]
                 + _token_specs(x_base, tile_offset) + [
                  pl.BlockSpec((1, TM, LRU_WIDTH), off),
                  pl.BlockSpec((1, TM, LRU_WIDTH), off),
                  pl.BlockSpec((1, TM, LRU_WIDTH), lambda b, i: (b, i + tile_offset, 1)),
                  pl.BlockSpec((1, TM, SC_WIDTH), off),
                  pl.BlockSpec((1, 1, MOD_ROWS, D_MODEL),
                               lambda b, i: (b, jnp.minimum(i + tile_offset, 1), 0, 0)),
                  _const_spec((D_MODEL, D_MODEL)),
                  _const_spec((1, D_MODEL))],
        out_specs=[pl.BlockSpec((1, TM, D_MODEL), lambda b, i: (b, i, 0)),
                   pl.BlockSpec((1, TM, D_MODEL), lambda b, i: (b, i, 0)),
                   pl.BlockSpec((1, 1, F32_SUBLANES, D_MODEL), lambda b, i: (b, i, 0, 0))],
        out_shape=[jax.ShapeDtypeStruct((bsz, nq * TM, D_MODEL), F32),
                   jax.ShapeDtypeStruct((bsz, nq * TM, D_MODEL), BF16),
                   jax.ShapeDtypeStruct((bsz, nq, F32_SUBLANES, D_MODEL), F32)],
        scratch_shapes=[pltpu.VMEM((ATTN_WIDTH, TM), F32), pltpu.VMEM((SCORE_SLOTS, KEY_CHUNK, 2 * TM), F32)],
        compiler_params=_params(40),
        name="mixer",
    )(qT, k, vT, ctx_arr, x_arr, hf, hb, rest, sc, mods, w_out, g_ffn)


def _ffn_kernel(h_ref, edge_ref, x1_ref, mod_ref, wup_ref, cw_ref, wdn_ref, gfin_ref,
                o_ref, acc_ref, up_ref, *, nt, final_norm):
    hcat = h_ref[0]
    row = lax.broadcasted_iota(jnp.int32, (TM, FF_CHUNK), 0)
    tile = pl.program_id(0) * nt + pl.program_id(1)
    edge_row = (tile % (F32_SUBLANES // 2)) * 2

    def conv(t, w, cols):
        before = jnp.where(row == 0, edge_ref[pl.ds(edge_row, 1), cols], pltpu.roll(t, 1, 0))
        after = jnp.where(row == TM - 1, edge_ref[pl.ds(edge_row + 1, 1), cols],
                          pltpu.roll(t, TM - 1, 0))
        return w[0:1] * before + w[1:2] * t + w[2:3] * after

    n_chunks = D_FF // FF_CHUNK

    def up_project(c):
        cu = slice(c * FF_CHUNK, (c + 1) * FF_CHUNK)
        cg = slice(D_FF + c * FF_CHUNK, D_FF + (c + 1) * FF_CHUNK)
        up_ref[c % UP_SLOTS, :, :FF_CHUNK] = jnp.dot(hcat, wup_ref[:, cu], preferred_element_type=F32)
        up_ref[c % UP_SLOTS, :, FF_CHUNK:] = jnp.dot(hcat, wup_ref[:, cg], preferred_element_type=F32)

    for c in range(min(UP_LOOKAHEAD, n_chunks)):
        up_project(c)
    for c in range(n_chunks):
        if c + UP_LOOKAHEAD < n_chunks:
            up_project(c + UP_LOOKAHEAD)
        cu = slice(c * FF_CHUNK, (c + 1) * FF_CHUNK)
        cg = slice(D_FF + c * FF_CHUNK, D_FF + (c + 1) * FF_CHUNK)
        u = conv(up_ref[c % UP_SLOTS, :, :FF_CHUNK], cw_ref[:, cu], cu)
        g = conv(up_ref[c % UP_SLOTS, :, FF_CHUNK:], cw_ref[:, cg], cg)
        act = ((g * jax.nn.sigmoid(g)) * u).astype(BF16)
        part = jnp.dot(act, wdn_ref[cu, :], preferred_element_type=F32)
        if c == 0:
            acc_ref[...] = part
        elif c < n_chunks - 1:
            acc_ref[...] += part

    gate_ffn = mod_ref[0, 0][5:6]
    x2 = x1_ref[0] + gate_ffn * (acc_ref[...] + part)
    if final_norm:
        x2 = x2 * _rms_scale(x2) * gfin_ref[...]
    o_ref[0] = x2


def _edge_up_kernel(h_ref, w_ref, o_ref):
    o_ref[...] = jnp.dot(h_ref[...], w_ref[...], preferred_element_type=F32)


def _edge_up(tile_edges, w_up, ctx_tiles):
    bsz, nt = tile_edges.shape[:2]
    first, last = tile_edges[:, :, 0].astype(BF16), tile_edges[:, :, 1].astype(BF16)
    t_idx = jnp.arange(nt)
    has_before = jnp.logical_and(t_idx != 0, t_idx != ctx_tiles)[None, :, None]
    has_after = jnp.logical_and(t_idx != nt - 1, t_idx != ctx_tiles - 1)[None, :, None]
    before = jnp.where(has_before, jnp.roll(last, 1, axis=1), 0)
    after = jnp.where(has_after, jnp.roll(first, -1, axis=1), 0)
    rows = jnp.stack([before, after], axis=2).reshape(bsz * nt * 2, D_MODEL)
    n_rows = rows.shape[0]
    padded = -(-n_rows // BF16_SUBLANES) * BF16_SUBLANES
    rows = jnp.pad(rows, ((0, padded - n_rows), (0, 0)))
    n_blk = 2 * D_FF // EDGE_COLS
    return pl.pallas_call(
        _edge_up_kernel,
        grid=(n_blk,),
        in_specs=[pl.BlockSpec((padded, D_MODEL), lambda j: (0, 0)),
                  pl.BlockSpec((D_MODEL, EDGE_COLS), lambda j: (0, j))],
        out_specs=pl.BlockSpec((padded, EDGE_COLS), lambda j: (0, j)),
        out_shape=jax.ShapeDtypeStruct((padded, 2 * D_FF), F32),
        name="edge_up",
    )(rows, w_up)


def _ffn(h2, tile_edges, x1, mods, w_up, conv_w, w_down, g_final, ctx_tiles, final_norm):
    bsz, t_len, _ = h2.shape
    nt = t_len // TM
    edges = _edge_up(tile_edges, w_up, ctx_tiles)
    tiles_per_block = F32_SUBLANES // 2
    return pl.pallas_call(
        functools.partial(_ffn_kernel, nt=nt, final_norm=final_norm),
        grid=(bsz, nt),
        in_specs=[pl.BlockSpec((1, TM, D_MODEL), lambda b, i: (b, i, 0)),
                  pl.BlockSpec((F32_SUBLANES, 2 * D_FF),
                               lambda b, i: ((b * nt + i) // tiles_per_block, 0)),
                  pl.BlockSpec((1, TM, D_MODEL), lambda b, i: (b, i, 0)),
                  pl.BlockSpec((1, 1, MOD_ROWS, D_MODEL),
                               lambda b, i: (b, jnp.minimum(i + 1 - ctx_tiles, 1), 0, 0)),
                  _const_spec((D_MODEL, 2 * D_FF)),
                  _const_spec((3, 2 * D_FF)),
                  _const_spec((D_FF, D_MODEL)),
                  _const_spec((1, D_MODEL))],
        out_specs=pl.BlockSpec((1, TM, D_MODEL), lambda b, i: (b, i, 0)),
        out_shape=jax.ShapeDtypeStruct((bsz, t_len, D_MODEL), F32),
        scratch_shapes=[pltpu.VMEM((TM, D_MODEL), F32),
                        pltpu.VMEM((UP_SLOTS, TM, 2 * FF_CHUNK), F32)],
        compiler_params=_params(48),
        name="conv_ffn",
    )(h2, edges, x1, mods, w_up, conv_w, w_down, g_final)


def _rope_tables(seq, ctx_len):
    rows = seq // GRID_W
    pos_r = np.repeat(np.arange(rows, dtype=np.float32), GRID_W)
    pos_c = np.tile(np.arange(GRID_W, dtype=np.float32), rows)
    n_f = HEAD_DIM // 4
    inv = (np.float32(ROPE_THETA) ** (-np.arange(n_f, dtype=np.float32) / n_f)).astype(np.float32)
    ang = np.concatenate([pos_r[:, None] * inv, pos_c[:, None] * inv], axis=-1).astype(np.float32)
    cos = np.concatenate([np.ones((ctx_len, HEAD_DIM // 2)), np.cos(ang.astype(np.float64))], axis=0)
    sin = np.concatenate([np.zeros((ctx_len, HEAD_DIM // 2)), np.sin(ang.astype(np.float64))], axis=0)
    return jnp.asarray(cos.T, F32), jnp.asarray(sin.T, F32)


def _gate_weights(wa, ba, wi, bi):
    def dense(w):
        eye = jnp.eye(LRU_BLOCKS, dtype=w.dtype)
        return jnp.einsum("nde,nm->ndme", w, eye).reshape(LRU_WIDTH, LRU_WIDTH)
    w = jnp.concatenate([dense(wa[0]), dense(wi[0]), dense(wa[1]), dense(wi[1])], axis=1)
    b = jnp.concatenate([ba[0], bi[0], ba[1], bi[1]])[None, :]
    return w.astype(BF16), b


def kernel(x, c, ctx, c_ctx, w_mod, b_mod, g_mix, g_ffn, w_in, g_q, g_k, lru_conv_w, lru_conv_b,
           lru_wa, lru_ba, lru_wi, lru_bi, lru_lam, sc_conv_w, w_out, w_up, ffn_conv_w, w_down,
           g_final):
    bsz, seq, _ = x.shape
    ctx_len = ctx.shape[1]
    depth = w_mod.shape[0]
    assert ctx_len == TM and seq % TM == 0 and seq % GRID_W == 0

    cos_t, sin_t = _rope_tables(seq, ctx_len)
    ctx_arr, x_arr, x_base = ctx, x, 0
    cc_rows = -(-(bsz + 1) // F32_SUBLANES) * F32_SUBLANES
    cc = jnp.zeros((cc_rows, D_MODEL), F32).at[:bsz].set(c).at[bsz].set(c_ctx)

    mod_layers = _modulation(cc, w_mod, b_mod)

    out = None
    for l in range(depth):
        last = l == depth - 1
        mod_all = mod_layers[l].reshape(cc_rows, N_MOD, D_MODEL)
        mod_x = mod_all[:bsz]
        mod_c = jnp.broadcast_to(mod_all[bsz][None], (bsz, N_MOD, D_MODEL))
        mods = jnp.stack([mod_c, mod_x], axis=1)
        mods = jnp.pad(mods, ((0, 0), (0, 0), (0, MOD_ROWS - N_MOD), (0, 0)))

        gq = jnp.broadcast_to((g_q[l] * Q_SCALE)[:, None], (HEAD_DIM, TM))
        gk = jnp.broadcast_to(g_k[l][:, None], (HEAD_DIM, TM))
        qT, k, vT, rest = _inproj(ctx_arr, x_arr, x_base, mods, g_mix[l][None, :],
                                  w_in[l].astype(BF16), gq, gk, cos_t, sin_t)

        w_gate, b_gate = _gate_weights(lru_wa[l], lru_ba[l], lru_wi[l], lru_bi[l])
        hf, hb, sc = _lru(rest, lru_conv_w[l], lru_conv_b[l][None, :], w_gate, b_gate, lru_lam[l],
                          sc_conv_w[l])

        tile_offset = 1 if last else 0
        x1, h2, tile_edges = _mixer(qT, k, vT, ctx_arr, x_arr, x_base, hf, hb, rest, sc, mods,
                                    w_out[l].astype(BF16), g_ffn[l][None, :], tile_offset)
        out = _ffn(h2, tile_edges, x1, mods, w_up[l].astype(BF16), ffn_conv_w[l],
                   w_down[l].astype(BF16), g_final[None, :], ctx_tiles=1 - tile_offset,
                   final_norm=last)
        ctx_arr, x_arr, x_base = out, out, 1
    return out
```

```python
import functools

import numpy as np
import jax
import jax.numpy as jnp
from jax import lax
from jax.experimental import pallas as pl
from jax.experimental.pallas import tpu as pltpu

F32 = jnp.float32
BF16 = jnp.bfloat16

D_MODEL = 1024
HEAD_DIM = 64
N_HEADS = 8
N_KV_HEADS = 2
HEADS_PER_KV = N_HEADS // N_KV_HEADS
ATTN_WIDTH = N_HEADS * HEAD_DIM
KV_WIDTH = N_KV_HEADS * HEAD_DIM
LRU_WIDTH = 256
LRU_BLOCKS = 4
LRU_BLOCK = LRU_WIDTH // LRU_BLOCKS
LRU_CONV = 4
LRU_C = 8.0
SC_WIDTH = 256
IN_WIDTH = 2048
D_FF = 2816
FF_CHUNK = 256
EDGE_COLS = 1408
GRID_W = 64
ROPE_THETA = 10000.0
EPS = 1e-6
Q_SCALE = HEAD_DIM ** -0.5 * 1.4426950408889634

TM = 256
SCORE_LOOKAHEAD = 3
SCORE_SLOTS = SCORE_LOOKAHEAD + 1
UP_LOOKAHEAD = 3
UP_SLOTS = UP_LOOKAHEAD + 1
LANES = 128
F32_SUBLANES = 8
BF16_SUBLANES = 16
N_MOD = 6
MOD_ROWS = 8

OFF_Q = 0
OFF_K = OFF_Q + ATTN_WIDTH
OFF_V = OFF_K + KV_WIDTH
OFF_R = OFF_V + KV_WIDTH
OFF_G = OFF_R + LRU_WIDTH
OFF_B = OFF_G + LRU_WIDTH
OFF_C = OFF_B + SC_WIDTH
OFF_U = OFF_C + SC_WIDTH


def _const_spec(shape):
    nd = len(shape)
    return pl.BlockSpec(shape, lambda *_: (0,) * nd, pipeline_mode=pl.Buffered(1))


def _params(vmem_mb, n_grid_dims=2):
    return pltpu.CompilerParams(dimension_semantics=("arbitrary",) * n_grid_dims,
                                vmem_limit_bytes=vmem_mb * 1024 * 1024)


def _rms_scale(t):
    return lax.rsqrt(jnp.mean(t * t, axis=-1, keepdims=True) + EPS)


def _mod_kernel(c_ref, w_ref, b_ref, o_ref):
    c = c_ref[...]
    h = (c * jax.nn.sigmoid(c)).astype(BF16)
    o_ref[...] = jnp.dot(h, w_ref[...].astype(BF16), preferred_element_type=F32) + b_ref[...]


def _modulation(cc, w_mod, b_mod):
    rows = cc.shape[0]
    depth = w_mod.shape[0]
    return pl.pallas_call(
        _mod_kernel,
        grid=(depth, N_MOD),
        in_specs=[pl.BlockSpec((rows, D_MODEL), lambda l, j: (0, 0)),
                  pl.BlockSpec((None, D_MODEL, D_MODEL), lambda l, j: (l, 0, j)),
                  pl.BlockSpec((None, 1, D_MODEL), lambda l, j: (l, 0, j))],
        out_specs=pl.BlockSpec((None, rows, D_MODEL), lambda l, j: (l, 0, j)),
        out_shape=jax.ShapeDtypeStruct((depth, rows, N_MOD * D_MODEL), F32),
        name="modulation",
    )(cc, w_mod, b_mod[:, None, :])


def _norm_rope_transposed(t, gain, cos, sin):
    half = HEAD_DIM // 2
    out = []
    for h in range(t.shape[0] // HEAD_DIM):
        r = t[h * HEAD_DIM:(h + 1) * HEAD_DIM]
        r = r * lax.rsqrt(jnp.mean(r * r, axis=0, keepdims=True) + EPS) * gain
        x1, x2 = r[:half], r[half:]
        out += [x1 * cos - x2 * sin, x1 * sin + x2 * cos]
    return jnp.concatenate(out, axis=0)


def _inproj_kernel(ctx_ref, x_ref, mod_ref, gmix_ref, w_ref, gq_ref, gk_ref, cos_ref, sin_ref,
                   qT_ref, k_ref, vT_ref, rest_ref, y_even_ref, y_odd_ref, *, nt, n_tiles):
    s = pl.program_id(0)
    tile = jnp.minimum(s, n_tiles - 1) % nt

    def project(y_ref):
        x = jnp.where(tile == 0, ctx_ref[0], x_ref[0])
        mod = mod_ref[0, 0]
        shift, scale = mod[0:1], mod[1:2]
        h = x * _rms_scale(x) * gmix_ref[...]
        h = h * (1.0 + scale) + shift
        y_ref[...] = jnp.dot(h.astype(BF16), w_ref[...], preferred_element_type=F32)

    def finish(y_ref):
        cos, sin = cos_ref[...], sin_ref[...]
        q = _norm_rope_transposed(y_ref[:, OFF_Q:OFF_Q + ATTN_WIDTH].T, gq_ref[...], cos, sin)
        qT_ref[0] = q.astype(BF16)
        k = _norm_rope_transposed(y_ref[:, OFF_K:OFF_K + KV_WIDTH].T, gk_ref[...], cos, sin)
        k_ref[0] = k.T.astype(BF16)
        vT_ref[0] = y_ref[:, OFF_V:OFF_V + KV_WIDTH].T.astype(BF16)
        rest_ref[0, :, 0:LRU_WIDTH] = y_ref[:, OFF_R:OFF_R + LRU_WIDTH]
        rest_ref[0, :, LRU_WIDTH:2 * LRU_WIDTH] = jax.nn.gelu(y_ref[:, OFF_G:OFF_G + LRU_WIDTH])
        rest_ref[0, :, 2 * LRU_WIDTH:2 * LRU_WIDTH + SC_WIDTH] = y_ref[:, OFF_B:OFF_B + SC_WIDTH]
        rest_ref[0, :, 2 * LRU_WIDTH + SC_WIDTH:] = (y_ref[:, OFF_C:OFF_C + SC_WIDTH]
                                                     * y_ref[:, OFF_U:OFF_U + SC_WIDTH])

    @pl.when(s == 0)
    def _():
        y_odd_ref[...] = jnp.zeros_like(y_odd_ref)

    @pl.when(s % 2 == 0)
    def _():
        project(y_even_ref)
        finish(y_odd_ref)

    @pl.when(s % 2 == 1)
    def _():
        project(y_odd_ref)
        finish(y_even_ref)


def _token_specs(x_base, tile_offset=0):
    return [pl.BlockSpec((1, TM, D_MODEL), lambda b, i: (b, 0, 0)),
            pl.BlockSpec((1, TM, D_MODEL),
                         lambda b, i: (b, x_base + jnp.maximum(i + tile_offset - 1, 0), 0))]


def _inproj(ctx_arr, x_arr, x_base, mods, g_mix, w_in, gq, gk, cos_t, sin_t):
    bsz = x_arr.shape[0]
    nt = x_arr.shape[1] // TM - x_base + 1
    t_all = nt * TM
    n_tiles = bsz * nt
    rest_w = 2 * LRU_WIDTH + 2 * SC_WIDTH

    def cur(s):
        sc = jnp.minimum(s, n_tiles - 1)
        return sc // nt, sc % nt

    def prev(s):
        sp = jnp.maximum(s - 1, 0)
        return sp // nt, sp % nt

    rope_spec = pl.BlockSpec((HEAD_DIM // 2, TM), lambda s: (0, prev(s)[1]))
    return pl.pallas_call(
        functools.partial(_inproj_kernel, nt=nt, n_tiles=n_tiles),
        grid=(n_tiles + 1,),
        in_specs=[pl.BlockSpec((1, TM, D_MODEL), lambda s: (cur(s)[0], 0, 0)),
                  pl.BlockSpec((1, TM, D_MODEL),
                               lambda s: (cur(s)[0], x_base + jnp.maximum(cur(s)[1] - 1, 0), 0)),
                  pl.BlockSpec((1, 1, MOD_ROWS, D_MODEL),
                               lambda s: (cur(s)[0], jnp.minimum(cur(s)[1], 1), 0, 0)),
                  _const_spec((1, D_MODEL)),
                  _const_spec((D_MODEL, IN_WIDTH)),
                  _const_spec((HEAD_DIM, TM)),
                  _const_spec((HEAD_DIM, TM)),
                  rope_spec,
                  rope_spec],
        out_specs=[pl.BlockSpec((1, ATTN_WIDTH, TM), lambda s: (prev(s)[0], 0, prev(s)[1])),
                   pl.BlockSpec((1, TM, KV_WIDTH), lambda s: (prev(s)[0], prev(s)[1], 0)),
                   pl.BlockSpec((1, KV_WIDTH, TM), lambda s: (prev(s)[0], 0, prev(s)[1])),
                   pl.BlockSpec((1, TM, rest_w), lambda s: (prev(s)[0], prev(s)[1], 0))],
        out_shape=[jax.ShapeDtypeStruct((bsz, ATTN_WIDTH, t_all), BF16),
                   jax.ShapeDtypeStruct((bsz, t_all, KV_WIDTH), BF16),
                   jax.ShapeDtypeStruct((bsz, KV_WIDTH, t_all), BF16),
                   jax.ShapeDtypeStruct((bsz, t_all, rest_w), F32)],
        scratch_shapes=[pltpu.VMEM((TM, IN_WIDTH), F32)] * 2,
        compiler_params=_params(40, n_grid_dims=1),
        name="inproj",
    )(ctx_arr, x_arr, mods, g_mix, w_in, gq, gk, cos_t, sin_t)


def _with_halo(prev_ref, cur_ref, next_ref, valid_prev, valid_next):
    prev = prev_ref[0] * valid_prev.astype(F32)
    nxt = next_ref[0] * valid_next.astype(F32)
    return jnp.concatenate([prev, cur_ref[0], nxt], axis=0)


def _shifted(tcat, offset):
    rows = tcat.shape[0]
    if offset == 0:
        return tcat[F32_SUBLANES:F32_SUBLANES + TM]
    return pltpu.roll(tcat, (-offset) % rows, 0)[F32_SUBLANES:F32_SUBLANES + TM]


def _lru_coeffs(rcat, conv_w, conv_b, w_gate, b_gate, sp_lam):
    xc = conv_b
    for j in range(LRU_CONV):
        xc = xc + conv_w[j:j + 1] * _shifted(rcat, j - 2)
    z = jnp.dot(xc.astype(BF16), w_gate, preferred_element_type=F32) + b_gate
    r_gate = jax.nn.sigmoid(z[:, :LRU_WIDTH])
    i_gate = jax.nn.sigmoid(z[:, LRU_WIDTH:])
    log_a = (-LRU_C) * r_gate * sp_lam
    a = jnp.exp(log_a)
    one_minus_a2 = -jnp.tanh(log_a) * (1.0 + a * a)
    return a, jnp.sqrt(one_minus_a2) * (i_gate * xc)


def _lru_kernel(rf_ref, rfp_ref, rfn_ref, rb_ref, rbp_ref, rbn_ref, bg_ref, cu_ref, cup_ref, cun_ref,
                cw_ref, cb_ref, wg_ref, bgate_ref, lam_ref, scw_ref,
                hf_ref, hb_ref, sc_ref,
                af_s, bf_s, ab_s, bb_s, hf_state, hb_state, *, nt):
    i = pl.program_id(1)
    j = jnp.where(i == 0, 0, nt - i)

    @pl.when(i == 0)
    def _():
        hf_state[...] = jnp.zeros_like(hf_state)
        hb_state[...] = jnp.zeros_like(hb_state)

    def halo_valid(t):
        return jnp.logical_and(t != 0, t != 1), jnp.logical_and(t != 0, t != nt - 1)

    lam = lam_ref[...]
    sp_lam = jnp.maximum(-lam, 0.0) + jnp.log1p(jnp.exp(-jnp.abs(lam)))
    cw = cw_ref[...]
    cb = cb_ref[...]

    vp, vn = halo_valid(i)
    a, b = _lru_coeffs(_with_halo(rfp_ref, rf_ref, rfn_ref, vp, vn), cw, cb,
                       wg_ref[:, :2 * LRU_WIDTH], bgate_ref[:, :2 * LRU_WIDTH], sp_lam[0:1])
    _block_prefix(a, b, af_s, bf_s, reverse=False)
    cucat = _with_halo(cup_ref, cu_ref, cun_ref, vp, vn)
    scw = scw_ref[...]
    conv = scw[0:1] * _shifted(cucat, -1) + scw[1:2] * _shifted(cucat, 0) + scw[2:3] * _shifted(cucat, 1)
    sc_ref[0] = (bg_ref[0] * conv).astype(BF16)

    vp, vn = halo_valid(j)
    a, b = _lru_coeffs(_with_halo(rbp_ref, rb_ref, rbn_ref, vp, vn), cw, cb,
                       wg_ref[:, 2 * LRU_WIDTH:], bgate_ref[:, 2 * LRU_WIDTH:], sp_lam[1:2])
    _block_prefix(a, b, ab_s, bb_s, reverse=True)

    _carry_blocks(af_s, bf_s, hf_ref, hf_state, reverse=False)
    _carry_blocks(ab_s, bb_s, hb_ref, hb_state, reverse=True)


def _block_prefix(a, b, a_s, b_s, reverse):
    n = F32_SUBLANES
    a = a.reshape(TM // n, n, LRU_WIDTH)
    b = b.reshape(TM // n, n, LRU_WIDTH)
    row = lax.broadcasted_iota(jnp.int32, a.shape, 1)
    d = 1
    while d < n:
        shift, keep = (n - d, row < n - d) if reverse else (d, row >= d)
        a_prev = jnp.where(keep, pltpu.roll(a, shift, 1), 1.0)
        b_prev = jnp.where(keep, pltpu.roll(b, shift, 1), 0.0)
        b = a * b_prev + b
        a = a * a_prev
        d *= 2
    a_s[...] = a
    b_s[...] = b


def _carry_blocks(a_s, b_s, out_ref, state_ref, reverse):
    n = F32_SUBLANES
    nb = TM // n
    last = 0 if reverse else n - 1
    h_in = state_ref[...]
    for v in (reversed(range(nb)) if reverse else range(nb)):
        h = a_s[v] * h_in + b_s[v]
        out_ref[0, v * n:(v + 1) * n, :] = h
        h_in = h[last:last + 1]
    state_ref[...] = h_in


def _lru(rest, conv_w, conv_b, w_gate, b_gate, lam, sc_w):
    bsz, t_all, _ = rest.shape
    nt = t_all // TM
    hpt = TM // F32_SUBLANES
    nhb = t_all // F32_SUBLANES

    def rev(i):
        return jnp.where(i == 0, 0, nt - i)

    def tile(col, order):
        return pl.BlockSpec((1, TM, LRU_WIDTH), lambda b, i: (b, order(i), col))

    def prev_halo(col, order):
        return pl.BlockSpec((1, F32_SUBLANES, LRU_WIDTH),
                            lambda b, i: (b, jnp.maximum(order(i) * hpt - 1, 0), col))

    def next_halo(col, order):
        return pl.BlockSpec((1, F32_SUBLANES, LRU_WIDTH),
                            lambda b, i: (b, jnp.minimum((order(i) + 1) * hpt, nhb - 1), col))

    fwd = lambda i: i
    out_f32 = jax.ShapeDtypeStruct((bsz, t_all, LRU_WIDTH), F32)
    return pl.pallas_call(
        functools.partial(_lru_kernel, nt=nt),
        grid=(bsz, nt),
        in_specs=[tile(0, fwd), prev_halo(0, fwd), next_halo(0, fwd),
                  tile(0, rev), prev_halo(0, rev), next_halo(0, rev),
                  tile(2, fwd),
                  tile(3, fwd), prev_halo(3, fwd), next_halo(3, fwd),
                  _const_spec((LRU_CONV, LRU_WIDTH)),
                  _const_spec((1, LRU_WIDTH)),
                  _const_spec((LRU_WIDTH, 4 * LRU_WIDTH)),
                  _const_spec((1, 4 * LRU_WIDTH)),
                  _const_spec((2, LRU_WIDTH)),
                  _const_spec((3, SC_WIDTH))],
        out_specs=[pl.BlockSpec((1, TM, LRU_WIDTH), lambda b, i: (b, i, 0)),
                   pl.BlockSpec((1, TM, LRU_WIDTH), lambda b, i: (b, rev(i), 0)),
                   pl.BlockSpec((1, TM, SC_WIDTH), lambda b, i: (b, i, 0))],
        out_shape=[out_f32, out_f32, jax.ShapeDtypeStruct((bsz, t_all, SC_WIDTH), BF16)],
        scratch_shapes=([pltpu.VMEM((TM // F32_SUBLANES, F32_SUBLANES, LRU_WIDTH), F32)] * 4
                        + [pltpu.VMEM((1, LRU_WIDTH), F32)] * 2),
        compiler_params=_params(32),
        name="lru_scan",
    )(rest, rest, rest, rest, rest, rest, rest, rest, rest, rest,
      conv_w, conv_b, w_gate, b_gate, lam, sc_w)


def _mixer_kernel(qT_ref, k_ref, vT_ref, ctx_ref, x_ref, hf_ref, hb_ref, gg_ref, sc_ref, mod_ref, wo_ref,
                  gffn_ref, x1_ref, h2_ref, edge_ref, oT_ref, s_ref, *, tile_offset, n_key_tiles):
    i = pl.program_id(1) + tile_offset
    ones_rows = jnp.ones((BF16_SUBLANES, TM), BF16)

    def project_out():
        x = x_ref[0]
        if tile_offset == 0:
            x = jnp.where(i == 0, ctx_ref[0], x)
        mod = mod_ref[0, 0]
        gate_mix, shift, scale = mod[2:3], mod[3:4], mod[4:5]
        att = oT_ref[...].T.astype(BF16)
        lru = (gg_ref[0] * (hf_ref[0] + hb_ref[0])).astype(BF16)
        y = (jnp.dot(att, wo_ref[0:ATTN_WIDTH], preferred_element_type=F32)
             + jnp.dot(lru, wo_ref[ATTN_WIDTH:ATTN_WIDTH + LRU_WIDTH], preferred_element_type=F32)
             + jnp.dot(sc_ref[0], wo_ref[ATTN_WIDTH + LRU_WIDTH:], preferred_element_type=F32))
        x1 = x + gate_mix * y
        x1_ref[0] = x1
        h2 = x1 * _rms_scale(x1) * gffn_ref[...]
        h2 = h2 * (1.0 + scale) + shift
        h2_ref[0] = h2.astype(BF16)
        edge_ref[0, 0] = jnp.concatenate(
            [h2[0:1], h2[TM - 1:TM], jnp.zeros((F32_SUBLANES - 2, D_MODEL), F32)], axis=0)

    def run(n_tiles):
        items = [(pr, c) for pr in range(N_HEADS // 2) for c in range(n_tiles)]
        qpads = {}

        def qpad_of(pr):
            if pr not in qpads:
                row0 = pr * 2 * HEAD_DIM
                q2 = jnp.concatenate([qT_ref[0, row0:row0 + HEAD_DIM, :],
                                      qT_ref[0, row0 + HEAD_DIM:row0 + 2 * HEAD_DIM, :]], axis=1)
                zero = jnp.zeros_like(q2)
                first_kv = (2 * pr) // HEADS_PER_KV == 0
                qpads[pr] = jnp.concatenate([q2, zero] if first_kv else [zero, q2], axis=0)
            return qpads[pr]

        def scores(t):
            pr, c = items[t]
            s = jnp.dot(k_ref[0, c * TM:(c + 1) * TM, :], qpad_of(pr), preferred_element_type=F32)
            s_ref[t % SCORE_SLOTS] = s
            return jnp.max(s, axis=0, keepdims=True)

        tile_max = [scores(t) for t in range(min(SCORE_LOOKAHEAD, len(items)))]
        m = acc = None
        for t, (pr, c) in enumerate(items):
            if t + SCORE_LOOKAHEAD < len(items):
                tile_max.append(scores(t + SCORE_LOOKAHEAD))
            if c == 0:
                m = jnp.full((1, 2 * TM), -jnp.inf, F32)
                acc = jnp.zeros((HEAD_DIM + BF16_SUBLANES, 2 * TM), F32)
            g = (2 * pr) // HEADS_PER_KV
            m_new = jnp.maximum(m, tile_max[t])
            alpha = jnp.exp2(m - m_new)
            p = jnp.exp2(s_ref[t % SCORE_SLOTS] - m_new).astype(BF16)
            vt = jnp.concatenate([vT_ref[0, g * HEAD_DIM:(g + 1) * HEAD_DIM, c * TM:(c + 1) * TM],
                                  ones_rows], axis=0)
            acc = alpha * acc + jnp.dot(vt, p, preferred_element_type=F32)
            m = m_new
            tile_max[t] = None
            if c == n_tiles - 1:
                o = acc[:HEAD_DIM] / acc[HEAD_DIM:HEAD_DIM + 1]
                row0 = pr * 2 * HEAD_DIM
                oT_ref[row0:row0 + HEAD_DIM, :] = o[:, :TM]
                oT_ref[row0 + HEAD_DIM:row0 + 2 * HEAD_DIM, :] = o[:, TM:]
        project_out()

    @pl.when(i == 0)
    def _():
        run(1)

    @pl.when(i != 0)
    def _():
        run(n_key_tiles)


def _mixer(qT, k, vT, ctx_arr, x_arr, x_base, hf, hb, rest, sc, mods, w_out, g_ffn, tile_offset):
    bsz, t_all, _ = k.shape
    nt = t_all // TM
    nq = nt - tile_offset
    off = lambda b, i: (b, i + tile_offset, 0)
    return pl.pallas_call(
        functools.partial(_mixer_kernel, tile_offset=tile_offset, n_key_tiles=nt),
        grid=(bsz, nq),
        in_specs=[pl.BlockSpec((1, ATTN_WIDTH, TM), lambda b, i: (b, 0, i + tile_offset)),
                  pl.BlockSpec((1, t_all, KV_WIDTH), lambda b, i: (b, 0, 0)),
                  pl.BlockSpec((1, KV_WIDTH, t_all), lambda b, i: (b, 0, 0))]
                 + _token_specs(x_base, tile_offset) + [
                  pl.BlockSpec((1, TM, LRU_WIDTH), off),
                  pl.BlockSpec((1, TM, LRU_WIDTH), off),
                  pl.BlockSpec((1, TM, LRU_WIDTH), lambda b, i: (b, i + tile_offset, 1)),
                  pl.BlockSpec((1, TM, SC_WIDTH), off),
                  pl.BlockSpec((1, 1, MOD_ROWS, D_MODEL),
                               lambda b, i: (b, jnp.minimum(i + tile_offset, 1), 0, 0)),
                  _const_spec((D_MODEL, D_MODEL)),
                  _const_spec((1, D_MODEL))],
        out_specs=[pl.BlockSpec((1, TM, D_MODEL), lambda b, i: (b, i, 0)),
                   pl.BlockSpec((1, TM, D_MODEL), lambda b, i: (b, i, 0)),
                   pl.BlockSpec((1, 1, F32_SUBLANES, D_MODEL), lambda b, i: (b, i, 0, 0))],
        out_shape=[jax.ShapeDtypeStruct((bsz, nq * TM, D_MODEL), F32),
                   jax.ShapeDtypeStruct((bsz, nq * TM, D_MODEL), BF16),
                   jax.ShapeDtypeStruct((bsz, nq, F32_SUBLANES, D_MODEL), F32)],
        scratch_shapes=[pltpu.VMEM((ATTN_WIDTH, TM), F32), pltpu.VMEM((SCORE_SLOTS, TM, 2 * TM), F32)],
        compiler_params=_params(40),
        name="mixer",
    )(qT, k, vT, ctx_arr, x_arr, hf, hb, rest, sc, mods, w_out, g_ffn)


def _ffn_kernel(h_ref, edge_ref, x1_ref, mod_ref, wup_ref, cw_ref, wdn_ref, gfin_ref,
                o_ref, acc_ref, up_ref, *, nt, final_norm):
    hcat = h_ref[0]
    row = lax.broadcasted_iota(jnp.int32, (TM, FF_CHUNK), 0)
    tile = pl.program_id(0) * nt + pl.program_id(1)
    edge_row = (tile % (F32_SUBLANES // 2)) * 2

    def conv(t, w, cols):
        before = jnp.where(row == 0, edge_ref[pl.ds(edge_row, 1), cols], pltpu.roll(t, 1, 0))
        after = jnp.where(row == TM - 1, edge_ref[pl.ds(edge_row + 1, 1), cols],
                          pltpu.roll(t, TM - 1, 0))
        return w[0:1] * before + w[1:2] * t + w[2:3] * after

    n_chunks = D_FF // FF_CHUNK

    def up_project(c):
        cu = slice(c * FF_CHUNK, (c + 1) * FF_CHUNK)
        cg = slice(D_FF + c * FF_CHUNK, D_FF + (c + 1) * FF_CHUNK)
        up_ref[c % UP_SLOTS, :, :FF_CHUNK] = jnp.dot(hcat, wup_ref[:, cu], preferred_element_type=F32)
        up_ref[c % UP_SLOTS, :, FF_CHUNK:] = jnp.dot(hcat, wup_ref[:, cg], preferred_element_type=F32)

    for c in range(min(UP_LOOKAHEAD, n_chunks)):
        up_project(c)
    for c in range(n_chunks):
        if c + UP_LOOKAHEAD < n_chunks:
            up_project(c + UP_LOOKAHEAD)
        cu = slice(c * FF_CHUNK, (c + 1) * FF_CHUNK)
        cg = slice(D_FF + c * FF_CHUNK, D_FF + (c + 1) * FF_CHUNK)
        u = conv(up_ref[c % UP_SLOTS, :, :FF_CHUNK], cw_ref[:, cu], cu)
        g = conv(up_ref[c % UP_SLOTS, :, FF_CHUNK:], cw_ref[:, cg], cg)
        act = ((g * jax.nn.sigmoid(g)) * u).astype(BF16)
        part = jnp.dot(act, wdn_ref[cu, :], preferred_element_type=F32)
        if c == 0:
            acc_ref[...] = part
        elif c < n_chunks - 1:
            acc_ref[...] += part

    gate_ffn = mod_ref[0, 0][5:6]
    x2 = x1_ref[0] + gate_ffn * (acc_ref[...] + part)
    if final_norm:
        x2 = x2 * _rms_scale(x2) * gfin_ref[...]
    o_ref[0] = x2


def _edge_up_kernel(h_ref, w_ref, o_ref):
    o_ref[...] = jnp.dot(h_ref[...], w_ref[...], preferred_element_type=F32)


def _edge_up(tile_edges, w_up, ctx_tiles):
    bsz, nt = tile_edges.shape[:2]
    first, last = tile_edges[:, :, 0].astype(BF16), tile_edges[:, :, 1].astype(BF16)
    t_idx = jnp.arange(nt)
    has_before = jnp.logical_and(t_idx != 0, t_idx != ctx_tiles)[None, :, None]
    has_after = jnp.logical_and(t_idx != nt - 1, t_idx != ctx_tiles - 1)[None, :, None]
    before = jnp.where(has_before, jnp.roll(last, 1, axis=1), 0)
    after = jnp.where(has_after, jnp.roll(first, -1, axis=1), 0)
    rows = jnp.stack([before, after], axis=2).reshape(bsz * nt * 2, D_MODEL)
    n_rows = rows.shape[0]
    padded = -(-n_rows // BF16_SUBLANES) * BF16_SUBLANES
    rows = jnp.pad(rows, ((0, padded - n_rows), (0, 0)))
    n_blk = 2 * D_FF // EDGE_COLS
    return pl.pallas_call(
        _edge_up_kernel,
        grid=(n_blk,),
        in_specs=[pl.BlockSpec((padded, D_MODEL), lambda j: (0, 0)),
                  pl.BlockSpec((D_MODEL, EDGE_COLS), lambda j: (0, j))],
        out_specs=pl.BlockSpec((padded, EDGE_COLS), lambda j: (0, j)),
        out_shape=jax.ShapeDtypeStruct((padded, 2 * D_FF), F32),
        name="edge_up",
    )(rows, w_up)


def _ffn(h2, tile_edges, x1, mods, w_up, conv_w, w_down, g_final, ctx_tiles, final_norm):
    bsz, t_len, _ = h2.shape
    nt = t_len // TM
    edges = _edge_up(tile_edges, w_up, ctx_tiles)
    tiles_per_block = F32_SUBLANES // 2
    return pl.pallas_call(
        functools.partial(_ffn_kernel, nt=nt, final_norm=final_norm),
        grid=(bsz, nt),
        in_specs=[pl.BlockSpec((1, TM, D_MODEL), lambda b, i: (b, i, 0)),
                  pl.BlockSpec((F32_SUBLANES, 2 * D_FF),
                               lambda b, i: ((b * nt + i) // tiles_per_block, 0)),
                  pl.BlockSpec((1, TM, D_MODEL), lambda b, i: (b, i, 0)),
                  pl.BlockSpec((1, 1, MOD_ROWS, D_MODEL),
                               lambda b, i: (b, jnp.minimum(i + 1 - ctx_tiles, 1), 0, 0)),
                  _const_spec((D_MODEL, 2 * D_FF)),
                  _const_spec((3, 2 * D_FF)),
                  _const_spec((D_FF, D_MODEL)),
                  _const_spec((1, D_MODEL))],
        out_specs=pl.BlockSpec((1, TM, D_MODEL), lambda b, i: (b, i, 0)),
        out_shape=jax.ShapeDtypeStruct((bsz, t_len, D_MODEL), F32),
        scratch_shapes=[pltpu.VMEM((TM, D_MODEL), F32),
                        pltpu.VMEM((UP_SLOTS, TM, 2 * FF_CHUNK), F32)],
        compiler_params=_params(48),
        name="conv_ffn",
    )(h2, edges, x1, mods, w_up, conv_w, w_down, g_final)


def _rope_tables(seq, ctx_len):
    rows = seq // GRID_W
    pos_r = np.repeat(np.arange(rows, dtype=np.float32), GRID_W)
    pos_c = np.tile(np.arange(GRID_W, dtype=np.float32), rows)
    n_f = HEAD_DIM // 4
    inv = (np.float32(ROPE_THETA) ** (-np.arange(n_f, dtype=np.float32) / n_f)).astype(np.float32)
    ang = np.concatenate([pos_r[:, None] * inv, pos_c[:, None] * inv], axis=-1).astype(np.float32)
    cos = np.concatenate([np.ones((ctx_len, HEAD_DIM // 2)), np.cos(ang.astype(np.float64))], axis=0)
    sin = np.concatenate([np.zeros((ctx_len, HEAD_DIM // 2)), np.sin(ang.astype(np.float64))], axis=0)
    return jnp.asarray(cos.T, F32), jnp.asarray(sin.T, F32)


def _gate_weights(wa, ba, wi, bi):
    def dense(w):
        eye = jnp.eye(LRU_BLOCKS, dtype=w.dtype)
        return jnp.einsum("nde,nm->ndme", w, eye).reshape(LRU_WIDTH, LRU_WIDTH)
    w = jnp.concatenate([dense(wa[0]), dense(wi[0]), dense(wa[1]), dense(wi[1])], axis=1)
    b = jnp.concatenate([ba[0], bi[0], ba[1], bi[1]])[None, :]
    return w.astype(BF16), b


def kernel(x, c, ctx, c_ctx, w_mod, b_mod, g_mix, g_ffn, w_in, g_q, g_k, lru_conv_w, lru_conv_b,
           lru_wa, lru_ba, lru_wi, lru_bi, lru_lam, sc_conv_w, w_out, w_up, ffn_conv_w, w_down,
           g_final):
    bsz, seq, _ = x.shape
    ctx_len = ctx.shape[1]
    depth = w_mod.shape[0]
    assert ctx_len == TM and seq % TM == 0 and seq % GRID_W == 0

    cos_t, sin_t = _rope_tables(seq, ctx_len)
    ctx_arr, x_arr, x_base = ctx, x, 0
    cc_rows = -(-(bsz + 1) // F32_SUBLANES) * F32_SUBLANES
    cc = jnp.zeros((cc_rows, D_MODEL), F32).at[:bsz].set(c).at[bsz].set(c_ctx)

    mod_layers = _modulation(cc, w_mod, b_mod)

    out = None
    for l in range(depth):
        last = l == depth - 1
        mod_all = mod_layers[l].reshape(cc_rows, N_MOD, D_MODEL)
        mod_x = mod_all[:bsz]
        mod_c = jnp.broadcast_to(mod_all[bsz][None], (bsz, N_MOD, D_MODEL))
        mods = jnp.stack([mod_c, mod_x], axis=1)
        mods = jnp.pad(mods, ((0, 0), (0, 0), (0, MOD_ROWS - N_MOD), (0, 0)))

        gq = jnp.broadcast_to((g_q[l] * Q_SCALE)[:, None], (HEAD_DIM, TM))
        gk = jnp.broadcast_to(g_k[l][:, None], (HEAD_DIM, TM))
        qT, k, vT, rest = _inproj(ctx_arr, x_arr, x_base, mods, g_mix[l][None, :],
                                  w_in[l].astype(BF16), gq, gk, cos_t, sin_t)

        w_gate, b_gate = _gate_weights(lru_wa[l], lru_ba[l], lru_wi[l], lru_bi[l])
        hf, hb, sc = _lru(rest, lru_conv_w[l], lru_conv_b[l][None, :], w_gate, b_gate, lru_lam[l],
                          sc_conv_w[l])

        tile_offset = 1 if last else 0
        x1, h2, tile_edges = _mixer(qT, k, vT, ctx_arr, x_arr, x_base, hf, hb, rest, sc, mods,
                                    w_out[l].astype(BF16), g_ffn[l][None, :], tile_offset)
        out = _ffn(h2, tile_edges, x1, mods, w_up[l].astype(BF16), ffn_conv_w[l],
                   w_down[l].astype(BF16), g_final[None, :], ctx_tiles=1 - tile_offset,
                   final_norm=last)
        ctx_arr, x_arr, x_base = out, out, 1
    return out
```

```python
import functools

import numpy as np
import jax
import jax.numpy as jnp
from jax import lax
from jax.experimental import pallas as pl
from jax.experimental.pallas import tpu as pltpu

F32 = jnp.float32
BF16 = jnp.bfloat16

D_MODEL = 1024
HEAD_DIM = 64
N_HEADS = 8
N_KV_HEADS = 2
HEADS_PER_KV = N_HEADS // N_KV_HEADS
ATTN_WIDTH = N_HEADS * HEAD_DIM
KV_WIDTH = N_KV_HEADS * HEAD_DIM
LRU_WIDTH = 256
LRU_BLOCKS = 4
LRU_BLOCK = LRU_WIDTH // LRU_BLOCKS
LRU_CONV = 4
LRU_C = 8.0
SC_WIDTH = 256
IN_WIDTH = 2048
D_FF = 2816
FF_CHUNK = 256
EDGE_COLS = 1408
GRID_W = 64
ROPE_THETA = 10000.0
EPS = 1e-6
Q_SCALE = HEAD_DIM ** -0.5 * 1.4426950408889634

TM = 256
SCORE_LOOKAHEAD = 3
SCORE_SLOTS = SCORE_LOOKAHEAD + 1
UP_LOOKAHEAD = 3
UP_SLOTS = UP_LOOKAHEAD + 1
LANES = 128
F32_SUBLANES = 8
BF16_SUBLANES = 16
N_MOD = 6
MOD_ROWS = 8

OFF_Q = 0
OFF_K = OFF_Q + ATTN_WIDTH
OFF_V = OFF_K + KV_WIDTH
OFF_R = OFF_V + KV_WIDTH
OFF_G = OFF_R + LRU_WIDTH
OFF_B = OFF_G + LRU_WIDTH
OFF_C = OFF_B + SC_WIDTH
OFF_U = OFF_C + SC_WIDTH


def _const_spec(shape):
    nd = len(shape)
    return pl.BlockSpec(shape, lambda *_: (0,) * nd, pipeline_mode=pl.Buffered(1))


def _layer_spec(shape, layer):
    nd = len(shape)
    return pl.BlockSpec((None,) + tuple(shape), lambda *_: (layer,) + (0,) * nd,
                        pipeline_mode=pl.Buffered(1))


def _cast_kernel(x_ref, o_ref):
    o_ref[...] = x_ref[...].astype(BF16)


def _to_bf16(w, block_rows):
    depth, rows, cols = w.shape
    spec = pl.BlockSpec((None, block_rows, cols), lambda l, r: (l, r, 0))
    return pl.pallas_call(
        _cast_kernel,
        grid=(depth, rows // block_rows),
        in_specs=[spec],
        out_specs=spec,
        out_shape=jax.ShapeDtypeStruct(w.shape, BF16),
        name="weights_to_bf16",
    )(w)


def _params(vmem_mb, n_grid_dims=2):
    return pltpu.CompilerParams(dimension_semantics=("arbitrary",) * n_grid_dims,
                                vmem_limit_bytes=vmem_mb * 1024 * 1024)


def _rms_scale(t):
    return lax.rsqrt(jnp.mean(t * t, axis=-1, keepdims=True) + EPS)


def _mod_kernel(c_ref, w_ref, b_ref, o_ref):
    c = c_ref[...]
    h = (c * jax.nn.sigmoid(c)).astype(BF16)
    o_ref[...] = jnp.dot(h, w_ref[...].astype(BF16), preferred_element_type=F32) + b_ref[...]


def _modulation(cc, w_mod, b_mod):
    rows = cc.shape[0]
    depth = w_mod.shape[0]
    return pl.pallas_call(
        _mod_kernel,
        grid=(depth, N_MOD),
        in_specs=[pl.BlockSpec((rows, D_MODEL), lambda l, j: (0, 0)),
                  pl.BlockSpec((None, D_MODEL, D_MODEL), lambda l, j: (l, 0, j)),
                  pl.BlockSpec((None, 1, D_MODEL), lambda l, j: (l, 0, j))],
        out_specs=pl.BlockSpec((None, rows, D_MODEL), lambda l, j: (l, 0, j)),
        out_shape=jax.ShapeDtypeStruct((depth, rows, N_MOD * D_MODEL), F32),
        name="modulation",
    )(cc, w_mod, b_mod[:, None, :])


def _norm_rope_transposed(t, gain, cos, sin):
    half = HEAD_DIM // 2
    out = []
    for h in range(t.shape[0] // HEAD_DIM):
        r = t[h * HEAD_DIM:(h + 1) * HEAD_DIM]
        r = r * lax.rsqrt(jnp.mean(r * r, axis=0, keepdims=True) + EPS) * gain
        x1, x2 = r[:half], r[half:]
        out += [x1 * cos - x2 * sin, x1 * sin + x2 * cos]
    return jnp.concatenate(out, axis=0)


def _inproj_kernel(ctx_ref, x_ref, mod_ref, gmix_ref, w_ref, gq_ref, gk_ref, cos_ref, sin_ref,
                   qT_ref, k_ref, vT_ref, rest_ref, y_even_ref, y_odd_ref, *, nt, n_tiles):
    s = pl.program_id(0)
    tile = jnp.minimum(s, n_tiles - 1) % nt

    def project(y_ref):
        x = jnp.where(tile == 0, ctx_ref[0], x_ref[0])
        mod = mod_ref[0, 0]
        shift, scale = mod[0:1], mod[1:2]
        h = x * _rms_scale(x) * gmix_ref[...]
        h = h * (1.0 + scale) + shift
        y_ref[...] = jnp.dot(h.astype(BF16), w_ref[...], preferred_element_type=F32)

    def finish(y_ref):
        cos, sin = cos_ref[...], sin_ref[...]
        q = _norm_rope_transposed(y_ref[:, OFF_Q:OFF_Q + ATTN_WIDTH].T, gq_ref[...], cos, sin)
        qT_ref[0] = q.astype(BF16)
        k = _norm_rope_transposed(y_ref[:, OFF_K:OFF_K + KV_WIDTH].T, gk_ref[...], cos, sin)
        k_ref[0] = k.T.astype(BF16)
        vT_ref[0] = y_ref[:, OFF_V:OFF_V + KV_WIDTH].T.astype(BF16)
        rest_ref[0, :, 0:LRU_WIDTH] = y_ref[:, OFF_R:OFF_R + LRU_WIDTH]
        rest_ref[0, :, LRU_WIDTH:2 * LRU_WIDTH] = jax.nn.gelu(y_ref[:, OFF_G:OFF_G + LRU_WIDTH])
        rest_ref[0, :, 2 * LRU_WIDTH:2 * LRU_WIDTH + SC_WIDTH] = y_ref[:, OFF_B:OFF_B + SC_WIDTH]
        rest_ref[0, :, 2 * LRU_WIDTH + SC_WIDTH:] = (y_ref[:, OFF_C:OFF_C + SC_WIDTH]
                                                     * y_ref[:, OFF_U:OFF_U + SC_WIDTH])

    @pl.when(s == 0)
    def _():
        y_odd_ref[...] = jnp.zeros_like(y_odd_ref)

    @pl.when(s % 2 == 0)
    def _():
        project(y_even_ref)
        finish(y_odd_ref)

    @pl.when(s % 2 == 1)
    def _():
        project(y_odd_ref)
        finish(y_even_ref)


def _token_specs(x_base, tile_offset=0):
    return [pl.BlockSpec((1, TM, D_MODEL), lambda b, i: (b, 0, 0)),
            pl.BlockSpec((1, TM, D_MODEL),
                         lambda b, i: (b, x_base + jnp.maximum(i + tile_offset - 1, 0), 0))]


def _inproj(ctx_arr, x_arr, x_base, mods, g_mix, w_in, layer, gq, gk, cos_t, sin_t):
    bsz = x_arr.shape[0]
    nt = x_arr.shape[1] // TM - x_base + 1
    t_all = nt * TM
    n_tiles = bsz * nt
    rest_w = 2 * LRU_WIDTH + 2 * SC_WIDTH

    def cur(s):
        sc = jnp.minimum(s, n_tiles - 1)
        return sc // nt, sc % nt

    def prev(s):
        sp = jnp.maximum(s - 1, 0)
        return sp // nt, sp % nt

    rope_spec = pl.BlockSpec((HEAD_DIM // 2, TM), lambda s: (0, prev(s)[1]))
    return pl.pallas_call(
        functools.partial(_inproj_kernel, nt=nt, n_tiles=n_tiles),
        grid=(n_tiles + 1,),
        in_specs=[pl.BlockSpec((1, TM, D_MODEL), lambda s: (cur(s)[0], 0, 0)),
                  pl.BlockSpec((1, TM, D_MODEL),
                               lambda s: (cur(s)[0], x_base + jnp.maximum(cur(s)[1] - 1, 0), 0)),
                  pl.BlockSpec((1, 1, MOD_ROWS, D_MODEL),
                               lambda s: (cur(s)[0], jnp.minimum(cur(s)[1], 1), 0, 0)),
                  _const_spec((1, D_MODEL)),
                  _layer_spec((D_MODEL, IN_WIDTH), layer),
                  _const_spec((HEAD_DIM, TM)),
                  _const_spec((HEAD_DIM, TM)),
                  rope_spec,
                  rope_spec],
        out_specs=[pl.BlockSpec((1, ATTN_WIDTH, TM), lambda s: (prev(s)[0], 0, prev(s)[1])),
                   pl.BlockSpec((1, TM, KV_WIDTH), lambda s: (prev(s)[0], prev(s)[1], 0)),
                   pl.BlockSpec((1, KV_WIDTH, TM), lambda s: (prev(s)[0], 0, prev(s)[1])),
                   pl.BlockSpec((1, TM, rest_w), lambda s: (prev(s)[0], prev(s)[1], 0))],
        out_shape=[jax.ShapeDtypeStruct((bsz, ATTN_WIDTH, t_all), BF16),
                   jax.ShapeDtypeStruct((bsz, t_all, KV_WIDTH), BF16),
                   jax.ShapeDtypeStruct((bsz, KV_WIDTH, t_all), BF16),
                   jax.ShapeDtypeStruct((bsz, t_all, rest_w), F32)],
        scratch_shapes=[pltpu.VMEM((TM, IN_WIDTH), F32)] * 2,
        compiler_params=_params(40, n_grid_dims=1),
        name="inproj",
    )(ctx_arr, x_arr, mods, g_mix, w_in, gq, gk, cos_t, sin_t)


def _with_halo(prev_ref, cur_ref, next_ref, valid_prev, valid_next):
    prev = prev_ref[0] * valid_prev.astype(F32)
    nxt = next_ref[0] * valid_next.astype(F32)
    return jnp.concatenate([prev, cur_ref[0], nxt], axis=0)


def _shifted(tcat, offset):
    rows = tcat.shape[0]
    if offset == 0:
        return tcat[F32_SUBLANES:F32_SUBLANES + TM]
    return pltpu.roll(tcat, (-offset) % rows, 0)[F32_SUBLANES:F32_SUBLANES + TM]


def _lru_coeffs(rcat, conv_w, conv_b, w_gate, b_gate, sp_lam):
    xc = conv_b
    for j in range(LRU_CONV):
        xc = xc + conv_w[j:j + 1] * _shifted(rcat, j - 2)
    z = jnp.dot(xc.astype(BF16), w_gate, preferred_element_type=F32) + b_gate
    r_gate = jax.nn.sigmoid(z[:, :LRU_WIDTH])
    i_gate = jax.nn.sigmoid(z[:, LRU_WIDTH:])
    log_a = (-LRU_C) * r_gate * sp_lam
    a = jnp.exp(log_a)
    one_minus_a2 = -jnp.tanh(log_a) * (1.0 + a * a)
    return a, jnp.sqrt(one_minus_a2) * (i_gate * xc)


def _lru_kernel(rf_ref, rfp_ref, rfn_ref, rb_ref, rbp_ref, rbn_ref, bg_ref, cu_ref, cup_ref, cun_ref,
                cw_ref, cb_ref, wg_ref, bgate_ref, lam_ref, scw_ref,
                hf_ref, hb_ref, sc_ref,
                af_s, bf_s, ab_s, bb_s, hf_state, hb_state, *, nt):
    i = pl.program_id(1)
    j = jnp.where(i == 0, 0, nt - i)

    @pl.when(i == 0)
    def _():
        hf_state[...] = jnp.zeros_like(hf_state)
        hb_state[...] = jnp.zeros_like(hb_state)

    def halo_valid(t):
        return jnp.logical_and(t != 0, t != 1), jnp.logical_and(t != 0, t != nt - 1)

    lam = lam_ref[...]
    sp_lam = jnp.maximum(-lam, 0.0) + jnp.log1p(jnp.exp(-jnp.abs(lam)))
    cw = cw_ref[...]
    cb = cb_ref[...]

    vp, vn = halo_valid(i)
    a, b = _lru_coeffs(_with_halo(rfp_ref, rf_ref, rfn_ref, vp, vn), cw, cb,
                       wg_ref[:, :2 * LRU_WIDTH], bgate_ref[:, :2 * LRU_WIDTH], sp_lam[0:1])
    _block_prefix(a, b, af_s, bf_s, reverse=False)
    cucat = _with_halo(cup_ref, cu_ref, cun_ref, vp, vn)
    scw = scw_ref[...]
    conv = scw[0:1] * _shifted(cucat, -1) + scw[1:2] * _shifted(cucat, 0) + scw[2:3] * _shifted(cucat, 1)
    sc_ref[0] = (bg_ref[0] * conv).astype(BF16)

    vp, vn = halo_valid(j)
    a, b = _lru_coeffs(_with_halo(rbp_ref, rb_ref, rbn_ref, vp, vn), cw, cb,
                       wg_ref[:, 2 * LRU_WIDTH:], bgate_ref[:, 2 * LRU_WIDTH:], sp_lam[1:2])
    _block_prefix(a, b, ab_s, bb_s, reverse=True)

    _carry_blocks(af_s, bf_s, hf_ref, hf_state, reverse=False)
    _carry_blocks(ab_s, bb_s, hb_ref, hb_state, reverse=True)


def _block_prefix(a, b, a_s, b_s, reverse):
    n = F32_SUBLANES
    a = a.reshape(TM // n, n, LRU_WIDTH)
    b = b.reshape(TM // n, n, LRU_WIDTH)
    row = lax.broadcasted_iota(jnp.int32, a.shape, 1)
    d = 1
    while d < n:
        shift, keep = (n - d, row < n - d) if reverse else (d, row >= d)
        a_prev = jnp.where(keep, pltpu.roll(a, shift, 1), 1.0)
        b_prev = jnp.where(keep, pltpu.roll(b, shift, 1), 0.0)
        b = a * b_prev + b
        a = a * a_prev
        d *= 2
    a_s[...] = a
    b_s[...] = b


def _carry_blocks(a_s, b_s, out_ref, state_ref, reverse):
    n = F32_SUBLANES
    nb = TM // n
    last = 0 if reverse else n - 1
    h_in = state_ref[...]
    for v in (reversed(range(nb)) if reverse else range(nb)):
        h = a_s[v] * h_in + b_s[v]
        out_ref[0, v * n:(v + 1) * n, :] = h
        h_in = h[last:last + 1]
    state_ref[...] = h_in


def _lru(rest, conv_w, conv_b, w_gate, b_gate, lam, sc_w):
    bsz, t_all, _ = rest.shape
    nt = t_all // TM
    hpt = TM // F32_SUBLANES
    nhb = t_all // F32_SUBLANES

    def rev(i):
        return jnp.where(i == 0, 0, nt - i)

    def tile(col, order):
        return pl.BlockSpec((1, TM, LRU_WIDTH), lambda b, i: (b, order(i), col))

    def prev_halo(col, order):
        return pl.BlockSpec((1, F32_SUBLANES, LRU_WIDTH),
                            lambda b, i: (b, jnp.maximum(order(i) * hpt - 1, 0), col))

    def next_halo(col, order):
        return pl.BlockSpec((1, F32_SUBLANES, LRU_WIDTH),
                            lambda b, i: (b, jnp.minimum((order(i) + 1) * hpt, nhb - 1), col))

    fwd = lambda i: i
    out_f32 = jax.ShapeDtypeStruct((bsz, t_all, LRU_WIDTH), F32)
    return pl.pallas_call(
        functools.partial(_lru_kernel, nt=nt),
        grid=(bsz, nt),
        in_specs=[tile(0, fwd), prev_halo(0, fwd), next_halo(0, fwd),
                  tile(0, rev), prev_halo(0, rev), next_halo(0, rev),
                  tile(2, fwd),
                  tile(3, fwd), prev_halo(3, fwd), next_halo(3, fwd),
                  _const_spec((LRU_CONV, LRU_WIDTH)),
                  _const_spec((1, LRU_WIDTH)),
                  _const_spec((LRU_WIDTH, 4 * LRU_WIDTH)),
                  _const_spec((1, 4 * LRU_WIDTH)),
                  _const_spec((2, LRU_WIDTH)),
                  _const_spec((3, SC_WIDTH))],
        out_specs=[pl.BlockSpec((1, TM, LRU_WIDTH), lambda b, i: (b, i, 0)),
                   pl.BlockSpec((1, TM, LRU_WIDTH), lambda b, i: (b, rev(i), 0)),
                   pl.BlockSpec((1, TM, SC_WIDTH), lambda b, i: (b, i, 0))],
        out_shape=[out_f32, out_f32, jax.ShapeDtypeStruct((bsz, t_all, SC_WIDTH), BF16)],
        scratch_shapes=([pltpu.VMEM((TM // F32_SUBLANES, F32_SUBLANES, LRU_WIDTH), F32)] * 4
                        + [pltpu.VMEM((1, LRU_WIDTH), F32)] * 2),
        compiler_params=_params(32),
        name="lru_scan",
    )(rest, rest, rest, rest, rest, rest, rest, rest, rest, rest,
      conv_w, conv_b, w_gate, b_gate, lam, sc_w)


def _mixer_kernel(qT_ref, k_ref, vT_ref, ctx_ref, x_ref, hf_ref, hb_ref, gg_ref, sc_ref, mod_ref, wo_ref,
                  gffn_ref, x1_ref, h2_ref, edge_ref, oT_ref, s_ref, *, tile_offset, n_key_tiles):
    i = pl.program_id(1) + tile_offset
    ones_rows = jnp.ones((BF16_SUBLANES, TM), BF16)

    def project_out():
        x = x_ref[0]
        if tile_offset == 0:
            x = jnp.where(i == 0, ctx_ref[0], x)
        mod = mod_ref[0, 0]
        gate_mix, shift, scale = mod[2:3], mod[3:4], mod[4:5]
        att = oT_ref[...].T.astype(BF16)
        lru = (gg_ref[0] * (hf_ref[0] + hb_ref[0])).astype(BF16)
        y = (jnp.dot(att, wo_ref[0:ATTN_WIDTH], preferred_element_type=F32)
             + jnp.dot(lru, wo_ref[ATTN_WIDTH:ATTN_WIDTH + LRU_WIDTH], preferred_element_type=F32)
             + jnp.dot(sc_ref[0], wo_ref[ATTN_WIDTH + LRU_WIDTH:], preferred_element_type=F32))
        x1 = x + gate_mix * y
        x1_ref[0] = x1
        h2 = x1 * _rms_scale(x1) * gffn_ref[...]
        h2 = h2 * (1.0 + scale) + shift
        h2_ref[0] = h2.astype(BF16)
        edge_ref[0, 0] = jnp.concatenate(
            [h2[0:1], h2[TM - 1:TM], jnp.zeros((F32_SUBLANES - 2, D_MODEL), F32)], axis=0)

    def run(n_tiles):
        items = [(pr, c) for pr in range(N_HEADS // 2) for c in range(n_tiles)]
        qpads = {}

        def qpad_of(pr):
            if pr not in qpads:
                row0 = pr * 2 * HEAD_DIM
                q2 = jnp.concatenate([qT_ref[0, row0:row0 + HEAD_DIM, :],
                                      qT_ref[0, row0 + HEAD_DIM:row0 + 2 * HEAD_DIM, :]], axis=1)
                zero = jnp.zeros_like(q2)
                first_kv = (2 * pr) // HEADS_PER_KV == 0
                qpads[pr] = jnp.concatenate([q2, zero] if first_kv else [zero, q2], axis=0)
            return qpads[pr]

        def scores(t):
            pr, c = items[t]
            s = jnp.dot(k_ref[0, c * TM:(c + 1) * TM, :], qpad_of(pr), preferred_element_type=F32)
            s_ref[t % SCORE_SLOTS] = s
            return jnp.max(s, axis=0, keepdims=True)

        tile_max = [scores(t) for t in range(min(SCORE_LOOKAHEAD, len(items)))]
        m = acc = None
        for t, (pr, c) in enumerate(items):
            if t + SCORE_LOOKAHEAD < len(items):
                tile_max.append(scores(t + SCORE_LOOKAHEAD))
            if c == 0:
                m = jnp.full((1, 2 * TM), -jnp.inf, F32)
                acc = jnp.zeros((HEAD_DIM + BF16_SUBLANES, 2 * TM), F32)
            g = (2 * pr) // HEADS_PER_KV
            m_new = jnp.maximum(m, tile_max[t])
            alpha = jnp.exp2(m - m_new)
            p = jnp.exp2(s_ref[t % SCORE_SLOTS] - m_new).astype(BF16)
            vt = jnp.concatenate([vT_ref[0, g * HEAD_DIM:(g + 1) * HEAD_DIM, c * TM:(c + 1) * TM],
                                  ones_rows], axis=0)
            acc = alpha * acc + jnp.dot(vt, p, preferred_element_type=F32)
            m = m_new
            tile_max[t] = None
            if c == n_tiles - 1:
                o = acc[:HEAD_DIM] / acc[HEAD_DIM:HEAD_DIM + 1]
                row0 = pr * 2 * HEAD_DIM
                oT_ref[row0:row0 + HEAD_DIM, :] = o[:, :TM]
                oT_ref[row0 + HEAD_DIM:row0 + 2 * HEAD_DIM, :] = o[:, TM:]
        project_out()

    @pl.when(i == 0)
    def _():
        run(1)

    @pl.when(i != 0)
    def _():
        run(n_key_tiles)


def _mixer(qT, k, vT, ctx_arr, x_arr, x_base, hf, hb, rest, sc, mods, w_out, layer, g_ffn, tile_offset):
    bsz, t_all, _ = k.shape
    nt = t_all // TM
    nq = nt - tile_offset
    off = lambda b, i: (b, i + tile_offset, 0)
    return pl.pallas_call(
        functools.partial(_mixer_kernel, tile_offset=tile_offset, n_key_tiles=nt),
        grid=(bsz, nq),
        in_specs=[pl.BlockSpec((1, ATTN_WIDTH, TM), lambda b, i: (b, 0, i + tile_offset)),
                  pl.BlockSpec((1, t_all, KV_WIDTH), lambda b, i: (b, 0, 0)),
                  pl.BlockSpec((1, KV_WIDTH, t_all), lambda b, i: (b, 0, 0))]
                 + _token_specs(x_base, tile_offset) + [
                  pl.BlockSpec((1, TM, LRU_WIDTH), off),
                  pl.BlockSpec((1, TM, LRU_WIDTH), off),
                  pl.BlockSpec((1, TM, LRU_WIDTH), lambda b, i: (b, i + tile_offset, 1)),
                  pl.BlockSpec((1, TM, SC_WIDTH), off),
                  pl.BlockSpec((1, 1, MOD_ROWS, D_MODEL),
                               lambda b, i: (b, jnp.minimum(i + tile_offset, 1), 0, 0)),
                  _layer_spec((D_MODEL, D_MODEL), layer),
                  _const_spec((1, D_MODEL))],
        out_specs=[pl.BlockSpec((1, TM, D_MODEL), lambda b, i: (b, i, 0)),
                   pl.BlockSpec((1, TM, D_MODEL), lambda b, i: (b, i, 0)),
                   pl.BlockSpec((1, 1, F32_SUBLANES, D_MODEL), lambda b, i: (b, i, 0, 0))],
        out_shape=[jax.ShapeDtypeStruct((bsz, nq * TM, D_MODEL), F32),
                   jax.ShapeDtypeStruct((bsz, nq * TM, D_MODEL), BF16),
                   jax.ShapeDtypeStruct((bsz, nq, F32_SUBLANES, D_MODEL), F32)],
        scratch_shapes=[pltpu.VMEM((ATTN_WIDTH, TM), F32), pltpu.VMEM((SCORE_SLOTS, TM, 2 * TM), F32)],
        compiler_params=_params(40),
        name="mixer",
    )(qT, k, vT, ctx_arr, x_arr, hf, hb, rest, sc, mods, w_out, g_ffn)


def _ffn_kernel(h_ref, edge_ref, x1_ref, mod_ref, wup_ref, cw_ref, wdn_ref, gfin_ref,
                o_ref, acc_ref, up_ref, *, nt, final_norm):
    hcat = h_ref[0]
    row = lax.broadcasted_iota(jnp.int32, (TM, FF_CHUNK), 0)
    tile = pl.program_id(0) * nt + pl.program_id(1)
    edge_row = (tile % (F32_SUBLANES // 2)) * 2

    def conv(t, w, cols):
        before = jnp.where(row == 0, edge_ref[pl.ds(edge_row, 1), cols], pltpu.roll(t, 1, 0))
        after = jnp.where(row == TM - 1, edge_ref[pl.ds(edge_row + 1, 1), cols],
                          pltpu.roll(t, TM - 1, 0))
        return w[0:1] * before + w[1:2] * t + w[2:3] * after

    n_chunks = D_FF // FF_CHUNK

    def up_project(c):
        cu = slice(c * FF_CHUNK, (c + 1) * FF_CHUNK)
        cg = slice(D_FF + c * FF_CHUNK, D_FF + (c + 1) * FF_CHUNK)
        up_ref[c % UP_SLOTS, :, :FF_CHUNK] = jnp.dot(hcat, wup_ref[:, cu], preferred_element_type=F32)
        up_ref[c % UP_SLOTS, :, FF_CHUNK:] = jnp.dot(hcat, wup_ref[:, cg], preferred_element_type=F32)

    for c in range(min(UP_LOOKAHEAD, n_chunks)):
        up_project(c)
    for c in range(n_chunks):
        if c + UP_LOOKAHEAD < n_chunks:
            up_project(c + UP_LOOKAHEAD)
        cu = slice(c * FF_CHUNK, (c + 1) * FF_CHUNK)
        cg = slice(D_FF + c * FF_CHUNK, D_FF + (c + 1) * FF_CHUNK)
        u = conv(up_ref[c % UP_SLOTS, :, :FF_CHUNK], cw_ref[:, cu], cu)
        g = conv(up_ref[c % UP_SLOTS, :, FF_CHUNK:], cw_ref[:, cg], cg)
        act = ((g * jax.nn.sigmoid(g)) * u).astype(BF16)
        part = jnp.dot(act, wdn_ref[cu, :], preferred_element_type=F32)
        if c == 0:
            acc_ref[...] = part
        elif c < n_chunks - 1:
            acc_ref[...] += part

    gate_ffn = mod_ref[0, 0][5:6]
    x2 = x1_ref[0] + gate_ffn * (acc_ref[...] + part)
    if final_norm:
        x2 = x2 * _rms_scale(x2) * gfin_ref[...]
    o_ref[0] = x2


def _edge_up_kernel(h_ref, w_ref, o_ref):
    o_ref[...] = jnp.dot(h_ref[...], w_ref[...], preferred_element_type=F32)


def _edge_up(tile_edges, w_up, layer, ctx_tiles):
    bsz, nt = tile_edges.shape[:2]
    first, last = tile_edges[:, :, 0].astype(BF16), tile_edges[:, :, 1].astype(BF16)
    t_idx = jnp.arange(nt)
    has_before = jnp.logical_and(t_idx != 0, t_idx != ctx_tiles)[None, :, None]
    has_after = jnp.logical_and(t_idx != nt - 1, t_idx != ctx_tiles - 1)[None, :, None]
    before = jnp.where(has_before, jnp.roll(last, 1, axis=1), 0)
    after = jnp.where(has_after, jnp.roll(first, -1, axis=1), 0)
    rows = jnp.stack([before, after], axis=2).reshape(bsz * nt * 2, D_MODEL)
    n_rows = rows.shape[0]
    padded = -(-n_rows // BF16_SUBLANES) * BF16_SUBLANES
    rows = jnp.pad(rows, ((0, padded - n_rows), (0, 0)))
    n_blk = 2 * D_FF // EDGE_COLS
    return pl.pallas_call(
        _edge_up_kernel,
        grid=(n_blk,),
        in_specs=[pl.BlockSpec((padded, D_MODEL), lambda j: (0, 0)),
                  pl.BlockSpec((None, D_MODEL, EDGE_COLS), lambda j: (layer, 0, j))],
        out_specs=pl.BlockSpec((padded, EDGE_COLS), lambda j: (0, j)),
        out_shape=jax.ShapeDtypeStruct((padded, 2 * D_FF), F32),
        name="edge_up",
    )(rows, w_up)


def _ffn(h2, tile_edges, x1, mods, w_up, conv_w, w_down, layer, g_final, ctx_tiles, final_norm):
    bsz, t_len, _ = h2.shape
    nt = t_len // TM
    edges = _edge_up(tile_edges, w_up, layer, ctx_tiles)
    tiles_per_block = F32_SUBLANES // 2
    return pl.pallas_call(
        functools.partial(_ffn_kernel, nt=nt, final_norm=final_norm),
        grid=(bsz, nt),
        in_specs=[pl.BlockSpec((1, TM, D_MODEL), lambda b, i: (b, i, 0)),
                  pl.BlockSpec((F32_SUBLANES, 2 * D_FF),
                               lambda b, i: ((b * nt + i) // tiles_per_block, 0)),
                  pl.BlockSpec((1, TM, D_MODEL), lambda b, i: (b, i, 0)),
                  pl.BlockSpec((1, 1, MOD_ROWS, D_MODEL),
                               lambda b, i: (b, jnp.minimum(i + 1 - ctx_tiles, 1), 0, 0)),
                  _layer_spec((D_MODEL, 2 * D_FF), layer),
                  _const_spec((3, 2 * D_FF)),
                  _layer_spec((D_FF, D_MODEL), layer),
                  _const_spec((1, D_MODEL))],
        out_specs=pl.BlockSpec((1, TM, D_MODEL), lambda b, i: (b, i, 0)),
        out_shape=jax.ShapeDtypeStruct((bsz, t_len, D_MODEL), F32),
        scratch_shapes=[pltpu.VMEM((TM, D_MODEL), F32),
                        pltpu.VMEM((UP_SLOTS, TM, 2 * FF_CHUNK), F32)],
        compiler_params=_params(48),
        name="conv_ffn",
    )(h2, edges, x1, mods, w_up, conv_w, w_down, g_final)


def _rope_tables(seq, ctx_len):
    rows = seq // GRID_W
    pos_r = np.repeat(np.arange(rows, dtype=np.float32), GRID_W)
    pos_c = np.tile(np.arange(GRID_W, dtype=np.float32), rows)
    n_f = HEAD_DIM // 4
    inv = (np.float32(ROPE_THETA) ** (-np.arange(n_f, dtype=np.float32) / n_f)).astype(np.float32)
    ang = np.concatenate([pos_r[:, None] * inv, pos_c[:, None] * inv], axis=-1).astype(np.float32)
    cos = np.concatenate([np.ones((ctx_len, HEAD_DIM // 2)), np.cos(ang.astype(np.float64))], axis=0)
    sin = np.concatenate([np.zeros((ctx_len, HEAD_DIM // 2)), np.sin(ang.astype(np.float64))], axis=0)
    return jnp.asarray(cos.T, F32), jnp.asarray(sin.T, F32)


def _gate_weights(wa, ba, wi, bi):
    def dense(w):
        eye = jnp.eye(LRU_BLOCKS, dtype=w.dtype)
        return jnp.einsum("nde,nm->ndme", w, eye).reshape(LRU_WIDTH, LRU_WIDTH)
    w = jnp.concatenate([dense(wa[0]), dense(wi[0]), dense(wa[1]), dense(wi[1])], axis=1)
    b = jnp.concatenate([ba[0], bi[0], ba[1], bi[1]])[None, :]
    return w.astype(BF16), b


def kernel(x, c, ctx, c_ctx, w_mod, b_mod, g_mix, g_ffn, w_in, g_q, g_k, lru_conv_w, lru_conv_b,
           lru_wa, lru_ba, lru_wi, lru_bi, lru_lam, sc_conv_w, w_out, w_up, ffn_conv_w, w_down,
           g_final):
    bsz, seq, _ = x.shape
    ctx_len = ctx.shape[1]
    depth = w_mod.shape[0]
    assert ctx_len == TM and seq % TM == 0 and seq % GRID_W == 0

    cos_t, sin_t = _rope_tables(seq, ctx_len)
    ctx_arr, x_arr, x_base = ctx, x, 0
    cc_rows = -(-(bsz + 1) // F32_SUBLANES) * F32_SUBLANES
    cc = jnp.zeros((cc_rows, D_MODEL), F32).at[:bsz].set(c).at[bsz].set(c_ctx)

    mod_layers = _modulation(cc, w_mod, b_mod)
    w_in_b = _to_bf16(w_in, D_MODEL // 2)
    w_out_b = _to_bf16(w_out, D_MODEL)
    w_up_b = _to_bf16(w_up, D_MODEL // 8)
    w_down_b = _to_bf16(w_down, D_FF // 4)

    out = None
    for l in range(depth):
        last = l == depth - 1
        mod_all = mod_layers[l].reshape(cc_rows, N_MOD, D_MODEL)
        mod_x = mod_all[:bsz]
        mod_c = jnp.broadcast_to(mod_all[bsz][None], (bsz, N_MOD, D_MODEL))
        mods = jnp.stack([mod_c, mod_x], axis=1)
        mods = jnp.pad(mods, ((0, 0), (0, 0), (0, MOD_ROWS - N_MOD), (0, 0)))

        gq = jnp.broadcast_to((g_q[l] * Q_SCALE)[:, None], (HEAD_DIM, TM))
        gk = jnp.broadcast_to(g_k[l][:, None], (HEAD_DIM, TM))
        qT, k, vT, rest = _inproj(ctx_arr, x_arr, x_base, mods, g_mix[l][None, :], w_in_b, l,
                                  gq, gk, cos_t, sin_t)

        w_gate, b_gate = _gate_weights(lru_wa[l], lru_ba[l], lru_wi[l], lru_bi[l])
        hf, hb, sc = _lru(rest, lru_conv_w[l], lru_conv_b[l][None, :], w_gate, b_gate, lru_lam[l],
                          sc_conv_w[l])

        tile_offset = 1 if last else 0
        x1, h2, tile_edges = _mixer(qT, k, vT, ctx_arr, x_arr, x_base, hf, hb, rest, sc, mods,
                                    w_out_b, l, g_ffn[l][None, :], tile_offset)
        out = _ffn(h2, tile_edges, x1, mods, w_up_b, ffn_conv_w[l], w_down_b, l, g_final[None, :],
                   ctx_tiles=1 - tile_offset, final_norm=last)
        ctx_arr, x_arr, x_base = out, out, 1
    return out
```

```python
import functools

import numpy as np
import jax
import jax.numpy as jnp
from jax import lax
from jax.experimental import pallas as pl
from jax.experimental.pallas import tpu as pltpu

F32 = jnp.float32
BF16 = jnp.bfloat16

D_MODEL = 1024
HEAD_DIM = 64
N_HEADS = 8
N_KV_HEADS = 2
HEADS_PER_KV = N_HEADS // N_KV_HEADS
ATTN_WIDTH = N_HEADS * HEAD_DIM
KV_WIDTH = N_KV_HEADS * HEAD_DIM
LRU_WIDTH = 256
LRU_BLOCKS = 4
LRU_BLOCK = LRU_WIDTH // LRU_BLOCKS
LRU_CONV = 4
LRU_C = 8.0
SC_WIDTH = 256
IN_WIDTH = 2048
D_FF = 2816
FF_CHUNK = 256
EDGE_COLS = 1408
GRID_W = 64
ROPE_THETA = 10000.0
EPS = 1e-6
Q_SCALE = HEAD_DIM ** -0.5 * 1.4426950408889634

TM = 256
SCORE_LOOKAHEAD = 3
SCORE_SLOTS = SCORE_LOOKAHEAD + 1
UP_LOOKAHEAD = 3
UP_SLOTS = UP_LOOKAHEAD + 1
LANES = 128
F32_SUBLANES = 8
BF16_SUBLANES = 16
N_MOD = 6
MOD_ROWS = 8

OFF_Q = 0
OFF_K = OFF_Q + ATTN_WIDTH
OFF_V = OFF_K + KV_WIDTH
OFF_R = OFF_V + KV_WIDTH
OFF_G = OFF_R + LRU_WIDTH
OFF_B = OFF_G + LRU_WIDTH
OFF_C = OFF_B + SC_WIDTH
OFF_U = OFF_C + SC_WIDTH


def _const_spec(shape):
    nd = len(shape)
    return pl.BlockSpec(shape, lambda *_: (0,) * nd, pipeline_mode=pl.Buffered(1))


def _layer_spec(shape, layer):
    nd = len(shape)
    return pl.BlockSpec((None,) + tuple(shape), lambda *_: (layer,) + (0,) * nd,
                        pipeline_mode=pl.Buffered(1))


CAST_ROW_BLOCKS = 8


def _cast_kernel(*refs):
    n = len(refs) // 2
    for x_ref, o_ref in zip(refs[:n], refs[n:]):
        o_ref[...] = x_ref[...].astype(BF16)


def _to_bf16(*stacks):
    depth = stacks[0].shape[0]
    specs = [pl.BlockSpec((None, w.shape[1] // CAST_ROW_BLOCKS, w.shape[2]), lambda l, r: (l, r, 0))
             for w in stacks]
    return pl.pallas_call(
        _cast_kernel,
        grid=(depth, CAST_ROW_BLOCKS),
        in_specs=specs,
        out_specs=specs,
        out_shape=[jax.ShapeDtypeStruct(w.shape, BF16) for w in stacks],
        compiler_params=_params(40),
        name="weights_to_bf16",
    )(*stacks)


def _params(vmem_mb, n_grid_dims=2):
    return pltpu.CompilerParams(dimension_semantics=("arbitrary",) * n_grid_dims,
                                vmem_limit_bytes=vmem_mb * 1024 * 1024)


def _rms_scale(t):
    return lax.rsqrt(jnp.mean(t * t, axis=-1, keepdims=True) + EPS)


def _mod_kernel(c_ref, w_ref, b_ref, o_ref):
    c = c_ref[...]
    h = (c * jax.nn.sigmoid(c)).astype(BF16)
    o_ref[...] = jnp.dot(h, w_ref[...].astype(BF16), preferred_element_type=F32) + b_ref[...]


def _modulation(cc, w_mod, b_mod):
    rows = cc.shape[0]
    depth = w_mod.shape[0]
    return pl.pallas_call(
        _mod_kernel,
        grid=(depth, N_MOD),
        in_specs=[pl.BlockSpec((rows, D_MODEL), lambda l, j: (0, 0)),
                  pl.BlockSpec((None, D_MODEL, D_MODEL), lambda l, j: (l, 0, j)),
                  pl.BlockSpec((None, 1, D_MODEL), lambda l, j: (l, 0, j))],
        out_specs=pl.BlockSpec((None, rows, D_MODEL), lambda l, j: (l, 0, j)),
        out_shape=jax.ShapeDtypeStruct((depth, rows, N_MOD * D_MODEL), F32),
        name="modulation",
    )(cc, w_mod, b_mod[:, None, :])


def _norm_rope_transposed(t, gain, cos, sin):
    half = HEAD_DIM // 2
    out = []
    for h in range(t.shape[0] // HEAD_DIM):
        r = t[h * HEAD_DIM:(h + 1) * HEAD_DIM]
        r = r * lax.rsqrt(jnp.mean(r * r, axis=0, keepdims=True) + EPS) * gain
        x1, x2 = r[:half], r[half:]
        out += [x1 * cos - x2 * sin, x1 * sin + x2 * cos]
    return jnp.concatenate(out, axis=0)


def _inproj_kernel(ctx_ref, x_ref, mod_ref, gmix_ref, w_ref, gq_ref, gk_ref, cos_ref, sin_ref,
                   qT_ref, k_ref, vT_ref, rest_ref, y_even_ref, y_odd_ref, *, nt, n_tiles):
    s = pl.program_id(0)
    tile = jnp.minimum(s, n_tiles - 1) % nt

    def project(y_ref):
        x = jnp.where(tile == 0, ctx_ref[0], x_ref[0])
        mod = mod_ref[0, 0]
        shift, scale = mod[0:1], mod[1:2]
        h = x * _rms_scale(x) * gmix_ref[...]
        h = h * (1.0 + scale) + shift
        y_ref[...] = jnp.dot(h.astype(BF16), w_ref[...], preferred_element_type=F32)

    def finish(y_ref):
        cos, sin = cos_ref[...], sin_ref[...]
        q = _norm_rope_transposed(y_ref[:, OFF_Q:OFF_Q + ATTN_WIDTH].T, gq_ref[...], cos, sin)
        qT_ref[0] = q.astype(BF16)
        k = _norm_rope_transposed(y_ref[:, OFF_K:OFF_K + KV_WIDTH].T, gk_ref[...], cos, sin)
        k_ref[0] = k.T.astype(BF16)
        vT_ref[0] = y_ref[:, OFF_V:OFF_V + KV_WIDTH].T.astype(BF16)
        rest_ref[0, :, 0:LRU_WIDTH] = y_ref[:, OFF_R:OFF_R + LRU_WIDTH]
        rest_ref[0, :, LRU_WIDTH:2 * LRU_WIDTH] = jax.nn.gelu(y_ref[:, OFF_G:OFF_G + LRU_WIDTH])
        rest_ref[0, :, 2 * LRU_WIDTH:2 * LRU_WIDTH + SC_WIDTH] = y_ref[:, OFF_B:OFF_B + SC_WIDTH]
        rest_ref[0, :, 2 * LRU_WIDTH + SC_WIDTH:] = (y_ref[:, OFF_C:OFF_C + SC_WIDTH]
                                                     * y_ref[:, OFF_U:OFF_U + SC_WIDTH])

    @pl.when(s == 0)
    def _():
        y_odd_ref[...] = jnp.zeros_like(y_odd_ref)

    @pl.when(s % 2 == 0)
    def _():
        project(y_even_ref)
        finish(y_odd_ref)

    @pl.when(s % 2 == 1)
    def _():
        project(y_odd_ref)
        finish(y_even_ref)


def _token_specs(x_base, tile_offset=0):
    return [pl.BlockSpec((1, TM, D_MODEL), lambda b, i: (b, 0, 0)),
            pl.BlockSpec((1, TM, D_MODEL),
                         lambda b, i: (b, x_base + jnp.maximum(i + tile_offset - 1, 0), 0))]


def _inproj(ctx_arr, x_arr, x_base, mods, g_mix, w_in, layer, gq, gk, cos_t, sin_t):
    bsz = x_arr.shape[0]
    nt = x_arr.shape[1] // TM - x_base + 1
    t_all = nt * TM
    n_tiles = bsz * nt
    rest_w = 2 * LRU_WIDTH + 2 * SC_WIDTH

    def cur(s):
        sc = jnp.minimum(s, n_tiles - 1)
        return sc // nt, sc % nt

    def prev(s):
        sp = jnp.maximum(s - 1, 0)
        return sp // nt, sp % nt

    rope_spec = pl.BlockSpec((HEAD_DIM // 2, TM), lambda s: (0, prev(s)[1]))
    return pl.pallas_call(
        functools.partial(_inproj_kernel, nt=nt, n_tiles=n_tiles),
        grid=(n_tiles + 1,),
        in_specs=[pl.BlockSpec((1, TM, D_MODEL), lambda s: (cur(s)[0], 0, 0)),
                  pl.BlockSpec((1, TM, D_MODEL),
                               lambda s: (cur(s)[0], x_base + jnp.maximum(cur(s)[1] - 1, 0), 0)),
                  pl.BlockSpec((1, 1, MOD_ROWS, D_MODEL),
                               lambda s: (cur(s)[0], jnp.minimum(cur(s)[1], 1), 0, 0)),
                  _const_spec((1, D_MODEL)),
                  _layer_spec((D_MODEL, IN_WIDTH), layer),
                  _const_spec((HEAD_DIM, TM)),
                  _const_spec((HEAD_DIM, TM)),
                  rope_spec,
                  rope_spec],
        out_specs=[pl.BlockSpec((1, ATTN_WIDTH, TM), lambda s: (prev(s)[0], 0, prev(s)[1])),
                   pl.BlockSpec((1, TM, KV_WIDTH), lambda s: (prev(s)[0], prev(s)[1], 0)),
                   pl.BlockSpec((1, KV_WIDTH, TM), lambda s: (prev(s)[0], 0, prev(s)[1])),
                   pl.BlockSpec((1, TM, rest_w), lambda s: (prev(s)[0], prev(s)[1], 0))],
        out_shape=[jax.ShapeDtypeStruct((bsz, ATTN_WIDTH, t_all), BF16),
                   jax.ShapeDtypeStruct((bsz, t_all, KV_WIDTH), BF16),
                   jax.ShapeDtypeStruct((bsz, KV_WIDTH, t_all), BF16),
                   jax.ShapeDtypeStruct((bsz, t_all, rest_w), F32)],
        scratch_shapes=[pltpu.VMEM((TM, IN_WIDTH), F32)] * 2,
        compiler_params=_params(40, n_grid_dims=1),
        name="inproj",
    )(ctx_arr, x_arr, mods, g_mix, w_in, gq, gk, cos_t, sin_t)


def _with_halo(prev_ref, cur_ref, next_ref, valid_prev, valid_next):
    prev = prev_ref[0] * valid_prev.astype(F32)
    nxt = next_ref[0] * valid_next.astype(F32)
    return jnp.concatenate([prev, cur_ref[0], nxt], axis=0)


def _shifted(tcat, offset):
    rows = tcat.shape[0]
    if offset == 0:
        return tcat[F32_SUBLANES:F32_SUBLANES + TM]
    return pltpu.roll(tcat, (-offset) % rows, 0)[F32_SUBLANES:F32_SUBLANES + TM]


def _lru_coeffs(rcat, conv_w, conv_b, w_gate, b_gate, sp_lam):
    xc = conv_b
    for j in range(LRU_CONV):
        xc = xc + conv_w[j:j + 1] * _shifted(rcat, j - 2)
    z = jnp.dot(xc.astype(BF16), w_gate, preferred_element_type=F32) + b_gate
    r_gate = jax.nn.sigmoid(z[:, :LRU_WIDTH])
    i_gate = jax.nn.sigmoid(z[:, LRU_WIDTH:])
    log_a = (-LRU_C) * r_gate * sp_lam
    a = jnp.exp(log_a)
    one_minus_a2 = -jnp.tanh(log_a) * (1.0 + a * a)
    return a, jnp.sqrt(one_minus_a2) * (i_gate * xc)


def _lru_kernel(rf_ref, rfp_ref, rfn_ref, rb_ref, rbp_ref, rbn_ref, bg_ref, cu_ref, cup_ref, cun_ref,
                cw_ref, cb_ref, wg_ref, bgate_ref, lam_ref, scw_ref,
                hf_ref, hb_ref, sc_ref,
                af_s, bf_s, ab_s, bb_s, hf_state, hb_state, *, nt):
    i = pl.program_id(1)
    j = jnp.where(i == 0, 0, nt - i)

    @pl.when(i == 0)
    def _():
        hf_state[...] = jnp.zeros_like(hf_state)
        hb_state[...] = jnp.zeros_like(hb_state)

    def halo_valid(t):
        return jnp.logical_and(t != 0, t != 1), jnp.logical_and(t != 0, t != nt - 1)

    lam = lam_ref[...]
    sp_lam = jnp.maximum(-lam, 0.0) + jnp.log1p(jnp.exp(-jnp.abs(lam)))
    cw = cw_ref[...]
    cb = cb_ref[...]

    vp, vn = halo_valid(i)
    a, b = _lru_coeffs(_with_halo(rfp_ref, rf_ref, rfn_ref, vp, vn), cw, cb,
                       wg_ref[:, :2 * LRU_WIDTH], bgate_ref[:, :2 * LRU_WIDTH], sp_lam[0:1])
    _block_prefix(a, b, af_s, bf_s, reverse=False)
    cucat = _with_halo(cup_ref, cu_ref, cun_ref, vp, vn)
    scw = scw_ref[...]
    conv = scw[0:1] * _shifted(cucat, -1) + scw[1:2] * _shifted(cucat, 0) + scw[2:3] * _shifted(cucat, 1)
    sc_ref[0] = (bg_ref[0] * conv).astype(BF16)

    vp, vn = halo_valid(j)
    a, b = _lru_coeffs(_with_halo(rbp_ref, rb_ref, rbn_ref, vp, vn), cw, cb,
                       wg_ref[:, 2 * LRU_WIDTH:], bgate_ref[:, 2 * LRU_WIDTH:], sp_lam[1:2])
    _block_prefix(a, b, ab_s, bb_s, reverse=True)

    _carry_blocks(af_s, bf_s, hf_ref, hf_state, reverse=False)
    _carry_blocks(ab_s, bb_s, hb_ref, hb_state, reverse=True)


def _block_prefix(a, b, a_s, b_s, reverse):
    n = F32_SUBLANES
    a = a.reshape(TM // n, n, LRU_WIDTH)
    b = b.reshape(TM // n, n, LRU_WIDTH)
    row = lax.broadcasted_iota(jnp.int32, a.shape, 1)
    d = 1
    while d < n:
        shift, keep = (n - d, row < n - d) if reverse else (d, row >= d)
        a_prev = jnp.where(keep, pltpu.roll(a, shift, 1), 1.0)
        b_prev = jnp.where(keep, pltpu.roll(b, shift, 1), 0.0)
        b = a * b_prev + b
        a = a * a_prev
        d *= 2
    a_s[...] = a
    b_s[...] = b


def _carry_blocks(a_s, b_s, out_ref, state_ref, reverse):
    n = F32_SUBLANES
    nb = TM // n
    last = 0 if reverse else n - 1
    h_in = state_ref[...]
    for v in (reversed(range(nb)) if reverse else range(nb)):
        h = a_s[v] * h_in + b_s[v]
        out_ref[0, v * n:(v + 1) * n, :] = h
        h_in = h[last:last + 1]
    state_ref[...] = h_in


def _lru(rest, conv_w, conv_b, w_gate, b_gate, lam, sc_w):
    bsz, t_all, _ = rest.shape
    nt = t_all // TM
    hpt = TM // F32_SUBLANES
    nhb = t_all // F32_SUBLANES

    def rev(i):
        return jnp.where(i == 0, 0, nt - i)

    def tile(col, order):
        return pl.BlockSpec((1, TM, LRU_WIDTH), lambda b, i: (b, order(i), col))

    def prev_halo(col, order):
        return pl.BlockSpec((1, F32_SUBLANES, LRU_WIDTH),
                            lambda b, i: (b, jnp.maximum(order(i) * hpt - 1, 0), col))

    def next_halo(col, order):
        return pl.BlockSpec((1, F32_SUBLANES, LRU_WIDTH),
                            lambda b, i: (b, jnp.minimum((order(i) + 1) * hpt, nhb - 1), col))

    fwd = lambda i: i
    out_f32 = jax.ShapeDtypeStruct((bsz, t_all, LRU_WIDTH), F32)
    return pl.pallas_call(
        functools.partial(_lru_kernel, nt=nt),
        grid=(bsz, nt),
        in_specs=[tile(0, fwd), prev_halo(0, fwd), next_halo(0, fwd),
                  tile(0, rev), prev_halo(0, rev), next_halo(0, rev),
                  tile(2, fwd),
                  tile(3, fwd), prev_halo(3, fwd), next_halo(3, fwd),
                  _const_spec((LRU_CONV, LRU_WIDTH)),
                  _const_spec((1, LRU_WIDTH)),
                  _const_spec((LRU_WIDTH, 4 * LRU_WIDTH)),
                  _const_spec((1, 4 * LRU_WIDTH)),
                  _const_spec((2, LRU_WIDTH)),
                  _const_spec((3, SC_WIDTH))],
        out_specs=[pl.BlockSpec((1, TM, LRU_WIDTH), lambda b, i: (b, i, 0)),
                   pl.BlockSpec((1, TM, LRU_WIDTH), lambda b, i: (b, rev(i), 0)),
                   pl.BlockSpec((1, TM, SC_WIDTH), lambda b, i: (b, i, 0))],
        out_shape=[out_f32, out_f32, jax.ShapeDtypeStruct((bsz, t_all, SC_WIDTH), BF16)],
        scratch_shapes=([pltpu.VMEM((TM // F32_SUBLANES, F32_SUBLANES, LRU_WIDTH), F32)] * 4
                        + [pltpu.VMEM((1, LRU_WIDTH), F32)] * 2),
        compiler_params=_params(32),
        name="lru_scan",
    )(rest, rest, rest, rest, rest, rest, rest, rest, rest, rest,
      conv_w, conv_b, w_gate, b_gate, lam, sc_w)


def _mixer_kernel(qT_ref, k_ref, vT_ref, ctx_ref, x_ref, hf_ref, hb_ref, gg_ref, sc_ref, mod_ref, wo_ref,
                  gffn_ref, x1_ref, h2_ref, edge_ref, oT_ref, s_ref, *, tile_offset, n_key_tiles):
    i = pl.program_id(1) + tile_offset
    ones_rows = jnp.ones((BF16_SUBLANES, TM), BF16)

    def project_out():
        x = x_ref[0]
        if tile_offset == 0:
            x = jnp.where(i == 0, ctx_ref[0], x)
        mod = mod_ref[0, 0]
        gate_mix, shift, scale = mod[2:3], mod[3:4], mod[4:5]
        att = oT_ref[...].T.astype(BF16)
        lru = (gg_ref[0] * (hf_ref[0] + hb_ref[0])).astype(BF16)
        y = (jnp.dot(att, wo_ref[0:ATTN_WIDTH], preferred_element_type=F32)
             + jnp.dot(lru, wo_ref[ATTN_WIDTH:ATTN_WIDTH + LRU_WIDTH], preferred_element_type=F32)
             + jnp.dot(sc_ref[0], wo_ref[ATTN_WIDTH + LRU_WIDTH:], preferred_element_type=F32))
        x1 = x + gate_mix * y
        x1_ref[0] = x1
        h2 = x1 * _rms_scale(x1) * gffn_ref[...]
        h2 = h2 * (1.0 + scale) + shift
        h2_ref[0] = h2.astype(BF16)
        edge_ref[0, 0] = jnp.concatenate(
            [h2[0:1], h2[TM - 1:TM], jnp.zeros((F32_SUBLANES - 2, D_MODEL), F32)], axis=0)

    def run(n_tiles):
        items = [(pr, c) for pr in range(N_HEADS // 2) for c in range(n_tiles)]
        qpads = {}

        def qpad_of(pr):
            if pr not in qpads:
                row0 = pr * 2 * HEAD_DIM
                q2 = jnp.concatenate([qT_ref[0, row0:row0 + HEAD_DIM, :],
                                      qT_ref[0, row0 + HEAD_DIM:row0 + 2 * HEAD_DIM, :]], axis=1)
                zero = jnp.zeros_like(q2)
                first_kv = (2 * pr) // HEADS_PER_KV == 0
                qpads[pr] = jnp.concatenate([q2, zero] if first_kv else [zero, q2], axis=0)
            return qpads[pr]

        def scores(t):
            pr, c = items[t]
            s = jnp.dot(k_ref[0, c * TM:(c + 1) * TM, :], qpad_of(pr), preferred_element_type=F32)
            s_ref[t % SCORE_SLOTS] = s
            return jnp.max(s, axis=0, keepdims=True)

        tile_max = [scores(t) for t in range(min(SCORE_LOOKAHEAD, len(items)))]
        m = acc = None
        for t, (pr, c) in enumerate(items):
            if t + SCORE_LOOKAHEAD < len(items):
                tile_max.append(scores(t + SCORE_LOOKAHEAD))
            if c == 0:
                m = jnp.full((1, 2 * TM), -jnp.inf, F32)
                acc = jnp.zeros((HEAD_DIM + BF16_SUBLANES, 2 * TM), F32)
            g = (2 * pr) // HEADS_PER_KV
            m_new = jnp.maximum(m, tile_max[t])
            alpha = jnp.exp2(m - m_new)
            p = jnp.exp2(s_ref[t % SCORE_SLOTS] - m_new).astype(BF16)
            vt = jnp.concatenate([vT_ref[0, g * HEAD_DIM:(g + 1) * HEAD_DIM, c * TM:(c + 1) * TM],
                                  ones_rows], axis=0)
            acc = alpha * acc + jnp.dot(vt, p, preferred_element_type=F32)
            m = m_new
            tile_max[t] = None
            if c == n_tiles - 1:
                o = acc[:HEAD_DIM] / acc[HEAD_DIM:HEAD_DIM + 1]
                row0 = pr * 2 * HEAD_DIM
                oT_ref[row0:row0 + HEAD_DIM, :] = o[:, :TM]
                oT_ref[row0 + HEAD_DIM:row0 + 2 * HEAD_DIM, :] = o[:, TM:]
        project_out()

    @pl.when(i == 0)
    def _():
        run(1)

    @pl.when(i != 0)
    def _():
        run(n_key_tiles)


def _mixer(qT, k, vT, ctx_arr, x_arr, x_base, hf, hb, rest, sc, mods, w_out, layer, g_ffn, tile_offset):
    bsz, t_all, _ = k.shape
    nt = t_all // TM
    nq = nt - tile_offset
    off = lambda b, i: (b, i + tile_offset, 0)
    return pl.pallas_call(
        functools.partial(_mixer_kernel, tile_offset=tile_offset, n_key_tiles=nt),
        grid=(bsz, nq),
        in_specs=[pl.BlockSpec((1, ATTN_WIDTH, TM), lambda b, i: (b, 0, i + tile_offset)),
                  pl.BlockSpec((1, t_all, KV_WIDTH), lambda b, i: (b, 0, 0)),
                  pl.BlockSpec((1, KV_WIDTH, t_all), lambda b, i: (b, 0, 0))]
                 + _token_specs(x_base, tile_offset) + [
                  pl.BlockSpec((1, TM, LRU_WIDTH), off),
                  pl.BlockSpec((1, TM, LRU_WIDTH), off),
                  pl.BlockSpec((1, TM, LRU_WIDTH), lambda b, i: (b, i + tile_offset, 1)),
                  pl.BlockSpec((1, TM, SC_WIDTH), off),
                  pl.BlockSpec((1, 1, MOD_ROWS, D_MODEL),
                               lambda b, i: (b, jnp.minimum(i + tile_offset, 1), 0, 0)),
                  _layer_spec((D_MODEL, D_MODEL), layer),
                  _const_spec((1, D_MODEL))],
        out_specs=[pl.BlockSpec((1, TM, D_MODEL), lambda b, i: (b, i, 0)),
                   pl.BlockSpec((1, TM, D_MODEL), lambda b, i: (b, i, 0)),
                   pl.BlockSpec((1, 1, F32_SUBLANES, D_MODEL), lambda b, i: (b, i, 0, 0))],
        out_shape=[jax.ShapeDtypeStruct((bsz, nq * TM, D_MODEL), F32),
                   jax.ShapeDtypeStruct((bsz, nq * TM, D_MODEL), BF16),
                   jax.ShapeDtypeStruct((bsz, nq, F32_SUBLANES, D_MODEL), F32)],
        scratch_shapes=[pltpu.VMEM((ATTN_WIDTH, TM), F32), pltpu.VMEM((SCORE_SLOTS, TM, 2 * TM), F32)],
        compiler_params=_params(40),
        name="mixer",
    )(qT, k, vT, ctx_arr, x_arr, hf, hb, rest, sc, mods, w_out, g_ffn)


def _ffn_kernel(h_ref, edge_ref, x1_ref, mod_ref, wup_ref, cw_ref, wdn_ref, gfin_ref,
                o_ref, acc_ref, up_ref, *, nt, final_norm):
    hcat = h_ref[0]
    row = lax.broadcasted_iota(jnp.int32, (TM, FF_CHUNK), 0)
    tile = pl.program_id(0) * nt + pl.program_id(1)
    edge_row = (tile % (F32_SUBLANES // 2)) * 2

    def conv(t, w, cols):
        before = jnp.where(row == 0, edge_ref[pl.ds(edge_row, 1), cols], pltpu.roll(t, 1, 0))
        after = jnp.where(row == TM - 1, edge_ref[pl.ds(edge_row + 1, 1), cols],
                          pltpu.roll(t, TM - 1, 0))
        return w[0:1] * before + w[1:2] * t + w[2:3] * after

    n_chunks = D_FF // FF_CHUNK

    def up_project(c):
        cu = slice(c * FF_CHUNK, (c + 1) * FF_CHUNK)
        cg = slice(D_FF + c * FF_CHUNK, D_FF + (c + 1) * FF_CHUNK)
        up_ref[c % UP_SLOTS, :, :FF_CHUNK] = jnp.dot(hcat, wup_ref[:, cu], preferred_element_type=F32)
        up_ref[c % UP_SLOTS, :, FF_CHUNK:] = jnp.dot(hcat, wup_ref[:, cg], preferred_element_type=F32)

    for c in range(min(UP_LOOKAHEAD, n_chunks)):
        up_project(c)
    for c in range(n_chunks):
        if c + UP_LOOKAHEAD < n_chunks:
            up_project(c + UP_LOOKAHEAD)
        cu = slice(c * FF_CHUNK, (c + 1) * FF_CHUNK)
        cg = slice(D_FF + c * FF_CHUNK, D_FF + (c + 1) * FF_CHUNK)
        u = conv(up_ref[c % UP_SLOTS, :, :FF_CHUNK], cw_ref[:, cu], cu)
        g = conv(up_ref[c % UP_SLOTS, :, FF_CHUNK:], cw_ref[:, cg], cg)
        act = ((g * jax.nn.sigmoid(g)) * u).astype(BF16)
        part = jnp.dot(act, wdn_ref[cu, :], preferred_element_type=F32)
        if c == 0:
            acc_ref[...] = part
        elif c < n_chunks - 1:
            acc_ref[...] += part

    gate_ffn = mod_ref[0, 0][5:6]
    x2 = x1_ref[0] + gate_ffn * (acc_ref[...] + part)
    if final_norm:
        x2 = x2 * _rms_scale(x2) * gfin_ref[...]
    o_ref[0] = x2


def _edge_up_kernel(h_ref, w_ref, o_ref):
    o_ref[...] = jnp.dot(h_ref[...], w_ref[...], preferred_element_type=F32)


def _edge_up(tile_edges, w_up, layer, ctx_tiles):
    bsz, nt = tile_edges.shape[:2]
    first, last = tile_edges[:, :, 0].astype(BF16), tile_edges[:, :, 1].astype(BF16)
    t_idx = jnp.arange(nt)
    has_before = jnp.logical_and(t_idx != 0, t_idx != ctx_tiles)[None, :, None]
    has_after = jnp.logical_and(t_idx != nt - 1, t_idx != ctx_tiles - 1)[None, :, None]
    before = jnp.where(has_before, jnp.roll(last, 1, axis=1), 0)
    after = jnp.where(has_after, jnp.roll(first, -1, axis=1), 0)
    rows = jnp.stack([before, after], axis=2).reshape(bsz * nt * 2, D_MODEL)
    n_rows = rows.shape[0]
    padded = -(-n_rows // BF16_SUBLANES) * BF16_SUBLANES
    rows = jnp.pad(rows, ((0, padded - n_rows), (0, 0)))
    n_blk = 2 * D_FF // EDGE_COLS
    return pl.pallas_call(
        _edge_up_kernel,
        grid=(n_blk,),
        in_specs=[pl.BlockSpec((padded, D_MODEL), lambda j: (0, 0)),
                  pl.BlockSpec((None, D_MODEL, EDGE_COLS), lambda j: (layer, 0, j))],
        out_specs=pl.BlockSpec((padded, EDGE_COLS), lambda j: (0, j)),
        out_shape=jax.ShapeDtypeStruct((padded, 2 * D_FF), F32),
        name="edge_up",
    )(rows, w_up)


def _ffn(h2, tile_edges, x1, mods, w_up, conv_w, w_down, layer, g_final, ctx_tiles, final_norm):
    bsz, t_len, _ = h2.shape
    nt = t_len // TM
    edges = _edge_up(tile_edges, w_up, layer, ctx_tiles)
    tiles_per_block = F32_SUBLANES // 2
    return pl.pallas_call(
        functools.partial(_ffn_kernel, nt=nt, final_norm=final_norm),
        grid=(bsz, nt),
        in_specs=[pl.BlockSpec((1, TM, D_MODEL), lambda b, i: (b, i, 0)),
                  pl.BlockSpec((F32_SUBLANES, 2 * D_FF),
                               lambda b, i: ((b * nt + i) // tiles_per_block, 0)),
                  pl.BlockSpec((1, TM, D_MODEL), lambda b, i: (b, i, 0)),
                  pl.BlockSpec((1, 1, MOD_ROWS, D_MODEL),
                               lambda b, i: (b, jnp.minimum(i + 1 - ctx_tiles, 1), 0, 0)),
                  _layer_spec((D_MODEL, 2 * D_FF), layer),
                  _const_spec((3, 2 * D_FF)),
                  _layer_spec((D_FF, D_MODEL), layer),
                  _const_spec((1, D_MODEL))],
        out_specs=pl.BlockSpec((1, TM, D_MODEL), lambda b, i: (b, i, 0)),
        out_shape=jax.ShapeDtypeStruct((bsz, t_len, D_MODEL), F32),
        scratch_shapes=[pltpu.VMEM((TM, D_MODEL), F32),
                        pltpu.VMEM((UP_SLOTS, TM, 2 * FF_CHUNK), F32)],
        compiler_params=_params(48),
        name="conv_ffn",
    )(h2, edges, x1, mods, w_up, conv_w, w_down, g_final)


def _rope_tables(seq, ctx_len):
    rows = seq // GRID_W
    pos_r = np.repeat(np.arange(rows, dtype=np.float32), GRID_W)
    pos_c = np.tile(np.arange(GRID_W, dtype=np.float32), rows)
    n_f = HEAD_DIM // 4
    inv = (np.float32(ROPE_THETA) ** (-np.arange(n_f, dtype=np.float32) / n_f)).astype(np.float32)
    ang = np.concatenate([pos_r[:, None] * inv, pos_c[:, None] * inv], axis=-1).astype(np.float32)
    cos = np.concatenate([np.ones((ctx_len, HEAD_DIM // 2)), np.cos(ang.astype(np.float64))], axis=0)
    sin = np.concatenate([np.zeros((ctx_len, HEAD_DIM // 2)), np.sin(ang.astype(np.float64))], axis=0)
    return jnp.asarray(cos.T, F32), jnp.asarray(sin.T, F32)


def _gate_weights(wa, ba, wi, bi):
    def dense(w):
        eye = jnp.eye(LRU_BLOCKS, dtype=w.dtype)
        return jnp.einsum("nde,nm->ndme", w, eye).reshape(LRU_WIDTH, LRU_WIDTH)
    w = jnp.concatenate([dense(wa[0]), dense(wi[0]), dense(wa[1]), dense(wi[1])], axis=1)
    b = jnp.concatenate([ba[0], bi[0], ba[1], bi[1]])[None, :]
    return w.astype(BF16), b


def kernel(x, c, ctx, c_ctx, w_mod, b_mod, g_mix, g_ffn, w_in, g_q, g_k, lru_conv_w, lru_conv_b,
           lru_wa, lru_ba, lru_wi, lru_bi, lru_lam, sc_conv_w, w_out, w_up, ffn_conv_w, w_down,
           g_final):
    bsz, seq, _ = x.shape
    ctx_len = ctx.shape[1]
    depth = w_mod.shape[0]
    assert ctx_len == TM and seq % TM == 0 and seq % GRID_W == 0

    cos_t, sin_t = _rope_tables(seq, ctx_len)
    ctx_arr, x_arr, x_base = ctx, x, 0
    cc_rows = -(-(bsz + 1) // F32_SUBLANES) * F32_SUBLANES
    cc = jnp.zeros((cc_rows, D_MODEL), F32).at[:bsz].set(c).at[bsz].set(c_ctx)

    mod_layers = _modulation(cc, w_mod, b_mod)
    w_in_b, w_out_b, w_up_b, w_down_b = _to_bf16(w_in, w_out, w_up, w_down)

    out = None
    for l in range(depth):
        last = l == depth - 1
        mod_all = mod_layers[l].reshape(cc_rows, N_MOD, D_MODEL)
        mod_x = mod_all[:bsz]
        mod_c = jnp.broadcast_to(mod_all[bsz][None], (bsz, N_MOD, D_MODEL))
        mods = jnp.stack([mod_c, mod_x], axis=1)
        mods = jnp.pad(mods, ((0, 0), (0, 0), (0, MOD_ROWS - N_MOD), (0, 0)))

        gq = jnp.broadcast_to((g_q[l] * Q_SCALE)[:, None], (HEAD_DIM, TM))
        gk = jnp.broadcast_to(g_k[l][:, None], (HEAD_DIM, TM))
        qT, k, vT, rest = _inproj(ctx_arr, x_arr, x_base, mods, g_mix[l][None, :], w_in_b, l,
                                  gq, gk, cos_t, sin_t)

        w_gate, b_gate = _gate_weights(lru_wa[l], lru_ba[l], lru_wi[l], lru_bi[l])
        hf, hb, sc = _lru(rest, lru_conv_w[l], lru_conv_b[l][None, :], w_gate, b_gate, lru_lam[l],
                          sc_conv_w[l])

        tile_offset = 1 if last else 0
        x1, h2, tile_edges = _mixer(qT, k, vT, ctx_arr, x_arr, x_base, hf, hb, rest, sc, mods,
                                    w_out_b, l, g_ffn[l][None, :], tile_offset)
        out = _ffn(h2, tile_edges, x1, mods, w_up_b, ffn_conv_w[l], w_down_b, l, g_final[None, :],
                   ctx_tiles=1 - tile_offset, final_norm=last)
        ctx_arr, x_arr, x_base = out, out, 1
    return out
```
